```python
import math
import jax, jax.numpy as jnp
from jax import lax
import numpy as np

D_MODEL = 2048
BATCH = 1
SEQ = 16384
DEPTH = 1
DEC_BATCH = 128
DEC_SEQ = 8
PAST_LEN = 16384
PAGE_SIZE = 128

HEAD_DIM = 128
SWA_HEADS = 8
SWA_KV_HEADS = 2
GQA_GROUP = SWA_HEADS // SWA_KV_HEADS
WINDOW = 128
ML_HEADS = 4
ML_DQK = 64
ML_DV = 128
MEM_HEADS = 4
MEM_LEN = 256
CHUNK = 128
ROPE_THETA = 10000.0
EPS = 1e-6
FORGET_BIAS = 3.0

SWA_W = SWA_HEADS * HEAD_DIM
SWA_KV_W = SWA_KV_HEADS * HEAD_DIM
ML_W = ML_HEADS * ML_DV
ML_QK_W = ML_HEADS * ML_DQK
MEM_W = MEM_HEADS * HEAD_DIM
D_MIX = SWA_W + ML_W + MEM_W
IN_WIDTHS = (SWA_W, SWA_KV_W, SWA_KV_W, SWA_W, ML_QK_W, ML_QK_W, ML_W, ML_HEADS, ML_HEADS, ML_W, ML_W, MEM_W, MEM_W)
D_IN = sum(IN_WIDTHS)
F_OFFSET = SWA_W + 2 * SWA_KV_W + SWA_W + 2 * ML_QK_W + ML_W + ML_HEADS

kernel_name = 'hymba_swa_mlstm_memxattn_step'


def rmsnorm(x, g):
    xf = x.astype(jnp.float32)
    y = xf * lax.rsqrt(jnp.mean(xf * xf, axis=-1, keepdims=True) + EPS)
    return (y * g.astype(jnp.float32)).astype(x.dtype)


def rope(x, pos):
    dh = x.shape[-1]
    half = dh // 2
    inv = jnp.power(ROPE_THETA, -(jnp.arange(half, dtype=jnp.float32) * 2.0 / dh))
    ang = pos.astype(jnp.float32)[:, None] * inv[None, :]
    shape = (1, ang.shape[0]) + (1,) * (x.ndim - 3) + (half,)
    cos = jnp.cos(ang).reshape(shape)
    sin = jnp.sin(ang).reshape(shape)
    xf = x.astype(jnp.float32)
    x1, x2 = xf[..., :half], xf[..., half:]
    return jnp.concatenate([x1 * cos - x2 * sin, x2 * cos + x1 * sin], axis=-1).astype(x.dtype)


def split_cols(z):
    idx = []
    acc = 0
    for w in IN_WIDTHS[:-1]:
        acc += w
        idx.append(acc)
    return jnp.split(z, idx, axis=-1)


def project(x, g_norm, w_in, b_in):
    u = rmsnorm(x, g_norm)
    z = jnp.einsum('bsd,de->bse', u, w_in) + b_in
    return split_cols(z)


def sink_attend(q, k, v, mask, sinks):
    s = jnp.einsum('...qkgd,...skd->...kgqs', q, k).astype(jnp.float32) * (HEAD_DIM ** -0.5)
    s = jnp.where(mask, s, -jnp.inf)
    sk = sinks.astype(jnp.float32)[:, :, None, None]
    m = jnp.maximum(jnp.max(s, axis=-1, keepdims=True), sk)
    p = jnp.exp(s - m)
    p = p / (jnp.sum(p, axis=-1, keepdims=True) + jnp.exp(sk - m))
    return jnp.einsum('...kgqs,...skd->...qkgd', p.astype(v.dtype), v)


def swa_prompt(q, k, v, sinks):
    B, S = q.shape[:2]
    blk = WINDOW
    nb = S // blk
    qb = q.reshape(B, nb, blk, SWA_KV_HEADS, GQA_GROUP, HEAD_DIM)
    pad = ((0, 0), (blk, 0), (0, 0), (0, 0))
    kp = jnp.pad(k, pad)
    vp = jnp.pad(v, pad)
    kb = jnp.concatenate([kp[:, :S].reshape(B, nb, blk, SWA_KV_HEADS, HEAD_DIM),
                          k.reshape(B, nb, blk, SWA_KV_HEADS, HEAD_DIM)], axis=2)
    vb = jnp.concatenate([vp[:, :S].reshape(B, nb, blk, SWA_KV_HEADS, HEAD_DIM),
                          v.reshape(B, nb, blk, SWA_KV_HEADS, HEAD_DIM)], axis=2)
    i = jnp.arange(blk)[:, None]
    j = jnp.arange(2 * blk)[None, :]
    diff = i + blk - j
    band = (diff >= 0) & (diff < WINDOW)
    kpos_ok = (jnp.arange(nb)[:, None, None] * blk - blk + j[None]) >= 0
    mask = (band[None] & kpos_ok)[:, None, None]
    o = sink_attend(qb, kb, vb, mask, sinks)
    return o.reshape(B, S, SWA_W)


def swa_sample(q, k_new, v_new, k_buf, v_buf, sinks):
    B, T = q.shape[:2]
    wb = k_buf.shape[1]
    k_all = jnp.concatenate([k_buf, k_new], axis=1)
    v_all = jnp.concatenate([v_buf, v_new], axis=1)
    kpos = PAST_LEN - wb + jnp.arange(wb + T)
    qpos = PAST_LEN + jnp.arange(T)
    mask = (kpos[None, :] <= qpos[:, None]) & (qpos[:, None] - kpos[None, :] < WINDOW)
    o = sink_attend(q, k_all, v_all, mask, sinks)
    return o.reshape(B, T, SWA_W), k_all[:, -wb:], v_all[:, -wb:]


def mlstm_inputs(mq, mk, mv, mi, mf):
    B, S = mq.shape[:2]
    q = mq.reshape(B, S, ML_HEADS, ML_DQK).astype(jnp.float32)
    k = mk.reshape(B, S, ML_HEADS, ML_DQK).astype(jnp.float32) * (ML_DQK ** -0.5)
    v = mv.reshape(B, S, ML_HEADS, ML_DV).astype(jnp.float32)
    li = mi.astype(jnp.float32)
    lf = jax.nn.log_sigmoid(mf.astype(jnp.float32))
    return q, k, v, li, lf


def mlstm_chunk(carry, inp):
    C, n, m = carry
    q, k, v, li, lf = inp
    L = q.shape[1]
    bt = jnp.cumsum(lf, axis=1).transpose(0, 2, 1)
    it = li.transpose(0, 2, 1)
    causal = jnp.tril(jnp.ones((L, L), dtype=bool))
    dlog = jnp.where(causal, bt[..., :, None] - bt[..., None, :] + it[..., None, :], -jnp.inf)
    inter = bt + m[..., None]
    m_t = jnp.maximum(inter, jnp.max(dlog, axis=-1))
    w_intra = jnp.exp(dlog - m_t[..., None])
    w_state = jnp.exp(inter - m_t)
    a = w_intra * jnp.einsum('bthd,bshd->bhts', q, k)
    num = jnp.einsum('bhts,bshe->bthe', a, v) + jnp.einsum('bht,bhed,bthd->bthe', w_state, C, q)
    den = jnp.sum(a, axis=-1) + w_state * jnp.einsum('bhd,bthd->bht', n, q)
    denom = jnp.maximum(jnp.abs(den), jnp.exp(-m_t))
    h = num / denom.transpose(0, 2, 1)[..., None]
    m_new = m_t[..., -1]
    w_s = jnp.exp(bt[..., -1:] - bt + it - m_new[..., None])
    decay = jnp.exp(bt[..., -1] + m - m_new)
    C_new = decay[..., None, None] * C + jnp.einsum('bhs,bshe,bshd->bhed', w_s, v, k)
    n_new = decay[..., None] * n + jnp.einsum('bhs,bshd->bhd', w_s, k)
    return (C_new, n_new, m_new), h


def mlstm_prompt(q, k, v, li, lf):
    B, S = q.shape[:2]
    L = min(CHUNK, S)
    nc = S // L

    def to_chunks(a):
        return a.reshape((B, nc, L) + a.shape[2:]).swapaxes(0, 1)

    init = (jnp.zeros((B, ML_HEADS, ML_DV, ML_DQK), jnp.float32),
            jnp.zeros((B, ML_HEADS, ML_DQK), jnp.float32),
            jnp.zeros((B, ML_HEADS), jnp.float32))
    carry, h = lax.scan(mlstm_chunk, init, (to_chunks(q), to_chunks(k), to_chunks(v), to_chunks(li), to_chunks(lf)))
    return h.swapaxes(0, 1).reshape(B, S, ML_HEADS, ML_DV), carry


def mlstm_out(h, mo, dtype):
    B, S = h.shape[:2]
    return (h.reshape(B, S, ML_W) * jax.nn.sigmoid(mo.astype(jnp.float32))).astype(dtype)


def mem_kv(mem, g_mem, w_mem_kv):
    B, M, _ = mem.shape
    kv = jnp.einsum('bmd,de->bme', rmsnorm(mem, g_mem), w_mem_kv)
    k, v = jnp.split(kv, 2, axis=-1)
    return k.reshape(B, M, MEM_HEADS, HEAD_DIM), v.reshape(B, M, MEM_HEADS, HEAD_DIM)


def mem_attend(cq, mk, mv):
    B, S = cq.shape[:2]
    q = cq.reshape(B, S, MEM_HEADS, HEAD_DIM)
    s = jnp.einsum('bshd,bmhd->bhsm', q, mk).astype(jnp.float32) * (HEAD_DIM ** -0.5)
    p = jax.nn.softmax(s, axis=-1).astype(mv.dtype)
    return jnp.einsum('bhsm,bmhd->bshd', p, mv).reshape(B, S, MEM_W)


def merge(x, swa_o, sg, ml_o, mg, cr_o, cg, w_out):
    y = jnp.concatenate([swa_o * jax.nn.silu(sg), ml_o * jax.nn.silu(mg), cr_o * jax.nn.silu(cg)], axis=-1)
    return x + jnp.einsum('bse,ed->bsd', y, w_out)


def prompt_layer(x, mem, g_norm, w_in, b_in, sinks, g_mem, w_mem_kv, w_out):
    B, S, _ = x.shape
    pos = jnp.arange(S, dtype=jnp.int32)
    sq, sk, sv, sg, mq, mk, mv, mi, mf, mo, mg, cq, cg = project(x, g_norm, w_in, b_in)
    qa = rope(sq.reshape(B, S, SWA_KV_HEADS, GQA_GROUP, HEAD_DIM), pos)
    ka = rope(sk.reshape(B, S, SWA_KV_HEADS, HEAD_DIM), pos)
    va = sv.reshape(B, S, SWA_KV_HEADS, HEAD_DIM)
    swa_o = swa_prompt(qa, ka, va, sinks)
    h_m, (C, n, m) = mlstm_prompt(*mlstm_inputs(mq, mk, mv, mi, mf))
    ml_o = mlstm_out(h_m, mo, x.dtype)
    mem_k, mem_v = mem_kv(mem, g_mem, w_mem_kv)
    cr_o = mem_attend(cq, mem_k, mem_v)
    x = merge(x, swa_o, sg, ml_o, mg, cr_o, cg, w_out)
    buf = min(WINDOW, S)
    return x, (ka[:, S - buf:], va[:, S - buf:], C, n, m, mem_k, mem_v)


def sample_layer(x, k_buf, v_buf, C, n, m, mem_k, mem_v, g_norm, w_in, b_in, sinks, w_out):
    B, T, _ = x.shape
    pos = PAST_LEN + jnp.arange(T, dtype=jnp.int32)
    sq, sk, sv, sg, mq, mk, mv, mi, mf, mo, mg, cq, cg = project(x, g_norm, w_in, b_in)
    qa = rope(sq.reshape(B, T, SWA_KV_HEADS, GQA_GROUP, HEAD_DIM), pos)
    ka = rope(sk.reshape(B, T, SWA_KV_HEADS, HEAD_DIM), pos)
    va = sv.reshape(B, T, SWA_KV_HEADS, HEAD_DIM)
    swa_o, k_new_buf, v_new_buf = swa_sample(qa, ka, va, k_buf, v_buf, sinks)
    state = (C.astype(jnp.float32), n.astype(jnp.float32), m.astype(jnp.float32))
    (C2, n2, m2), h_m = mlstm_chunk(state, mlstm_inputs(mq, mk, mv, mi, mf))
    ml_o = mlstm_out(h_m, mo, x.dtype)
    cr_o = mem_attend(cq, mem_k, mem_v)
    x = merge(x, swa_o, sg, ml_o, mg, cr_o, cg, w_out)
    return x, (k_new_buf, v_new_buf, C2, n2, m2)


def setup_inputs(seed: int = 0) -> dict:
    key = jax.random.key(seed)
    ks = jax.random.split(key, 20)
    f32 = jnp.float32
    wb = min(WINDOW, PAST_LEN)
    b_in = 0.02 * jax.random.normal(ks[0], (DEPTH, D_IN), f32)
    b_in = b_in.at[:, F_OFFSET:F_OFFSET + ML_HEADS].add(FORGET_BIAS)
    return {
        'x_prompt': jax.random.normal(ks[1], (BATCH, SEQ, D_MODEL), f32),
        'x_sample': jax.random.normal(ks[2], (DEC_BATCH, DEC_SEQ, D_MODEL), f32),
        'mem_prompt': jax.random.normal(ks[3], (BATCH, MEM_LEN, D_MODEL), f32),
        'cache_swa_k': jax.random.normal(ks[4], (DEPTH, DEC_BATCH, wb, SWA_KV_HEADS, HEAD_DIM), f32),
        'cache_swa_v': jax.random.normal(ks[5], (DEPTH, DEC_BATCH, wb, SWA_KV_HEADS, HEAD_DIM), f32),
        'state_mlstm_C': 0.3 * jax.random.normal(ks[6], (DEPTH, DEC_BATCH, ML_HEADS, ML_DV, ML_DQK), f32),
        'state_mlstm_n': 0.3 * jax.random.normal(ks[7], (DEPTH, DEC_BATCH, ML_HEADS, ML_DQK), f32),
        'state_mlstm_m': jax.random.normal(ks[8], (DEPTH, DEC_BATCH, ML_HEADS), f32),
        'cache_mem_k': jax.random.normal(ks[9], (DEPTH, DEC_BATCH, MEM_LEN, MEM_HEADS, HEAD_DIM), f32),
        'cache_mem_v': jax.random.normal(ks[10], (DEPTH, DEC_BATCH, MEM_LEN, MEM_HEADS, HEAD_DIM), f32),
        'g_norm': 1.0 + 0.02 * jax.random.normal(ks[11], (DEPTH, D_MODEL), f32),
        'w_in': jax.random.normal(ks[12], (DEPTH, D_MODEL, D_IN), f32) * (D_MODEL ** -0.5),
        'b_in': b_in,
        'swa_sinks': jax.random.normal(ks[13], (DEPTH, SWA_KV_HEADS, GQA_GROUP), f32),
        'g_mem': 1.0 + 0.02 * jax.random.normal(ks[14], (DEPTH, D_MODEL), f32),
        'w_mem_kv': jax.random.normal(ks[15], (DEPTH, D_MODEL, 2 * MEM_W), f32) * (D_MODEL ** -0.5),
        'w_out': jax.random.normal(ks[16], (DEPTH, D_MIX, D_MODEL), f32) * (D_MIX ** -0.5),
        'g_final': 1.0 + 0.02 * jax.random.normal(ks[17], (D_MODEL,), f32),
    }


def reference(x_prompt, x_sample, mem_prompt, cache_swa_k, cache_swa_v, state_mlstm_C, state_mlstm_n,
              state_mlstm_m, cache_mem_k, cache_mem_v, g_norm, w_in, b_in, swa_sinks, g_mem, w_mem_kv,
              w_out, g_final):
    xp = x_prompt
    xs = x_sample
    p_states = []
    s_states = []
    for l in range(DEPTH):
        xp, st_p = prompt_layer(xp, mem_prompt, g_norm[l], w_in[l], b_in[l], swa_sinks[l], g_mem[l],
                                w_mem_kv[l], w_out[l])
        xs, st_s = sample_layer(xs, cache_swa_k[l], cache_swa_v[l], state_mlstm_C[l], state_mlstm_n[l],
                                state_mlstm_m[l], cache_mem_k[l], cache_mem_v[l], g_norm[l], w_in[l],
                                b_in[l], swa_sinks[l], w_out[l])
        p_states.append(st_p)
        s_states.append(st_s)
    y_prompt = rmsnorm(xp, g_final)
    y_sample = rmsnorm(xs, g_final)
    p_swa_k = jnp.stack([s[0] for s in p_states], 0)
    p_swa_v = jnp.stack([s[1] for s in p_states], 0)
    p_C = jnp.stack([s[2] for s in p_states], 0)
    p_n = jnp.stack([s[3] for s in p_states], 0)
    p_m = jnp.stack([s[4] for s in p_states], 0)
    p_mem_k = jnp.stack([s[5] for s in p_states], 0)
    p_mem_v = jnp.stack([s[6] for s in p_states], 0)
    s_swa_k = jnp.stack([s[0] for s in s_states], 0)
    s_swa_v = jnp.stack([s[1] for s in s_states], 0)
    s_C = jnp.stack([s[2] for s in s_states], 0)
    s_n = jnp.stack([s[3] for s in s_states], 0)
    s_m = jnp.stack([s[4] for s in s_states], 0)
    return (y_prompt, y_sample, p_swa_k, p_swa_v, p_C, p_n, p_m, p_mem_k, p_mem_v, s_swa_k, s_swa_v, s_C, s_n, s_m)
```

```python
import functools
import math

import jax
import jax.numpy as jnp
from jax import lax
from jax.experimental import pallas as pl
from jax.experimental.pallas import tpu as pltpu

F32 = jnp.float32
BF16 = jnp.bfloat16

D_MODEL = 2048
HEAD_DIM = 128
SWA_HEADS = 8
SWA_KV_HEADS = 2
GQA_GROUP = 4
WINDOW = 128
ML_HEADS = 4
ML_DQK = 64
ML_DV = 128
MEM_HEADS = 4
MEM_LEN = 256
CHUNK = 128
ROPE_THETA = 10000.0
EPS = 1e-6
PAST_LEN = 16384
DEC_SEQ = 8

SWA_W = SWA_HEADS * HEAD_DIM
SWA_KV_W = SWA_KV_HEADS * HEAD_DIM
ML_W = ML_HEADS * ML_DV
ML_QK_W = ML_HEADS * ML_DQK
MEM_W = MEM_HEADS * HEAD_DIM
D_MIX = SWA_W + ML_W + MEM_W

_IN_WIDTHS = (SWA_W, SWA_KV_W, SWA_KV_W, SWA_W, ML_QK_W, ML_QK_W, ML_W, ML_HEADS, ML_HEADS, ML_W, ML_W, MEM_W, MEM_W)
_IN_OFFS = [0]
for _w in _IN_WIDTHS:
    _IN_OFFS.append(_IN_OFFS[-1] + _w)
(_R_SQ, _R_SK, _R_SV, _R_SG, _R_MQ, _R_MK, _R_MV, _R_MI, _R_MF, _R_MO, _R_MG, _R_CQ, _R_CG, _R_END) = _IN_OFFS

O_SQ = 0
O_SK = O_SQ + SWA_W
O_SV = O_SK + SWA_KV_W
O_SG = O_SV + SWA_KV_W
O_MQ = O_SG + SWA_W
O_MK = O_MQ + ML_QK_W
O_MV = O_MK + ML_QK_W
O_MO = O_MV + ML_W
O_MG = O_MO + ML_W
O_CQ = O_MG + ML_W
O_CG = O_CQ + MEM_W
D_MAIN = O_CG + MEM_W
GATE_LANES = 128
LANE_I = 0
LANE_F = ML_HEADS

VMEM_LIMIT = 56 * 1024 * 1024

_NT = (((1,), (1,)), ((), ()))


def _dot(a, b):
    return jnp.dot(a, b, preferred_element_type=F32)


def _dot_nt(a, b):
    return lax.dot_general(a, b, _NT, preferred_element_type=F32)


def _exact_sel_dot(sel_bf16, x):
    hi = x.astype(BF16)
    r1 = x - hi.astype(F32)
    mid = r1.astype(BF16)
    lo = (r1 - mid.astype(F32)).astype(BF16)
    return _dot(sel_bf16, hi) + _dot(sel_bf16, mid) + _dot(sel_bf16, lo)


def _silu(x):
    return x * (1.0 / (1.0 + jnp.exp(-x)))


def _sigmoid(x):
    return 1.0 / (1.0 + jnp.exp(-x))


def _log_sigmoid(x):
    return jnp.minimum(x, 0.0) - jnp.log1p(jnp.exp(-jnp.abs(x)))


PROJ_TM = 1024
PROJ_TN = 512
NORM_ROWS = 256


def _proj_kernel(x_ref, g_ref, w_ref, b_ref, wg_ref, bg_ref, z_ref, zg_ref, u_ref):
    j = pl.program_id(1)

    @pl.when(j == 0)
    def _():
        g = g_ref[...]
        for r in range(PROJ_TM // NORM_ROWS):
            rows = pl.ds(r * NORM_ROWS, NORM_ROWS)
            xf = x_ref[rows, :]
            ms = jnp.mean(xf * xf, axis=-1, keepdims=True)
            u_ref[rows, :] = ((xf * lax.rsqrt(ms + EPS)) * g).astype(BF16)
        zg_ref[...] = _dot(u_ref[...], wg_ref[...]) + bg_ref[...]

    z_ref[...] = _dot(u_ref[...], w_ref[...]) + b_ref[...]


def _proj(x2d, g, w, b, wg, bg):
    n = x2d.shape[0]
    grid = (n // PROJ_TM, D_MAIN // PROJ_TN)
    return pl.pallas_call(
        _proj_kernel,
        grid=grid,
        in_specs=[
            pl.BlockSpec((PROJ_TM, D_MODEL), lambda i, j: (i, 0)),
            pl.BlockSpec((1, D_MODEL), lambda i, j: (0, 0)),
            pl.BlockSpec((D_MODEL, PROJ_TN), lambda i, j: (0, j)),
            pl.BlockSpec((1, PROJ_TN), lambda i, j: (0, j)),
            pl.BlockSpec((D_MODEL, GATE_LANES), lambda i, j: (0, 0)),
            pl.BlockSpec((1, GATE_LANES), lambda i, j: (0, 0)),
        ],
        out_specs=[
            pl.BlockSpec((PROJ_TM, PROJ_TN), lambda i, j: (i, j)),
            pl.BlockSpec((PROJ_TM, GATE_LANES), lambda i, j: (i, 0)),
        ],
        out_shape=[
            jax.ShapeDtypeStruct((n, D_MAIN), F32),
            jax.ShapeDtypeStruct((n, GATE_LANES), F32),
        ],
        scratch_shapes=[pltpu.VMEM((PROJ_TM, D_MODEL), BF16)],
        compiler_params=pltpu.CompilerParams(
            dimension_semantics=("arbitrary", "arbitrary"), vmem_limit_bytes=VMEM_LIMIT),
        name="proj",
    )(x2d, g, w, b, wg, bg)


MEMKV_TN = 256


def _memkv_kernel(mem_ref, g_ref, w_ref, o_ref):
    xf = mem_ref[...]
    ms = jnp.mean(xf * xf, axis=-1, keepdims=True)
    u = ((xf * lax.rsqrt(ms + EPS)) * g_ref[...]).astype(BF16)
    o_ref[...] = _dot(u, w_ref[...])


def _memkv(mem2d, g, w):
    return pl.pallas_call(
        _memkv_kernel,
        grid=(2 * MEM_W // MEMKV_TN,),
        in_specs=[
            pl.BlockSpec((MEM_LEN, D_MODEL), lambda j: (0, 0)),
            pl.BlockSpec((1, D_MODEL), lambda j: (0, 0)),
            pl.BlockSpec((D_MODEL, MEMKV_TN), lambda j: (0, j)),
        ],
        out_specs=pl.BlockSpec((MEM_LEN, MEMKV_TN), lambda j: (0, j)),
        out_shape=jax.ShapeDtypeStruct((MEM_LEN, 2 * MEM_W), F32),
        compiler_params=pltpu.CompilerParams(
            dimension_semantics=("arbitrary",), vmem_limit_bytes=VMEM_LIMIT),
        name="memkv",
    )(mem2d, g, w)


def _rope(x, cos, sin_signed):
    return x * cos + pltpu.roll(x, HEAD_DIM // 2, axis=1) * sin_signed


def _mix_prompt_kernel(sinks_ref, z_ref, zg_ref, cos_ref, sin_ref, mk_ref, mv_ref,
                       y_ref, ko_ref, vo_ref, ct_ref, n_ref, m_ref,
                       kprev_ref, vprev_ref):
    c = pl.program_id(0)
    L = CHUNK
    scale = HEAD_DIM ** -0.5

    @pl.when(c == 0)
    def _():
        kprev_ref[...] = jnp.zeros_like(kprev_ref)
        vprev_ref[...] = jnp.zeros_like(vprev_ref)
        ct_ref[...] = jnp.zeros_like(ct_ref)
        n_ref[...] = jnp.zeros_like(n_ref)
        m_ref[...] = jnp.zeros_like(m_ref)

    cos = cos_ref[...]
    sin = sin_ref[...]

    ri = lax.broadcasted_iota(jnp.int32, (GQA_GROUP * L, 2 * L), 0) & (L - 1)
    cj = lax.broadcasted_iota(jnp.int32, (GQA_GROUP * L, 2 * L), 1)
    j_low = jnp.where(c > 0, 0, L)
    band = (cj > ri) & (cj <= ri + L) & (cj >= j_low)
    for kv in range(SWA_KV_HEADS):
        k_new = _rope(z_ref[:, O_SK + kv * HEAD_DIM:O_SK + (kv + 1) * HEAD_DIM], cos, sin)
        v_new = z_ref[:, O_SV + kv * HEAD_DIM:O_SV + (kv + 1) * HEAD_DIM]
        ko_ref[:, kv * HEAD_DIM:(kv + 1) * HEAD_DIM] = k_new
        vo_ref[:, kv * HEAD_DIM:(kv + 1) * HEAD_DIM] = v_new
        k_new_b = k_new.astype(BF16)
        v_new_b = v_new.astype(BF16)
        kcat = jnp.concatenate([kprev_ref[:, kv * HEAD_DIM:(kv + 1) * HEAD_DIM], k_new_b], axis=0)
        vcat = jnp.concatenate([vprev_ref[:, kv * HEAD_DIM:(kv + 1) * HEAD_DIM], v_new_b], axis=0)
        kprev_ref[:, kv * HEAD_DIM:(kv + 1) * HEAD_DIM] = k_new_b
        vprev_ref[:, kv * HEAD_DIM:(kv + 1) * HEAD_DIM] = v_new_b
        qs = []
        sks = []
        for g in range(GQA_GROUP):
            h = kv * GQA_GROUP + g
            qs.append(_rope(z_ref[:, O_SQ + h * HEAD_DIM:O_SQ + (h + 1) * HEAD_DIM], cos, sin).astype(BF16))
            sks.append(jnp.full((L, 1), sinks_ref[h], F32))
        q_st = jnp.concatenate(qs, axis=0)
        sk = jnp.concatenate(sks, axis=0)
        s = _dot_nt(q_st, kcat) * scale
        s = jnp.where(band, s, -jnp.inf)
        m = jnp.maximum(jnp.max(s, axis=-1, keepdims=True), sk)
        p = jnp.exp(s - m)
        l = jnp.sum(p, axis=-1, keepdims=True) + jnp.exp(sk - m)
        o = _dot(p.astype(BF16), vcat) * (1.0 / l)
        for g in range(GQA_GROUP):
            h = kv * GQA_GROUP + g
            gate = _silu(z_ref[:, O_SG + h * HEAD_DIM:O_SG + (h + 1) * HEAD_DIM])
            y_ref[:, h * HEAD_DIM:(h + 1) * HEAD_DIM] = (o[g * L:(g + 1) * L] * gate).astype(BF16)

    for h in range(MEM_HEADS):
        q = z_ref[:, O_CQ + h * HEAD_DIM:O_CQ + (h + 1) * HEAD_DIM].astype(BF16)
        s = _dot_nt(q, mk_ref[:, h * HEAD_DIM:(h + 1) * HEAD_DIM]) * scale
        m = jnp.max(s, axis=-1, keepdims=True)
        p = jnp.exp(s - m)
        l = jnp.sum(p, axis=-1, keepdims=True)
        o = _dot(p.astype(BF16), mv_ref[:, h * HEAD_DIM:(h + 1) * HEAD_DIM]) * (1.0 / l)
        gate = _silu(z_ref[:, O_CG + h * HEAD_DIM:O_CG + (h + 1) * HEAD_DIM])
        col = SWA_W + ML_W + h * HEAD_DIM
        y_ref[:, col:col + HEAD_DIM] = (o * gate).astype(BF16)

    lane = lax.broadcasted_iota(jnp.int32, (1, GATE_LANES), 1)
    zg = zg_ref[...]
    gates = jnp.where(lane < LANE_F, zg, _log_sigmoid(zg))
    gates = jnp.where(lane < 2 * ML_HEADS, gates, 0.0)
    rr = lax.broadcasted_iota(jnp.int32, (L, L), 0)
    cc = lax.broadcasted_iota(jnp.int32, (L, L), 1)
    causal = rr >= cc
    tri = jnp.where(causal, 1.0, 0.0).astype(BF16)
    csum = _exact_sel_dot(tri, gates)
    gates_t = gates.T
    csum_t = csum.T
    half_lo = lane < ML_DQK
    for h in range(ML_HEADS):
        pair, half = divmod(h, 2)
        hmask = half_lo if half == 0 else jnp.logical_not(half_lo)
        q_pair = z_ref[:, O_MQ + pair * 128:O_MQ + (pair + 1) * 128]
        k_pair = z_ref[:, O_MK + pair * 128:O_MK + (pair + 1) * 128] * (ML_DQK ** -0.5)
        v_h = z_ref[:, O_MV + h * ML_DV:O_MV + (h + 1) * ML_DV].astype(BF16)
        q_h = jnp.where(hmask, q_pair, 0.0)
        q_hb = q_h.astype(BF16)
        k_pb = k_pair.astype(BF16)

        bt_c = csum[:, LANE_F + h:LANE_F + h + 1]
        it_c = gates[:, LANE_I + h:LANE_I + h + 1]
        bt_r = csum_t[LANE_F + h:LANE_F + h + 1, :]
        it_r = gates_t[LANE_I + h:LANE_I + h + 1, :]
        m_prev = m_ref[:, h:h + 1]

        dlog = jnp.where(causal, bt_c - bt_r + it_r, -jnp.inf)
        inter = bt_c + m_prev
        m_t = jnp.maximum(inter, jnp.max(dlog, axis=-1, keepdims=True))
        w_intra = jnp.exp(dlog - m_t)
        w_state = jnp.exp(inter - m_t)
        a = w_intra * _dot_nt(q_hb, k_pb)
        ct_pair = ct_ref[pair * 128:(pair + 1) * 128, :]
        num = _dot(a.astype(BF16), v_h) + w_state * _dot(q_hb, ct_pair.astype(BF16))
        n_pair = n_ref[:, pair * 128:(pair + 1) * 128]
        den = jnp.sum(a, axis=-1, keepdims=True) + w_state * jnp.sum(q_h * n_pair, axis=-1, keepdims=True)
        denom = jnp.maximum(jnp.abs(den), jnp.exp(-m_t))
        hid = num * (1.0 / denom)
        o_gate = _sigmoid(z_ref[:, O_MO + h * ML_DV:O_MO + (h + 1) * ML_DV])
        gate = _silu(z_ref[:, O_MG + h * ML_DV:O_MG + (h + 1) * ML_DV])
        col = SWA_W + h * ML_DV
        y_ref[:, col:col + ML_DV] = ((hid * o_gate) * gate).astype(BF16)

        m_new = m_t[L - 1:L, :]
        bt_last = bt_c[L - 1:L, :]
        w_s = jnp.exp(bt_last - bt_c + it_c - m_new)
        decay = jnp.exp(bt_last + m_prev - m_new)
        kw = k_pair * w_s
        upd = _dot(kw.T.astype(BF16), v_h)
        rows = slice(h * ML_DQK, (h + 1) * ML_DQK)
        ct_ref[rows, :] = decay * ct_ref[rows, :] + upd[half * ML_DQK:(half + 1) * ML_DQK, :]
        ksum = jnp.sum(kw, axis=0, keepdims=True)
        n_ref[:, rows] = decay * n_ref[:, rows] + ksum[:, half * ML_DQK:(half + 1) * ML_DQK]
        m_ref[:, h:h + 1] = m_new


def _mix_prompt(sinks, z, zg, cos, sin, mk, mv):
    n = z.shape[0]
    nc = n // CHUNK
    const = lambda c, s: (0, 0)
    return pl.pallas_call(
        _mix_prompt_kernel,
        grid_spec=pltpu.PrefetchScalarGridSpec(
            num_scalar_prefetch=1,
            grid=(nc,),
            in_specs=[
                pl.BlockSpec((CHUNK, D_MAIN), lambda c, s: (c, 0)),
                pl.BlockSpec((CHUNK, GATE_LANES), lambda c, s: (c, 0)),
                pl.BlockSpec((CHUNK, HEAD_DIM), lambda c, s: (c, 0)),
                pl.BlockSpec((CHUNK, HEAD_DIM), lambda c, s: (c, 0)),
                pl.BlockSpec((MEM_LEN, MEM_W), const),
                pl.BlockSpec((MEM_LEN, MEM_W), const),
            ],
            out_specs=[
                pl.BlockSpec((CHUNK, D_MIX), lambda c, s: (c, 0)),
                pl.BlockSpec((CHUNK, SWA_KV_W), const),
                pl.BlockSpec((CHUNK, SWA_KV_W), const),
                pl.BlockSpec((ML_QK_W, ML_DV), const),
                pl.BlockSpec((1, ML_QK_W), const),
                pl.BlockSpec((1, GATE_LANES), const),
            ],
            scratch_shapes=[
                pltpu.VMEM((CHUNK, SWA_KV_W), BF16),
                pltpu.VMEM((CHUNK, SWA_KV_W), BF16),
            ],
        ),
        out_shape=[
            jax.ShapeDtypeStruct((n, D_MIX), BF16),
            jax.ShapeDtypeStruct((CHUNK, SWA_KV_W), F32),
            jax.ShapeDtypeStruct((CHUNK, SWA_KV_W), F32),
            jax.ShapeDtypeStruct((ML_QK_W, ML_DV), F32),
            jax.ShapeDtypeStruct((1, ML_QK_W), F32),
            jax.ShapeDtypeStruct((1, GATE_LANES), F32),
        ],
        compiler_params=pltpu.CompilerParams(
            dimension_semantics=("arbitrary",), vmem_limit_bytes=VMEM_LIMIT),
        name="mix_prompt",
    )(sinks, z, zg, cos, sin, mk, mv)


OUT_TM = 512


def _outproj_kernel(y_ref, x_ref, w_ref, g_ref, o_ref):
    acc = _dot(y_ref[...], w_ref[...]) + x_ref[...]
    ms = jnp.mean(acc * acc, axis=-1, keepdims=True)
    o_ref[...] = (acc * lax.rsqrt(ms + EPS)) * g_ref[...]


def _outproj(y, x2d, w, g):
    n = x2d.shape[0]
    return pl.pallas_call(
        _outproj_kernel,
        grid=(n // OUT_TM,),
        in_specs=[
            pl.BlockSpec((OUT_TM, D_MIX), lambda i: (i, 0)),
            pl.BlockSpec((OUT_TM, D_MODEL), lambda i: (i, 0)),
            pl.BlockSpec((D_MIX, D_MODEL), lambda i: (0, 0)),
            pl.BlockSpec((1, D_MODEL), lambda i: (0, 0)),
        ],
        out_specs=pl.BlockSpec((OUT_TM, D_MODEL), lambda i: (i, 0)),
        out_shape=jax.ShapeDtypeStruct((n, D_MODEL), F32),
        compiler_params=pltpu.CompilerParams(
            dimension_semantics=("arbitrary",), vmem_limit_bytes=VMEM_LIMIT),
        name="outproj",
    )(y, x2d, w, g)


SB = 16
SR = SB * DEC_SEQ


def _mix_sample_kernel(sinks_ref, z_ref, zg_ref, cos_ref, sin_ref, ck_ref, cv_ref,
                       c_ref, nst_ref, nrep_ref, mrep_ref,
                       y_ref, ko_ref, vo_ref, co_ref, no_ref, mo_ref,
                       q_s, k_s, o_s):
    T = DEC_SEQ
    scale = HEAD_DIM ** -0.5
    cos = cos_ref[...]
    sin = sin_ref[...]

    for h in range(SWA_HEADS):
        q_s[:, h * HEAD_DIM:(h + 1) * HEAD_DIM] = _rope(
            z_ref[:, O_SQ + h * HEAD_DIM:O_SQ + (h + 1) * HEAD_DIM], cos, sin)
    for kv in range(SWA_KV_HEADS):
        k_s[:, kv * HEAD_DIM:(kv + 1) * HEAD_DIM] = _rope(
            z_ref[:, O_SK + kv * HEAD_DIM:O_SK + (kv + 1) * HEAD_DIM], cos, sin)

    KP = 2 * WINDOW
    rt = lax.broadcasted_iota(jnp.int32, (GQA_GROUP * T, KP), 0) & (T - 1)
    cj = lax.broadcasted_iota(jnp.int32, (GQA_GROUP * T, KP), 1)
    mask = (cj > rt) & (cj <= rt + WINDOW)
    kpad = jnp.zeros((KP - WINDOW - T, SWA_KV_W), F32)

    def per_seq(b, carry):
        rows = pl.ds(pl.multiple_of(b * T, T), T)
        kc = ck_ref[b]
        vc = cv_ref[b]
        k_new = k_s[rows, :]
        v_new = z_ref[rows, O_SV:O_SV + SWA_KV_W]
        ko_ref[b, 0:WINDOW - T, :] = kc[T:WINDOW, :]
        ko_ref[b, WINDOW - T:WINDOW, :] = k_new
        vo_ref[b, 0:WINDOW - T, :] = vc[T:WINDOW, :]
        vo_ref[b, WINDOW - T:WINDOW, :] = v_new
        k_all = jnp.concatenate([kc, k_new, kpad], axis=0).astype(BF16)
        v_all = jnp.concatenate([vc, v_new, kpad], axis=0).astype(BF16)
        for kv in range(SWA_KV_HEADS):
            cols = slice(kv * HEAD_DIM, (kv + 1) * HEAD_DIM)
            qs = []
            sks = []
            for g in range(GQA_GROUP):
                h = kv * GQA_GROUP + g
                qs.append(q_s[rows, h * HEAD_DIM:(h + 1) * HEAD_DIM])
                sks.append(jnp.full((T, 1), sinks_ref[h], F32))
            q_st = jnp.concatenate(qs, axis=0).astype(BF16)
            sk = jnp.concatenate(sks, axis=0)
            s = jnp.where(mask, _dot_nt(q_st, k_all[:, cols]) * scale, -jnp.inf)
            m = jnp.maximum(jnp.max(s, axis=-1, keepdims=True), sk)
            p = jnp.exp(s - m)
            l = jnp.sum(p, axis=-1, keepdims=True) + jnp.exp(sk - m)
            o = _dot(p.astype(BF16), v_all[:, cols]) * (1.0 / l)
            for g in range(GQA_GROUP):
                h = kv * GQA_GROUP + g
                o_s[rows, h * HEAD_DIM:(h + 1) * HEAD_DIM] = o[g * T:(g + 1) * T]
        return carry

    lax.fori_loop(0, SB, per_seq, 0)

    for h in range(SWA_HEADS):
        cols = slice(h * HEAD_DIM, (h + 1) * HEAD_DIM)
        gate = _silu(z_ref[:, O_SG + h * HEAD_DIM:O_SG + (h + 1) * HEAD_DIM])
        y_ref[:, cols] = (o_s[:, cols] * gate).astype(BF16)

    R = SR
    lane = lax.broadcasted_iota(jnp.int32, (1, GATE_LANES), 1)
    zg = zg_ref[...]
    gates = jnp.where(lane < LANE_F, zg, _log_sigmoid(zg))
    gates = jnp.where(lane < 2 * ML_HEADS, gates, 0.0)
    rr = lax.broadcasted_iota(jnp.int32, (R, R), 0)
    cc = lax.broadcasted_iota(jnp.int32, (R, R), 1)
    same_seq = (rr >> 3) == (cc >> 3)
    causal = same_seq & (rr >= cc)
    tri = jnp.where(causal, 1.0, 0.0).astype(BF16)
    csum = _exact_sel_dot(tri, gates)
    gates_t = gates.T
    csum_t = csum.T
    half_lo = lane < ML_DQK
    seq_of_col = lax.broadcasted_iota(jnp.int32, (SB, 1, R), 2) >> 3
    seq_id = lax.broadcasted_iota(jnp.int32, (SB, 1, R), 0)
    own_cols = seq_of_col == seq_id
    mrep = mrep_ref[...]
    tok3 = lax.broadcasted_iota(jnp.int32, (SB, T, 1), 1)
    mo_ref[...] = jnp.zeros_like(mo_ref)

    def per_seq_value(col):
        return jnp.max(col.reshape(SB, T, 1), axis=1, keepdims=True)

    def last_of_seq(col):
        c3 = jnp.where(tok3 == T - 1, col.reshape(SB, T, 1), -jnp.inf)
        return jnp.broadcast_to(jnp.max(c3, axis=1, keepdims=True), (SB, T, 1)).reshape(R, 1)

    for h in range(ML_HEADS):
        pair, half = divmod(h, 2)
        hmask = half_lo if half == 0 else jnp.logical_not(half_lo)
        q_pair = z_ref[:, O_MQ + pair * 128:O_MQ + (pair + 1) * 128]
        k_pair = z_ref[:, O_MK + pair * 128:O_MK + (pair + 1) * 128] * (ML_DQK ** -0.5)
        v_f = z_ref[:, O_MV + h * ML_DV:O_MV + (h + 1) * ML_DV]
        v_h = v_f.astype(BF16)
        q_h = jnp.where(hmask, q_pair, 0.0)
        q_hb = q_h.astype(BF16)
        k_pb = k_pair.astype(BF16)

        bt_c = csum[:, LANE_F + h:LANE_F + h + 1]
        it_c = gates[:, LANE_I + h:LANE_I + h + 1]
        bt_r = csum_t[LANE_F + h:LANE_F + h + 1, :]
        it_r = gates_t[LANE_I + h:LANE_I + h + 1, :]
        m_prev = mrep[:, h:h + 1]

        dlog = jnp.where(causal, bt_c - bt_r + it_r, -jnp.inf)
        inter = bt_c + m_prev
        m_t = jnp.maximum(inter, jnp.max(dlog, axis=-1, keepdims=True))
        w_intra = jnp.exp(dlog - m_t)
        w_state = jnp.exp(inter - m_t)
        a = w_intra * _dot_nt(q_hb, k_pb)

        qt_h = q_pair.T[half * ML_DQK:(half + 1) * ML_DQK, :].astype(BF16)
        c_all = c_ref[:, h]
        c_st = c_all.reshape(SB * ML_DV, ML_DQK).astype(BF16)
        rd = _dot(c_st, qt_h).reshape(SB, ML_DV, R)
        num_state = jnp.sum(jnp.where(own_cols, rd, 0.0), axis=0).T

        num = _dot(a.astype(BF16), v_h) + w_state * num_state
        n_pair = nrep_ref[:, pair * 128:(pair + 1) * 128]
        den = jnp.sum(a, axis=-1, keepdims=True) + w_state * jnp.sum(q_h * n_pair, axis=-1, keepdims=True)
        denom = jnp.maximum(jnp.abs(den), jnp.exp(-m_t))
        hid = num * (1.0 / denom)
        o_gate = _sigmoid(z_ref[:, O_MO + h * ML_DV:O_MO + (h + 1) * ML_DV])
        gate = _silu(z_ref[:, O_MG + h * ML_DV:O_MG + (h + 1) * ML_DV])
        col = SWA_W + h * ML_DV
        y_ref[:, col:col + ML_DV] = ((hid * o_gate) * gate).astype(BF16)

        m_new = last_of_seq(m_t)
        bt_last = last_of_seq(bt_c)
        w_s = jnp.exp(bt_last - bt_c + it_c - m_new)
        decay = jnp.exp(bt_last + m_prev - m_new)
        decay_seq = per_seq_value(decay)

        vw_t = (v_f * w_s).T
        lhs = jnp.where(own_cols, vw_t[None, :, :], 0.0).reshape(SB * ML_DV, R).astype(BF16)
        k_h = k_pair[:, half * ML_DQK:(half + 1) * ML_DQK].astype(BF16)
        upd = _dot(lhs, k_h).reshape(SB, ML_DV, ML_DQK)
        co_ref[:, h] = decay_seq * c_all + upd

        kw = (k_pair * w_s).reshape(SB, T, 128)
        ksum = jnp.sum(kw, axis=1)
        n_old = nst_ref[:, h * ML_DQK:(h + 1) * ML_DQK]
        dec2 = decay_seq.reshape(SB, 1)
        no_ref[:, h * ML_DQK:(h + 1) * ML_DQK] = dec2 * n_old + ksum[:, half * ML_DQK:(half + 1) * ML_DQK]
        mo_ref[:, h:h + 1] = per_seq_value(m_new).reshape(SB, 1)


def _mix_sample(sinks, z, zg, cos, sin, ck, cv, cst, nst, nrep, mrep):
    nb = ck.shape[0]
    steps = nb // SB
    const = lambda i, s: (0, 0)
    return pl.pallas_call(
        _mix_sample_kernel,
        grid_spec=pltpu.PrefetchScalarGridSpec(
            num_scalar_prefetch=1,
            grid=(steps,),
            in_specs=[
                pl.BlockSpec((SR, D_MAIN), lambda i, s: (i, 0)),
                pl.BlockSpec((SR, GATE_LANES), lambda i, s: (i, 0)),
                pl.BlockSpec((SR, HEAD_DIM), const),
                pl.BlockSpec((SR, HEAD_DIM), const),
                pl.BlockSpec((SB, WINDOW, SWA_KV_W), lambda i, s: (i, 0, 0)),
                pl.BlockSpec((SB, WINDOW, SWA_KV_W), lambda i, s: (i, 0, 0)),
                pl.BlockSpec((SB, ML_HEADS, ML_DV, ML_DQK), lambda i, s: (i, 0, 0, 0)),
                pl.BlockSpec((SB, ML_QK_W), lambda i, s: (i, 0)),
                pl.BlockSpec((SR, ML_QK_W), lambda i, s: (i, 0)),
                pl.BlockSpec((SR, GATE_LANES), lambda i, s: (i, 0)),
            ],
            out_specs=[
                pl.BlockSpec((SR, SWA_W + ML_W), lambda i, s: (i, 0)),
                pl.BlockSpec((SB, WINDOW, SWA_KV_W), lambda i, s: (i, 0, 0)),
                pl.BlockSpec((SB, WINDOW, SWA_KV_W), lambda i, s: (i, 0, 0)),
                pl.BlockSpec((SB, ML_HEADS, ML_DV, ML_DQK), lambda i, s: (i, 0, 0, 0)),
                pl.BlockSpec((SB, ML_QK_W), lambda i, s: (i, 0)),
                pl.BlockSpec((SB, GATE_LANES), lambda i, s: (i, 0)),
            ],
            scratch_shapes=[
                pltpu.VMEM((SR, SWA_W), F32),
                pltpu.VMEM((SR, SWA_KV_W), F32),
                pltpu.VMEM((SR, SWA_W), F32),
            ],
        ),
        out_shape=[
            jax.ShapeDtypeStruct((nb * DEC_SEQ, SWA_W + ML_W), BF16),
            jax.ShapeDtypeStruct((nb, WINDOW, SWA_KV_W), F32),
            jax.ShapeDtypeStruct((nb, WINDOW, SWA_KV_W), F32),
            jax.ShapeDtypeStruct((nb, ML_HEADS, ML_DV, ML_DQK), F32),
            jax.ShapeDtypeStruct((nb, ML_QK_W), F32),
            jax.ShapeDtypeStruct((nb, GATE_LANES), F32),
        ],
        compiler_params=pltpu.CompilerParams(
            dimension_semantics=("arbitrary",), vmem_limit_bytes=VMEM_LIMIT),
        name="mix_sample",
    )(sinks, z, zg, cos, sin, ck, cv, cst, nst, nrep, mrep)


MB = 8
MR = MB * DEC_SEQ


def _mem_sample_kernel(cq_ref, cg_ref, mk_ref, mv_ref, y_ref, o_s):
    T = DEC_SEQ
    scale = HEAD_DIM ** -0.5
    zpad = jnp.zeros((T, HEAD_DIM), F32)

    def per_seq(b, carry):
        rows = pl.ds(pl.multiple_of(b * T, T), T)
        kb = mk_ref[b].astype(BF16)
        vb = mv_ref[b].astype(BF16)
        for h in range(MEM_HEADS):
            cols = slice(h * HEAD_DIM, (h + 1) * HEAD_DIM)
            q = jnp.concatenate([cq_ref[rows, cols], zpad], axis=0).astype(BF16)
            s = _dot_nt(q, kb[:, cols]) * scale
            m = jnp.max(s, axis=-1, keepdims=True)
            p = jnp.exp(s - m)
            l = jnp.sum(p, axis=-1, keepdims=True)
            o = _dot(p.astype(BF16), vb[:, cols]) * (1.0 / l)
            o_s[rows, cols] = o[0:T]
        return carry

    lax.fori_loop(0, MB, per_seq, 0)
    y_ref[...] = (o_s[...] * _silu(cg_ref[...])).astype(BF16)


def _mem_sample(z, mk, mv):
    nb = mk.shape[0]
    return pl.pallas_call(
        _mem_sample_kernel,
        grid=(nb // MB,),
        in_specs=[
            pl.BlockSpec((MR, MEM_W), lambda i: (i, O_CQ // MEM_W)),
            pl.BlockSpec((MR, MEM_W), lambda i: (i, O_CG // MEM_W)),
            pl.BlockSpec((MB, MEM_LEN, MEM_W), lambda i: (i, 0, 0)),
            pl.BlockSpec((MB, MEM_LEN, MEM_W), lambda i: (i, 0, 0)),
        ],
        out_specs=pl.BlockSpec((MR, MEM_W), lambda i: (i, 0)),
        out_shape=jax.ShapeDtypeStruct((nb * DEC_SEQ, MEM_W), BF16),
        scratch_shapes=[pltpu.VMEM((MR, MEM_W), F32)],
        compiler_params=pltpu.CompilerParams(
            dimension_semantics=("arbitrary",), vmem_limit_bytes=VMEM_LIMIT),
        name="mem_sample",
    )(z, z, mk, mv)


def _rope_tables(pos):
    half = HEAD_DIM // 2
    inv = jnp.power(ROPE_THETA, -(jnp.arange(half, dtype=F32) * 2.0 / HEAD_DIM))
    ang = pos.astype(F32)[:, None] * inv[None, :]
    cos = jnp.cos(ang)
    sin = jnp.sin(ang)
    return jnp.concatenate([cos, cos], axis=-1), jnp.concatenate([-sin, sin], axis=-1)


def _relayout_in_proj(w_in, b_in):
    def main(a):
        return jnp.concatenate([a[..., _R_SQ:_R_MI], a[..., _R_MO:_R_END]], axis=-1)

    def gates(a):
        pad = jnp.zeros(a.shape[:-1] + (GATE_LANES - 2 * ML_HEADS,), a.dtype)
        return jnp.concatenate([a[..., _R_MI:_R_MO], pad], axis=-1)

    return (main(w_in).astype(BF16), main(b_in)[None, :],
            gates(w_in).astype(BF16), gates(b_in)[None, :])


def _layer(xp, xs, mem, ck, cv, c_st, n_st, m_st, cmk, cmv,
           g_norm, w_in, b_in, sinks, g_mem, w_mem_kv, w_out, g_final):
    bp, sp, _ = xp.shape
    bs, ts, _ = xs.shape
    xp2 = xp.reshape(bp * sp, D_MODEL)
    xs2 = xs.reshape(bs * ts, D_MODEL)
    w_main, b_main, w_gate, b_gate = _relayout_in_proj(w_in, b_in)
    g_norm2 = g_norm[None, :]
    w_out_b = w_out.astype(BF16)
    g_final2 = g_final[None, :]
    sinks_flat = sinks.reshape(SWA_HEADS)

    zp, zgp = _proj(xp2, g_norm2, w_main, b_main, w_gate, b_gate)
    memkv = _memkv(mem.reshape(MEM_LEN, D_MODEL), g_mem[None, :], w_mem_kv.astype(BF16))
    mem_k = memkv[:, :MEM_W]
    mem_v = memkv[:, MEM_W:]
    cos_p, sin_p = _rope_tables(jnp.arange(sp, dtype=jnp.int32))
    yp, pk, pv, ct, n_p, m_p = _mix_prompt(sinks_flat, zp, zgp, cos_p, sin_p,
                                            mem_k.astype(BF16), mem_v.astype(BF16))
    out_p = _outproj(yp, xp2, w_out_b, g_final2)

    zs, zgs = _proj(xs2, g_norm2, w_main, b_main, w_gate, b_gate)
    cos_s, sin_s = _rope_tables(PAST_LEN + jnp.arange(ts, dtype=jnp.int32))
    cos_s = jnp.tile(cos_s, (SB, 1))
    sin_s = jnp.tile(sin_s, (SB, 1))
    nrep = jnp.repeat(n_st.reshape(bs, ML_QK_W), ts, axis=0)
    m_pad = jnp.pad(m_st, ((0, 0), (0, GATE_LANES - ML_HEADS)))
    mrep = jnp.repeat(m_pad, ts, axis=0)
    ya, sk_o, sv_o, c_o, n_o, m_o = _mix_sample(
        sinks_flat, zs, zgs, cos_s, sin_s,
        ck.reshape(bs, WINDOW, SWA_KV_W), cv.reshape(bs, WINDOW, SWA_KV_W),
        c_st, n_st.reshape(bs, ML_QK_W), nrep, mrep)
    yb = _mem_sample(zs, cmk.reshape(bs, MEM_LEN, MEM_W), cmv.reshape(bs, MEM_LEN, MEM_W))
    out_s = _outproj(jnp.concatenate([ya, yb], axis=-1), xs2, w_out_b, g_final2)

    p_state = (
        pk.reshape(bp, WINDOW, SWA_KV_HEADS, HEAD_DIM),
        pv.reshape(bp, WINDOW, SWA_KV_HEADS, HEAD_DIM),
        ct.reshape(ML_HEADS, ML_DQK, ML_DV).transpose(0, 2, 1)[None],
        n_p.reshape(bp, ML_HEADS, ML_DQK),
        m_p[:, :ML_HEADS],
        mem_k.reshape(bp, MEM_LEN, MEM_HEADS, HEAD_DIM),
        mem_v.reshape(bp, MEM_LEN, MEM_HEADS, HEAD_DIM),
    )
    s_state = (
        sk_o.reshape(bs, WINDOW, SWA_KV_HEADS, HEAD_DIM),
        sv_o.reshape(bs, WINDOW, SWA_KV_HEADS, HEAD_DIM),
        c_o,
        n_o.reshape(bs, ML_HEADS, ML_DQK),
        m_o[:, :ML_HEADS],
    )
    return out_p.reshape(bp, sp, D_MODEL), out_s.reshape(bs, ts, D_MODEL), p_state, s_state


def kernel(x_prompt, x_sample, mem_prompt, cache_swa_k, cache_swa_v, state_mlstm_C, state_mlstm_n,
           state_mlstm_m, cache_mem_k, cache_mem_v, g_norm, w_in, b_in, swa_sinks, g_mem, w_mem_kv,
           w_out, g_final):
    depth = g_norm.shape[0]
    assert depth == 1 and x_prompt.shape[0] == 1
    y_p, y_s, p_state, s_state = _layer(
        x_prompt, x_sample, mem_prompt, cache_swa_k[0], cache_swa_v[0], state_mlstm_C[0],
        state_mlstm_n[0], state_mlstm_m[0], cache_mem_k[0], cache_mem_v[0],
        g_norm[0], w_in[0], b_in[0], swa_sinks[0], g_mem[0], w_mem_kv[0], w_out[0], g_final)
    return (y_p, y_s) + tuple(s[None] for s in p_state) + tuple(s[None] for s in s_state)
```

```python
import functools
import math

import jax
import jax.numpy as jnp
from jax import lax
from jax.experimental import pallas as pl
from jax.experimental.pallas import tpu as pltpu

F32 = jnp.float32
BF16 = jnp.bfloat16

D_MODEL = 2048
HEAD_DIM = 128
SWA_HEADS = 8
SWA_KV_HEADS = 2
GQA_GROUP = 4
WINDOW = 128
ML_HEADS = 4
ML_DQK = 64
ML_DV = 128
MEM_HEADS = 4
MEM_LEN = 256
CHUNK = 128
ROPE_THETA = 10000.0
EPS = 1e-6
PAST_LEN = 16384
DEC_SEQ = 8

SWA_W = SWA_HEADS * HEAD_DIM
SWA_KV_W = SWA_KV_HEADS * HEAD_DIM
ML_W = ML_HEADS * ML_DV
ML_QK_W = ML_HEADS * ML_DQK
MEM_W = MEM_HEADS * HEAD_DIM
D_MIX = SWA_W + ML_W + MEM_W

_IN_WIDTHS = (SWA_W, SWA_KV_W, SWA_KV_W, SWA_W, ML_QK_W, ML_QK_W, ML_W, ML_HEADS, ML_HEADS, ML_W, ML_W, MEM_W, MEM_W)
_IN_OFFS = [0]
for _w in _IN_WIDTHS:
    _IN_OFFS.append(_IN_OFFS[-1] + _w)
(_R_SQ, _R_SK, _R_SV, _R_SG, _R_MQ, _R_MK, _R_MV, _R_MI, _R_MF, _R_MO, _R_MG, _R_CQ, _R_CG, _R_END) = _IN_OFFS

O_SQ = 0
O_SK = O_SQ + SWA_W
O_SV = O_SK + SWA_KV_W
O_SG = O_SV + SWA_KV_W
O_MQ = O_SG + SWA_W
O_MK = O_MQ + ML_QK_W
O_MV = O_MK + ML_QK_W
O_MO = O_MV + ML_W
O_MG = O_MO + ML_W
O_CQ = O_MG + ML_W
O_CG = O_CQ + MEM_W
D_MAIN = O_CG + MEM_W
GATE_LANES = 128
LANE_I = 0
LANE_F = ML_HEADS

VMEM_LIMIT = 56 * 1024 * 1024

_NT = (((1,), (1,)), ((), ()))


def _dot(a, b):
    return jnp.dot(a, b, preferred_element_type=F32)


def _dot_nt(a, b):
    return lax.dot_general(a, b, _NT, preferred_element_type=F32)


def _exact_sel_dot(sel_bf16, x):
    hi = x.astype(BF16)
    r1 = x - hi.astype(F32)
    mid = r1.astype(BF16)
    lo = (r1 - mid.astype(F32)).astype(BF16)
    return _dot(sel_bf16, hi) + _dot(sel_bf16, mid) + _dot(sel_bf16, lo)


def _silu(x):
    return x * (1.0 / (1.0 + jnp.exp(-x)))


def _sigmoid(x):
    return 1.0 / (1.0 + jnp.exp(-x))


def _log_sigmoid(x):
    return jnp.minimum(x, 0.0) - jnp.log1p(jnp.exp(-jnp.abs(x)))


PROJ_TM = 1024
PROJ_TN = 512
NORM_ROWS = 256


def _proj_kernel(x_ref, g_ref, w_ref, b_ref, wg_ref, bg_ref, z_ref, zg_ref, u_ref):
    j = pl.program_id(1)

    @pl.when(j == 0)
    def _():
        g = g_ref[...]
        for r in range(PROJ_TM // NORM_ROWS):
            rows = pl.ds(r * NORM_ROWS, NORM_ROWS)
            xf = x_ref[rows, :]
            ms = jnp.mean(xf * xf, axis=-1, keepdims=True)
            u_ref[rows, :] = ((xf * lax.rsqrt(ms + EPS)) * g).astype(BF16)
        zg_ref[...] = _dot(u_ref[...], wg_ref[...]) + bg_ref[...]

    z_ref[...] = _dot(u_ref[...], w_ref[...]) + b_ref[...]


def _proj(x2d, g, w, b, wg, bg):
    n = x2d.shape[0]
    grid = (n // PROJ_TM, D_MAIN // PROJ_TN)
    return pl.pallas_call(
        _proj_kernel,
        grid=grid,
        in_specs=[
            pl.BlockSpec((PROJ_TM, D_MODEL), lambda i, j: (i, 0)),
            pl.BlockSpec((1, D_MODEL), lambda i, j: (0, 0)),
            pl.BlockSpec((D_MODEL, PROJ_TN), lambda i, j: (0, j)),
            pl.BlockSpec((1, PROJ_TN), lambda i, j: (0, j)),
            pl.BlockSpec((D_MODEL, GATE_LANES), lambda i, j: (0, 0)),
            pl.BlockSpec((1, GATE_LANES), lambda i, j: (0, 0)),
        ],
        out_specs=[
            pl.BlockSpec((PROJ_TM, PROJ_TN), lambda i, j: (i, j)),
            pl.BlockSpec((PROJ_TM, GATE_LANES), lambda i, j: (i, 0)),
        ],
        out_shape=[
            jax.ShapeDtypeStruct((n, D_MAIN), F32),
            jax.ShapeDtypeStruct((n, GATE_LANES), F32),
        ],
        scratch_shapes=[pltpu.VMEM((PROJ_TM, D_MODEL), BF16)],
        compiler_params=pltpu.CompilerParams(
            dimension_semantics=("arbitrary", "arbitrary"), vmem_limit_bytes=VMEM_LIMIT),
        name="proj",
    )(x2d, g, w, b, wg, bg)


MEMKV_TN = 256


def _memkv_kernel(mem_ref, g_ref, w_ref, o_ref):
    xf = mem_ref[...]
    ms = jnp.mean(xf * xf, axis=-1, keepdims=True)
    u = ((xf * lax.rsqrt(ms + EPS)) * g_ref[...]).astype(BF16)
    o_ref[...] = _dot(u, w_ref[...])


def _memkv(mem2d, g, w):
    return pl.pallas_call(
        _memkv_kernel,
        grid=(2 * MEM_W // MEMKV_TN,),
        in_specs=[
            pl.BlockSpec((MEM_LEN, D_MODEL), lambda j: (0, 0)),
            pl.BlockSpec((1, D_MODEL), lambda j: (0, 0)),
            pl.BlockSpec((D_MODEL, MEMKV_TN), lambda j: (0, j)),
        ],
        out_specs=pl.BlockSpec((MEM_LEN, MEMKV_TN), lambda j: (0, j)),
        out_shape=jax.ShapeDtypeStruct((MEM_LEN, 2 * MEM_W), F32),
        compiler_params=pltpu.CompilerParams(
            dimension_semantics=("arbitrary",), vmem_limit_bytes=VMEM_LIMIT),
        name="memkv",
    )(mem2d, g, w)


def _rope(x, cos, sin_signed):
    return x * cos + pltpu.roll(x, HEAD_DIM // 2, axis=1) * sin_signed


def _mix_prompt_kernel(sinks_ref, z_ref, zg_ref, rope_c_ref, rope_i_ref, mk_ref, mv_ref,
                       y_ref, ko_ref, vo_ref, ct_ref, n_ref, m_ref,
                       kprev_ref, vprev_ref):
    c = pl.program_id(0)
    L = CHUNK
    scale = HEAD_DIM ** -0.5

    @pl.when(c == 0)
    def _():
        kprev_ref[...] = jnp.zeros_like(kprev_ref)
        vprev_ref[...] = jnp.zeros_like(vprev_ref)
        ct_ref[...] = jnp.zeros_like(ct_ref)
        n_ref[...] = jnp.zeros_like(n_ref)
        m_ref[...] = jnp.zeros_like(m_ref)

    cc = rope_c_ref[0, pl.ds(c, 1), :]
    sc = rope_c_ref[1, pl.ds(c, 1), :]
    cos = cc * rope_i_ref[0] - sc * rope_i_ref[1]
    sin = sc * rope_i_ref[2] + cc * rope_i_ref[3]

    ri = lax.broadcasted_iota(jnp.int32, (GQA_GROUP * L, 2 * L), 0) & (L - 1)
    cj = lax.broadcasted_iota(jnp.int32, (GQA_GROUP * L, 2 * L), 1)
    j_low = jnp.where(c > 0, 0, L)
    band = (cj > ri) & (cj <= ri + L) & (cj >= j_low)
    for kv in range(SWA_KV_HEADS):
        k_new = _rope(z_ref[:, O_SK + kv * HEAD_DIM:O_SK + (kv + 1) * HEAD_DIM], cos, sin)
        v_new = z_ref[:, O_SV + kv * HEAD_DIM:O_SV + (kv + 1) * HEAD_DIM]
        ko_ref[:, kv * HEAD_DIM:(kv + 1) * HEAD_DIM] = k_new
        vo_ref[:, kv * HEAD_DIM:(kv + 1) * HEAD_DIM] = v_new
        k_new_b = k_new.astype(BF16)
        v_new_b = v_new.astype(BF16)
        kcat = jnp.concatenate([kprev_ref[:, kv * HEAD_DIM:(kv + 1) * HEAD_DIM], k_new_b], axis=0)
        vcat = jnp.concatenate([vprev_ref[:, kv * HEAD_DIM:(kv + 1) * HEAD_DIM], v_new_b], axis=0)
        kprev_ref[:, kv * HEAD_DIM:(kv + 1) * HEAD_DIM] = k_new_b
        vprev_ref[:, kv * HEAD_DIM:(kv + 1) * HEAD_DIM] = v_new_b
        qs = []
        sks = []
        for g in range(GQA_GROUP):
            h = kv * GQA_GROUP + g
            qs.append(_rope(z_ref[:, O_SQ + h * HEAD_DIM:O_SQ + (h + 1) * HEAD_DIM], cos, sin).astype(BF16))
            sks.append(jnp.full((L, 1), sinks_ref[h], F32))
        q_st = jnp.concatenate(qs, axis=0)
        sk = jnp.concatenate(sks, axis=0)
        s = _dot_nt(q_st, kcat) * scale
        s = jnp.where(band, s, -jnp.inf)
        m = jnp.maximum(jnp.max(s, axis=-1, keepdims=True), sk)
        p = jnp.exp(s - m)
        l = jnp.sum(p, axis=-1, keepdims=True) + jnp.exp(sk - m)
        o = _dot(p.astype(BF16), vcat) * (1.0 / l)
        for g in range(GQA_GROUP):
            h = kv * GQA_GROUP + g
            gate = _silu(z_ref[:, O_SG + h * HEAD_DIM:O_SG + (h + 1) * HEAD_DIM])
            y_ref[:, h * HEAD_DIM:(h + 1) * HEAD_DIM] = (o[g * L:(g + 1) * L] * gate).astype(BF16)

    for h in range(MEM_HEADS):
        q = z_ref[:, O_CQ + h * HEAD_DIM:O_CQ + (h + 1) * HEAD_DIM].astype(BF16)
        s = _dot_nt(q, mk_ref[:, h * HEAD_DIM:(h + 1) * HEAD_DIM]) * scale
        m = jnp.max(s, axis=-1, keepdims=True)
        p = jnp.exp(s - m)
        l = jnp.sum(p, axis=-1, keepdims=True)
        o = _dot(p.astype(BF16), mv_ref[:, h * HEAD_DIM:(h + 1) * HEAD_DIM]) * (1.0 / l)
        gate = _silu(z_ref[:, O_CG + h * HEAD_DIM:O_CG + (h + 1) * HEAD_DIM])
        col = SWA_W + ML_W + h * HEAD_DIM
        y_ref[:, col:col + HEAD_DIM] = (o * gate).astype(BF16)

    lane = lax.broadcasted_iota(jnp.int32, (1, GATE_LANES), 1)
    zg = zg_ref[...]
    gates = jnp.where(lane < LANE_F, zg, _log_sigmoid(zg))
    gates = jnp.where(lane < 2 * ML_HEADS, gates, 0.0)
    rr = lax.broadcasted_iota(jnp.int32, (L, L), 0)
    cc = lax.broadcasted_iota(jnp.int32, (L, L), 1)
    causal = rr >= cc
    tri = jnp.where(causal, 1.0, 0.0).astype(BF16)
    csum = _exact_sel_dot(tri, gates)
    gates_t = gates.T
    csum_t = csum.T
    half_lo = lane < ML_DQK
    for h in range(ML_HEADS):
        pair, half = divmod(h, 2)
        hmask = half_lo if half == 0 else jnp.logical_not(half_lo)
        q_pair = z_ref[:, O_MQ + pair * 128:O_MQ + (pair + 1) * 128]
        k_pair = z_ref[:, O_MK + pair * 128:O_MK + (pair + 1) * 128] * (ML_DQK ** -0.5)
        v_h = z_ref[:, O_MV + h * ML_DV:O_MV + (h + 1) * ML_DV].astype(BF16)
        q_h = jnp.where(hmask, q_pair, 0.0)
        q_hb = q_h.astype(BF16)
        k_pb = k_pair.astype(BF16)

        bt_c = csum[:, LANE_F + h:LANE_F + h + 1]
        it_c = gates[:, LANE_I + h:LANE_I + h + 1]
        bt_r = csum_t[LANE_F + h:LANE_F + h + 1, :]
        it_r = gates_t[LANE_I + h:LANE_I + h + 1, :]
        m_prev = m_ref[:, h:h + 1]

        dlog = jnp.where(causal, bt_c - bt_r + it_r, -jnp.inf)
        inter = bt_c + m_prev
        m_t = jnp.maximum(inter, jnp.max(dlog, axis=-1, keepdims=True))
        w_intra = jnp.exp(dlog - m_t)
        w_state = jnp.exp(inter - m_t)
        a = w_intra * _dot_nt(q_hb, k_pb)
        ct_pair = ct_ref[pair * 128:(pair + 1) * 128, :]
        num = _dot(a.astype(BF16), v_h) + w_state * _dot(q_hb, ct_pair.astype(BF16))
        n_pair = n_ref[:, pair * 128:(pair + 1) * 128]
        den = jnp.sum(a, axis=-1, keepdims=True) + w_state * jnp.sum(q_h * n_pair, axis=-1, keepdims=True)
        denom = jnp.maximum(jnp.abs(den), jnp.exp(-m_t))
        hid = num * (1.0 / denom)
        o_gate = _sigmoid(z_ref[:, O_MO + h * ML_DV:O_MO + (h + 1) * ML_DV])
        gate = _silu(z_ref[:, O_MG + h * ML_DV:O_MG + (h + 1) * ML_DV])
        col = SWA_W + h * ML_DV
        y_ref[:, col:col + ML_DV] = ((hid * o_gate) * gate).astype(BF16)

        m_new = m_t[L - 1:L, :]
        bt_last = bt_c[L - 1:L, :]
        w_s = jnp.exp(bt_last - bt_c + it_c - m_new)
        decay = jnp.exp(bt_last + m_prev - m_new)
        kw = k_pair * w_s
        upd = _dot(kw.T.astype(BF16), v_h)
        rows = slice(h * ML_DQK, (h + 1) * ML_DQK)
        ct_ref[rows, :] = decay * ct_ref[rows, :] + upd[half * ML_DQK:(half + 1) * ML_DQK, :]
        ksum = jnp.sum(kw, axis=0, keepdims=True)
        n_ref[:, rows] = decay * n_ref[:, rows] + ksum[:, half * ML_DQK:(half + 1) * ML_DQK]
        m_ref[:, h:h + 1] = m_new


def _mix_prompt(sinks, z, zg, rope_c, rope_i, mk, mv):
    n = z.shape[0]
    nc = n // CHUNK
    const = lambda c, s: (0, 0)
    return pl.pallas_call(
        _mix_prompt_kernel,
        grid_spec=pltpu.PrefetchScalarGridSpec(
            num_scalar_prefetch=1,
            grid=(nc,),
            in_specs=[
                pl.BlockSpec((CHUNK, D_MAIN), lambda c, s: (c, 0)),
                pl.BlockSpec((CHUNK, GATE_LANES), lambda c, s: (c, 0)),
                pl.BlockSpec((2, nc, HEAD_DIM), lambda c, s: (0, 0, 0)),
                pl.BlockSpec((4, CHUNK, HEAD_DIM), lambda c, s: (0, 0, 0)),
                pl.BlockSpec((MEM_LEN, MEM_W), const),
                pl.BlockSpec((MEM_LEN, MEM_W), const),
            ],
            out_specs=[
                pl.BlockSpec((CHUNK, D_MIX), lambda c, s: (c, 0)),
                pl.BlockSpec((CHUNK, SWA_KV_W), const),
                pl.BlockSpec((CHUNK, SWA_KV_W), const),
                pl.BlockSpec((ML_QK_W, ML_DV), const),
                pl.BlockSpec((1, ML_QK_W), const),
                pl.BlockSpec((1, GATE_LANES), const),
            ],
            scratch_shapes=[
                pltpu.VMEM((CHUNK, SWA_KV_W), BF16),
                pltpu.VMEM((CHUNK, SWA_KV_W), BF16),
            ],
        ),
        out_shape=[
            jax.ShapeDtypeStruct((n, D_MIX), BF16),
            jax.ShapeDtypeStruct((CHUNK, SWA_KV_W), F32),
            jax.ShapeDtypeStruct((CHUNK, SWA_KV_W), F32),
            jax.ShapeDtypeStruct((ML_QK_W, ML_DV), F32),
            jax.ShapeDtypeStruct((1, ML_QK_W), F32),
            jax.ShapeDtypeStruct((1, GATE_LANES), F32),
        ],
        compiler_params=pltpu.CompilerParams(
            dimension_semantics=("arbitrary",), vmem_limit_bytes=VMEM_LIMIT),
        name="mix_prompt",
    )(sinks, z, zg, rope_c, rope_i, mk, mv)


OUT_TM = 512


def _outproj_kernel(y_ref, x_ref, w_ref, g_ref, o_ref):
    acc = _dot(y_ref[...], w_ref[...]) + x_ref[...]
    ms = jnp.mean(acc * acc, axis=-1, keepdims=True)
    o_ref[...] = (acc * lax.rsqrt(ms + EPS)) * g_ref[...]


def _outproj(y, x2d, w, g):
    n = x2d.shape[0]
    return pl.pallas_call(
        _outproj_kernel,
        grid=(n // OUT_TM,),
        in_specs=[
            pl.BlockSpec((OUT_TM, D_MIX), lambda i: (i, 0)),
            pl.BlockSpec((OUT_TM, D_MODEL), lambda i: (i, 0)),
            pl.BlockSpec((D_MIX, D_MODEL), lambda i: (0, 0)),
            pl.BlockSpec((1, D_MODEL), lambda i: (0, 0)),
        ],
        out_specs=pl.BlockSpec((OUT_TM, D_MODEL), lambda i: (i, 0)),
        out_shape=jax.ShapeDtypeStruct((n, D_MODEL), F32),
        compiler_params=pltpu.CompilerParams(
            dimension_semantics=("arbitrary",), vmem_limit_bytes=VMEM_LIMIT),
        name="outproj",
    )(y, x2d, w, g)


SB = 16
SR = SB * DEC_SEQ


def _mix_sample_kernel(sinks_ref, z_ref, zg_ref, cos_ref, sin_ref, ck_ref, cv_ref,
                       c_ref, nst_ref, nrep_ref, mrep_ref,
                       y_ref, ko_ref, vo_ref, co_ref, no_ref, mo_ref,
                       q_s, k_s, o_s):
    T = DEC_SEQ
    scale = HEAD_DIM ** -0.5
    cos = cos_ref[...]
    sin = sin_ref[...]

    for h in range(SWA_HEADS):
        q_s[:, h * HEAD_DIM:(h + 1) * HEAD_DIM] = _rope(
            z_ref[:, O_SQ + h * HEAD_DIM:O_SQ + (h + 1) * HEAD_DIM], cos, sin)
    for kv in range(SWA_KV_HEADS):
        k_s[:, kv * HEAD_DIM:(kv + 1) * HEAD_DIM] = _rope(
            z_ref[:, O_SK + kv * HEAD_DIM:O_SK + (kv + 1) * HEAD_DIM], cos, sin)

    KP = 2 * WINDOW
    rt = lax.broadcasted_iota(jnp.int32, (GQA_GROUP * T, KP), 0) & (T - 1)
    cj = lax.broadcasted_iota(jnp.int32, (GQA_GROUP * T, KP), 1)
    mask = (cj > rt) & (cj <= rt + WINDOW)
    kpad = jnp.zeros((KP - WINDOW - T, HEAD_DIM), F32)
    NKV = SWA_KV_HEADS
    SEQ_ROWS = WINDOW * NKV

    def per_seq(b, carry):
        rows = pl.ds(pl.multiple_of(b * T, T), T)
        base = pl.multiple_of(b * SEQ_ROWS, SEQ_ROWS)
        k_new = k_s[rows, :]
        v_new = z_ref[rows, O_SV:O_SV + SWA_KV_W]
        keep = SEQ_ROWS - T * NKV
        ko_ref[pl.ds(base, keep), :] = ck_ref[pl.ds(base + T * NKV, keep), :]
        vo_ref[pl.ds(base, keep), :] = cv_ref[pl.ds(base + T * NKV, keep), :]
        for kv in range(SWA_KV_HEADS):
            cols = slice(kv * HEAD_DIM, (kv + 1) * HEAD_DIM)
            ko_ref[pl.ds(base + keep + kv, T, stride=NKV), :] = k_new[:, cols]
            vo_ref[pl.ds(base + keep + kv, T, stride=NKV), :] = v_new[:, cols]
            kc = ck_ref[pl.ds(base + kv, WINDOW, stride=NKV), :]
            vc = cv_ref[pl.ds(base + kv, WINDOW, stride=NKV), :]
            k_all = jnp.concatenate([kc, k_new[:, cols], kpad], axis=0).astype(BF16)
            v_all = jnp.concatenate([vc, v_new[:, cols], kpad], axis=0).astype(BF16)
            qs = []
            sks = []
            for g in range(GQA_GROUP):
                h = kv * GQA_GROUP + g
                qs.append(q_s[rows, h * HEAD_DIM:(h + 1) * HEAD_DIM])
                sks.append(jnp.full((T, 1), sinks_ref[h], F32))
            q_st = jnp.concatenate(qs, axis=0).astype(BF16)
            sk = jnp.concatenate(sks, axis=0)
            s = jnp.where(mask, _dot_nt(q_st, k_all) * scale, -jnp.inf)
            m = jnp.maximum(jnp.max(s, axis=-1, keepdims=True), sk)
            p = jnp.exp(s - m)
            l = jnp.sum(p, axis=-1, keepdims=True) + jnp.exp(sk - m)
            o = _dot(p.astype(BF16), v_all) * (1.0 / l)
            for g in range(GQA_GROUP):
                h = kv * GQA_GROUP + g
                o_s[rows, h * HEAD_DIM:(h + 1) * HEAD_DIM] = o[g * T:(g + 1) * T]
        return carry

    lax.fori_loop(0, SB, per_seq, 0)

    for h in range(SWA_HEADS):
        cols = slice(h * HEAD_DIM, (h + 1) * HEAD_DIM)
        gate = _silu(z_ref[:, O_SG + h * HEAD_DIM:O_SG + (h + 1) * HEAD_DIM])
        y_ref[:, cols] = (o_s[:, cols] * gate).astype(BF16)

    R = SR
    lane = lax.broadcasted_iota(jnp.int32, (1, GATE_LANES), 1)
    zg = zg_ref[...]
    gates = jnp.where(lane < LANE_F, zg, _log_sigmoid(zg))
    gates = jnp.where(lane < 2 * ML_HEADS, gates, 0.0)
    rr = lax.broadcasted_iota(jnp.int32, (R, R), 0)
    cc = lax.broadcasted_iota(jnp.int32, (R, R), 1)
    same_seq = (rr >> 3) == (cc >> 3)
    causal = same_seq & (rr >= cc)
    tri = jnp.where(causal, 1.0, 0.0).astype(BF16)
    csum = _exact_sel_dot(tri, gates)
    gates_t = gates.T
    csum_t = csum.T
    half_lo = lane < ML_DQK
    seq_of_col = lax.broadcasted_iota(jnp.int32, (SB, 1, R), 2) >> 3
    seq_id = lax.broadcasted_iota(jnp.int32, (SB, 1, R), 0)
    own_cols = seq_of_col == seq_id
    own_blk = ((lax.broadcasted_iota(jnp.int32, (R, SB * 128), 0) >> 3)
               == (lax.broadcasted_iota(jnp.int32, (R, SB * 128), 1) >> 7))
    k_t = [(z_ref[:, O_MK + p * 128:O_MK + (p + 1) * 128] * (ML_DQK ** -0.5)).T for p in range(2)]
    mrep = mrep_ref[...]
    tok3 = lax.broadcasted_iota(jnp.int32, (SB, T, 1), 1)
    mo_ref[...] = jnp.zeros_like(mo_ref)

    def per_seq_value(col):
        return jnp.max(col.reshape(SB, T, 1), axis=1, keepdims=True)

    def last_of_seq(col):
        c3 = jnp.where(tok3 == T - 1, col.reshape(SB, T, 1), -jnp.inf)
        return jnp.broadcast_to(jnp.max(c3, axis=1, keepdims=True), (SB, T, 1)).reshape(R, 1)

    for h in range(ML_HEADS):
        pair, half = divmod(h, 2)
        hmask = half_lo if half == 0 else jnp.logical_not(half_lo)
        q_pair = z_ref[:, O_MQ + pair * 128:O_MQ + (pair + 1) * 128]
        k_pair = z_ref[:, O_MK + pair * 128:O_MK + (pair + 1) * 128] * (ML_DQK ** -0.5)
        v_f = z_ref[:, O_MV + h * ML_DV:O_MV + (h + 1) * ML_DV]
        v_h = v_f.astype(BF16)
        q_h = jnp.where(hmask, q_pair, 0.0)
        q_hb = q_h.astype(BF16)
        k_pb = k_pair.astype(BF16)

        bt_c = csum[:, LANE_F + h:LANE_F + h + 1]
        it_c = gates[:, LANE_I + h:LANE_I + h + 1]
        bt_r = csum_t[LANE_F + h:LANE_F + h + 1, :]
        it_r = gates_t[LANE_I + h:LANE_I + h + 1, :]
        m_prev = mrep[:, h:h + 1]

        dlog = jnp.where(causal, bt_c - bt_r + it_r, -jnp.inf)
        inter = bt_c + m_prev
        m_t = jnp.maximum(inter, jnp.max(dlog, axis=-1, keepdims=True))
        w_intra = jnp.exp(dlog - m_t)
        w_state = jnp.exp(inter - m_t)
        a = w_intra * _dot_nt(q_hb, k_pb)

        ct_st = c_ref[:, 2 * pair:2 * pair + 2].reshape(SB * 128, ML_DV).astype(BF16)
        q_blk = jnp.where(own_blk, jnp.tile(q_h, (1, SB)), 0.0).astype(BF16)
        num_state = _dot(q_blk, ct_st)

        num = _dot(a.astype(BF16), v_h) + w_state * num_state
        n_pair = nrep_ref[:, pair * 128:(pair + 1) * 128]
        den = jnp.sum(a, axis=-1, keepdims=True) + w_state * jnp.sum(q_h * n_pair, axis=-1, keepdims=True)
        denom = jnp.maximum(jnp.abs(den), jnp.exp(-m_t))
        hid = num * (1.0 / denom)
        o_gate = _sigmoid(z_ref[:, O_MO + h * ML_DV:O_MO + (h + 1) * ML_DV])
        gate = _silu(z_ref[:, O_MG + h * ML_DV:O_MG + (h + 1) * ML_DV])
        col = SWA_W + h * ML_DV
        y_ref[:, col:col + ML_DV] = ((hid * o_gate) * gate).astype(BF16)

        m_new = last_of_seq(m_t)
        bt_last = last_of_seq(bt_c)
        w_s = jnp.exp(bt_last - bt_c + it_c - m_new)
        decay = jnp.exp(bt_last + m_prev - m_new)
        decay_seq = per_seq_value(decay)

        kt_h = k_t[pair][half * ML_DQK:(half + 1) * ML_DQK, :]
        lhs = jnp.where(own_cols, kt_h[None, :, :], 0.0).reshape(SB * ML_DQK, R).astype(BF16)
        upd = _dot(lhs, (v_f * w_s).astype(BF16)).reshape(SB, ML_DQK, ML_DV)
        co_ref[:, h] = decay_seq * c_ref[:, h] + upd

        kw = (k_pair * w_s).reshape(SB, T, 128)
        ksum = jnp.sum(kw, axis=1)
        n_old = nst_ref[:, h * ML_DQK:(h + 1) * ML_DQK]
        dec2 = decay_seq.reshape(SB, 1)
        no_ref[:, h * ML_DQK:(h + 1) * ML_DQK] = dec2 * n_old + ksum[:, half * ML_DQK:(half + 1) * ML_DQK]
        mo_ref[:, h:h + 1] = per_seq_value(m_new).reshape(SB, 1)


def _mix_sample(sinks, z, zg, cos, sin, ck, cv, cst, nst, nrep, mrep):
    nb = cst.shape[0]
    steps = nb // SB
    const = lambda i, s: (0, 0)
    cache_rows = SB * WINDOW * SWA_KV_HEADS
    return pl.pallas_call(
        _mix_sample_kernel,
        grid_spec=pltpu.PrefetchScalarGridSpec(
            num_scalar_prefetch=1,
            grid=(steps,),
            in_specs=[
                pl.BlockSpec((SR, D_MAIN), lambda i, s: (i, 0)),
                pl.BlockSpec((SR, GATE_LANES), lambda i, s: (i, 0)),
                pl.BlockSpec((SR, HEAD_DIM), const),
                pl.BlockSpec((SR, HEAD_DIM), const),
                pl.BlockSpec((cache_rows, HEAD_DIM), lambda i, s: (i, 0)),
                pl.BlockSpec((cache_rows, HEAD_DIM), lambda i, s: (i, 0)),
                pl.BlockSpec((SB, ML_HEADS, ML_DQK, ML_DV), lambda i, s: (i, 0, 0, 0)),
                pl.BlockSpec((SB, ML_QK_W), lambda i, s: (i, 0)),
                pl.BlockSpec((SR, ML_QK_W), lambda i, s: (i, 0)),
                pl.BlockSpec((SR, GATE_LANES), lambda i, s: (i, 0)),
            ],
            out_specs=[
                pl.BlockSpec((SR, SWA_W + ML_W), lambda i, s: (i, 0)),
                pl.BlockSpec((cache_rows, HEAD_DIM), lambda i, s: (i, 0)),
                pl.BlockSpec((cache_rows, HEAD_DIM), lambda i, s: (i, 0)),
                pl.BlockSpec((SB, ML_HEADS, ML_DQK, ML_DV), lambda i, s: (i, 0, 0, 0)),
                pl.BlockSpec((SB, ML_QK_W), lambda i, s: (i, 0)),
                pl.BlockSpec((SB, GATE_LANES), lambda i, s: (i, 0)),
            ],
            scratch_shapes=[
                pltpu.VMEM((SR, SWA_W), F32),
                pltpu.VMEM((SR, SWA_KV_W), F32),
                pltpu.VMEM((SR, SWA_W), F32),
            ],
        ),
        out_shape=[
            jax.ShapeDtypeStruct((nb * DEC_SEQ, SWA_W + ML_W), BF16),
            jax.ShapeDtypeStruct(ck.shape, F32),
            jax.ShapeDtypeStruct(cv.shape, F32),
            jax.ShapeDtypeStruct((nb, ML_HEADS, ML_DQK, ML_DV), F32),
            jax.ShapeDtypeStruct((nb, ML_QK_W), F32),
            jax.ShapeDtypeStruct((nb, GATE_LANES), F32),
        ],
        compiler_params=pltpu.CompilerParams(
            dimension_semantics=("arbitrary",), vmem_limit_bytes=VMEM_LIMIT),
        name="mix_sample",
    )(sinks, z, zg, cos, sin, ck, cv, cst, nst, nrep, mrep)


MB = 8
MR = MB * DEC_SEQ


def _mem_sample_kernel(cq_ref, cg_ref, mk_ref, mv_ref, y_ref, o_s):
    T = DEC_SEQ
    scale = HEAD_DIM ** -0.5
    zpad = jnp.zeros((T, HEAD_DIM), F32)

    seq_rows = MEM_LEN * MEM_HEADS

    def per_seq(b, carry):
        rows = pl.ds(pl.multiple_of(b * T, T), T)
        base = pl.multiple_of(b * seq_rows, seq_rows)
        for h in range(MEM_HEADS):
            cols = slice(h * HEAD_DIM, (h + 1) * HEAD_DIM)
            kb = mk_ref[pl.ds(base + h, MEM_LEN, stride=MEM_HEADS), :].astype(BF16)
            vb = mv_ref[pl.ds(base + h, MEM_LEN, stride=MEM_HEADS), :].astype(BF16)
            q = jnp.concatenate([cq_ref[rows, cols], zpad], axis=0).astype(BF16)
            s = _dot_nt(q, kb) * scale
            m = jnp.max(s, axis=-1, keepdims=True)
            p = jnp.exp(s - m)
            l = jnp.sum(p, axis=-1, keepdims=True)
            o = _dot(p.astype(BF16), vb) * (1.0 / l)
            o_s[rows, cols] = o[0:T]
        return carry

    lax.fori_loop(0, MB, per_seq, 0)
    y_ref[...] = (o_s[...] * _silu(cg_ref[...])).astype(BF16)


def _mem_sample(z, mk, mv):
    nb = z.shape[0] // DEC_SEQ
    cache_rows = MB * MEM_LEN * MEM_HEADS
    return pl.pallas_call(
        _mem_sample_kernel,
        grid=(nb // MB,),
        in_specs=[
            pl.BlockSpec((MR, MEM_W), lambda i: (i, O_CQ // MEM_W)),
            pl.BlockSpec((MR, MEM_W), lambda i: (i, O_CG // MEM_W)),
            pl.BlockSpec((cache_rows, HEAD_DIM), lambda i: (i, 0)),
            pl.BlockSpec((cache_rows, HEAD_DIM), lambda i: (i, 0)),
        ],
        out_specs=pl.BlockSpec((MR, MEM_W), lambda i: (i, 0)),
        out_shape=jax.ShapeDtypeStruct((nb * DEC_SEQ, MEM_W), BF16),
        scratch_shapes=[pltpu.VMEM((MR, MEM_W), F32)],
        compiler_params=pltpu.CompilerParams(
            dimension_semantics=("arbitrary",), vmem_limit_bytes=VMEM_LIMIT),
        name="mem_sample",
    )(z, z, mk, mv)


def _rope_cos_sin(pos):
    half = HEAD_DIM // 2
    inv = jnp.power(ROPE_THETA, -(jnp.arange(half, dtype=F32) * 2.0 / HEAD_DIM))
    ang = pos.astype(F32)[:, None] * inv[None, :]
    return jnp.cos(ang), jnp.sin(ang)


def _rope_tables(pos):
    cos, sin = _rope_cos_sin(pos)
    return jnp.concatenate([cos, cos], axis=-1), jnp.concatenate([-sin, sin], axis=-1)


def _rope_split_tables(n_chunks):
    ca, sa = _rope_cos_sin(jnp.arange(n_chunks, dtype=jnp.int32) * CHUNK)
    cb, sb = _rope_cos_sin(jnp.arange(CHUNK, dtype=jnp.int32))
    dup = lambda t: jnp.concatenate([t, t], axis=-1)
    sgn = lambda t: jnp.concatenate([-t, t], axis=-1)
    return jnp.stack([dup(ca), dup(sa)]), jnp.stack([dup(cb), dup(sb), sgn(cb), sgn(sb)])


def _relayout_in_proj(w_in, b_in):
    def main(a):
        return jnp.concatenate([a[..., _R_SQ:_R_MI], a[..., _R_MO:_R_END]], axis=-1)

    def gates(a):
        pad = jnp.zeros(a.shape[:-1] + (GATE_LANES - 2 * ML_HEADS,), a.dtype)
        return jnp.concatenate([a[..., _R_MI:_R_MO], pad], axis=-1)

    return (main(w_in).astype(BF16), main(b_in)[None, :],
            gates(w_in).astype(BF16), gates(b_in)[None, :])


def _layer(xp, xs, mem, ck, cv, c_st, n_st, m_st, cmk, cmv,
           g_norm, w_in, b_in, sinks, g_mem, w_mem_kv, w_out, g_final):
    bp, sp, _ = xp.shape
    bs, ts, _ = xs.shape
    xp2 = xp.reshape(bp * sp, D_MODEL)
    xs2 = xs.reshape(bs * ts, D_MODEL)
    w_main, b_main, w_gate, b_gate = _relayout_in_proj(w_in, b_in)
    g_norm2 = g_norm[None, :]
    w_out_b = w_out.astype(BF16)
    g_final2 = g_final[None, :]
    sinks_flat = sinks.reshape(SWA_HEADS)

    zp, zgp = _proj(xp2, g_norm2, w_main, b_main, w_gate, b_gate)
    memkv = _memkv(mem.reshape(MEM_LEN, D_MODEL), g_mem[None, :], w_mem_kv.astype(BF16))
    mem_k = memkv[:, :MEM_W]
    mem_v = memkv[:, MEM_W:]
    rope_c, rope_i = _rope_split_tables(sp // CHUNK)
    yp, pk, pv, ct, n_p, m_p = _mix_prompt(sinks_flat, zp, zgp, rope_c, rope_i,
                                            mem_k.astype(BF16), mem_v.astype(BF16))
    out_p = _outproj(yp, xp2, w_out_b, g_final2)

    zs, zgs = _proj(xs2, g_norm2, w_main, b_main, w_gate, b_gate)
    cos_s, sin_s = _rope_tables(PAST_LEN + jnp.arange(ts, dtype=jnp.int32))
    cos_s = jnp.tile(cos_s, (SB, 1))
    sin_s = jnp.tile(sin_s, (SB, 1))
    nrep = jnp.repeat(n_st.reshape(bs, ML_QK_W), ts, axis=0)
    m_pad = jnp.pad(m_st, ((0, 0), (0, GATE_LANES - ML_HEADS)))
    mrep = jnp.repeat(m_pad, ts, axis=0)
    ya, sk_o, sv_o, ct_o, n_o, m_o = _mix_sample(
        sinks_flat, zs, zgs, cos_s, sin_s,
        ck.reshape(bs * WINDOW * SWA_KV_HEADS, HEAD_DIM), cv.reshape(bs * WINDOW * SWA_KV_HEADS, HEAD_DIM),
        jnp.swapaxes(c_st, -1, -2), n_st.reshape(bs, ML_QK_W), nrep, mrep)
    yb = _mem_sample(zs, cmk.reshape(bs * MEM_LEN * MEM_HEADS, HEAD_DIM),
                     cmv.reshape(bs * MEM_LEN * MEM_HEADS, HEAD_DIM))
    out_s = _outproj(jnp.concatenate([ya, yb], axis=-1), xs2, w_out_b, g_final2)
    c_o = jnp.swapaxes(ct_o, -1, -2)

    p_state = (
        pk.reshape(bp, WINDOW, SWA_KV_HEADS, HEAD_DIM),
        pv.reshape(bp, WINDOW, SWA_KV_HEADS, HEAD_DIM),
        ct.reshape(ML_HEADS, ML_DQK, ML_DV).transpose(0, 2, 1)[None],
        n_p.reshape(bp, ML_HEADS, ML_DQK),
        m_p[:, :ML_HEADS],
        mem_k.reshape(bp, MEM_LEN, MEM_HEADS, HEAD_DIM),
        mem_v.reshape(bp, MEM_LEN, MEM_HEADS, HEAD_DIM),
    )
    s_state = (
        sk_o.reshape(bs, WINDOW, SWA_KV_HEADS, HEAD_DIM),
        sv_o.reshape(bs, WINDOW, SWA_KV_HEADS, HEAD_DIM),
        c_o,
        n_o.reshape(bs, ML_HEADS, ML_DQK),
        m_o[:, :ML_HEADS],
    )
    return out_p.reshape(bp, sp, D_MODEL), out_s.reshape(bs, ts, D_MODEL), p_state, s_state


def kernel(x_prompt, x_sample, mem_prompt, cache_swa_k, cache_swa_v, state_mlstm_C, state_mlstm_n,
           state_mlstm_m, cache_mem_k, cache_mem_v, g_norm, w_in, b_in, swa_sinks, g_mem, w_mem_kv,
           w_out, g_final):
    depth = g_norm.shape[0]
    assert depth == 1 and x_prompt.shape[0] == 1
    y_p, y_s, p_state, s_state = _layer(
        x_prompt, x_sample, mem_prompt, cache_swa_k[0], cache_swa_v[0], state_mlstm_C[0],
        state_mlstm_n[0], state_mlstm_m[0], cache_mem_k[0], cache_mem_v[0],
        g_norm[0], w_in[0], b_in[0], swa_sinks[0], g_mem[0], w_mem_kv[0], w_out[0], g_final)
    return (y_p, y_s) + tuple(s[None] for s in p_state) + tuple(s[None] for s in s_state)
```

```python
import functools
import math

import jax
import jax.numpy as jnp
from jax import lax
from jax.experimental import pallas as pl
from jax.experimental.pallas import tpu as pltpu

F32 = jnp.float32
BF16 = jnp.bfloat16

D_MODEL = 2048
HEAD_DIM = 128
SWA_HEADS = 8
SWA_KV_HEADS = 2
GQA_GROUP = 4
WINDOW = 128
ML_HEADS = 4
ML_DQK = 64
ML_DV = 128
MEM_HEADS = 4
MEM_LEN = 256
CHUNK = 128
ROPE_THETA = 10000.0
EPS = 1e-6
PAST_LEN = 16384
DEC_SEQ = 8

SWA_W = SWA_HEADS * HEAD_DIM
SWA_KV_W = SWA_KV_HEADS * HEAD_DIM
ML_W = ML_HEADS * ML_DV
ML_QK_W = ML_HEADS * ML_DQK
MEM_W = MEM_HEADS * HEAD_DIM
D_MIX = SWA_W + ML_W + MEM_W

_IN_WIDTHS = (SWA_W, SWA_KV_W, SWA_KV_W, SWA_W, ML_QK_W, ML_QK_W, ML_W, ML_HEADS, ML_HEADS, ML_W, ML_W, MEM_W, MEM_W)
_IN_OFFS = [0]
for _w in _IN_WIDTHS:
    _IN_OFFS.append(_IN_OFFS[-1] + _w)
(_R_SQ, _R_SK, _R_SV, _R_SG, _R_MQ, _R_MK, _R_MV, _R_MI, _R_MF, _R_MO, _R_MG, _R_CQ, _R_CG, _R_END) = _IN_OFFS

O_SQ = 0
O_SK = O_SQ + SWA_W
O_SV = O_SK + SWA_KV_W
O_SG = O_SV + SWA_KV_W
O_MQ = O_SG + SWA_W
O_MK = O_MQ + ML_QK_W
O_MV = O_MK + ML_QK_W
O_MO = O_MV + ML_W
O_MG = O_MO + ML_W
O_CQ = O_MG + ML_W
O_CG = O_CQ + MEM_W
D_MAIN = O_CG + MEM_W
GATE_LANES = 128
LANE_I = 0
LANE_F = ML_HEADS

VMEM_LIMIT = 56 * 1024 * 1024

_NT = (((1,), (1,)), ((), ()))


def _dot(a, b):
    return jnp.dot(a, b, preferred_element_type=F32)


def _dot_nt(a, b):
    return lax.dot_general(a, b, _NT, preferred_element_type=F32)


def _exact_sel_dot(sel_bf16, x):
    hi = x.astype(BF16)
    r1 = x - hi.astype(F32)
    mid = r1.astype(BF16)
    lo = (r1 - mid.astype(F32)).astype(BF16)
    return _dot(sel_bf16, hi) + _dot(sel_bf16, mid) + _dot(sel_bf16, lo)


def _silu(x):
    return x * (1.0 / (1.0 + jnp.exp(-x)))


def _sigmoid(x):
    return 1.0 / (1.0 + jnp.exp(-x))


def _log_sigmoid(x):
    return jnp.minimum(x, 0.0) - jnp.log1p(jnp.exp(-jnp.abs(x)))


PROJ_TM = 1024
PROJ_TN = 512
NORM_ROWS = 256


def _proj_kernel(x_ref, g_ref, w_ref, b_ref, wg_ref, bg_ref, z_ref, zg_ref, u_ref):
    j = pl.program_id(1)

    @pl.when(j == 0)
    def _():
        g = g_ref[...]
        for r in range(PROJ_TM // NORM_ROWS):
            rows = pl.ds(r * NORM_ROWS, NORM_ROWS)
            xf = x_ref[rows, :]
            ms = jnp.mean(xf * xf, axis=-1, keepdims=True)
            u_ref[rows, :] = ((xf * lax.rsqrt(ms + EPS)) * g).astype(BF16)
        zg_ref[...] = _dot(u_ref[...], wg_ref[...]) + bg_ref[...]

    z_ref[...] = _dot(u_ref[...], w_ref[...]) + b_ref[...]


def _proj(x2d, g, w, b, wg, bg):
    n = x2d.shape[0]
    grid = (n // PROJ_TM, D_MAIN // PROJ_TN)
    return pl.pallas_call(
        _proj_kernel,
        grid=grid,
        in_specs=[
            pl.BlockSpec((PROJ_TM, D_MODEL), lambda i, j: (i, 0)),
            pl.BlockSpec((1, D_MODEL), lambda i, j: (0, 0)),
            pl.BlockSpec((D_MODEL, PROJ_TN), lambda i, j: (0, j)),
            pl.BlockSpec((1, PROJ_TN), lambda i, j: (0, j)),
            pl.BlockSpec((D_MODEL, GATE_LANES), lambda i, j: (0, 0)),
            pl.BlockSpec((1, GATE_LANES), lambda i, j: (0, 0)),
        ],
        out_specs=[
            pl.BlockSpec((PROJ_TM, PROJ_TN), lambda i, j: (i, j)),
            pl.BlockSpec((PROJ_TM, GATE_LANES), lambda i, j: (i, 0)),
        ],
        out_shape=[
            jax.ShapeDtypeStruct((n, D_MAIN), F32),
            jax.ShapeDtypeStruct((n, GATE_LANES), F32),
        ],
        scratch_shapes=[pltpu.VMEM((PROJ_TM, D_MODEL), BF16)],
        compiler_params=pltpu.CompilerParams(
            dimension_semantics=("arbitrary", "arbitrary"), vmem_limit_bytes=VMEM_LIMIT),
        name="proj",
    )(x2d, g, w, b, wg, bg)


MEMKV_TN = 256


def _memkv_kernel(mem_ref, g_ref, w_ref, o_ref):
    xf = mem_ref[...]
    ms = jnp.mean(xf * xf, axis=-1, keepdims=True)
    u = ((xf * lax.rsqrt(ms + EPS)) * g_ref[...]).astype(BF16)
    o_ref[...] = _dot(u, w_ref[...])


def _memkv(mem2d, g, w):
    return pl.pallas_call(
        _memkv_kernel,
        grid=(2 * MEM_W // MEMKV_TN,),
        in_specs=[
            pl.BlockSpec((MEM_LEN, D_MODEL), lambda j: (0, 0)),
            pl.BlockSpec((1, D_MODEL), lambda j: (0, 0)),
            pl.BlockSpec((D_MODEL, MEMKV_TN), lambda j: (0, j)),
        ],
        out_specs=pl.BlockSpec((MEM_LEN, MEMKV_TN), lambda j: (0, j)),
        out_shape=jax.ShapeDtypeStruct((MEM_LEN, 2 * MEM_W), F32),
        compiler_params=pltpu.CompilerParams(
            dimension_semantics=("arbitrary",), vmem_limit_bytes=VMEM_LIMIT),
        name="memkv",
    )(mem2d, g, w)


OUT_TN = 256


def _rope(x, cos, sin_signed):
    return x * cos + pltpu.roll(x, HEAD_DIM // 2, axis=1) * sin_signed


def _mix_prompt_kernel(sinks_ref, z_ref, zg_ref, rope_c_ref, rope_i_ref, mk_ref, mv_ref,
                       x_ref, wout_ref, gfin_ref,
                       out_ref, ko_ref, vo_ref, ct_ref, n_ref, m_ref,
                       kprev_ref, vprev_ref, y_ref, yprev_ref, acc_ref):
    c = pl.program_id(0)
    nc = pl.num_programs(0) - 1
    n_tiles = D_MODEL // OUT_TN

    def project_tile(t):
        cols = slice(t * OUT_TN, (t + 1) * OUT_TN)
        acc_ref[:, cols] = _dot(yprev_ref[...], wout_ref[:, cols]) + x_ref[:, cols]

    def project_finish():
        acc = acc_ref[...]
        ms = jnp.mean(acc * acc, axis=-1, keepdims=True)
        out_ref[...] = (acc * lax.rsqrt(ms + EPS)) * gfin_ref[...]

    def mix_current(between):
        _prompt_mixers(c, sinks_ref, z_ref, zg_ref, rope_c_ref, rope_i_ref, mk_ref, mv_ref,
                       y_ref, ko_ref, vo_ref, ct_ref, n_ref, m_ref, kprev_ref, vprev_ref, between)
        yprev_ref[...] = y_ref[...]

    @pl.when(c == 0)
    def _():
        kprev_ref[...] = jnp.zeros_like(kprev_ref)
        vprev_ref[...] = jnp.zeros_like(vprev_ref)
        ct_ref[...] = jnp.zeros_like(ct_ref)
        n_ref[...] = jnp.zeros_like(n_ref)
        m_ref[...] = jnp.zeros_like(m_ref)
        mix_current(lambda: None)

    @pl.when((c > 0) & (c < nc))
    def _():
        pending = list(range(n_tiles))

        def between():
            if pending:
                project_tile(pending.pop(0))

        mix_current(between)
        assert not pending
        project_finish()

    @pl.when(c == nc)
    def _():
        for t in range(n_tiles):
            project_tile(t)
        project_finish()


def _prompt_mixers(c, sinks_ref, z_ref, zg_ref, rope_c_ref, rope_i_ref, mk_ref, mv_ref,
                   y_ref, ko_ref, vo_ref, ct_ref, n_ref, m_ref, kprev_ref, vprev_ref, between):
    L = CHUNK
    scale = HEAD_DIM ** -0.5

    cc = rope_c_ref[0, pl.ds(c, 1), :]
    sc = rope_c_ref[1, pl.ds(c, 1), :]
    cos = cc * rope_i_ref[0] - sc * rope_i_ref[1]
    sin = sc * rope_i_ref[2] + cc * rope_i_ref[3]

    ri = lax.broadcasted_iota(jnp.int32, (GQA_GROUP * L, 2 * L), 0) & (L - 1)
    cj = lax.broadcasted_iota(jnp.int32, (GQA_GROUP * L, 2 * L), 1)
    j_low = jnp.where(c > 0, 0, L)
    band = (cj > ri) & (cj <= ri + L) & (cj >= j_low)
    for kv in range(SWA_KV_HEADS):
        between()
        k_new = _rope(z_ref[:, O_SK + kv * HEAD_DIM:O_SK + (kv + 1) * HEAD_DIM], cos, sin)
        v_new = z_ref[:, O_SV + kv * HEAD_DIM:O_SV + (kv + 1) * HEAD_DIM]
        ko_ref[:, kv * HEAD_DIM:(kv + 1) * HEAD_DIM] = k_new
        vo_ref[:, kv * HEAD_DIM:(kv + 1) * HEAD_DIM] = v_new
        k_new_b = k_new.astype(BF16)
        v_new_b = v_new.astype(BF16)
        kcat = jnp.concatenate([kprev_ref[:, kv * HEAD_DIM:(kv + 1) * HEAD_DIM], k_new_b], axis=0)
        vcat = jnp.concatenate([vprev_ref[:, kv * HEAD_DIM:(kv + 1) * HEAD_DIM], v_new_b], axis=0)
        kprev_ref[:, kv * HEAD_DIM:(kv + 1) * HEAD_DIM] = k_new_b
        vprev_ref[:, kv * HEAD_DIM:(kv + 1) * HEAD_DIM] = v_new_b
        qs = []
        sks = []
        for g in range(GQA_GROUP):
            h = kv * GQA_GROUP + g
            qs.append(_rope(z_ref[:, O_SQ + h * HEAD_DIM:O_SQ + (h + 1) * HEAD_DIM], cos, sin).astype(BF16))
            sks.append(jnp.full((L, 1), sinks_ref[h], F32))
        q_st = jnp.concatenate(qs, axis=0)
        sk = jnp.concatenate(sks, axis=0)
        s = _dot_nt(q_st, kcat) * scale
        s = jnp.where(band, s, -jnp.inf)
        m = jnp.maximum(jnp.max(s, axis=-1, keepdims=True), sk)
        p = jnp.exp(s - m)
        l = jnp.sum(p, axis=-1, keepdims=True) + jnp.exp(sk - m)
        o = _dot(p.astype(BF16), vcat) * (1.0 / l)
        for g in range(GQA_GROUP):
            h = kv * GQA_GROUP + g
            gate = _silu(z_ref[:, O_SG + h * HEAD_DIM:O_SG + (h + 1) * HEAD_DIM])
            y_ref[:, h * HEAD_DIM:(h + 1) * HEAD_DIM] = (o[g * L:(g + 1) * L] * gate).astype(BF16)

    for h in range(MEM_HEADS):
        between()
        q = z_ref[:, O_CQ + h * HEAD_DIM:O_CQ + (h + 1) * HEAD_DIM].astype(BF16)
        s = _dot_nt(q, mk_ref[:, h * HEAD_DIM:(h + 1) * HEAD_DIM]) * scale
        m = jnp.max(s, axis=-1, keepdims=True)
        p = jnp.exp(s - m)
        l = jnp.sum(p, axis=-1, keepdims=True)
        o = _dot(p.astype(BF16), mv_ref[:, h * HEAD_DIM:(h + 1) * HEAD_DIM]) * (1.0 / l)
        gate = _silu(z_ref[:, O_CG + h * HEAD_DIM:O_CG + (h + 1) * HEAD_DIM])
        col = SWA_W + ML_W + h * HEAD_DIM
        y_ref[:, col:col + HEAD_DIM] = (o * gate).astype(BF16)

    lane = lax.broadcasted_iota(jnp.int32, (1, GATE_LANES), 1)
    zg = zg_ref[...]
    gates = jnp.where(lane < LANE_F, zg, _log_sigmoid(zg))
    gates = jnp.where(lane < 2 * ML_HEADS, gates, 0.0)
    rr = lax.broadcasted_iota(jnp.int32, (L, L), 0)
    cc = lax.broadcasted_iota(jnp.int32, (L, L), 1)
    causal = rr >= cc
    tri = jnp.where(causal, 1.0, 0.0).astype(BF16)
    csum = _exact_sel_dot(tri, gates)
    gates_t = gates.T
    csum_t = csum.T
    half_lo = lane < ML_DQK
    for h in range(ML_HEADS):
        between()
        pair, half = divmod(h, 2)
        hmask = half_lo if half == 0 else jnp.logical_not(half_lo)
        q_pair = z_ref[:, O_MQ + pair * 128:O_MQ + (pair + 1) * 128]
        k_pair = z_ref[:, O_MK + pair * 128:O_MK + (pair + 1) * 128] * (ML_DQK ** -0.5)
        v_h = z_ref[:, O_MV + h * ML_DV:O_MV + (h + 1) * ML_DV].astype(BF16)
        q_h = jnp.where(hmask, q_pair, 0.0)
        q_hb = q_h.astype(BF16)
        k_pb = k_pair.astype(BF16)

        bt_c = csum[:, LANE_F + h:LANE_F + h + 1]
        it_c = gates[:, LANE_I + h:LANE_I + h + 1]
        bt_r = csum_t[LANE_F + h:LANE_F + h + 1, :]
        it_r = gates_t[LANE_I + h:LANE_I + h + 1, :]
        m_prev = m_ref[:, h:h + 1]

        dlog = jnp.where(causal, bt_c - bt_r + it_r, -jnp.inf)
        inter = bt_c + m_prev
        m_t = jnp.maximum(inter, jnp.max(dlog, axis=-1, keepdims=True))
        w_intra = jnp.exp(dlog - m_t)
        w_state = jnp.exp(inter - m_t)
        a = w_intra * _dot_nt(q_hb, k_pb)
        ct_pair = ct_ref[pair * 128:(pair + 1) * 128, :]
        num = _dot(a.astype(BF16), v_h) + w_state * _dot(q_hb, ct_pair.astype(BF16))
        n_pair = n_ref[:, pair * 128:(pair + 1) * 128]
        den = jnp.sum(a, axis=-1, keepdims=True) + w_state * jnp.sum(q_h * n_pair, axis=-1, keepdims=True)
        denom = jnp.maximum(jnp.abs(den), jnp.exp(-m_t))
        hid = num * (1.0 / denom)
        o_gate = _sigmoid(z_ref[:, O_MO + h * ML_DV:O_MO + (h + 1) * ML_DV])
        gate = _silu(z_ref[:, O_MG + h * ML_DV:O_MG + (h + 1) * ML_DV])
        col = SWA_W + h * ML_DV
        y_ref[:, col:col + ML_DV] = ((hid * o_gate) * gate).astype(BF16)

        m_new = m_t[L - 1:L, :]
        bt_last = bt_c[L - 1:L, :]
        w_s = jnp.exp(bt_last - bt_c + it_c - m_new)
        decay = jnp.exp(bt_last + m_prev - m_new)
        kw = k_pair * w_s
        upd = _dot(kw.T.astype(BF16), v_h)
        rows = slice(h * ML_DQK, (h + 1) * ML_DQK)
        ct_ref[rows, :] = decay * ct_ref[rows, :] + upd[half * ML_DQK:(half + 1) * ML_DQK, :]
        ksum = jnp.sum(kw, axis=0, keepdims=True)
        n_ref[:, rows] = decay * n_ref[:, rows] + ksum[:, half * ML_DQK:(half + 1) * ML_DQK]
        m_ref[:, h:h + 1] = m_new


def _mix_prompt(sinks, z, zg, rope_c, rope_i, mk, mv, x2d, w_out, g_final):
    n = z.shape[0]
    nc = n // CHUNK
    const = lambda c, s: (0, 0)
    cur = lambda c, s: (jnp.minimum(c, nc - 1), 0)
    prev = lambda c, s: (jnp.maximum(c - 1, 0), 0)
    return pl.pallas_call(
        _mix_prompt_kernel,
        grid_spec=pltpu.PrefetchScalarGridSpec(
            num_scalar_prefetch=1,
            grid=(nc + 1,),
            in_specs=[
                pl.BlockSpec((CHUNK, D_MAIN), cur),
                pl.BlockSpec((CHUNK, GATE_LANES), cur),
                pl.BlockSpec((2, nc, HEAD_DIM), lambda c, s: (0, 0, 0)),
                pl.BlockSpec((4, CHUNK, HEAD_DIM), lambda c, s: (0, 0, 0)),
                pl.BlockSpec((MEM_LEN, MEM_W), const),
                pl.BlockSpec((MEM_LEN, MEM_W), const),
                pl.BlockSpec((CHUNK, D_MODEL), prev),
                pl.BlockSpec((D_MIX, D_MODEL), const),
                pl.BlockSpec((1, D_MODEL), const),
            ],
            out_specs=[
                pl.BlockSpec((CHUNK, D_MODEL), prev),
                pl.BlockSpec((CHUNK, SWA_KV_W), const),
                pl.BlockSpec((CHUNK, SWA_KV_W), const),
                pl.BlockSpec((ML_QK_W, ML_DV), const),
                pl.BlockSpec((1, ML_QK_W), const),
                pl.BlockSpec((1, GATE_LANES), const),
            ],
            scratch_shapes=[
                pltpu.VMEM((CHUNK, SWA_KV_W), BF16),
                pltpu.VMEM((CHUNK, SWA_KV_W), BF16),
                pltpu.VMEM((CHUNK, D_MIX), BF16),
                pltpu.VMEM((CHUNK, D_MIX), BF16),
                pltpu.VMEM((CHUNK, D_MODEL), F32),
            ],
        ),
        out_shape=[
            jax.ShapeDtypeStruct((n, D_MODEL), F32),
            jax.ShapeDtypeStruct((CHUNK, SWA_KV_W), F32),
            jax.ShapeDtypeStruct((CHUNK, SWA_KV_W), F32),
            jax.ShapeDtypeStruct((ML_QK_W, ML_DV), F32),
            jax.ShapeDtypeStruct((1, ML_QK_W), F32),
            jax.ShapeDtypeStruct((1, GATE_LANES), F32),
        ],
        compiler_params=pltpu.CompilerParams(
            dimension_semantics=("arbitrary",), vmem_limit_bytes=VMEM_LIMIT),
        name="mix_prompt",
    )(sinks, z, zg, rope_c, rope_i, mk, mv, x2d, w_out, g_final)


OUT_TM = 512


def _outproj_kernel(y_ref, x_ref, w_ref, g_ref, o_ref):
    acc = _dot(y_ref[...], w_ref[...]) + x_ref[...]
    ms = jnp.mean(acc * acc, axis=-1, keepdims=True)
    o_ref[...] = (acc * lax.rsqrt(ms + EPS)) * g_ref[...]


def _outproj(y, x2d, w, g):
    n = x2d.shape[0]
    return pl.pallas_call(
        _outproj_kernel,
        grid=(n // OUT_TM,),
        in_specs=[
            pl.BlockSpec((OUT_TM, D_MIX), lambda i: (i, 0)),
            pl.BlockSpec((OUT_TM, D_MODEL), lambda i: (i, 0)),
            pl.BlockSpec((D_MIX, D_MODEL), lambda i: (0, 0)),
            pl.BlockSpec((1, D_MODEL), lambda i: (0, 0)),
        ],
        out_specs=pl.BlockSpec((OUT_TM, D_MODEL), lambda i: (i, 0)),
        out_shape=jax.ShapeDtypeStruct((n, D_MODEL), F32),
        compiler_params=pltpu.CompilerParams(
            dimension_semantics=("arbitrary",), vmem_limit_bytes=VMEM_LIMIT),
        name="outproj",
    )(y, x2d, w, g)


SB = 16
SR = SB * DEC_SEQ


def _mix_sample_kernel(sinks_ref, z_ref, zg_ref, cos_ref, sin_ref, ck_ref, cv_ref,
                       c_ref, nst_ref, nrep_ref, mrep_ref,
                       y_ref, ko_ref, vo_ref, co_ref, no_ref, mo_ref,
                       q_s, k_s, o_s):
    T = DEC_SEQ
    scale = HEAD_DIM ** -0.5
    cos = cos_ref[...]
    sin = sin_ref[...]

    for h in range(SWA_HEADS):
        q_s[:, h * HEAD_DIM:(h + 1) * HEAD_DIM] = _rope(
            z_ref[:, O_SQ + h * HEAD_DIM:O_SQ + (h + 1) * HEAD_DIM], cos, sin)
    for kv in range(SWA_KV_HEADS):
        k_s[:, kv * HEAD_DIM:(kv + 1) * HEAD_DIM] = _rope(
            z_ref[:, O_SK + kv * HEAD_DIM:O_SK + (kv + 1) * HEAD_DIM], cos, sin)

    KP = 2 * WINDOW
    rt = lax.broadcasted_iota(jnp.int32, (GQA_GROUP * T, KP), 0) & (T - 1)
    cj = lax.broadcasted_iota(jnp.int32, (GQA_GROUP * T, KP), 1)
    mask = (cj > rt) & (cj <= rt + WINDOW)
    kpad = jnp.zeros((KP - WINDOW - T, HEAD_DIM), F32)
    NKV = SWA_KV_HEADS
    SEQ_ROWS = WINDOW * NKV

    def per_seq(b, carry):
        rows = pl.ds(pl.multiple_of(b * T, T), T)
        base = pl.multiple_of(b * SEQ_ROWS, SEQ_ROWS)
        k_new = k_s[rows, :]
        v_new = z_ref[rows, O_SV:O_SV + SWA_KV_W]
        keep = SEQ_ROWS - T * NKV
        ko_ref[pl.ds(base, keep), :] = ck_ref[pl.ds(base + T * NKV, keep), :]
        vo_ref[pl.ds(base, keep), :] = cv_ref[pl.ds(base + T * NKV, keep), :]
        for kv in range(SWA_KV_HEADS):
            cols = slice(kv * HEAD_DIM, (kv + 1) * HEAD_DIM)
            ko_ref[pl.ds(base + keep + kv, T, stride=NKV), :] = k_new[:, cols]
            vo_ref[pl.ds(base + keep + kv, T, stride=NKV), :] = v_new[:, cols]
            kc = ck_ref[pl.ds(base + kv, WINDOW, stride=NKV), :]
            vc = cv_ref[pl.ds(base + kv, WINDOW, stride=NKV), :]
            k_all = jnp.concatenate([kc, k_new[:, cols], kpad], axis=0).astype(BF16)
            v_all = jnp.concatenate([vc, v_new[:, cols], kpad], axis=0).astype(BF16)
            qs = []
            sks = []
            for g in range(GQA_GROUP):
                h = kv * GQA_GROUP + g
                qs.append(q_s[rows, h * HEAD_DIM:(h + 1) * HEAD_DIM])
                sks.append(jnp.full((T, 1), sinks_ref[h], F32))
            q_st = jnp.concatenate(qs, axis=0).astype(BF16)
            sk = jnp.concatenate(sks, axis=0)
            s = jnp.where(mask, _dot_nt(q_st, k_all) * scale, -jnp.inf)
            m = jnp.maximum(jnp.max(s, axis=-1, keepdims=True), sk)
            p = jnp.exp(s - m)
            l = jnp.sum(p, axis=-1, keepdims=True) + jnp.exp(sk - m)
            o = _dot(p.astype(BF16), v_all) * (1.0 / l)
            for g in range(GQA_GROUP):
                h = kv * GQA_GROUP + g
                o_s[rows, h * HEAD_DIM:(h + 1) * HEAD_DIM] = o[g * T:(g + 1) * T]
        return carry

    lax.fori_loop(0, SB, per_seq, 0)

    for h in range(SWA_HEADS):
        cols = slice(h * HEAD_DIM, (h + 1) * HEAD_DIM)
        gate = _silu(z_ref[:, O_SG + h * HEAD_DIM:O_SG + (h + 1) * HEAD_DIM])
        y_ref[:, cols] = (o_s[:, cols] * gate).astype(BF16)

    R = SR
    lane = lax.broadcasted_iota(jnp.int32, (1, GATE_LANES), 1)
    zg = zg_ref[...]
    gates = jnp.where(lane < LANE_F, zg, _log_sigmoid(zg))
    gates = jnp.where(lane < 2 * ML_HEADS, gates, 0.0)
    rr = lax.broadcasted_iota(jnp.int32, (R, R), 0)
    cc = lax.broadcasted_iota(jnp.int32, (R, R), 1)
    same_seq = (rr >> 3) == (cc >> 3)
    causal = same_seq & (rr >= cc)
    tri = jnp.where(causal, 1.0, 0.0).astype(BF16)
    csum = _exact_sel_dot(tri, gates)
    gates_t = gates.T
    csum_t = csum.T
    half_lo = lane < ML_DQK
    seq_of_col = lax.broadcasted_iota(jnp.int32, (SB, 1, R), 2) >> 3
    seq_id = lax.broadcasted_iota(jnp.int32, (SB, 1, R), 0)
    own_cols = seq_of_col == seq_id
    own_blk = ((lax.broadcasted_iota(jnp.int32, (R, SB * 128), 0) >> 3)
               == (lax.broadcasted_iota(jnp.int32, (R, SB * 128), 1) >> 7))
    k_t = [(z_ref[:, O_MK + p * 128:O_MK + (p + 1) * 128] * (ML_DQK ** -0.5)).T for p in range(2)]
    mrep = mrep_ref[...]
    tok3 = lax.broadcasted_iota(jnp.int32, (SB, T, 1), 1)
    mo_ref[...] = jnp.zeros_like(mo_ref)

    def per_seq_value(col):
        return jnp.max(col.reshape(SB, T, 1), axis=1, keepdims=True)

    def last_of_seq(col):
        c3 = jnp.where(tok3 == T - 1, col.reshape(SB, T, 1), -jnp.inf)
        return jnp.broadcast_to(jnp.max(c3, axis=1, keepdims=True), (SB, T, 1)).reshape(R, 1)

    for h in range(ML_HEADS):
        pair, half = divmod(h, 2)
        hmask = half_lo if half == 0 else jnp.logical_not(half_lo)
        q_pair = z_ref[:, O_MQ + pair * 128:O_MQ + (pair + 1) * 128]
        k_pair = z_ref[:, O_MK + pair * 128:O_MK + (pair + 1) * 128] * (ML_DQK ** -0.5)
        v_f = z_ref[:, O_MV + h * ML_DV:O_MV + (h + 1) * ML_DV]
        v_h = v_f.astype(BF16)
        q_h = jnp.where(hmask, q_pair, 0.0)
        q_hb = q_h.astype(BF16)
        k_pb = k_pair.astype(BF16)

        bt_c = csum[:, LANE_F + h:LANE_F + h + 1]
        it_c = gates[:, LANE_I + h:LANE_I + h + 1]
        bt_r = csum_t[LANE_F + h:LANE_F + h + 1, :]
        it_r = gates_t[LANE_I + h:LANE_I + h + 1, :]
        m_prev = mrep[:, h:h + 1]

        dlog = jnp.where(causal, bt_c - bt_r + it_r, -jnp.inf)
        inter = bt_c + m_prev
        m_t = jnp.maximum(inter, jnp.max(dlog, axis=-1, keepdims=True))
        w_intra = jnp.exp(dlog - m_t)
        w_state = jnp.exp(inter - m_t)
        a = w_intra * _dot_nt(q_hb, k_pb)

        ct_st = c_ref[:, 2 * pair:2 * pair + 2].reshape(SB * 128, ML_DV).astype(BF16)
        q_blk = jnp.where(own_blk, jnp.tile(q_h, (1, SB)), 0.0).astype(BF16)
        num_state = _dot(q_blk, ct_st)

        num = _dot(a.astype(BF16), v_h) + w_state * num_state
        n_pair = nrep_ref[:, pair * 128:(pair + 1) * 128]
        den = jnp.sum(a, axis=-1, keepdims=True) + w_state * jnp.sum(q_h * n_pair, axis=-1, keepdims=True)
        denom = jnp.maximum(jnp.abs(den), jnp.exp(-m_t))
        hid = num * (1.0 / denom)
        o_gate = _sigmoid(z_ref[:, O_MO + h * ML_DV:O_MO + (h + 1) * ML_DV])
        gate = _silu(z_ref[:, O_MG + h * ML_DV:O_MG + (h + 1) * ML_DV])
        col = SWA_W + h * ML_DV
        y_ref[:, col:col + ML_DV] = ((hid * o_gate) * gate).astype(BF16)

        m_new = last_of_seq(m_t)
        bt_last = last_of_seq(bt_c)
        w_s = jnp.exp(bt_last - bt_c + it_c - m_new)
        decay = jnp.exp(bt_last + m_prev - m_new)
        decay_seq = per_seq_value(decay)

        kt_h = k_t[pair][half * ML_DQK:(half + 1) * ML_DQK, :]
        lhs = jnp.where(own_cols, kt_h[None, :, :], 0.0).reshape(SB * ML_DQK, R).astype(BF16)
        upd = _dot(lhs, (v_f * w_s).astype(BF16)).reshape(SB, ML_DQK, ML_DV)
        co_ref[:, h] = decay_seq * c_ref[:, h] + upd

        kw = (k_pair * w_s).reshape(SB, T, 128)
        ksum = jnp.sum(kw, axis=1)
        n_old = nst_ref[:, h * ML_DQK:(h + 1) * ML_DQK]
        dec2 = decay_seq.reshape(SB, 1)
        no_ref[:, h * ML_DQK:(h + 1) * ML_DQK] = dec2 * n_old + ksum[:, half * ML_DQK:(half + 1) * ML_DQK]
        mo_ref[:, h:h + 1] = per_seq_value(m_new).reshape(SB, 1)


def _mix_sample(sinks, z, zg, cos, sin, ck, cv, cst, nst, nrep, mrep):
    nb = cst.shape[0]
    steps = nb // SB
    const = lambda i, s: (0, 0)
    cache_rows = SB * WINDOW * SWA_KV_HEADS
    return pl.pallas_call(
        _mix_sample_kernel,
        grid_spec=pltpu.PrefetchScalarGridSpec(
            num_scalar_prefetch=1,
            grid=(steps,),
            in_specs=[
                pl.BlockSpec((SR, D_MAIN), lambda i, s: (i, 0)),
                pl.BlockSpec((SR, GATE_LANES), lambda i, s: (i, 0)),
                pl.BlockSpec((SR, HEAD_DIM), const),
                pl.BlockSpec((SR, HEAD_DIM), const),
                pl.BlockSpec((cache_rows, HEAD_DIM), lambda i, s: (i, 0)),
                pl.BlockSpec((cache_rows, HEAD_DIM), lambda i, s: (i, 0)),
                pl.BlockSpec((SB, ML_HEADS, ML_DQK, ML_DV), lambda i, s: (i, 0, 0, 0)),
                pl.BlockSpec((SB, ML_QK_W), lambda i, s: (i, 0)),
                pl.BlockSpec((SR, ML_QK_W), lambda i, s: (i, 0)),
                pl.BlockSpec((SR, GATE_LANES), lambda i, s: (i, 0)),
            ],
            out_specs=[
                pl.BlockSpec((SR, SWA_W + ML_W), lambda i, s: (i, 0)),
                pl.BlockSpec((cache_rows, HEAD_DIM), lambda i, s: (i, 0)),
                pl.BlockSpec((cache_rows, HEAD_DIM), lambda i, s: (i, 0)),
                pl.BlockSpec((SB, ML_HEADS, ML_DQK, ML_DV), lambda i, s: (i, 0, 0, 0)),
                pl.BlockSpec((SB, ML_QK_W), lambda i, s: (i, 0)),
                pl.BlockSpec((SB, GATE_LANES), lambda i, s: (i, 0)),
            ],
            scratch_shapes=[
                pltpu.VMEM((SR, SWA_W), F32),
                pltpu.VMEM((SR, SWA_KV_W), F32),
                pltpu.VMEM((SR, SWA_W), F32),
            ],
        ),
        out_shape=[
            jax.ShapeDtypeStruct((nb * DEC_SEQ, SWA_W + ML_W), BF16),
            jax.ShapeDtypeStruct(ck.shape, F32),
            jax.ShapeDtypeStruct(cv.shape, F32),
            jax.ShapeDtypeStruct((nb, ML_HEADS, ML_DQK, ML_DV), F32),
            jax.ShapeDtypeStruct((nb, ML_QK_W), F32),
            jax.ShapeDtypeStruct((nb, GATE_LANES), F32),
        ],
        compiler_params=pltpu.CompilerParams(
            dimension_semantics=("arbitrary",), vmem_limit_bytes=VMEM_LIMIT),
        name="mix_sample",
    )(sinks, z, zg, cos, sin, ck, cv, cst, nst, nrep, mrep)


MB = 8
MR = MB * DEC_SEQ


def _mem_sample_kernel(cq_ref, cg_ref, mk_ref, mv_ref, y_ref, o_s):
    T = DEC_SEQ
    scale = HEAD_DIM ** -0.5
    zpad = jnp.zeros((T, HEAD_DIM), F32)

    seq_rows = MEM_LEN * MEM_HEADS

    def per_seq(b, carry):
        rows = pl.ds(pl.multiple_of(b * T, T), T)
        base = pl.multiple_of(b * seq_rows, seq_rows)
        for h in range(MEM_HEADS):
            cols = slice(h * HEAD_DIM, (h + 1) * HEAD_DIM)
            kb = mk_ref[pl.ds(base + h, MEM_LEN, stride=MEM_HEADS), :].astype(BF16)
            vb = mv_ref[pl.ds(base + h, MEM_LEN, stride=MEM_HEADS), :].astype(BF16)
            q = jnp.concatenate([cq_ref[rows, cols], zpad], axis=0).astype(BF16)
            s = _dot_nt(q, kb) * scale
            m = jnp.max(s, axis=-1, keepdims=True)
            p = jnp.exp(s - m)
            l = jnp.sum(p, axis=-1, keepdims=True)
            o = _dot(p.astype(BF16), vb) * (1.0 / l)
            o_s[rows, cols] = o[0:T]
        return carry

    lax.fori_loop(0, MB, per_seq, 0)
    y_ref[...] = (o_s[...] * _silu(cg_ref[...])).astype(BF16)


def _mem_sample(z, mk, mv):
    nb = z.shape[0] // DEC_SEQ
    cache_rows = MB * MEM_LEN * MEM_HEADS
    return pl.pallas_call(
        _mem_sample_kernel,
        grid=(nb // MB,),
        in_specs=[
            pl.BlockSpec((MR, MEM_W), lambda i: (i, O_CQ // MEM_W)),
            pl.BlockSpec((MR, MEM_W), lambda i: (i, O_CG // MEM_W)),
            pl.BlockSpec((cache_rows, HEAD_DIM), lambda i: (i, 0)),
            pl.BlockSpec((cache_rows, HEAD_DIM), lambda i: (i, 0)),
        ],
        out_specs=pl.BlockSpec((MR, MEM_W), lambda i: (i, 0)),
        out_shape=jax.ShapeDtypeStruct((nb * DEC_SEQ, MEM_W), BF16),
        scratch_shapes=[pltpu.VMEM((MR, MEM_W), F32)],
        compiler_params=pltpu.CompilerParams(
            dimension_semantics=("arbitrary",), vmem_limit_bytes=VMEM_LIMIT),
        name="mem_sample",
    )(z, z, mk, mv)


def _rope_cos_sin(pos):
    half = HEAD_DIM // 2
    inv = jnp.power(ROPE_THETA, -(jnp.arange(half, dtype=F32) * 2.0 / HEAD_DIM))
    ang = pos.astype(F32)[:, None] * inv[None, :]
    return jnp.cos(ang), jnp.sin(ang)


def _rope_tables(pos):
    cos, sin = _rope_cos_sin(pos)
    return jnp.concatenate([cos, cos], axis=-1), jnp.concatenate([-sin, sin], axis=-1)


def _rope_split_tables(n_chunks):
    ca, sa = _rope_cos_sin(jnp.arange(n_chunks, dtype=jnp.int32) * CHUNK)
    cb, sb = _rope_cos_sin(jnp.arange(CHUNK, dtype=jnp.int32))
    dup = lambda t: jnp.concatenate([t, t], axis=-1)
    sgn = lambda t: jnp.concatenate([-t, t], axis=-1)
    return jnp.stack([dup(ca), dup(sa)]), jnp.stack([dup(cb), dup(sb), sgn(cb), sgn(sb)])


def _relayout_in_proj(w_in, b_in):
    def main(a):
        return jnp.concatenate([a[..., _R_SQ:_R_MI], a[..., _R_MO:_R_END]], axis=-1)

    def gates(a):
        pad = jnp.zeros(a.shape[:-1] + (GATE_LANES - 2 * ML_HEADS,), a.dtype)
        return jnp.concatenate([a[..., _R_MI:_R_MO], pad], axis=-1)

    return (main(w_in).astype(BF16), main(b_in)[None, :],
            gates(w_in).astype(BF16), gates(b_in)[None, :])


def _layer(xp, xs, mem, ck, cv, c_st, n_st, m_st, cmk, cmv,
           g_norm, w_in, b_in, sinks, g_mem, w_mem_kv, w_out, g_final):
    bp, sp, _ = xp.shape
    bs, ts, _ = xs.shape
    xp2 = xp.reshape(bp * sp, D_MODEL)
    xs2 = xs.reshape(bs * ts, D_MODEL)
    w_main, b_main, w_gate, b_gate = _relayout_in_proj(w_in, b_in)
    g_norm2 = g_norm[None, :]
    w_out_b = w_out.astype(BF16)
    g_final2 = g_final[None, :]
    sinks_flat = sinks.reshape(SWA_HEADS)

    zp, zgp = _proj(xp2, g_norm2, w_main, b_main, w_gate, b_gate)
    memkv = _memkv(mem.reshape(MEM_LEN, D_MODEL), g_mem[None, :], w_mem_kv.astype(BF16))
    mem_k = memkv[:, :MEM_W]
    mem_v = memkv[:, MEM_W:]
    rope_c, rope_i = _rope_split_tables(sp // CHUNK)
    out_p, pk, pv, ct, n_p, m_p = _mix_prompt(sinks_flat, zp, zgp, rope_c, rope_i,
                                               mem_k.astype(BF16), mem_v.astype(BF16),
                                               xp2, w_out_b, g_final2)

    zs, zgs = _proj(xs2, g_norm2, w_main, b_main, w_gate, b_gate)
    cos_s, sin_s = _rope_tables(PAST_LEN + jnp.arange(ts, dtype=jnp.int32))
    cos_s = jnp.tile(cos_s, (SB, 1))
    sin_s = jnp.tile(sin_s, (SB, 1))
    nrep = jnp.repeat(n_st.reshape(bs, ML_QK_W), ts, axis=0)
    m_pad = jnp.pad(m_st, ((0, 0), (0, GATE_LANES - ML_HEADS)))
    mrep = jnp.repeat(m_pad, ts, axis=0)
    ya, sk_o, sv_o, ct_o, n_o, m_o = _mix_sample(
        sinks_flat, zs, zgs, cos_s, sin_s,
        ck.reshape(bs * WINDOW * SWA_KV_HEADS, HEAD_DIM), cv.reshape(bs * WINDOW * SWA_KV_HEADS, HEAD_DIM),
        jnp.swapaxes(c_st, -1, -2), n_st.reshape(bs, ML_QK_W), nrep, mrep)
    yb = _mem_sample(zs, cmk.reshape(bs * MEM_LEN * MEM_HEADS, HEAD_DIM),
                     cmv.reshape(bs * MEM_LEN * MEM_HEADS, HEAD_DIM))
    out_s = _outproj(jnp.concatenate([ya, yb], axis=-1), xs2, w_out_b, g_final2)
    c_o = jnp.swapaxes(ct_o, -1, -2)

    p_state = (
        pk.reshape(bp, WINDOW, SWA_KV_HEADS, HEAD_DIM),
        pv.reshape(bp, WINDOW, SWA_KV_HEADS, HEAD_DIM),
        ct.reshape(ML_HEADS, ML_DQK, ML_DV).transpose(0, 2, 1)[None],
        n_p.reshape(bp, ML_HEADS, ML_DQK),
        m_p[:, :ML_HEADS],
        mem_k.reshape(bp, MEM_LEN, MEM_HEADS, HEAD_DIM),
        mem_v.reshape(bp, MEM_LEN, MEM_HEADS, HEAD_DIM),
    )
    s_state = (
        sk_o.reshape(bs, WINDOW, SWA_KV_HEADS, HEAD_DIM),
        sv_o.reshape(bs, WINDOW, SWA_KV_HEADS, HEAD_DIM),
        c_o,
        n_o.reshape(bs, ML_HEADS, ML_DQK),
        m_o[:, :ML_HEADS],
    )
    return out_p.reshape(bp, sp, D_MODEL), out_s.reshape(bs, ts, D_MODEL), p_state, s_state


def kernel(x_prompt, x_sample, mem_prompt, cache_swa_k, cache_swa_v, state_mlstm_C, state_mlstm_n,
           state_mlstm_m, cache_mem_k, cache_mem_v, g_norm, w_in, b_in, swa_sinks, g_mem, w_mem_kv,
           w_out, g_final):
    depth = g_norm.shape[0]
    assert depth == 1 and x_prompt.shape[0] == 1
    y_p, y_s, p_state, s_state = _layer(
        x_prompt, x_sample, mem_prompt, cache_swa_k[0], cache_swa_v[0], state_mlstm_C[0],
        state_mlstm_n[0], state_mlstm_m[0], cache_mem_k[0], cache_mem_v[0],
        g_norm[0], w_in[0], b_in[0], swa_sinks[0], g_mem[0], w_mem_kv[0], w_out[0], g_final)
    return (y_p, y_s) + tuple(s[None] for s in p_state) + tuple(s[None] for s in s_state)
```

```python
import functools
import math

import jax
import jax.numpy as jnp
from jax import lax
from jax.experimental import pallas as pl
from jax.experimental.pallas import tpu as pltpu

F32 = jnp.float32
BF16 = jnp.bfloat16

D_MODEL = 2048
HEAD_DIM = 128
SWA_HEADS = 8
SWA_KV_HEADS = 2
GQA_GROUP = 4
WINDOW = 128
ML_HEADS = 4
ML_DQK = 64
ML_DV = 128
MEM_HEADS = 4
MEM_LEN = 256
CHUNK = 128
ROPE_THETA = 10000.0
EPS = 1e-6
PAST_LEN = 16384
DEC_SEQ = 8

SWA_W = SWA_HEADS * HEAD_DIM
SWA_KV_W = SWA_KV_HEADS * HEAD_DIM
ML_W = ML_HEADS * ML_DV
ML_QK_W = ML_HEADS * ML_DQK
MEM_W = MEM_HEADS * HEAD_DIM
D_MIX = SWA_W + ML_W + MEM_W

_IN_WIDTHS = (SWA_W, SWA_KV_W, SWA_KV_W, SWA_W, ML_QK_W, ML_QK_W, ML_W, ML_HEADS, ML_HEADS, ML_W, ML_W, MEM_W, MEM_W)
_IN_OFFS = [0]
for _w in _IN_WIDTHS:
    _IN_OFFS.append(_IN_OFFS[-1] + _w)
(_R_SQ, _R_SK, _R_SV, _R_SG, _R_MQ, _R_MK, _R_MV, _R_MI, _R_MF, _R_MO, _R_MG, _R_CQ, _R_CG, _R_END) = _IN_OFFS

O_SQ = 0
O_SK = O_SQ + SWA_W
O_SV = O_SK + SWA_KV_W
O_SG = O_SV + SWA_KV_W
O_MQ = O_SG + SWA_W
O_MK = O_MQ + ML_QK_W
O_MV = O_MK + ML_QK_W
O_MO = O_MV + ML_W
O_MG = O_MO + ML_W
O_CQ = O_MG + ML_W
O_CG = O_CQ + MEM_W
D_MAIN = O_CG + MEM_W
GATE_LANES = 128
LANE_I = 0
LANE_F = ML_HEADS

VMEM_LIMIT = 56 * 1024 * 1024

_NT = (((1,), (1,)), ((), ()))


def _dot(a, b):
    return jnp.dot(a, b, preferred_element_type=F32)


def _dot_nt(a, b):
    return lax.dot_general(a, b, _NT, preferred_element_type=F32)


def _exact_sel_dot(sel_bf16, x):
    hi = x.astype(BF16)
    r1 = x - hi.astype(F32)
    mid = r1.astype(BF16)
    lo = (r1 - mid.astype(F32)).astype(BF16)
    return _dot(sel_bf16, hi) + _dot(sel_bf16, mid) + _dot(sel_bf16, lo)


def _silu(x):
    return x * (1.0 / (1.0 + jnp.exp(-x)))


def _sigmoid(x):
    return 1.0 / (1.0 + jnp.exp(-x))


def _log_sigmoid(x):
    return jnp.minimum(x, 0.0) - jnp.log1p(jnp.exp(-jnp.abs(x)))


PROJ_TM = 1024
PROJ_TN = 512
NORM_ROWS = 256


def _proj_kernel(x_ref, g_ref, w_ref, b_ref, wg_ref, bg_ref, z_ref, zg_ref, u_ref):
    j = pl.program_id(1)

    @pl.when(j == 0)
    def _():
        g = g_ref[...]
        for r in range(PROJ_TM // NORM_ROWS):
            rows = pl.ds(r * NORM_ROWS, NORM_ROWS)
            xf = x_ref[rows, :]
            ms = jnp.mean(xf * xf, axis=-1, keepdims=True)
            u_ref[rows, :] = ((xf * lax.rsqrt(ms + EPS)) * g).astype(BF16)
        zg_ref[...] = _dot(u_ref[...], wg_ref[...]) + bg_ref[...]

    z_ref[...] = _dot(u_ref[...], w_ref[...]) + b_ref[...]


def _proj(x2d, g, w, b, wg, bg):
    n = x2d.shape[0]
    grid = (n // PROJ_TM, D_MAIN // PROJ_TN)
    return pl.pallas_call(
        _proj_kernel,
        grid=grid,
        in_specs=[
            pl.BlockSpec((PROJ_TM, D_MODEL), lambda i, j: (i, 0)),
            pl.BlockSpec((1, D_MODEL), lambda i, j: (0, 0)),
            pl.BlockSpec((D_MODEL, PROJ_TN), lambda i, j: (0, j)),
            pl.BlockSpec((1, PROJ_TN), lambda i, j: (0, j)),
            pl.BlockSpec((D_MODEL, GATE_LANES), lambda i, j: (0, 0)),
            pl.BlockSpec((1, GATE_LANES), lambda i, j: (0, 0)),
        ],
        out_specs=[
            pl.BlockSpec((PROJ_TM, PROJ_TN), lambda i, j: (i, j)),
            pl.BlockSpec((PROJ_TM, GATE_LANES), lambda i, j: (i, 0)),
        ],
        out_shape=[
            jax.ShapeDtypeStruct((n, D_MAIN), F32),
            jax.ShapeDtypeStruct((n, GATE_LANES), F32),
        ],
        scratch_shapes=[pltpu.VMEM((PROJ_TM, D_MODEL), BF16)],
        compiler_params=pltpu.CompilerParams(
            dimension_semantics=("arbitrary", "arbitrary"), vmem_limit_bytes=VMEM_LIMIT),
        name="proj",
    )(x2d, g, w, b, wg, bg)


MEMKV_TN = 256


def _memkv_kernel(mem_ref, g_ref, w_ref, o_ref):
    xf = mem_ref[...]
    ms = jnp.mean(xf * xf, axis=-1, keepdims=True)
    u = ((xf * lax.rsqrt(ms + EPS)) * g_ref[...]).astype(BF16)
    o_ref[...] = _dot(u, w_ref[...])


def _memkv(mem2d, g, w):
    return pl.pallas_call(
        _memkv_kernel,
        grid=(2 * MEM_W // MEMKV_TN,),
        in_specs=[
            pl.BlockSpec((MEM_LEN, D_MODEL), lambda j: (0, 0)),
            pl.BlockSpec((1, D_MODEL), lambda j: (0, 0)),
            pl.BlockSpec((D_MODEL, MEMKV_TN), lambda j: (0, j)),
        ],
        out_specs=pl.BlockSpec((MEM_LEN, MEMKV_TN), lambda j: (0, j)),
        out_shape=jax.ShapeDtypeStruct((MEM_LEN, 2 * MEM_W), F32),
        compiler_params=pltpu.CompilerParams(
            dimension_semantics=("arbitrary",), vmem_limit_bytes=VMEM_LIMIT),
        name="memkv",
    )(mem2d, g, w)


OUT_TN = 256


def _rope(x, cos, sin_signed):
    return x * cos + pltpu.roll(x, HEAD_DIM // 2, axis=1) * sin_signed


IN_TN = 256
MIXER_WEIGHTS = ([1.0, 3.0, 1.0] * SWA_KV_HEADS + [1.0, 0.5] * MEM_HEADS
                 + [1.0] + [1.5, 1.0, 0.5] * ML_HEADS)


def _spread(tasks, weights):
    total = sum(weights)
    bounds = [0]
    acc = 0.0
    for w in weights:
        acc += w
        bounds.append(round(len(tasks) * acc / total))
    return [tasks[a:b] for a, b in zip(bounds[:-1], bounds[1:])]


def _alternate(a, b):
    keyed = ([((i + 0.5) / len(a), 0, t) for i, t in enumerate(a)]
             + [((j + 0.5) / len(b), 1, t) for j, t in enumerate(b)])
    return [t for _, _, t in sorted(keyed, key=lambda k: k[:2])]


def _prompt_kernel(sinks_ref, xin_ref, xres_ref, gn_ref, w_ref, b_ref, wg_ref, bg_ref,
                   rope_c_ref, rope_i_ref, mk_ref, mv_ref, wout_ref, gfin_ref,
                   out_ref, ko_ref, vo_ref, ct_ref, n_ref, m_ref,
                   kprev_ref, vprev_ref, u_ref, z_ref, zg_ref, zprev_ref, zgprev_ref,
                   y_ref, yprev_ref, acc_ref):
    s = pl.program_id(0)
    nc = pl.num_programs(0) - 2
    n_in = D_MAIN // IN_TN
    n_out = D_MODEL // OUT_TN

    def in_norm():
        xf = xin_ref[...]
        ms = jnp.mean(xf * xf, axis=-1, keepdims=True)
        u_ref[...] = ((xf * lax.rsqrt(ms + EPS)) * gn_ref[...]).astype(BF16)

    def in_tile(t):
        if t == n_in:
            zg_ref[...] = _dot(u_ref[...], wg_ref[...]) + bg_ref[...]
        else:
            cols = slice(t * IN_TN, (t + 1) * IN_TN)
            z_ref[:, cols] = _dot(u_ref[...], w_ref[:, cols]) + b_ref[:, cols]

    def in_rotate():
        zprev_ref[...] = z_ref[...]
        zgprev_ref[...] = zg_ref[...]

    def out_tile(t):
        cols = slice(t * OUT_TN, (t + 1) * OUT_TN)
        acc_ref[:, cols] = _dot(yprev_ref[...], wout_ref[:, cols]) + xres_ref[:, cols]

    def out_finish():
        acc = acc_ref[...]
        ms = jnp.mean(acc * acc, axis=-1, keepdims=True)
        out_ref[...] = (acc * lax.rsqrt(ms + EPS)) * gfin_ref[...]

    in_tasks = [functools.partial(in_tile, t) for t in range(n_in + 1)]
    out_tasks = [functools.partial(out_tile, t) for t in range(n_out)]

    def mix(tasks):
        groups = iter(_spread(tasks, MIXER_WEIGHTS))

        def between():
            for task in next(groups):
                task()

        _prompt_mixers(s - 1, sinks_ref, zprev_ref, zgprev_ref, rope_c_ref, rope_i_ref, mk_ref, mv_ref,
                       y_ref, ko_ref, vo_ref, ct_ref, n_ref, m_ref, kprev_ref, vprev_ref, between)
        assert next(groups, None) is None

    @pl.when(s == 0)
    def _():
        kprev_ref[...] = jnp.zeros_like(kprev_ref)
        vprev_ref[...] = jnp.zeros_like(vprev_ref)
        ct_ref[...] = jnp.zeros_like(ct_ref)
        n_ref[...] = jnp.zeros_like(n_ref)
        m_ref[...] = jnp.zeros_like(m_ref)
        in_norm()
        for task in in_tasks:
            task()
        in_rotate()

    @pl.when(s == 1)
    def _():
        in_norm()
        mix(in_tasks)
        in_rotate()
        yprev_ref[...] = y_ref[...]

    @pl.when((s >= 2) & (s < nc))
    def _():
        in_norm()
        mix(_alternate(in_tasks, out_tasks))
        out_finish()
        in_rotate()
        yprev_ref[...] = y_ref[...]

    @pl.when(s == nc)
    def _():
        mix(out_tasks)
        out_finish()
        yprev_ref[...] = y_ref[...]

    @pl.when(s == nc + 1)
    def _():
        for task in out_tasks:
            task()
        out_finish()


def _prompt_mixers(c, sinks_ref, z_ref, zg_ref, rope_c_ref, rope_i_ref, mk_ref, mv_ref,
                   y_ref, ko_ref, vo_ref, ct_ref, n_ref, m_ref, kprev_ref, vprev_ref, between):
    L = CHUNK
    scale = HEAD_DIM ** -0.5

    cc = rope_c_ref[0, pl.ds(c, 1), :]
    sc = rope_c_ref[1, pl.ds(c, 1), :]
    cos = cc * rope_i_ref[0] - sc * rope_i_ref[1]
    sin = sc * rope_i_ref[2] + cc * rope_i_ref[3]

    ri = lax.broadcasted_iota(jnp.int32, (GQA_GROUP * L, 2 * L), 0) & (L - 1)
    cj = lax.broadcasted_iota(jnp.int32, (GQA_GROUP * L, 2 * L), 1)
    j_low = jnp.where(c > 0, 0, L)
    band = (cj > ri) & (cj <= ri + L) & (cj >= j_low)
    for kv in range(SWA_KV_HEADS):
        between()
        k_new = _rope(z_ref[:, O_SK + kv * HEAD_DIM:O_SK + (kv + 1) * HEAD_DIM], cos, sin)
        v_new = z_ref[:, O_SV + kv * HEAD_DIM:O_SV + (kv + 1) * HEAD_DIM]
        ko_ref[:, kv * HEAD_DIM:(kv + 1) * HEAD_DIM] = k_new
        vo_ref[:, kv * HEAD_DIM:(kv + 1) * HEAD_DIM] = v_new
        k_new_b = k_new.astype(BF16)
        v_new_b = v_new.astype(BF16)
        kcat = jnp.concatenate([kprev_ref[:, kv * HEAD_DIM:(kv + 1) * HEAD_DIM], k_new_b], axis=0)
        vcat = jnp.concatenate([vprev_ref[:, kv * HEAD_DIM:(kv + 1) * HEAD_DIM], v_new_b], axis=0)
        kprev_ref[:, kv * HEAD_DIM:(kv + 1) * HEAD_DIM] = k_new_b
        vprev_ref[:, kv * HEAD_DIM:(kv + 1) * HEAD_DIM] = v_new_b
        qs = []
        sks = []
        for g in range(GQA_GROUP):
            h = kv * GQA_GROUP + g
            qs.append(_rope(z_ref[:, O_SQ + h * HEAD_DIM:O_SQ + (h + 1) * HEAD_DIM], cos, sin).astype(BF16))
            sks.append(jnp.full((L, 1), sinks_ref[h], F32))
        q_st = jnp.concatenate(qs, axis=0)
        sk = jnp.concatenate(sks, axis=0)
        s = _dot_nt(q_st, kcat)
        between()
        s = jnp.where(band, s * scale, -jnp.inf)
        m = jnp.maximum(jnp.max(s, axis=-1, keepdims=True), sk)
        p = jnp.exp(s - m)
        l = jnp.sum(p, axis=-1, keepdims=True) + jnp.exp(sk - m)
        o = _dot(p.astype(BF16), vcat)
        between()
        o = o * (1.0 / l)
        for g in range(GQA_GROUP):
            h = kv * GQA_GROUP + g
            gate = _silu(z_ref[:, O_SG + h * HEAD_DIM:O_SG + (h + 1) * HEAD_DIM])
            y_ref[:, h * HEAD_DIM:(h + 1) * HEAD_DIM] = (o[g * L:(g + 1) * L] * gate).astype(BF16)

    for h in range(MEM_HEADS):
        q = z_ref[:, O_CQ + h * HEAD_DIM:O_CQ + (h + 1) * HEAD_DIM].astype(BF16)
        s = _dot_nt(q, mk_ref[:, h * HEAD_DIM:(h + 1) * HEAD_DIM])
        between()
        s = s * scale
        m = jnp.max(s, axis=-1, keepdims=True)
        p = jnp.exp(s - m)
        l = jnp.sum(p, axis=-1, keepdims=True)
        o = _dot(p.astype(BF16), mv_ref[:, h * HEAD_DIM:(h + 1) * HEAD_DIM])
        between()
        o = o * (1.0 / l)
        gate = _silu(z_ref[:, O_CG + h * HEAD_DIM:O_CG + (h + 1) * HEAD_DIM])
        col = SWA_W + ML_W + h * HEAD_DIM
        y_ref[:, col:col + HEAD_DIM] = (o * gate).astype(BF16)

    lane = lax.broadcasted_iota(jnp.int32, (1, GATE_LANES), 1)
    zg = zg_ref[...]
    gates = jnp.where(lane < LANE_F, zg, _log_sigmoid(zg))
    gates = jnp.where(lane < 2 * ML_HEADS, gates, 0.0)
    rr = lax.broadcasted_iota(jnp.int32, (L, L), 0)
    cc = lax.broadcasted_iota(jnp.int32, (L, L), 1)
    causal = rr >= cc
    tri = jnp.where(causal, 1.0, 0.0).astype(BF16)
    csum = _exact_sel_dot(tri, gates)
    gates_t = gates.T
    csum_t = csum.T
    half_lo = lane < ML_DQK
    between()
    for h in range(ML_HEADS):
        pair, half = divmod(h, 2)
        hmask = half_lo if half == 0 else jnp.logical_not(half_lo)
        q_pair = z_ref[:, O_MQ + pair * 128:O_MQ + (pair + 1) * 128]
        k_pair = z_ref[:, O_MK + pair * 128:O_MK + (pair + 1) * 128] * (ML_DQK ** -0.5)
        v_h = z_ref[:, O_MV + h * ML_DV:O_MV + (h + 1) * ML_DV].astype(BF16)
        q_h = jnp.where(hmask, q_pair, 0.0)
        q_hb = q_h.astype(BF16)
        k_pb = k_pair.astype(BF16)

        bt_c = csum[:, LANE_F + h:LANE_F + h + 1]
        it_c = gates[:, LANE_I + h:LANE_I + h + 1]
        bt_r = csum_t[LANE_F + h:LANE_F + h + 1, :]
        it_r = gates_t[LANE_I + h:LANE_I + h + 1, :]
        m_prev = m_ref[:, h:h + 1]
        ct_pair = ct_ref[pair * 128:(pair + 1) * 128, :]
        qk = _dot_nt(q_hb, k_pb)
        state_read = _dot(q_hb, ct_pair.astype(BF16))
        between()

        dlog = jnp.where(causal, bt_c - bt_r + it_r, -jnp.inf)
        inter = bt_c + m_prev
        m_t = jnp.maximum(inter, jnp.max(dlog, axis=-1, keepdims=True))
        w_intra = jnp.exp(dlog - m_t)
        w_state = jnp.exp(inter - m_t)
        a = w_intra * qk
        intra = _dot(a.astype(BF16), v_h)
        between()
        num = intra + w_state * state_read
        n_pair = n_ref[:, pair * 128:(pair + 1) * 128]
        den = jnp.sum(a, axis=-1, keepdims=True) + w_state * jnp.sum(q_h * n_pair, axis=-1, keepdims=True)
        denom = jnp.maximum(jnp.abs(den), jnp.exp(-m_t))
        hid = num * (1.0 / denom)
        o_gate = _sigmoid(z_ref[:, O_MO + h * ML_DV:O_MO + (h + 1) * ML_DV])
        gate = _silu(z_ref[:, O_MG + h * ML_DV:O_MG + (h + 1) * ML_DV])
        col = SWA_W + h * ML_DV
        y_ref[:, col:col + ML_DV] = ((hid * o_gate) * gate).astype(BF16)

        m_new = m_t[L - 1:L, :]
        bt_last = bt_c[L - 1:L, :]
        w_s = jnp.exp(bt_last - bt_c + it_c - m_new)
        decay = jnp.exp(bt_last + m_prev - m_new)
        kw = k_pair * w_s
        upd = _dot(kw.T.astype(BF16), v_h)
        between()
        rows = slice(h * ML_DQK, (h + 1) * ML_DQK)
        ct_ref[rows, :] = decay * ct_ref[rows, :] + upd[half * ML_DQK:(half + 1) * ML_DQK, :]
        ksum = jnp.sum(kw, axis=0, keepdims=True)
        n_ref[:, rows] = decay * n_ref[:, rows] + ksum[:, half * ML_DQK:(half + 1) * ML_DQK]
        m_ref[:, h:h + 1] = m_new


def _prompt(sinks, x2d, g_norm, w, b, wg, bg, rope_c, rope_i, mk, mv, w_out, g_final):
    n = x2d.shape[0]
    nc = n // CHUNK
    const = lambda c, s: (0, 0)
    resident = pl.Buffered(1)
    x_in = lambda c, s: (jnp.minimum(c, nc - 1), 0)
    x_res = lambda c, s: (jnp.clip(c - 2, 0, nc - 1), 0)
    return pl.pallas_call(
        _prompt_kernel,
        grid_spec=pltpu.PrefetchScalarGridSpec(
            num_scalar_prefetch=1,
            grid=(nc + 2,),
            in_specs=[
                pl.BlockSpec((CHUNK, D_MODEL), x_in),
                pl.BlockSpec((CHUNK, D_MODEL), x_res),
                pl.BlockSpec((1, D_MODEL), const, pipeline_mode=resident),
                pl.BlockSpec((D_MODEL, D_MAIN), const, pipeline_mode=resident),
                pl.BlockSpec((1, D_MAIN), const, pipeline_mode=resident),
                pl.BlockSpec((D_MODEL, GATE_LANES), const, pipeline_mode=resident),
                pl.BlockSpec((1, GATE_LANES), const, pipeline_mode=resident),
                pl.BlockSpec((2, nc, HEAD_DIM), lambda c, s: (0, 0, 0), pipeline_mode=resident),
                pl.BlockSpec((4, CHUNK, HEAD_DIM), lambda c, s: (0, 0, 0), pipeline_mode=resident),
                pl.BlockSpec((MEM_LEN, MEM_W), const, pipeline_mode=resident),
                pl.BlockSpec((MEM_LEN, MEM_W), const, pipeline_mode=resident),
                pl.BlockSpec((D_MIX, D_MODEL), const, pipeline_mode=resident),
                pl.BlockSpec((1, D_MODEL), const, pipeline_mode=resident),
            ],
            out_specs=[
                pl.BlockSpec((CHUNK, D_MODEL), x_res),
                pl.BlockSpec((CHUNK, SWA_KV_W), const),
                pl.BlockSpec((CHUNK, SWA_KV_W), const),
                pl.BlockSpec((ML_QK_W, ML_DV), const),
                pl.BlockSpec((1, ML_QK_W), const),
                pl.BlockSpec((1, GATE_LANES), const),
            ],
            scratch_shapes=[
                pltpu.VMEM((CHUNK, SWA_KV_W), BF16),
                pltpu.VMEM((CHUNK, SWA_KV_W), BF16),
                pltpu.VMEM((CHUNK, D_MODEL), BF16),
                pltpu.VMEM((CHUNK, D_MAIN), F32),
                pltpu.VMEM((CHUNK, GATE_LANES), F32),
                pltpu.VMEM((CHUNK, D_MAIN), F32),
                pltpu.VMEM((CHUNK, GATE_LANES), F32),
                pltpu.VMEM((CHUNK, D_MIX), BF16),
                pltpu.VMEM((CHUNK, D_MIX), BF16),
                pltpu.VMEM((CHUNK, D_MODEL), F32),
            ],
        ),
        out_shape=[
            jax.ShapeDtypeStruct((n, D_MODEL), F32),
            jax.ShapeDtypeStruct((CHUNK, SWA_KV_W), F32),
            jax.ShapeDtypeStruct((CHUNK, SWA_KV_W), F32),
            jax.ShapeDtypeStruct((ML_QK_W, ML_DV), F32),
            jax.ShapeDtypeStruct((1, ML_QK_W), F32),
            jax.ShapeDtypeStruct((1, GATE_LANES), F32),
        ],
        compiler_params=pltpu.CompilerParams(
            dimension_semantics=("arbitrary",), vmem_limit_bytes=VMEM_LIMIT),
        name="prompt",
    )(sinks, x2d, x2d, g_norm, w, b, wg, bg, rope_c, rope_i, mk, mv, w_out, g_final)


OUT_TM = 512


def _outproj_kernel(y_ref, x_ref, w_ref, g_ref, o_ref):
    acc = _dot(y_ref[...], w_ref[...]) + x_ref[...]
    ms = jnp.mean(acc * acc, axis=-1, keepdims=True)
    o_ref[...] = (acc * lax.rsqrt(ms + EPS)) * g_ref[...]


def _outproj(y, x2d, w, g):
    n = x2d.shape[0]
    return pl.pallas_call(
        _outproj_kernel,
        grid=(n // OUT_TM,),
        in_specs=[
            pl.BlockSpec((OUT_TM, D_MIX), lambda i: (i, 0)),
            pl.BlockSpec((OUT_TM, D_MODEL), lambda i: (i, 0)),
            pl.BlockSpec((D_MIX, D_MODEL), lambda i: (0, 0)),
            pl.BlockSpec((1, D_MODEL), lambda i: (0, 0)),
        ],
        out_specs=pl.BlockSpec((OUT_TM, D_MODEL), lambda i: (i, 0)),
        out_shape=jax.ShapeDtypeStruct((n, D_MODEL), F32),
        compiler_params=pltpu.CompilerParams(
            dimension_semantics=("arbitrary",), vmem_limit_bytes=VMEM_LIMIT),
        name="outproj",
    )(y, x2d, w, g)


SB = 16
SR = SB * DEC_SEQ


def _mix_sample_kernel(sinks_ref, z_ref, zg_ref, cos_ref, sin_ref, ck_ref, cv_ref,
                       c_ref, nst_ref, nrep_ref, mrep_ref,
                       y_ref, ko_ref, vo_ref, co_ref, no_ref, mo_ref,
                       q_s, k_s, o_s):
    T = DEC_SEQ
    scale = HEAD_DIM ** -0.5
    cos = cos_ref[...]
    sin = sin_ref[...]

    for h in range(SWA_HEADS):
        q_s[:, h * HEAD_DIM:(h + 1) * HEAD_DIM] = _rope(
            z_ref[:, O_SQ + h * HEAD_DIM:O_SQ + (h + 1) * HEAD_DIM], cos, sin)
    for kv in range(SWA_KV_HEADS):
        k_s[:, kv * HEAD_DIM:(kv + 1) * HEAD_DIM] = _rope(
            z_ref[:, O_SK + kv * HEAD_DIM:O_SK + (kv + 1) * HEAD_DIM], cos, sin)

    KP = 2 * WINDOW
    rt = lax.broadcasted_iota(jnp.int32, (GQA_GROUP * T, KP), 0) & (T - 1)
    cj = lax.broadcasted_iota(jnp.int32, (GQA_GROUP * T, KP), 1)
    mask = (cj > rt) & (cj <= rt + WINDOW)
    kpad = jnp.zeros((KP - WINDOW - T, HEAD_DIM), F32)
    NKV = SWA_KV_HEADS
    SEQ_ROWS = WINDOW * NKV

    def per_seq(b, carry):
        rows = pl.ds(pl.multiple_of(b * T, T), T)
        base = pl.multiple_of(b * SEQ_ROWS, SEQ_ROWS)
        k_new = k_s[rows, :]
        v_new = z_ref[rows, O_SV:O_SV + SWA_KV_W]
        keep = SEQ_ROWS - T * NKV
        ko_ref[pl.ds(base, keep), :] = ck_ref[pl.ds(base + T * NKV, keep), :]
        vo_ref[pl.ds(base, keep), :] = cv_ref[pl.ds(base + T * NKV, keep), :]
        for kv in range(SWA_KV_HEADS):
            cols = slice(kv * HEAD_DIM, (kv + 1) * HEAD_DIM)
            ko_ref[pl.ds(base + keep + kv, T, stride=NKV), :] = k_new[:, cols]
            vo_ref[pl.ds(base + keep + kv, T, stride=NKV), :] = v_new[:, cols]
            kc = ck_ref[pl.ds(base + kv, WINDOW, stride=NKV), :]
            vc = cv_ref[pl.ds(base + kv, WINDOW, stride=NKV), :]
            k_all = jnp.concatenate([kc, k_new[:, cols], kpad], axis=0).astype(BF16)
            v_all = jnp.concatenate([vc, v_new[:, cols], kpad], axis=0).astype(BF16)
            qs = []
            sks = []
            for g in range(GQA_GROUP):
                h = kv * GQA_GROUP + g
                qs.append(q_s[rows, h * HEAD_DIM:(h + 1) * HEAD_DIM])
                sks.append(jnp.full((T, 1), sinks_ref[h], F32))
            q_st = jnp.concatenate(qs, axis=0).astype(BF16)
            sk = jnp.concatenate(sks, axis=0)
            s = jnp.where(mask, _dot_nt(q_st, k_all) * scale, -jnp.inf)
            m = jnp.maximum(jnp.max(s, axis=-1, keepdims=True), sk)
            p = jnp.exp(s - m)
            l = jnp.sum(p, axis=-1, keepdims=True) + jnp.exp(sk - m)
            o = _dot(p.astype(BF16), v_all) * (1.0 / l)
            for g in range(GQA_GROUP):
                h = kv * GQA_GROUP + g
                o_s[rows, h * HEAD_DIM:(h + 1) * HEAD_DIM] = o[g * T:(g + 1) * T]
        return carry

    lax.fori_loop(0, SB, per_seq, 0)

    for h in range(SWA_HEADS):
        cols = slice(h * HEAD_DIM, (h + 1) * HEAD_DIM)
        gate = _silu(z_ref[:, O_SG + h * HEAD_DIM:O_SG + (h + 1) * HEAD_DIM])
        y_ref[:, cols] = (o_s[:, cols] * gate).astype(BF16)

    R = SR
    lane = lax.broadcasted_iota(jnp.int32, (1, GATE_LANES), 1)
    zg = zg_ref[...]
    gates = jnp.where(lane < LANE_F, zg, _log_sigmoid(zg))
    gates = jnp.where(lane < 2 * ML_HEADS, gates, 0.0)
    rr = lax.broadcasted_iota(jnp.int32, (R, R), 0)
    cc = lax.broadcasted_iota(jnp.int32, (R, R), 1)
    same_seq = (rr >> 3) == (cc >> 3)
    causal = same_seq & (rr >= cc)
    tri = jnp.where(causal, 1.0, 0.0).astype(BF16)
    csum = _exact_sel_dot(tri, gates)
    gates_t = gates.T
    csum_t = csum.T
    half_lo = lane < ML_DQK
    seq_of_col = lax.broadcasted_iota(jnp.int32, (SB, 1, R), 2) >> 3
    seq_id = lax.broadcasted_iota(jnp.int32, (SB, 1, R), 0)
    own_cols = seq_of_col == seq_id
    own_blk = ((lax.broadcasted_iota(jnp.int32, (R, SB * 128), 0) >> 3)
               == (lax.broadcasted_iota(jnp.int32, (R, SB * 128), 1) >> 7))
    k_t = [(z_ref[:, O_MK + p * 128:O_MK + (p + 1) * 128] * (ML_DQK ** -0.5)).T for p in range(2)]
    mrep = mrep_ref[...]
    tok3 = lax.broadcasted_iota(jnp.int32, (SB, T, 1), 1)
    mo_ref[...] = jnp.zeros_like(mo_ref)

    def per_seq_value(col):
        return jnp.max(col.reshape(SB, T, 1), axis=1, keepdims=True)

    def last_of_seq(col):
        c3 = jnp.where(tok3 == T - 1, col.reshape(SB, T, 1), -jnp.inf)
        return jnp.broadcast_to(jnp.max(c3, axis=1, keepdims=True), (SB, T, 1)).reshape(R, 1)

    for h in range(ML_HEADS):
        pair, half = divmod(h, 2)
        hmask = half_lo if half == 0 else jnp.logical_not(half_lo)
        q_pair = z_ref[:, O_MQ + pair * 128:O_MQ + (pair + 1) * 128]
        k_pair = z_ref[:, O_MK + pair * 128:O_MK + (pair + 1) * 128] * (ML_DQK ** -0.5)
        v_f = z_ref[:, O_MV + h * ML_DV:O_MV + (h + 1) * ML_DV]
        v_h = v_f.astype(BF16)
        q_h = jnp.where(hmask, q_pair, 0.0)
        q_hb = q_h.astype(BF16)
        k_pb = k_pair.astype(BF16)

        bt_c = csum[:, LANE_F + h:LANE_F + h + 1]
        it_c = gates[:, LANE_I + h:LANE_I + h + 1]
        bt_r = csum_t[LANE_F + h:LANE_F + h + 1, :]
        it_r = gates_t[LANE_I + h:LANE_I + h + 1, :]
        m_prev = mrep[:, h:h + 1]

        dlog = jnp.where(causal, bt_c - bt_r + it_r, -jnp.inf)
        inter = bt_c + m_prev
        m_t = jnp.maximum(inter, jnp.max(dlog, axis=-1, keepdims=True))
        w_intra = jnp.exp(dlog - m_t)
        w_state = jnp.exp(inter - m_t)
        a = w_intra * _dot_nt(q_hb, k_pb)

        ct_st = c_ref[:, 2 * pair:2 * pair + 2].reshape(SB * 128, ML_DV).astype(BF16)
        q_blk = jnp.where(own_blk, jnp.tile(q_h, (1, SB)), 0.0).astype(BF16)
        num_state = _dot(q_blk, ct_st)

        num = _dot(a.astype(BF16), v_h) + w_state * num_state
        n_pair = nrep_ref[:, pair * 128:(pair + 1) * 128]
        den = jnp.sum(a, axis=-1, keepdims=True) + w_state * jnp.sum(q_h * n_pair, axis=-1, keepdims=True)
        denom = jnp.maximum(jnp.abs(den), jnp.exp(-m_t))
        hid = num * (1.0 / denom)
        o_gate = _sigmoid(z_ref[:, O_MO + h * ML_DV:O_MO + (h + 1) * ML_DV])
        gate = _silu(z_ref[:, O_MG + h * ML_DV:O_MG + (h + 1) * ML_DV])
        col = SWA_W + h * ML_DV
        y_ref[:, col:col + ML_DV] = ((hid * o_gate) * gate).astype(BF16)

        m_new = last_of_seq(m_t)
        bt_last = last_of_seq(bt_c)
        w_s = jnp.exp(bt_last - bt_c + it_c - m_new)
        decay = jnp.exp(bt_last + m_prev - m_new)
        decay_seq = per_seq_value(decay)

        kt_h = k_t[pair][half * ML_DQK:(half + 1) * ML_DQK, :]
        lhs = jnp.where(own_cols, kt_h[None, :, :], 0.0).reshape(SB * ML_DQK, R).astype(BF16)
        upd = _dot(lhs, (v_f * w_s).astype(BF16)).reshape(SB, ML_DQK, ML_DV)
        co_ref[:, h] = decay_seq * c_ref[:, h] + upd

        kw = (k_pair * w_s).reshape(SB, T, 128)
        ksum = jnp.sum(kw, axis=1)
        n_old = nst_ref[:, h * ML_DQK:(h + 1) * ML_DQK]
        dec2 = decay_seq.reshape(SB, 1)
        no_ref[:, h * ML_DQK:(h + 1) * ML_DQK] = dec2 * n_old + ksum[:, half * ML_DQK:(half + 1) * ML_DQK]
        mo_ref[:, h:h + 1] = per_seq_value(m_new).reshape(SB, 1)


def _mix_sample(sinks, z, zg, cos, sin, ck, cv, cst, nst, nrep, mrep):
    nb = cst.shape[0]
    steps = nb // SB
    const = lambda i, s: (0, 0)
    cache_rows = SB * WINDOW * SWA_KV_HEADS
    return pl.pallas_call(
        _mix_sample_kernel,
        grid_spec=pltpu.PrefetchScalarGridSpec(
            num_scalar_prefetch=1,
            grid=(steps,),
            in_specs=[
                pl.BlockSpec((SR, D_MAIN), lambda i, s: (i, 0)),
                pl.BlockSpec((SR, GATE_LANES), lambda i, s: (i, 0)),
                pl.BlockSpec((SR, HEAD_DIM), const),
                pl.BlockSpec((SR, HEAD_DIM), const),
                pl.BlockSpec((cache_rows, HEAD_DIM), lambda i, s: (i, 0)),
                pl.BlockSpec((cache_rows, HEAD_DIM), lambda i, s: (i, 0)),
                pl.BlockSpec((SB, ML_HEADS, ML_DQK, ML_DV), lambda i, s: (i, 0, 0, 0)),
                pl.BlockSpec((SB, ML_QK_W), lambda i, s: (i, 0)),
                pl.BlockSpec((SR, ML_QK_W), lambda i, s: (i, 0)),
                pl.BlockSpec((SR, GATE_LANES), lambda i, s: (i, 0)),
            ],
            out_specs=[
                pl.BlockSpec((SR, SWA_W + ML_W), lambda i, s: (i, 0)),
                pl.BlockSpec((cache_rows, HEAD_DIM), lambda i, s: (i, 0)),
                pl.BlockSpec((cache_rows, HEAD_DIM), lambda i, s: (i, 0)),
                pl.BlockSpec((SB, ML_HEADS, ML_DQK, ML_DV), lambda i, s: (i, 0, 0, 0)),
                pl.BlockSpec((SB, ML_QK_W), lambda i, s: (i, 0)),
                pl.BlockSpec((SB, GATE_LANES), lambda i, s: (i, 0)),
            ],
            scratch_shapes=[
                pltpu.VMEM((SR, SWA_W), F32),
                pltpu.VMEM((SR, SWA_KV_W), F32),
                pltpu.VMEM((SR, SWA_W), F32),
            ],
        ),
        out_shape=[
            jax.ShapeDtypeStruct((nb * DEC_SEQ, SWA_W + ML_W), BF16),
            jax.ShapeDtypeStruct(ck.shape, F32),
            jax.ShapeDtypeStruct(cv.shape, F32),
            jax.ShapeDtypeStruct((nb, ML_HEADS, ML_DQK, ML_DV), F32),
            jax.ShapeDtypeStruct((nb, ML_QK_W), F32),
            jax.ShapeDtypeStruct((nb, GATE_LANES), F32),
        ],
        compiler_params=pltpu.CompilerParams(
            dimension_semantics=("arbitrary",), vmem_limit_bytes=VMEM_LIMIT),
        name="mix_sample",
    )(sinks, z, zg, cos, sin, ck, cv, cst, nst, nrep, mrep)


MB = 8
MR = MB * DEC_SEQ


def _mem_sample_kernel(cq_ref, cg_ref, mk_ref, mv_ref, y_ref, o_s):
    T = DEC_SEQ
    scale = HEAD_DIM ** -0.5
    zpad = jnp.zeros((T, HEAD_DIM), F32)

    seq_rows = MEM_LEN * MEM_HEADS

    def per_seq(b, carry):
        rows = pl.ds(pl.multiple_of(b * T, T), T)
        base = pl.multiple_of(b * seq_rows, seq_rows)
        for h in range(MEM_HEADS):
            cols = slice(h * HEAD_DIM, (h + 1) * HEAD_DIM)
            kb = mk_ref[pl.ds(base + h, MEM_LEN, stride=MEM_HEADS), :].astype(BF16)
            vb = mv_ref[pl.ds(base + h, MEM_LEN, stride=MEM_HEADS), :].astype(BF16)
            q = jnp.concatenate([cq_ref[rows, cols], zpad], axis=0).astype(BF16)
            s = _dot_nt(q, kb) * scale
            m = jnp.max(s, axis=-1, keepdims=True)
            p = jnp.exp(s - m)
            l = jnp.sum(p, axis=-1, keepdims=True)
            o = _dot(p.astype(BF16), vb) * (1.0 / l)
            o_s[rows, cols] = o[0:T]
        return carry

    lax.fori_loop(0, MB, per_seq, 0)
    y_ref[...] = (o_s[...] * _silu(cg_ref[...])).astype(BF16)


def _mem_sample(z, mk, mv):
    nb = z.shape[0] // DEC_SEQ
    cache_rows = MB * MEM_LEN * MEM_HEADS
    return pl.pallas_call(
        _mem_sample_kernel,
        grid=(nb // MB,),
        in_specs=[
            pl.BlockSpec((MR, MEM_W), lambda i: (i, O_CQ // MEM_W)),
            pl.BlockSpec((MR, MEM_W), lambda i: (i, O_CG // MEM_W)),
            pl.BlockSpec((cache_rows, HEAD_DIM), lambda i: (i, 0)),
            pl.BlockSpec((cache_rows, HEAD_DIM), lambda i: (i, 0)),
        ],
        out_specs=pl.BlockSpec((MR, MEM_W), lambda i: (i, 0)),
        out_shape=jax.ShapeDtypeStruct((nb * DEC_SEQ, MEM_W), BF16),
        scratch_shapes=[pltpu.VMEM((MR, MEM_W), F32)],
        compiler_params=pltpu.CompilerParams(
            dimension_semantics=("arbitrary",), vmem_limit_bytes=VMEM_LIMIT),
        name="mem_sample",
    )(z, z, mk, mv)


def _rope_cos_sin(pos):
    half = HEAD_DIM // 2
    inv = jnp.power(ROPE_THETA, -(jnp.arange(half, dtype=F32) * 2.0 / HEAD_DIM))
    ang = pos.astype(F32)[:, None] * inv[None, :]
    return jnp.cos(ang), jnp.sin(ang)


def _rope_tables(pos):
    cos, sin = _rope_cos_sin(pos)
    return jnp.concatenate([cos, cos], axis=-1), jnp.concatenate([-sin, sin], axis=-1)


def _rope_split_tables(n_chunks):
    ca, sa = _rope_cos_sin(jnp.arange(n_chunks, dtype=jnp.int32) * CHUNK)
    cb, sb = _rope_cos_sin(jnp.arange(CHUNK, dtype=jnp.int32))
    dup = lambda t: jnp.concatenate([t, t], axis=-1)
    sgn = lambda t: jnp.concatenate([-t, t], axis=-1)
    return jnp.stack([dup(ca), dup(sa)]), jnp.stack([dup(cb), dup(sb), sgn(cb), sgn(sb)])


def _relayout_in_proj(w_in, b_in):
    def main(a):
        return jnp.concatenate([a[..., _R_SQ:_R_MI], a[..., _R_MO:_R_END]], axis=-1)

    def gates(a):
        pad = jnp.zeros(a.shape[:-1] + (GATE_LANES - 2 * ML_HEADS,), a.dtype)
        return jnp.concatenate([a[..., _R_MI:_R_MO], pad], axis=-1)

    return (main(w_in).astype(BF16), main(b_in)[None, :],
            gates(w_in).astype(BF16), gates(b_in)[None, :])


def _layer(xp, xs, mem, ck, cv, c_st, n_st, m_st, cmk, cmv,
           g_norm, w_in, b_in, sinks, g_mem, w_mem_kv, w_out, g_final):
    bp, sp, _ = xp.shape
    bs, ts, _ = xs.shape
    xp2 = xp.reshape(bp * sp, D_MODEL)
    xs2 = xs.reshape(bs * ts, D_MODEL)
    w_main, b_main, w_gate, b_gate = _relayout_in_proj(w_in, b_in)
    g_norm2 = g_norm[None, :]
    w_out_b = w_out.astype(BF16)
    g_final2 = g_final[None, :]
    sinks_flat = sinks.reshape(SWA_HEADS)

    memkv = _memkv(mem.reshape(MEM_LEN, D_MODEL), g_mem[None, :], w_mem_kv.astype(BF16))
    mem_k = memkv[:, :MEM_W]
    mem_v = memkv[:, MEM_W:]
    rope_c, rope_i = _rope_split_tables(sp // CHUNK)
    out_p, pk, pv, ct, n_p, m_p = _prompt(sinks_flat, xp2, g_norm2, w_main, b_main, w_gate, b_gate,
                                           rope_c, rope_i, mem_k.astype(BF16), mem_v.astype(BF16),
                                           w_out_b, g_final2)

    zs, zgs = _proj(xs2, g_norm2, w_main, b_main, w_gate, b_gate)
    cos_s, sin_s = _rope_tables(PAST_LEN + jnp.arange(ts, dtype=jnp.int32))
    cos_s = jnp.tile(cos_s, (SB, 1))
    sin_s = jnp.tile(sin_s, (SB, 1))
    nrep = jnp.repeat(n_st.reshape(bs, ML_QK_W), ts, axis=0)
    m_pad = jnp.pad(m_st, ((0, 0), (0, GATE_LANES - ML_HEADS)))
    mrep = jnp.repeat(m_pad, ts, axis=0)
    ya, sk_o, sv_o, ct_o, n_o, m_o = _mix_sample(
        sinks_flat, zs, zgs, cos_s, sin_s,
        ck.reshape(bs * WINDOW * SWA_KV_HEADS, HEAD_DIM), cv.reshape(bs * WINDOW * SWA_KV_HEADS, HEAD_DIM),
        jnp.swapaxes(c_st, -1, -2), n_st.reshape(bs, ML_QK_W), nrep, mrep)
    yb = _mem_sample(zs, cmk.reshape(bs * MEM_LEN * MEM_HEADS, HEAD_DIM),
                     cmv.reshape(bs * MEM_LEN * MEM_HEADS, HEAD_DIM))
    out_s = _outproj(jnp.concatenate([ya, yb], axis=-1), xs2, w_out_b, g_final2)
    c_o = jnp.swapaxes(ct_o, -1, -2)

    p_state = (
        pk.reshape(bp, WINDOW, SWA_KV_HEADS, HEAD_DIM),
        pv.reshape(bp, WINDOW, SWA_KV_HEADS, HEAD_DIM),
        ct.reshape(ML_HEADS, ML_DQK, ML_DV).transpose(0, 2, 1)[None],
        n_p.reshape(bp, ML_HEADS, ML_DQK),
        m_p[:, :ML_HEADS],
        mem_k.reshape(bp, MEM_LEN, MEM_HEADS, HEAD_DIM),
        mem_v.reshape(bp, MEM_LEN, MEM_HEADS, HEAD_DIM),
    )
    s_state = (
        sk_o.reshape(bs, WINDOW, SWA_KV_HEADS, HEAD_DIM),
        sv_o.reshape(bs, WINDOW, SWA_KV_HEADS, HEAD_DIM),
        c_o,
        n_o.reshape(bs, ML_HEADS, ML_DQK),
        m_o[:, :ML_HEADS],
    )
    return out_p.reshape(bp, sp, D_MODEL), out_s.reshape(bs, ts, D_MODEL), p_state, s_state


def kernel(x_prompt, x_sample, mem_prompt, cache_swa_k, cache_swa_v, state_mlstm_C, state_mlstm_n,
           state_mlstm_m, cache_mem_k, cache_mem_v, g_norm, w_in, b_in, swa_sinks, g_mem, w_mem_kv,
           w_out, g_final):
    depth = g_norm.shape[0]
    assert depth == 1 and x_prompt.shape[0] == 1
    y_p, y_s, p_state, s_state = _layer(
        x_prompt, x_sample, mem_prompt, cache_swa_k[0], cache_swa_v[0], state_mlstm_C[0],
        state_mlstm_n[0], state_mlstm_m[0], cache_mem_k[0], cache_mem_v[0],
        g_norm[0], w_in[0], b_in[0], swa_sinks[0], g_mem[0], w_mem_kv[0], w_out[0], g_final)
    return (y_p, y_s) + tuple(s[None] for s in p_state) + tuple(s[None] for s in s_state)
```

```python
import functools
import math

import jax
import jax.numpy as jnp
from jax import lax
from jax.experimental import pallas as pl
from jax.experimental.pallas import tpu as pltpu

F32 = jnp.float32
BF16 = jnp.bfloat16

D_MODEL = 2048
HEAD_DIM = 128
SWA_HEADS = 8
SWA_KV_HEADS = 2
GQA_GROUP = 4
WINDOW = 128
ML_HEADS = 4
ML_DQK = 64
ML_DV = 128
MEM_HEADS = 4
MEM_LEN = 256
CHUNK = 128
ROPE_THETA = 10000.0
EPS = 1e-6
PAST_LEN = 16384
DEC_SEQ = 8

SWA_W = SWA_HEADS * HEAD_DIM
SWA_KV_W = SWA_KV_HEADS * HEAD_DIM
ML_W = ML_HEADS * ML_DV
ML_QK_W = ML_HEADS * ML_DQK
MEM_W = MEM_HEADS * HEAD_DIM
D_MIX = SWA_W + ML_W + MEM_W

_IN_WIDTHS = (SWA_W, SWA_KV_W, SWA_KV_W, SWA_W, ML_QK_W, ML_QK_W, ML_W, ML_HEADS, ML_HEADS, ML_W, ML_W, MEM_W, MEM_W)
_IN_OFFS = [0]
for _w in _IN_WIDTHS:
    _IN_OFFS.append(_IN_OFFS[-1] + _w)
(_R_SQ, _R_SK, _R_SV, _R_SG, _R_MQ, _R_MK, _R_MV, _R_MI, _R_MF, _R_MO, _R_MG, _R_CQ, _R_CG, _R_END) = _IN_OFFS

O_SQ = 0
O_SK = O_SQ + SWA_W
O_SV = O_SK + SWA_KV_W
O_SG = O_SV + SWA_KV_W
O_MQ = O_SG + SWA_W
O_MK = O_MQ + ML_QK_W
O_MV = O_MK + ML_QK_W
O_MO = O_MV + ML_W
O_MG = O_MO + ML_W
O_CQ = O_MG + ML_W
O_CG = O_CQ + MEM_W
D_MAIN = O_CG + MEM_W
GATE_LANES = 128
LANE_I = 0
LANE_F = ML_HEADS

VMEM_LIMIT = 56 * 1024 * 1024

_NT = (((1,), (1,)), ((), ()))


def _dot(a, b):
    return jnp.dot(a, b, preferred_element_type=F32)


def _dot_nt(a, b):
    return lax.dot_general(a, b, _NT, preferred_element_type=F32)


def _exact_sel_dot(sel_bf16, x):
    hi = x.astype(BF16)
    r1 = x - hi.astype(F32)
    mid = r1.astype(BF16)
    lo = (r1 - mid.astype(F32)).astype(BF16)
    return _dot(sel_bf16, hi) + _dot(sel_bf16, mid) + _dot(sel_bf16, lo)


def _silu(x):
    return x * (1.0 / (1.0 + jnp.exp(-x)))


def _sigmoid(x):
    return 1.0 / (1.0 + jnp.exp(-x))


def _log_sigmoid(x):
    return jnp.minimum(x, 0.0) - jnp.log1p(jnp.exp(-jnp.abs(x)))


PROJ_TM = 1024
PROJ_TN = 512
NORM_ROWS = 256


def _proj_kernel(x_ref, g_ref, w_ref, b_ref, wg_ref, bg_ref, z_ref, zg_ref, u_ref):
    j = pl.program_id(1)

    @pl.when(j == 0)
    def _():
        g = g_ref[...]
        for r in range(PROJ_TM // NORM_ROWS):
            rows = pl.ds(r * NORM_ROWS, NORM_ROWS)
            xf = x_ref[rows, :]
            ms = jnp.mean(xf * xf, axis=-1, keepdims=True)
            u_ref[rows, :] = ((xf * lax.rsqrt(ms + EPS)) * g).astype(BF16)
        zg_ref[...] = _dot(u_ref[...], wg_ref[...]) + bg_ref[...]

    z_ref[...] = _dot(u_ref[...], w_ref[...]) + b_ref[...]


def _proj(x2d, g, w, b, wg, bg):
    n = x2d.shape[0]
    grid = (n // PROJ_TM, D_MAIN // PROJ_TN)
    return pl.pallas_call(
        _proj_kernel,
        grid=grid,
        in_specs=[
            pl.BlockSpec((PROJ_TM, D_MODEL), lambda i, j: (i, 0)),
            pl.BlockSpec((1, D_MODEL), lambda i, j: (0, 0)),
            pl.BlockSpec((D_MODEL, PROJ_TN), lambda i, j: (0, j)),
            pl.BlockSpec((1, PROJ_TN), lambda i, j: (0, j)),
            pl.BlockSpec((D_MODEL, GATE_LANES), lambda i, j: (0, 0)),
            pl.BlockSpec((1, GATE_LANES), lambda i, j: (0, 0)),
        ],
        out_specs=[
            pl.BlockSpec((PROJ_TM, PROJ_TN), lambda i, j: (i, j)),
            pl.BlockSpec((PROJ_TM, GATE_LANES), lambda i, j: (i, 0)),
        ],
        out_shape=[
            jax.ShapeDtypeStruct((n, D_MAIN), F32),
            jax.ShapeDtypeStruct((n, GATE_LANES), F32),
        ],
        scratch_shapes=[pltpu.VMEM((PROJ_TM, D_MODEL), BF16)],
        compiler_params=pltpu.CompilerParams(
            dimension_semantics=("arbitrary", "arbitrary"), vmem_limit_bytes=VMEM_LIMIT),
        name="proj",
    )(x2d, g, w, b, wg, bg)


MEMKV_TN = 256


def _memkv_kernel(mem_ref, g_ref, w_ref, o_ref):
    xf = mem_ref[...]
    ms = jnp.mean(xf * xf, axis=-1, keepdims=True)
    u = ((xf * lax.rsqrt(ms + EPS)) * g_ref[...]).astype(BF16)
    o_ref[...] = _dot(u, w_ref[...])


def _memkv(mem2d, g, w):
    return pl.pallas_call(
        _memkv_kernel,
        grid=(2 * MEM_W // MEMKV_TN,),
        in_specs=[
            pl.BlockSpec((MEM_LEN, D_MODEL), lambda j: (0, 0)),
            pl.BlockSpec((1, D_MODEL), lambda j: (0, 0)),
            pl.BlockSpec((D_MODEL, MEMKV_TN), lambda j: (0, j)),
        ],
        out_specs=pl.BlockSpec((MEM_LEN, MEMKV_TN), lambda j: (0, j)),
        out_shape=jax.ShapeDtypeStruct((MEM_LEN, 2 * MEM_W), F32),
        compiler_params=pltpu.CompilerParams(
            dimension_semantics=("arbitrary",), vmem_limit_bytes=VMEM_LIMIT),
        name="memkv",
    )(mem2d, g, w)


OUT_TN = 256


def _rope(x, cos, sin_signed):
    return x * cos + pltpu.roll(x, HEAD_DIM // 2, axis=1) * sin_signed


IN_TN = 256
MIXER_WEIGHTS = ([1.0, 3.0, 1.0] * SWA_KV_HEADS + [1.0, 0.5] * MEM_HEADS
                 + [1.0] + [1.5, 1.0, 0.5] * ML_HEADS)


def _spread(tasks, weights):
    total = sum(weights)
    bounds = [0]
    acc = 0.0
    for w in weights:
        acc += w
        bounds.append(round(len(tasks) * acc / total))
    return [tasks[a:b] for a, b in zip(bounds[:-1], bounds[1:])]


def _alternate(a, b):
    keyed = ([((i + 0.5) / len(a), 0, t) for i, t in enumerate(a)]
             + [((j + 0.5) / len(b), 1, t) for j, t in enumerate(b)])
    return [t for _, _, t in sorted(keyed, key=lambda k: k[:2])]


def _prompt_kernel(sinks_ref, xin_ref, xres_ref, gn_ref, w_ref, b_ref, wg_ref, bg_ref,
                   rope_c_ref, rope_i_ref, mk_ref, mv_ref, wout_ref, gfin_ref,
                   out_ref, ko_ref, vo_ref, ct_ref, n_ref, m_ref,
                   kprev_ref, vprev_ref, u_ref, z_ref, zg_ref, zprev_ref, zgprev_ref,
                   y_ref, yprev_ref, acc_ref):
    s = pl.program_id(0)
    nc = pl.num_programs(0) - 2
    n_in = D_MAIN // IN_TN
    n_out = D_MODEL // OUT_TN

    def in_norm():
        xf = xin_ref[...]
        ms = jnp.mean(xf * xf, axis=-1, keepdims=True)
        u_ref[...] = ((xf * lax.rsqrt(ms + EPS)) * gn_ref[...]).astype(BF16)

    def in_tile(t):
        if t == n_in:
            zg_ref[...] = _dot(u_ref[...], wg_ref[...]) + bg_ref[...]
        else:
            cols = slice(t * IN_TN, (t + 1) * IN_TN)
            z_ref[:, cols] = _dot(u_ref[...], w_ref[:, cols]) + b_ref[:, cols]

    def in_rotate():
        zprev_ref[...] = z_ref[...]
        zgprev_ref[...] = zg_ref[...]

    def out_tile(t):
        cols = slice(t * OUT_TN, (t + 1) * OUT_TN)
        acc_ref[:, cols] = _dot(yprev_ref[...], wout_ref[:, cols]) + xres_ref[:, cols]

    def out_finish():
        acc = acc_ref[...]
        ms = jnp.mean(acc * acc, axis=-1, keepdims=True)
        out_ref[...] = (acc * lax.rsqrt(ms + EPS)) * gfin_ref[...]

    in_tasks = [functools.partial(in_tile, t) for t in range(n_in + 1)]
    out_tasks = [functools.partial(out_tile, t) for t in range(n_out)]

    def mix(tasks):
        groups = iter(_spread(tasks, MIXER_WEIGHTS))

        def between():
            for task in next(groups):
                task()

        _prompt_mixers(s - 1, sinks_ref, zprev_ref, zgprev_ref, rope_c_ref, rope_i_ref, mk_ref, mv_ref,
                       y_ref, ko_ref, vo_ref, ct_ref, n_ref, m_ref, kprev_ref, vprev_ref, between)
        assert next(groups, None) is None

    @pl.when(s == 0)
    def _():
        kprev_ref[...] = jnp.zeros_like(kprev_ref)
        vprev_ref[...] = jnp.zeros_like(vprev_ref)
        ct_ref[...] = jnp.zeros_like(ct_ref)
        n_ref[...] = jnp.zeros_like(n_ref)
        m_ref[...] = jnp.zeros_like(m_ref)
        in_norm()
        for task in in_tasks:
            task()
        in_rotate()

    @pl.when(s == 1)
    def _():
        in_norm()
        mix(in_tasks)
        in_rotate()
        yprev_ref[...] = y_ref[...]

    @pl.when((s >= 2) & (s < nc))
    def _():
        in_norm()
        mix(_alternate(in_tasks, out_tasks))
        out_finish()
        in_rotate()
        yprev_ref[...] = y_ref[...]

    @pl.when(s == nc)
    def _():
        mix(out_tasks)
        out_finish()
        yprev_ref[...] = y_ref[...]

    @pl.when(s == nc + 1)
    def _():
        for task in out_tasks:
            task()
        out_finish()


def _prompt_mixers(c, sinks_ref, z_ref, zg_ref, rope_c_ref, rope_i_ref, mk_ref, mv_ref,
                   y_ref, ko_ref, vo_ref, ct_ref, n_ref, m_ref, kprev_ref, vprev_ref, between):
    L = CHUNK
    scale = HEAD_DIM ** -0.5

    cc = rope_c_ref[0, pl.ds(c, 1), :]
    sc = rope_c_ref[1, pl.ds(c, 1), :]
    cos = cc * rope_i_ref[0] - sc * rope_i_ref[1]
    sin = sc * rope_i_ref[2] + cc * rope_i_ref[3]

    ri = lax.broadcasted_iota(jnp.int32, (GQA_GROUP * L, 2 * L), 0) & (L - 1)
    cj = lax.broadcasted_iota(jnp.int32, (GQA_GROUP * L, 2 * L), 1)
    j_low = jnp.where(c > 0, 0, L)
    band = (cj > ri) & (cj <= ri + L) & (cj >= j_low)
    for kv in range(SWA_KV_HEADS):
        between()
        k_new = _rope(z_ref[:, O_SK + kv * HEAD_DIM:O_SK + (kv + 1) * HEAD_DIM], cos, sin)
        v_new = z_ref[:, O_SV + kv * HEAD_DIM:O_SV + (kv + 1) * HEAD_DIM]
        ko_ref[:, kv * HEAD_DIM:(kv + 1) * HEAD_DIM] = k_new
        vo_ref[:, kv * HEAD_DIM:(kv + 1) * HEAD_DIM] = v_new
        k_new_b = k_new.astype(BF16)
        v_new_b = v_new.astype(BF16)
        kcat = jnp.concatenate([kprev_ref[:, kv * HEAD_DIM:(kv + 1) * HEAD_DIM], k_new_b], axis=0)
        vcat = jnp.concatenate([vprev_ref[:, kv * HEAD_DIM:(kv + 1) * HEAD_DIM], v_new_b], axis=0)
        kprev_ref[:, kv * HEAD_DIM:(kv + 1) * HEAD_DIM] = k_new_b
        vprev_ref[:, kv * HEAD_DIM:(kv + 1) * HEAD_DIM] = v_new_b
        qs = []
        sks = []
        for g in range(GQA_GROUP):
            h = kv * GQA_GROUP + g
            qs.append(_rope(z_ref[:, O_SQ + h * HEAD_DIM:O_SQ + (h + 1) * HEAD_DIM], cos, sin).astype(BF16))
            sks.append(jnp.full((L, 1), sinks_ref[h], F32))
        q_st = jnp.concatenate(qs, axis=0)
        sk = jnp.concatenate(sks, axis=0)
        s = _dot_nt(q_st, kcat)
        between()
        s = jnp.where(band, s * scale, -jnp.inf)
        m = jnp.maximum(jnp.max(s, axis=-1, keepdims=True), sk)
        p = jnp.exp(s - m)
        l = jnp.sum(p, axis=-1, keepdims=True) + jnp.exp(sk - m)
        o = _dot(p.astype(BF16), vcat)
        between()
        o = o * (1.0 / l)
        for g in range(GQA_GROUP):
            h = kv * GQA_GROUP + g
            gate = _silu(z_ref[:, O_SG + h * HEAD_DIM:O_SG + (h + 1) * HEAD_DIM])
            y_ref[:, h * HEAD_DIM:(h + 1) * HEAD_DIM] = (o[g * L:(g + 1) * L] * gate).astype(BF16)

    for h in range(MEM_HEADS):
        q = z_ref[:, O_CQ + h * HEAD_DIM:O_CQ + (h + 1) * HEAD_DIM].astype(BF16)
        s = _dot_nt(q, mk_ref[:, h * HEAD_DIM:(h + 1) * HEAD_DIM])
        between()
        s = s * scale
        m = jnp.max(s, axis=-1, keepdims=True)
        p = jnp.exp(s - m)
        l = jnp.sum(p, axis=-1, keepdims=True)
        o = _dot(p.astype(BF16), mv_ref[:, h * HEAD_DIM:(h + 1) * HEAD_DIM])
        between()
        o = o * (1.0 / l)
        gate = _silu(z_ref[:, O_CG + h * HEAD_DIM:O_CG + (h + 1) * HEAD_DIM])
        col = SWA_W + ML_W + h * HEAD_DIM
        y_ref[:, col:col + HEAD_DIM] = (o * gate).astype(BF16)

    lane = lax.broadcasted_iota(jnp.int32, (1, GATE_LANES), 1)
    zg = zg_ref[...]
    gates = jnp.where(lane < LANE_F, zg, _log_sigmoid(zg))
    gates = jnp.where(lane < 2 * ML_HEADS, gates, 0.0)
    rr = lax.broadcasted_iota(jnp.int32, (L, L), 0)
    cc = lax.broadcasted_iota(jnp.int32, (L, L), 1)
    causal = rr >= cc
    tri = jnp.where(causal, 1.0, 0.0).astype(BF16)
    csum = _exact_sel_dot(tri, gates)
    gates_t = gates.T
    csum_t = csum.T
    half_lo = lane < ML_DQK
    between()
    for h in range(ML_HEADS):
        pair, half = divmod(h, 2)
        hmask = half_lo if half == 0 else jnp.logical_not(half_lo)
        q_pair = z_ref[:, O_MQ + pair * 128:O_MQ + (pair + 1) * 128]
        k_pair = z_ref[:, O_MK + pair * 128:O_MK + (pair + 1) * 128] * (ML_DQK ** -0.5)
        v_h = z_ref[:, O_MV + h * ML_DV:O_MV + (h + 1) * ML_DV].astype(BF16)
        q_h = jnp.where(hmask, q_pair, 0.0)
        q_hb = q_h.astype(BF16)
        k_pb = k_pair.astype(BF16)

        bt_c = csum[:, LANE_F + h:LANE_F + h + 1]
        it_c = gates[:, LANE_I + h:LANE_I + h + 1]
        bt_r = csum_t[LANE_F + h:LANE_F + h + 1, :]
        it_r = gates_t[LANE_I + h:LANE_I + h + 1, :]
        m_prev = m_ref[:, h:h + 1]
        ct_pair = ct_ref[pair * 128:(pair + 1) * 128, :]
        qk = _dot_nt(q_hb, k_pb)
        state_read = _dot(q_hb, ct_pair.astype(BF16))
        between()

        dlog = jnp.where(causal, bt_c - bt_r + it_r, -jnp.inf)
        inter = bt_c + m_prev
        m_t = jnp.maximum(inter, jnp.max(dlog, axis=-1, keepdims=True))
        w_intra = jnp.exp(dlog - m_t)
        w_state = jnp.exp(inter - m_t)
        a = w_intra * qk
        intra = _dot(a.astype(BF16), v_h)
        between()
        num = intra + w_state * state_read
        n_pair = n_ref[:, pair * 128:(pair + 1) * 128]
        den = jnp.sum(a, axis=-1, keepdims=True) + w_state * jnp.sum(q_h * n_pair, axis=-1, keepdims=True)
        denom = jnp.maximum(jnp.abs(den), jnp.exp(-m_t))
        hid = num * (1.0 / denom)
        o_gate = _sigmoid(z_ref[:, O_MO + h * ML_DV:O_MO + (h + 1) * ML_DV])
        gate = _silu(z_ref[:, O_MG + h * ML_DV:O_MG + (h + 1) * ML_DV])
        col = SWA_W + h * ML_DV
        y_ref[:, col:col + ML_DV] = ((hid * o_gate) * gate).astype(BF16)

        m_new = m_t[L - 1:L, :]
        bt_last = bt_c[L - 1:L, :]
        w_s = jnp.exp(bt_last - bt_c + it_c - m_new)
        decay = jnp.exp(bt_last + m_prev - m_new)
        kw = k_pair * w_s
        upd = _dot(kw.T.astype(BF16), v_h)
        between()
        rows = slice(h * ML_DQK, (h + 1) * ML_DQK)
        ct_ref[rows, :] = decay * ct_ref[rows, :] + upd[half * ML_DQK:(half + 1) * ML_DQK, :]
        ksum = jnp.sum(kw, axis=0, keepdims=True)
        n_ref[:, rows] = decay * n_ref[:, rows] + ksum[:, half * ML_DQK:(half + 1) * ML_DQK]
        m_ref[:, h:h + 1] = m_new


def _prompt(sinks, x2d, g_norm, w, b, wg, bg, rope_c, rope_i, mk, mv, w_out, g_final):
    n = x2d.shape[0]
    nc = n // CHUNK
    const = lambda c, s: (0, 0)
    resident = pl.Buffered(1)
    x_in = lambda c, s: (jnp.minimum(c, nc - 1), 0)
    x_res = lambda c, s: (jnp.clip(c - 2, 0, nc - 1), 0)
    return pl.pallas_call(
        _prompt_kernel,
        grid_spec=pltpu.PrefetchScalarGridSpec(
            num_scalar_prefetch=1,
            grid=(nc + 2,),
            in_specs=[
                pl.BlockSpec((CHUNK, D_MODEL), x_in),
                pl.BlockSpec((CHUNK, D_MODEL), x_res),
                pl.BlockSpec((1, D_MODEL), const, pipeline_mode=resident),
                pl.BlockSpec((D_MODEL, D_MAIN), const, pipeline_mode=resident),
                pl.BlockSpec((1, D_MAIN), const, pipeline_mode=resident),
                pl.BlockSpec((D_MODEL, GATE_LANES), const, pipeline_mode=resident),
                pl.BlockSpec((1, GATE_LANES), const, pipeline_mode=resident),
                pl.BlockSpec((2, nc, HEAD_DIM), lambda c, s: (0, 0, 0), pipeline_mode=resident),
                pl.BlockSpec((4, CHUNK, HEAD_DIM), lambda c, s: (0, 0, 0), pipeline_mode=resident),
                pl.BlockSpec((MEM_LEN, MEM_W), const, pipeline_mode=resident),
                pl.BlockSpec((MEM_LEN, MEM_W), const, pipeline_mode=resident),
                pl.BlockSpec((D_MIX, D_MODEL), const, pipeline_mode=resident),
                pl.BlockSpec((1, D_MODEL), const, pipeline_mode=resident),
            ],
            out_specs=[
                pl.BlockSpec((CHUNK, D_MODEL), x_res),
                pl.BlockSpec((CHUNK, SWA_KV_W), const),
                pl.BlockSpec((CHUNK, SWA_KV_W), const),
                pl.BlockSpec((ML_QK_W, ML_DV), const),
                pl.BlockSpec((1, ML_QK_W), const),
                pl.BlockSpec((1, GATE_LANES), const),
            ],
            scratch_shapes=[
                pltpu.VMEM((CHUNK, SWA_KV_W), BF16),
                pltpu.VMEM((CHUNK, SWA_KV_W), BF16),
                pltpu.VMEM((CHUNK, D_MODEL), BF16),
                pltpu.VMEM((CHUNK, D_MAIN), F32),
                pltpu.VMEM((CHUNK, GATE_LANES), F32),
                pltpu.VMEM((CHUNK, D_MAIN), F32),
                pltpu.VMEM((CHUNK, GATE_LANES), F32),
                pltpu.VMEM((CHUNK, D_MIX), BF16),
                pltpu.VMEM((CHUNK, D_MIX), BF16),
                pltpu.VMEM((CHUNK, D_MODEL), F32),
            ],
        ),
        out_shape=[
            jax.ShapeDtypeStruct((n, D_MODEL), F32),
            jax.ShapeDtypeStruct((CHUNK, SWA_KV_W), F32),
            jax.ShapeDtypeStruct((CHUNK, SWA_KV_W), F32),
            jax.ShapeDtypeStruct((ML_QK_W, ML_DV), F32),
            jax.ShapeDtypeStruct((1, ML_QK_W), F32),
            jax.ShapeDtypeStruct((1, GATE_LANES), F32),
        ],
        compiler_params=pltpu.CompilerParams(
            dimension_semantics=("arbitrary",), vmem_limit_bytes=VMEM_LIMIT),
        name="prompt",
    )(sinks, x2d, x2d, g_norm, w, b, wg, bg, rope_c, rope_i, mk, mv, w_out, g_final)


OUT_TM = 512


def _outproj_kernel(y_ref, x_ref, w_ref, g_ref, o_ref):
    acc = _dot(y_ref[...], w_ref[...]) + x_ref[...]
    ms = jnp.mean(acc * acc, axis=-1, keepdims=True)
    o_ref[...] = (acc * lax.rsqrt(ms + EPS)) * g_ref[...]


def _outproj(y, x2d, w, g):
    n = x2d.shape[0]
    return pl.pallas_call(
        _outproj_kernel,
        grid=(n // OUT_TM,),
        in_specs=[
            pl.BlockSpec((OUT_TM, D_MIX), lambda i: (i, 0)),
            pl.BlockSpec((OUT_TM, D_MODEL), lambda i: (i, 0)),
            pl.BlockSpec((D_MIX, D_MODEL), lambda i: (0, 0)),
            pl.BlockSpec((1, D_MODEL), lambda i: (0, 0)),
        ],
        out_specs=pl.BlockSpec((OUT_TM, D_MODEL), lambda i: (i, 0)),
        out_shape=jax.ShapeDtypeStruct((n, D_MODEL), F32),
        compiler_params=pltpu.CompilerParams(
            dimension_semantics=("arbitrary",), vmem_limit_bytes=VMEM_LIMIT),
        name="outproj",
    )(y, x2d, w, g)


SB = 16
SR = SB * DEC_SEQ
SEQ_UNROLL = 2


def _mix_sample_kernel(sinks_ref, z_ref, zg_ref, cos_ref, sin_ref, ck_ref, cv_ref,
                       c_ref, nst_ref, nrep_ref, mrep_ref,
                       y_ref, ko_ref, vo_ref, co_ref, no_ref, mo_ref,
                       q_s, k_s, o_s):
    T = DEC_SEQ
    scale = HEAD_DIM ** -0.5
    cos = cos_ref[...]
    sin = sin_ref[...]

    for h in range(SWA_HEADS):
        q_s[:, h * HEAD_DIM:(h + 1) * HEAD_DIM] = _rope(
            z_ref[:, O_SQ + h * HEAD_DIM:O_SQ + (h + 1) * HEAD_DIM], cos, sin)
    for kv in range(SWA_KV_HEADS):
        k_s[:, kv * HEAD_DIM:(kv + 1) * HEAD_DIM] = _rope(
            z_ref[:, O_SK + kv * HEAD_DIM:O_SK + (kv + 1) * HEAD_DIM], cos, sin)

    KP = 2 * WINDOW
    rt = lax.broadcasted_iota(jnp.int32, (GQA_GROUP * T, KP), 0) & (T - 1)
    cj = lax.broadcasted_iota(jnp.int32, (GQA_GROUP * T, KP), 1)
    mask = (cj > rt) & (cj <= rt + WINDOW)
    kpad = jnp.zeros((KP - WINDOW - T, HEAD_DIM), F32)
    NKV = SWA_KV_HEADS
    SEQ_ROWS = WINDOW * NKV

    keep = SEQ_ROWS - T * NKV
    sink_cols = [jnp.concatenate([jnp.full((T, 1), sinks_ref[kv * GQA_GROUP + g], F32)
                                  for g in range(GQA_GROUP)], axis=0) for kv in range(NKV)]

    def per_group(i, carry):
        chains = [(i * SEQ_UNROLL + j, kv) for j in range(SEQ_UNROLL) for kv in range(NKV)]
        scores, values = [], []
        for b, kv in chains:
            rows = pl.ds(pl.multiple_of(b * T, T), T)
            base = pl.multiple_of(b * SEQ_ROWS, SEQ_ROWS)
            cols = slice(kv * HEAD_DIM, (kv + 1) * HEAD_DIM)
            k_new = k_s[rows, cols]
            v_new = z_ref[rows, O_SV + kv * HEAD_DIM:O_SV + (kv + 1) * HEAD_DIM]
            if kv == 0:
                ko_ref[pl.ds(base, keep), :] = ck_ref[pl.ds(base + T * NKV, keep), :]
                vo_ref[pl.ds(base, keep), :] = cv_ref[pl.ds(base + T * NKV, keep), :]
            ko_ref[pl.ds(base + keep + kv, T, stride=NKV), :] = k_new
            vo_ref[pl.ds(base + keep + kv, T, stride=NKV), :] = v_new
            kc = ck_ref[pl.ds(base + kv, WINDOW, stride=NKV), :]
            vc = cv_ref[pl.ds(base + kv, WINDOW, stride=NKV), :]
            k_all = jnp.concatenate([kc, k_new, kpad], axis=0).astype(BF16)
            values.append(jnp.concatenate([vc, v_new, kpad], axis=0).astype(BF16))
            q_st = jnp.concatenate([q_s[rows, (kv * GQA_GROUP + g) * HEAD_DIM:(kv * GQA_GROUP + g + 1) * HEAD_DIM]
                                    for g in range(GQA_GROUP)], axis=0).astype(BF16)
            scores.append(_dot_nt(q_st, k_all))
        probs, norms = [], []
        for (b, kv), s in zip(chains, scores):
            s = jnp.where(mask, s * scale, -jnp.inf)
            m = jnp.maximum(jnp.max(s, axis=-1, keepdims=True), sink_cols[kv])
            p = jnp.exp(s - m)
            norms.append(jnp.sum(p, axis=-1, keepdims=True) + jnp.exp(sink_cols[kv] - m))
            probs.append(p.astype(BF16))
        outs = [_dot(p, v) for p, v in zip(probs, values)]
        for (b, kv), o, l in zip(chains, outs, norms):
            rows = pl.ds(pl.multiple_of(b * T, T), T)
            o = o * (1.0 / l)
            for g in range(GQA_GROUP):
                h = kv * GQA_GROUP + g
                o_s[rows, h * HEAD_DIM:(h + 1) * HEAD_DIM] = o[g * T:(g + 1) * T]
        return carry

    lax.fori_loop(0, SB // SEQ_UNROLL, per_group, 0)

    for h in range(SWA_HEADS):
        cols = slice(h * HEAD_DIM, (h + 1) * HEAD_DIM)
        gate = _silu(z_ref[:, O_SG + h * HEAD_DIM:O_SG + (h + 1) * HEAD_DIM])
        y_ref[:, cols] = (o_s[:, cols] * gate).astype(BF16)

    R = SR
    lane = lax.broadcasted_iota(jnp.int32, (1, GATE_LANES), 1)
    zg = zg_ref[...]
    gates = jnp.where(lane < LANE_F, zg, _log_sigmoid(zg))
    gates = jnp.where(lane < 2 * ML_HEADS, gates, 0.0)
    rr = lax.broadcasted_iota(jnp.int32, (R, R), 0)
    cc = lax.broadcasted_iota(jnp.int32, (R, R), 1)
    same_seq = (rr >> 3) == (cc >> 3)
    causal = same_seq & (rr >= cc)
    tri = jnp.where(causal, 1.0, 0.0).astype(BF16)
    csum = _exact_sel_dot(tri, gates)
    gates_t = gates.T
    csum_t = csum.T
    half_lo = lane < ML_DQK
    seq_of_col = lax.broadcasted_iota(jnp.int32, (SB, 1, R), 2) >> 3
    seq_id = lax.broadcasted_iota(jnp.int32, (SB, 1, R), 0)
    own_cols = seq_of_col == seq_id
    own_blk = ((lax.broadcasted_iota(jnp.int32, (R, SB * 128), 0) >> 3)
               == (lax.broadcasted_iota(jnp.int32, (R, SB * 128), 1) >> 7))
    k_t = [(z_ref[:, O_MK + p * 128:O_MK + (p + 1) * 128] * (ML_DQK ** -0.5)).T for p in range(2)]
    mrep = mrep_ref[...]
    tok3 = lax.broadcasted_iota(jnp.int32, (SB, T, 1), 1)
    mo_ref[...] = jnp.zeros_like(mo_ref)

    def per_seq_value(col):
        return jnp.max(col.reshape(SB, T, 1), axis=1, keepdims=True)

    def last_of_seq(col):
        c3 = jnp.where(tok3 == T - 1, col.reshape(SB, T, 1), -jnp.inf)
        return jnp.broadcast_to(jnp.max(c3, axis=1, keepdims=True), (SB, T, 1)).reshape(R, 1)

    for h in range(ML_HEADS):
        pair, half = divmod(h, 2)
        hmask = half_lo if half == 0 else jnp.logical_not(half_lo)
        q_pair = z_ref[:, O_MQ + pair * 128:O_MQ + (pair + 1) * 128]
        k_pair = z_ref[:, O_MK + pair * 128:O_MK + (pair + 1) * 128] * (ML_DQK ** -0.5)
        v_f = z_ref[:, O_MV + h * ML_DV:O_MV + (h + 1) * ML_DV]
        v_h = v_f.astype(BF16)
        q_h = jnp.where(hmask, q_pair, 0.0)
        q_hb = q_h.astype(BF16)
        k_pb = k_pair.astype(BF16)

        bt_c = csum[:, LANE_F + h:LANE_F + h + 1]
        it_c = gates[:, LANE_I + h:LANE_I + h + 1]
        bt_r = csum_t[LANE_F + h:LANE_F + h + 1, :]
        it_r = gates_t[LANE_I + h:LANE_I + h + 1, :]
        m_prev = mrep[:, h:h + 1]

        dlog = jnp.where(causal, bt_c - bt_r + it_r, -jnp.inf)
        inter = bt_c + m_prev
        m_t = jnp.maximum(inter, jnp.max(dlog, axis=-1, keepdims=True))
        w_intra = jnp.exp(dlog - m_t)
        w_state = jnp.exp(inter - m_t)
        a = w_intra * _dot_nt(q_hb, k_pb)

        ct_st = c_ref[:, 2 * pair:2 * pair + 2].reshape(SB * 128, ML_DV).astype(BF16)
        q_blk = jnp.where(own_blk, jnp.tile(q_h, (1, SB)), 0.0).astype(BF16)
        num_state = _dot(q_blk, ct_st)

        num = _dot(a.astype(BF16), v_h) + w_state * num_state
        n_pair = nrep_ref[:, pair * 128:(pair + 1) * 128]
        den = jnp.sum(a, axis=-1, keepdims=True) + w_state * jnp.sum(q_h * n_pair, axis=-1, keepdims=True)
        denom = jnp.maximum(jnp.abs(den), jnp.exp(-m_t))
        hid = num * (1.0 / denom)
        o_gate = _sigmoid(z_ref[:, O_MO + h * ML_DV:O_MO + (h + 1) * ML_DV])
        gate = _silu(z_ref[:, O_MG + h * ML_DV:O_MG + (h + 1) * ML_DV])
        col = SWA_W + h * ML_DV
        y_ref[:, col:col + ML_DV] = ((hid * o_gate) * gate).astype(BF16)

        m_new = last_of_seq(m_t)
        bt_last = last_of_seq(bt_c)
        w_s = jnp.exp(bt_last - bt_c + it_c - m_new)
        decay = jnp.exp(bt_last + m_prev - m_new)
        decay_seq = per_seq_value(decay)

        kt_h = k_t[pair][half * ML_DQK:(half + 1) * ML_DQK, :]
        lhs = jnp.where(own_cols, kt_h[None, :, :], 0.0).reshape(SB * ML_DQK, R).astype(BF16)
        upd = _dot(lhs, (v_f * w_s).astype(BF16)).reshape(SB, ML_DQK, ML_DV)
        co_ref[:, h] = decay_seq * c_ref[:, h] + upd

        kw = (k_pair * w_s).reshape(SB, T, 128)
        ksum = jnp.sum(kw, axis=1)
        n_old = nst_ref[:, h * ML_DQK:(h + 1) * ML_DQK]
        dec2 = decay_seq.reshape(SB, 1)
        no_ref[:, h * ML_DQK:(h + 1) * ML_DQK] = dec2 * n_old + ksum[:, half * ML_DQK:(half + 1) * ML_DQK]
        mo_ref[:, h:h + 1] = per_seq_value(m_new).reshape(SB, 1)


def _mix_sample(sinks, z, zg, cos, sin, ck, cv, cst, nst, nrep, mrep):
    nb = cst.shape[0]
    steps = nb // SB
    const = lambda i, s: (0, 0)
    cache_rows = SB * WINDOW * SWA_KV_HEADS
    return pl.pallas_call(
        _mix_sample_kernel,
        grid_spec=pltpu.PrefetchScalarGridSpec(
            num_scalar_prefetch=1,
            grid=(steps,),
            in_specs=[
                pl.BlockSpec((SR, D_MAIN), lambda i, s: (i, 0)),
                pl.BlockSpec((SR, GATE_LANES), lambda i, s: (i, 0)),
                pl.BlockSpec((SR, HEAD_DIM), const),
                pl.BlockSpec((SR, HEAD_DIM), const),
                pl.BlockSpec((cache_rows, HEAD_DIM), lambda i, s: (i, 0)),
                pl.BlockSpec((cache_rows, HEAD_DIM), lambda i, s: (i, 0)),
                pl.BlockSpec((SB, ML_HEADS, ML_DQK, ML_DV), lambda i, s: (i, 0, 0, 0)),
                pl.BlockSpec((SB, ML_QK_W), lambda i, s: (i, 0)),
                pl.BlockSpec((SR, ML_QK_W), lambda i, s: (i, 0)),
                pl.BlockSpec((SR, GATE_LANES), lambda i, s: (i, 0)),
            ],
            out_specs=[
                pl.BlockSpec((SR, SWA_W + ML_W), lambda i, s: (i, 0)),
                pl.BlockSpec((cache_rows, HEAD_DIM), lambda i, s: (i, 0)),
                pl.BlockSpec((cache_rows, HEAD_DIM), lambda i, s: (i, 0)),
                pl.BlockSpec((SB, ML_HEADS, ML_DQK, ML_DV), lambda i, s: (i, 0, 0, 0)),
                pl.BlockSpec((SB, ML_QK_W), lambda i, s: (i, 0)),
                pl.BlockSpec((SB, GATE_LANES), lambda i, s: (i, 0)),
            ],
            scratch_shapes=[
                pltpu.VMEM((SR, SWA_W), F32),
                pltpu.VMEM((SR, SWA_KV_W), F32),
                pltpu.VMEM((SR, SWA_W), F32),
            ],
        ),
        out_shape=[
            jax.ShapeDtypeStruct((nb * DEC_SEQ, SWA_W + ML_W), BF16),
            jax.ShapeDtypeStruct(ck.shape, F32),
            jax.ShapeDtypeStruct(cv.shape, F32),
            jax.ShapeDtypeStruct((nb, ML_HEADS, ML_DQK, ML_DV), F32),
            jax.ShapeDtypeStruct((nb, ML_QK_W), F32),
            jax.ShapeDtypeStruct((nb, GATE_LANES), F32),
        ],
        compiler_params=pltpu.CompilerParams(
            dimension_semantics=("arbitrary",), vmem_limit_bytes=VMEM_LIMIT),
        name="mix_sample",
    )(sinks, z, zg, cos, sin, ck, cv, cst, nst, nrep, mrep)


MB = 8
MR = MB * DEC_SEQ


def _mem_sample_kernel(cq_ref, cg_ref, mk_ref, mv_ref, y_ref, o_s):
    T = DEC_SEQ
    scale = HEAD_DIM ** -0.5
    zpad = jnp.zeros((T, HEAD_DIM), F32)

    seq_rows = MEM_LEN * MEM_HEADS

    def per_group(i, carry):
        chains = [(i * SEQ_UNROLL + j, h) for j in range(SEQ_UNROLL) for h in range(MEM_HEADS)]
        scores = []
        for b, h in chains:
            rows = pl.ds(pl.multiple_of(b * T, T), T)
            base = pl.multiple_of(b * seq_rows, seq_rows)
            kb = mk_ref[pl.ds(base + h, MEM_LEN, stride=MEM_HEADS), :].astype(BF16)
            q = jnp.concatenate([cq_ref[rows, h * HEAD_DIM:(h + 1) * HEAD_DIM], zpad], axis=0).astype(BF16)
            scores.append(_dot_nt(q, kb))
        probs, norms = [], []
        for s in scores:
            s = s * scale
            p = jnp.exp(s - jnp.max(s, axis=-1, keepdims=True))
            norms.append(jnp.sum(p, axis=-1, keepdims=True))
            probs.append(p.astype(BF16))
        outs = []
        for (b, h), p in zip(chains, probs):
            base = pl.multiple_of(b * seq_rows, seq_rows)
            vb = mv_ref[pl.ds(base + h, MEM_LEN, stride=MEM_HEADS), :].astype(BF16)
            outs.append(_dot(p, vb))
        for (b, h), o, l in zip(chains, outs, norms):
            rows = pl.ds(pl.multiple_of(b * T, T), T)
            o_s[rows, h * HEAD_DIM:(h + 1) * HEAD_DIM] = (o * (1.0 / l))[0:T]
        return carry

    lax.fori_loop(0, MB // SEQ_UNROLL, per_group, 0)
    y_ref[...] = (o_s[...] * _silu(cg_ref[...])).astype(BF16)


def _mem_sample(z, mk, mv):
    nb = z.shape[0] // DEC_SEQ
    cache_rows = MB * MEM_LEN * MEM_HEADS
    return pl.pallas_call(
        _mem_sample_kernel,
        grid=(nb // MB,),
        in_specs=[
            pl.BlockSpec((MR, MEM_W), lambda i: (i, O_CQ // MEM_W)),
            pl.BlockSpec((MR, MEM_W), lambda i: (i, O_CG // MEM_W)),
            pl.BlockSpec((cache_rows, HEAD_DIM), lambda i: (i, 0)),
            pl.BlockSpec((cache_rows, HEAD_DIM), lambda i: (i, 0)),
        ],
        out_specs=pl.BlockSpec((MR, MEM_W), lambda i: (i, 0)),
        out_shape=jax.ShapeDtypeStruct((nb * DEC_SEQ, MEM_W), BF16),
        scratch_shapes=[pltpu.VMEM((MR, MEM_W), F32)],
        compiler_params=pltpu.CompilerParams(
            dimension_semantics=("arbitrary",), vmem_limit_bytes=VMEM_LIMIT),
        name="mem_sample",
    )(z, z, mk, mv)


def _rope_cos_sin(pos):
    half = HEAD_DIM // 2
    inv = jnp.power(ROPE_THETA, -(jnp.arange(half, dtype=F32) * 2.0 / HEAD_DIM))
    ang = pos.astype(F32)[:, None] * inv[None, :]
    return jnp.cos(ang), jnp.sin(ang)


def _rope_tables(pos):
    cos, sin = _rope_cos_sin(pos)
    return jnp.concatenate([cos, cos], axis=-1), jnp.concatenate([-sin, sin], axis=-1)


def _rope_split_tables(n_chunks):
    ca, sa = _rope_cos_sin(jnp.arange(n_chunks, dtype=jnp.int32) * CHUNK)
    cb, sb = _rope_cos_sin(jnp.arange(CHUNK, dtype=jnp.int32))
    dup = lambda t: jnp.concatenate([t, t], axis=-1)
    sgn = lambda t: jnp.concatenate([-t, t], axis=-1)
    return jnp.stack([dup(ca), dup(sa)]), jnp.stack([dup(cb), dup(sb), sgn(cb), sgn(sb)])


def _relayout_in_proj(w_in, b_in):
    w_t = jnp.swapaxes(w_in, 0, 1)
    main_t = jnp.concatenate([w_t[_R_SQ:_R_MI], w_t[_R_MO:_R_END]], axis=0).astype(BF16)
    pad_t = jnp.zeros((GATE_LANES - 2 * ML_HEADS, D_MODEL), BF16)
    gate_t = jnp.concatenate([w_t[_R_MI:_R_MO].astype(BF16), pad_t], axis=0)
    b_main = jnp.concatenate([b_in[_R_SQ:_R_MI], b_in[_R_MO:_R_END]])[None, :]
    b_gate = jnp.pad(b_in[_R_MI:_R_MO], (0, GATE_LANES - 2 * ML_HEADS))[None, :]
    return jnp.swapaxes(main_t, 0, 1), b_main, jnp.swapaxes(gate_t, 0, 1), b_gate


def _layer(xp, xs, mem, ck, cv, c_st, n_st, m_st, cmk, cmv,
           g_norm, w_in, b_in, sinks, g_mem, w_mem_kv, w_out, g_final):
    bp, sp, _ = xp.shape
    bs, ts, _ = xs.shape
    xp2 = xp.reshape(bp * sp, D_MODEL)
    xs2 = xs.reshape(bs * ts, D_MODEL)
    w_main, b_main, w_gate, b_gate = _relayout_in_proj(w_in, b_in)
    g_norm2 = g_norm[None, :]
    w_out_b = w_out.astype(BF16)
    g_final2 = g_final[None, :]
    sinks_flat = sinks.reshape(SWA_HEADS)

    memkv = _memkv(mem.reshape(MEM_LEN, D_MODEL), g_mem[None, :], w_mem_kv.astype(BF16))
    mem_k = memkv[:, :MEM_W]
    mem_v = memkv[:, MEM_W:]
    rope_c, rope_i = _rope_split_tables(sp // CHUNK)
    out_p, pk, pv, ct, n_p, m_p = _prompt(sinks_flat, xp2, g_norm2, w_main, b_main, w_gate, b_gate,
                                           rope_c, rope_i, mem_k.astype(BF16), mem_v.astype(BF16),
                                           w_out_b, g_final2)

    zs, zgs = _proj(xs2, g_norm2, w_main, b_main, w_gate, b_gate)
    cos_s, sin_s = _rope_tables(PAST_LEN + jnp.arange(ts, dtype=jnp.int32))
    cos_s = jnp.tile(cos_s, (SB, 1))
    sin_s = jnp.tile(sin_s, (SB, 1))
    nrep = jnp.repeat(n_st.reshape(bs, ML_QK_W), ts, axis=0)
    m_pad = jnp.pad(m_st, ((0, 0), (0, GATE_LANES - ML_HEADS)))
    mrep = jnp.repeat(m_pad, ts, axis=0)
    ya, sk_o, sv_o, ct_o, n_o, m_o = _mix_sample(
        sinks_flat, zs, zgs, cos_s, sin_s,
        ck.reshape(bs * WINDOW * SWA_KV_HEADS, HEAD_DIM), cv.reshape(bs * WINDOW * SWA_KV_HEADS, HEAD_DIM),
        jnp.swapaxes(c_st, -1, -2), n_st.reshape(bs, ML_QK_W), nrep, mrep)
    yb = _mem_sample(zs, cmk.reshape(bs * MEM_LEN * MEM_HEADS, HEAD_DIM),
                     cmv.reshape(bs * MEM_LEN * MEM_HEADS, HEAD_DIM))
    out_s = _outproj(jnp.concatenate([ya, yb], axis=-1), xs2, w_out_b, g_final2)
    c_o = jnp.swapaxes(ct_o, -1, -2)

    p_state = (
        pk.reshape(bp, WINDOW, SWA_KV_HEADS, HEAD_DIM),
        pv.reshape(bp, WINDOW, SWA_KV_HEADS, HEAD_DIM),
        ct.reshape(ML_HEADS, ML_DQK, ML_DV).transpose(0, 2, 1)[None],
        n_p.reshape(bp, ML_HEADS, ML_DQK),
        m_p[:, :ML_HEADS],
        mem_k.reshape(bp, MEM_LEN, MEM_HEADS, HEAD_DIM),
        mem_v.reshape(bp, MEM_LEN, MEM_HEADS, HEAD_DIM),
    )
    s_state = (
        sk_o.reshape(bs, WINDOW, SWA_KV_HEADS, HEAD_DIM),
        sv_o.reshape(bs, WINDOW, SWA_KV_HEADS, HEAD_DIM),
        c_o,
        n_o.reshape(bs, ML_HEADS, ML_DQK),
        m_o[:, :ML_HEADS],
    )
    return out_p.reshape(bp, sp, D_MODEL), out_s.reshape(bs, ts, D_MODEL), p_state, s_state


def kernel(x_prompt, x_sample, mem_prompt, cache_swa_k, cache_swa_v, state_mlstm_C, state_mlstm_n,
           state_mlstm_m, cache_mem_k, cache_mem_v, g_norm, w_in, b_in, swa_sinks, g_mem, w_mem_kv,
           w_out, g_final):
    depth = g_norm.shape[0]
    assert depth == 1 and x_prompt.shape[0] == 1
    y_p, y_s, p_state, s_state = _layer(
        x_prompt, x_sample, mem_prompt, cache_swa_k[0], cache_swa_v[0], state_mlstm_C[0],
        state_mlstm_n[0], state_mlstm_m[0], cache_mem_k[0], cache_mem_v[0],
        g_norm[0], w_in[0], b_in[0], swa_sinks[0], g_mem[0], w_mem_kv[0], w_out[0], g_final)
    return (y_p, y_s) + tuple(s[None] for s in p_state) + tuple(s[None] for s in s_state)
```

```python
import functools
import math

import jax
import jax.numpy as jnp
from jax import lax
from jax.experimental import pallas as pl
from jax.experimental.pallas import tpu as pltpu

F32 = jnp.float32
BF16 = jnp.bfloat16

D_MODEL = 2048
HEAD_DIM = 128
SWA_HEADS = 8
SWA_KV_HEADS = 2
GQA_GROUP = 4
WINDOW = 128
ML_HEADS = 4
ML_DQK = 64
ML_DV = 128
MEM_HEADS = 4
MEM_LEN = 256
CHUNK = 128
ROPE_THETA = 10000.0
EPS = 1e-6
PAST_LEN = 16384
DEC_SEQ = 8

SWA_W = SWA_HEADS * HEAD_DIM
SWA_KV_W = SWA_KV_HEADS * HEAD_DIM
ML_W = ML_HEADS * ML_DV
ML_QK_W = ML_HEADS * ML_DQK
MEM_W = MEM_HEADS * HEAD_DIM
D_MIX = SWA_W + ML_W + MEM_W

_IN_WIDTHS = (SWA_W, SWA_KV_W, SWA_KV_W, SWA_W, ML_QK_W, ML_QK_W, ML_W, ML_HEADS, ML_HEADS, ML_W, ML_W, MEM_W, MEM_W)
_IN_OFFS = [0]
for _w in _IN_WIDTHS:
    _IN_OFFS.append(_IN_OFFS[-1] + _w)
(_R_SQ, _R_SK, _R_SV, _R_SG, _R_MQ, _R_MK, _R_MV, _R_MI, _R_MF, _R_MO, _R_MG, _R_CQ, _R_CG, _R_END) = _IN_OFFS

O_SQ = 0
O_SK = O_SQ + SWA_W
O_SV = O_SK + SWA_KV_W
O_SG = O_SV + SWA_KV_W
O_MQ = O_SG + SWA_W
O_MK = O_MQ + ML_QK_W
O_MV = O_MK + ML_QK_W
O_MO = O_MV + ML_W
O_MG = O_MO + ML_W
O_CQ = O_MG + ML_W
O_CG = O_CQ + MEM_W
D_MAIN = O_CG + MEM_W
GATE_LANES = 128
LANE_I = 0
LANE_F = ML_HEADS

VMEM_LIMIT = 56 * 1024 * 1024

_NT = (((1,), (1,)), ((), ()))


def _dot(a, b):
    return jnp.dot(a, b, preferred_element_type=F32)


def _dot_nt(a, b):
    return lax.dot_general(a, b, _NT, preferred_element_type=F32)


def _exact_sel_dot(sel_bf16, x):
    hi = x.astype(BF16)
    r1 = x - hi.astype(F32)
    mid = r1.astype(BF16)
    lo = (r1 - mid.astype(F32)).astype(BF16)
    return _dot(sel_bf16, hi) + _dot(sel_bf16, mid) + _dot(sel_bf16, lo)


def _silu(x):
    h = 0.5 * x
    return h + h * jnp.tanh(h)


def _sigmoid(x):
    return 0.5 + 0.5 * jnp.tanh(0.5 * x)


def _log_sigmoid(x):
    return jnp.minimum(x, 0.0) - jnp.log1p(jnp.exp(-jnp.abs(x)))


PROJ_TM = 1024
PROJ_TN = 512
NORM_ROWS = 256


def _proj_kernel(x_ref, g_ref, w_ref, b_ref, wg_ref, bg_ref, z_ref, zg_ref, u_ref):
    j = pl.program_id(1)

    @pl.when(j == 0)
    def _():
        g = g_ref[...]
        for r in range(PROJ_TM // NORM_ROWS):
            rows = pl.ds(r * NORM_ROWS, NORM_ROWS)
            xf = x_ref[rows, :]
            ms = jnp.mean(xf * xf, axis=-1, keepdims=True)
            u_ref[rows, :] = ((xf * lax.rsqrt(ms + EPS)) * g).astype(BF16)
        zg_ref[...] = _dot(u_ref[...], wg_ref[...]) + bg_ref[...]

    z_ref[...] = _dot(u_ref[...], w_ref[...]) + b_ref[...]


def _proj(x2d, g, w, b, wg, bg):
    n = x2d.shape[0]
    grid = (n // PROJ_TM, D_MAIN // PROJ_TN)
    return pl.pallas_call(
        _proj_kernel,
        grid=grid,
        in_specs=[
            pl.BlockSpec((PROJ_TM, D_MODEL), lambda i, j: (i, 0)),
            pl.BlockSpec((1, D_MODEL), lambda i, j: (0, 0)),
            pl.BlockSpec((D_MODEL, PROJ_TN), lambda i, j: (0, j)),
            pl.BlockSpec((1, PROJ_TN), lambda i, j: (0, j)),
            pl.BlockSpec((D_MODEL, GATE_LANES), lambda i, j: (0, 0)),
            pl.BlockSpec((1, GATE_LANES), lambda i, j: (0, 0)),
        ],
        out_specs=[
            pl.BlockSpec((PROJ_TM, PROJ_TN), lambda i, j: (i, j)),
            pl.BlockSpec((PROJ_TM, GATE_LANES), lambda i, j: (i, 0)),
        ],
        out_shape=[
            jax.ShapeDtypeStruct((n, D_MAIN), F32),
            jax.ShapeDtypeStruct((n, GATE_LANES), F32),
        ],
        scratch_shapes=[pltpu.VMEM((PROJ_TM, D_MODEL), BF16)],
        compiler_params=pltpu.CompilerParams(
            dimension_semantics=("arbitrary", "arbitrary"), vmem_limit_bytes=VMEM_LIMIT),
        name="proj",
    )(x2d, g, w, b, wg, bg)


MEMKV_TN = 256


def _memkv_kernel(mem_ref, g_ref, w_ref, o_ref):
    xf = mem_ref[...]
    ms = jnp.mean(xf * xf, axis=-1, keepdims=True)
    u = ((xf * lax.rsqrt(ms + EPS)) * g_ref[...]).astype(BF16)
    o_ref[...] = _dot(u, w_ref[...])


def _memkv(mem2d, g, w):
    return pl.pallas_call(
        _memkv_kernel,
        grid=(2 * MEM_W // MEMKV_TN,),
        in_specs=[
            pl.BlockSpec((MEM_LEN, D_MODEL), lambda j: (0, 0)),
            pl.BlockSpec((1, D_MODEL), lambda j: (0, 0)),
            pl.BlockSpec((D_MODEL, MEMKV_TN), lambda j: (0, j)),
        ],
        out_specs=pl.BlockSpec((MEM_LEN, MEMKV_TN), lambda j: (0, j)),
        out_shape=jax.ShapeDtypeStruct((MEM_LEN, 2 * MEM_W), F32),
        compiler_params=pltpu.CompilerParams(
            dimension_semantics=("arbitrary",), vmem_limit_bytes=VMEM_LIMIT),
        name="memkv",
    )(mem2d, g, w)


OUT_TN = 256


def _rope(x, cos, sin_signed):
    return x * cos + pltpu.roll(x, HEAD_DIM // 2, axis=1) * sin_signed


IN_TN = 256
SWA_POINTS = 3 * SWA_KV_HEADS
MEM_POINTS = 2 * MEM_HEADS
MIXER_WEIGHTS = ([1.0, 3.0, 1.0] * SWA_KV_HEADS + [1.0] * MEM_HEADS + [0.5] * MEM_HEADS
                 + [1.0] + [1.5] * ML_HEADS + [1.0] * ML_HEADS + [0.5] * ML_HEADS)


def _spread(tasks, weights):
    total = sum(weights)
    bounds = [0]
    acc = 0.0
    for w in weights:
        acc += w
        bounds.append(round(len(tasks) * acc / total))
    return [tasks[a:b] for a, b in zip(bounds[:-1], bounds[1:])]


def _zip_then_rest(a, b):
    k = min(len(a), len(b))
    return [t for pair in zip(a[:k], b[:k]) for t in pair] + a[k:] + b[k:]


def _prompt_kernel(sinks_ref, xin_ref, xres_ref, gn_ref, w_ref, b_ref, wg_ref, bg_ref,
                   rope_c_ref, rope_i_ref, mk_ref, mv_ref, wout_ref, gfin_ref,
                   out_ref, ko_ref, vo_ref, ct_ref, n_ref, m_ref,
                   kprev_ref, vprev_ref, u_ref, z_ref, zg_ref, zprev_ref, zgprev_ref,
                   y_ref, yprev_ref, acc_ref):
    s = pl.program_id(0)
    nc = pl.num_programs(0) - 2
    n_in = D_MAIN // IN_TN
    n_out = D_MODEL // OUT_TN

    def in_norm():
        xf = xin_ref[...]
        ms = jnp.mean(xf * xf, axis=-1, keepdims=True)
        u_ref[...] = ((xf * lax.rsqrt(ms + EPS)) * gn_ref[...]).astype(BF16)

    def in_tile(t):
        if t == n_in:
            zg_ref[...] = _dot(u_ref[...], wg_ref[...]) + bg_ref[...]
        else:
            cols = slice(t * IN_TN, (t + 1) * IN_TN)
            z_ref[:, cols] = _dot(u_ref[...], w_ref[:, cols]) + b_ref[:, cols]

    def in_rotate(lo, hi):
        zprev_ref[:, lo:hi] = z_ref[:, lo:hi]

    def in_rotate_rest():
        in_rotate(O_MQ, O_CQ)
        zgprev_ref[...] = zg_ref[...]

    def out_tile(t):
        cols = slice(t * OUT_TN, (t + 1) * OUT_TN)
        acc_ref[:, cols] = _dot(yprev_ref[...], wout_ref[:, cols]) + xres_ref[:, cols]

    def out_finish():
        acc = acc_ref[...]
        ms = jnp.mean(acc * acc, axis=-1, keepdims=True)
        out_ref[...] = (acc * lax.rsqrt(ms + EPS)) * gfin_ref[...]

    tile_of = lambda col: col // IN_TN
    in_order = (list(range(0, tile_of(O_MQ))) + list(range(tile_of(O_CQ), n_in))
                + list(range(tile_of(O_MQ), tile_of(O_CQ))) + [n_in])
    in_tasks = []
    for t in in_order:
        in_tasks.append((functools.partial(in_tile, t), 0))
        if t == tile_of(O_MQ) - 1:
            in_tasks.append((functools.partial(in_rotate, 0, O_MQ), SWA_POINTS))
        if t == n_in - 1:
            in_tasks.append((functools.partial(in_rotate, O_CQ, D_MAIN), SWA_POINTS + MEM_POINTS))
    out_tasks = [(functools.partial(out_tile, t), 0) for t in range(n_out)] + [(out_finish, 0)]

    def mix(tasks):
        groups = iter(_spread(tasks, MIXER_WEIGHTS))
        points_done = [0]

        def between():
            for emit, first_point in next(groups):
                assert points_done[0] >= first_point
                emit()
            points_done[0] += 1

        _prompt_mixers(s - 1, sinks_ref, zprev_ref, zgprev_ref, rope_c_ref, rope_i_ref, mk_ref, mv_ref,
                       y_ref, ko_ref, vo_ref, ct_ref, n_ref, m_ref, kprev_ref, vprev_ref, between)
        assert next(groups, None) is None

    def run(tasks):
        for emit, _ in tasks:
            emit()

    @pl.when(s == 0)
    def _():
        kprev_ref[...] = jnp.zeros_like(kprev_ref)
        vprev_ref[...] = jnp.zeros_like(vprev_ref)
        ct_ref[...] = jnp.zeros_like(ct_ref)
        n_ref[...] = jnp.zeros_like(n_ref)
        m_ref[...] = jnp.zeros_like(m_ref)
        in_norm()
        run(in_tasks)
        in_rotate_rest()

    @pl.when(s == 1)
    def _():
        in_norm()
        mix(in_tasks)
        in_rotate_rest()
        yprev_ref[...] = y_ref[...]

    @pl.when((s >= 2) & (s < nc))
    def _():
        in_norm()
        mix(out_tasks[:2] + _zip_then_rest(in_tasks, out_tasks[2:]))
        in_rotate_rest()
        yprev_ref[...] = y_ref[...]

    @pl.when(s == nc)
    def _():
        mix(out_tasks)
        yprev_ref[...] = y_ref[...]

    @pl.when(s == nc + 1)
    def _():
        run(out_tasks)


def _prompt_mixers(c, sinks_ref, z_ref, zg_ref, rope_c_ref, rope_i_ref, mk_ref, mv_ref,
                   y_ref, ko_ref, vo_ref, ct_ref, n_ref, m_ref, kprev_ref, vprev_ref, between):
    L = CHUNK
    scale = HEAD_DIM ** -0.5

    cc = rope_c_ref[0, pl.ds(c, 1), :]
    sc = rope_c_ref[1, pl.ds(c, 1), :]
    cos = cc * rope_i_ref[0] - sc * rope_i_ref[1]
    sin = sc * rope_i_ref[2] + cc * rope_i_ref[3]

    ri = lax.broadcasted_iota(jnp.int32, (GQA_GROUP * L, 2 * L), 0) & (L - 1)
    cj = lax.broadcasted_iota(jnp.int32, (GQA_GROUP * L, 2 * L), 1)
    j_low = jnp.where(c > 0, 0, L)
    band = (cj > ri) & (cj <= ri + L) & (cj >= j_low)
    for kv in range(SWA_KV_HEADS):
        between()
        k_new = _rope(z_ref[:, O_SK + kv * HEAD_DIM:O_SK + (kv + 1) * HEAD_DIM], cos, sin)
        v_new = z_ref[:, O_SV + kv * HEAD_DIM:O_SV + (kv + 1) * HEAD_DIM]
        ko_ref[:, kv * HEAD_DIM:(kv + 1) * HEAD_DIM] = k_new
        vo_ref[:, kv * HEAD_DIM:(kv + 1) * HEAD_DIM] = v_new
        k_new_b = k_new.astype(BF16)
        v_new_b = v_new.astype(BF16)
        kcat = jnp.concatenate([kprev_ref[:, kv * HEAD_DIM:(kv + 1) * HEAD_DIM], k_new_b], axis=0)
        vcat = jnp.concatenate([vprev_ref[:, kv * HEAD_DIM:(kv + 1) * HEAD_DIM], v_new_b], axis=0)
        kprev_ref[:, kv * HEAD_DIM:(kv + 1) * HEAD_DIM] = k_new_b
        vprev_ref[:, kv * HEAD_DIM:(kv + 1) * HEAD_DIM] = v_new_b
        qs = []
        sks = []
        for g in range(GQA_GROUP):
            h = kv * GQA_GROUP + g
            qs.append(_rope(z_ref[:, O_SQ + h * HEAD_DIM:O_SQ + (h + 1) * HEAD_DIM], cos, sin).astype(BF16))
            sks.append(jnp.full((L, 1), sinks_ref[h], F32))
        q_st = jnp.concatenate(qs, axis=0)
        sk = jnp.concatenate(sks, axis=0)
        s = _dot_nt(q_st, kcat)
        between()
        s = jnp.where(band, s * scale, -jnp.inf)
        m = jnp.maximum(jnp.max(s, axis=-1, keepdims=True), sk)
        p = jnp.exp(s - m)
        l = jnp.sum(p, axis=-1, keepdims=True) + jnp.exp(sk - m)
        o = _dot(p.astype(BF16), vcat)
        between()
        o = o * (1.0 / l)
        for g in range(GQA_GROUP):
            h = kv * GQA_GROUP + g
            gate = _silu(z_ref[:, O_SG + h * HEAD_DIM:O_SG + (h + 1) * HEAD_DIM])
            y_ref[:, h * HEAD_DIM:(h + 1) * HEAD_DIM] = (o[g * L:(g + 1) * L] * gate).astype(BF16)

    mem_s, mem_l, mem_o = [], [], []
    for h in range(MEM_HEADS):
        q = z_ref[:, O_CQ + h * HEAD_DIM:O_CQ + (h + 1) * HEAD_DIM].astype(BF16)
        mem_s.append(_dot_nt(q, mk_ref[:, h * HEAD_DIM:(h + 1) * HEAD_DIM]))
        between()
    for h in range(MEM_HEADS):
        s = mem_s[h] * scale
        m = jnp.max(s, axis=-1, keepdims=True)
        p = jnp.exp(s - m)
        mem_l.append(jnp.sum(p, axis=-1, keepdims=True))
        mem_o.append(_dot(p.astype(BF16), mv_ref[:, h * HEAD_DIM:(h + 1) * HEAD_DIM]))
        between()
    for h in range(MEM_HEADS):
        o = mem_o[h] * (1.0 / mem_l[h])
        gate = _silu(z_ref[:, O_CG + h * HEAD_DIM:O_CG + (h + 1) * HEAD_DIM])
        col = SWA_W + ML_W + h * HEAD_DIM
        y_ref[:, col:col + HEAD_DIM] = (o * gate).astype(BF16)

    lane = lax.broadcasted_iota(jnp.int32, (1, GATE_LANES), 1)
    zg = zg_ref[...]
    gates = jnp.where(lane < LANE_F, zg, _log_sigmoid(zg))
    gates = jnp.where(lane < 2 * ML_HEADS, gates, 0.0)
    rr = lax.broadcasted_iota(jnp.int32, (L, L), 0)
    cc = lax.broadcasted_iota(jnp.int32, (L, L), 1)
    causal = rr >= cc
    tri = jnp.where(causal, 1.0, 0.0).astype(BF16)
    csum = _exact_sel_dot(tri, gates)
    gates_t = gates.T
    csum_t = csum.T
    half_lo = lane < ML_DQK
    between()
    heads = []
    for h in range(ML_HEADS):
        pair, half = divmod(h, 2)
        hd = dict(pair=pair, half=half)
        hmask = half_lo if half == 0 else jnp.logical_not(half_lo)
        q_pair = z_ref[:, O_MQ + pair * 128:O_MQ + (pair + 1) * 128]
        hd["k_pair"] = z_ref[:, O_MK + pair * 128:O_MK + (pair + 1) * 128] * (ML_DQK ** -0.5)
        hd["v"] = z_ref[:, O_MV + h * ML_DV:O_MV + (h + 1) * ML_DV].astype(BF16)
        hd["q"] = jnp.where(hmask, q_pair, 0.0)
        q_hb = hd["q"].astype(BF16)
        hd["bt_c"] = csum[:, LANE_F + h:LANE_F + h + 1]
        hd["it_c"] = gates[:, LANE_I + h:LANE_I + h + 1]
        hd["m_prev"] = m_ref[:, h:h + 1]
        ct_pair = ct_ref[pair * 128:(pair + 1) * 128, :]
        hd["qk"] = _dot_nt(q_hb, hd["k_pair"].astype(BF16))
        hd["state_read"] = _dot(q_hb, ct_pair.astype(BF16))
        heads.append(hd)
        between()
    for h, hd in enumerate(heads):
        bt_r = csum_t[LANE_F + h:LANE_F + h + 1, :]
        it_r = gates_t[LANE_I + h:LANE_I + h + 1, :]
        dlog = jnp.where(causal, hd["bt_c"] - bt_r + it_r, -jnp.inf)
        inter = hd["bt_c"] + hd["m_prev"]
        hd["m_t"] = jnp.maximum(inter, jnp.max(dlog, axis=-1, keepdims=True))
        hd["w_state"] = jnp.exp(inter - hd["m_t"])
        a = jnp.exp(dlog - hd["m_t"]) * hd["qk"]
        hd["a_sum"] = jnp.sum(a, axis=-1, keepdims=True)
        hd["intra"] = _dot(a.astype(BF16), hd["v"])
        between()
    for h, hd in enumerate(heads):
        num = hd["intra"] + hd["w_state"] * hd["state_read"]
        n_pair = n_ref[:, hd["pair"] * 128:(hd["pair"] + 1) * 128]
        den = hd["a_sum"] + hd["w_state"] * jnp.sum(hd["q"] * n_pair, axis=-1, keepdims=True)
        denom = jnp.maximum(jnp.abs(den), jnp.exp(-hd["m_t"]))
        hid = num * (1.0 / denom)
        o_gate = _sigmoid(z_ref[:, O_MO + h * ML_DV:O_MO + (h + 1) * ML_DV])
        gate = _silu(z_ref[:, O_MG + h * ML_DV:O_MG + (h + 1) * ML_DV])
        col = SWA_W + h * ML_DV
        y_ref[:, col:col + ML_DV] = ((hid * o_gate) * gate).astype(BF16)

        hd["m_new"] = hd["m_t"][L - 1:L, :]
        bt_last = hd["bt_c"][L - 1:L, :]
        w_s = jnp.exp(bt_last - hd["bt_c"] + hd["it_c"] - hd["m_new"])
        hd["decay"] = jnp.exp(bt_last + hd["m_prev"] - hd["m_new"])
        kw = hd["k_pair"] * w_s
        hd["ksum"] = jnp.sum(kw, axis=0, keepdims=True)
        hd["upd"] = _dot(kw.T.astype(BF16), hd["v"])
        between()
    for h, hd in enumerate(heads):
        half, decay = hd["half"], hd["decay"]
        rows = slice(h * ML_DQK, (h + 1) * ML_DQK)
        ct_ref[rows, :] = decay * ct_ref[rows, :] + hd["upd"][half * ML_DQK:(half + 1) * ML_DQK, :]
        n_ref[:, rows] = decay * n_ref[:, rows] + hd["ksum"][:, half * ML_DQK:(half + 1) * ML_DQK]
        m_ref[:, h:h + 1] = hd["m_new"]


def _prompt(sinks, x2d, g_norm, w, b, wg, bg, rope_c, rope_i, mk, mv, w_out, g_final):
    n = x2d.shape[0]
    nc = n // CHUNK
    const = lambda c, s: (0, 0)
    resident = pl.Buffered(1)
    x_in = lambda c, s: (jnp.minimum(c, nc - 1), 0)
    x_res = lambda c, s: (jnp.clip(c - 2, 0, nc - 1), 0)
    return pl.pallas_call(
        _prompt_kernel,
        grid_spec=pltpu.PrefetchScalarGridSpec(
            num_scalar_prefetch=1,
            grid=(nc + 2,),
            in_specs=[
                pl.BlockSpec((CHUNK, D_MODEL), x_in),
                pl.BlockSpec((CHUNK, D_MODEL), x_res),
                pl.BlockSpec((1, D_MODEL), const, pipeline_mode=resident),
                pl.BlockSpec((D_MODEL, D_MAIN), const, pipeline_mode=resident),
                pl.BlockSpec((1, D_MAIN), const, pipeline_mode=resident),
                pl.BlockSpec((D_MODEL, GATE_LANES), const, pipeline_mode=resident),
                pl.BlockSpec((1, GATE_LANES), const, pipeline_mode=resident),
                pl.BlockSpec((2, nc, HEAD_DIM), lambda c, s: (0, 0, 0), pipeline_mode=resident),
                pl.BlockSpec((4, CHUNK, HEAD_DIM), lambda c, s: (0, 0, 0), pipeline_mode=resident),
                pl.BlockSpec((MEM_LEN, MEM_W), const, pipeline_mode=resident),
                pl.BlockSpec((MEM_LEN, MEM_W), const, pipeline_mode=resident),
                pl.BlockSpec((D_MIX, D_MODEL), const, pipeline_mode=resident),
                pl.BlockSpec((1, D_MODEL), const, pipeline_mode=resident),
            ],
            out_specs=[
                pl.BlockSpec((CHUNK, D_MODEL), x_res),
                pl.BlockSpec((CHUNK, SWA_KV_W), const),
                pl.BlockSpec((CHUNK, SWA_KV_W), const),
                pl.BlockSpec((ML_QK_W, ML_DV), const),
                pl.BlockSpec((1, ML_QK_W), const),
                pl.BlockSpec((1, GATE_LANES), const),
            ],
            scratch_shapes=[
                pltpu.VMEM((CHUNK, SWA_KV_W), BF16),
                pltpu.VMEM((CHUNK, SWA_KV_W), BF16),
                pltpu.VMEM((CHUNK, D_MODEL), BF16),
                pltpu.VMEM((CHUNK, D_MAIN), F32),
                pltpu.VMEM((CHUNK, GATE_LANES), F32),
                pltpu.VMEM((CHUNK, D_MAIN), F32),
                pltpu.VMEM((CHUNK, GATE_LANES), F32),
                pltpu.VMEM((CHUNK, D_MIX), BF16),
                pltpu.VMEM((CHUNK, D_MIX), BF16),
                pltpu.VMEM((CHUNK, D_MODEL), F32),
            ],
        ),
        out_shape=[
            jax.ShapeDtypeStruct((n, D_MODEL), F32),
            jax.ShapeDtypeStruct((CHUNK, SWA_KV_W), F32),
            jax.ShapeDtypeStruct((CHUNK, SWA_KV_W), F32),
            jax.ShapeDtypeStruct((ML_QK_W, ML_DV), F32),
            jax.ShapeDtypeStruct((1, ML_QK_W), F32),
            jax.ShapeDtypeStruct((1, GATE_LANES), F32),
        ],
        compiler_params=pltpu.CompilerParams(
            dimension_semantics=("arbitrary",), vmem_limit_bytes=VMEM_LIMIT),
        name="prompt",
    )(sinks, x2d, x2d, g_norm, w, b, wg, bg, rope_c, rope_i, mk, mv, w_out, g_final)


OUT_TM = 512


def _outproj_kernel(y_ref, x_ref, w_ref, g_ref, o_ref):
    acc = _dot(y_ref[...], w_ref[...]) + x_ref[...]
    ms = jnp.mean(acc * acc, axis=-1, keepdims=True)
    o_ref[...] = (acc * lax.rsqrt(ms + EPS)) * g_ref[...]


def _outproj(y, x2d, w, g):
    n = x2d.shape[0]
    return pl.pallas_call(
        _outproj_kernel,
        grid=(n // OUT_TM,),
        in_specs=[
            pl.BlockSpec((OUT_TM, D_MIX), lambda i: (i, 0)),
            pl.BlockSpec((OUT_TM, D_MODEL), lambda i: (i, 0)),
            pl.BlockSpec((D_MIX, D_MODEL), lambda i: (0, 0)),
            pl.BlockSpec((1, D_MODEL), lambda i: (0, 0)),
        ],
        out_specs=pl.BlockSpec((OUT_TM, D_MODEL), lambda i: (i, 0)),
        out_shape=jax.ShapeDtypeStruct((n, D_MODEL), F32),
        compiler_params=pltpu.CompilerParams(
            dimension_semantics=("arbitrary",), vmem_limit_bytes=VMEM_LIMIT),
        name="outproj",
    )(y, x2d, w, g)


SB = 16
SR = SB * DEC_SEQ
SEQ_UNROLL = 2


def _mix_sample_kernel(sinks_ref, z_ref, zg_ref, cos_ref, sin_ref, ck_ref, cv_ref,
                       c_ref, nst_ref, nrep_ref, mrep_ref,
                       y_ref, ko_ref, vo_ref, co_ref, no_ref, mo_ref,
                       q_s, k_s, o_s):
    T = DEC_SEQ
    scale = HEAD_DIM ** -0.5
    cos = cos_ref[...]
    sin = sin_ref[...]

    for h in range(SWA_HEADS):
        q_s[:, h * HEAD_DIM:(h + 1) * HEAD_DIM] = _rope(
            z_ref[:, O_SQ + h * HEAD_DIM:O_SQ + (h + 1) * HEAD_DIM], cos, sin)
    for kv in range(SWA_KV_HEADS):
        k_s[:, kv * HEAD_DIM:(kv + 1) * HEAD_DIM] = _rope(
            z_ref[:, O_SK + kv * HEAD_DIM:O_SK + (kv + 1) * HEAD_DIM], cos, sin)

    KP = 2 * WINDOW
    rt = lax.broadcasted_iota(jnp.int32, (GQA_GROUP * T, KP), 0) & (T - 1)
    cj = lax.broadcasted_iota(jnp.int32, (GQA_GROUP * T, KP), 1)
    mask = (cj > rt) & (cj <= rt + WINDOW)
    kpad = jnp.zeros((KP - WINDOW - T, HEAD_DIM), F32)
    NKV = SWA_KV_HEADS
    SEQ_ROWS = WINDOW * NKV

    keep = SEQ_ROWS - T * NKV
    sink_cols = [jnp.concatenate([jnp.full((T, 1), sinks_ref[kv * GQA_GROUP + g], F32)
                                  for g in range(GQA_GROUP)], axis=0) for kv in range(NKV)]

    def per_group(i, carry):
        chains = [(i * SEQ_UNROLL + j, kv) for j in range(SEQ_UNROLL) for kv in range(NKV)]
        scores, values = [], []
        for b, kv in chains:
            rows = pl.ds(pl.multiple_of(b * T, T), T)
            base = pl.multiple_of(b * SEQ_ROWS, SEQ_ROWS)
            cols = slice(kv * HEAD_DIM, (kv + 1) * HEAD_DIM)
            k_new = k_s[rows, cols]
            v_new = z_ref[rows, O_SV + kv * HEAD_DIM:O_SV + (kv + 1) * HEAD_DIM]
            if kv == 0:
                ko_ref[pl.ds(base, keep), :] = ck_ref[pl.ds(base + T * NKV, keep), :]
                vo_ref[pl.ds(base, keep), :] = cv_ref[pl.ds(base + T * NKV, keep), :]
            ko_ref[pl.ds(base + keep + kv, T, stride=NKV), :] = k_new
            vo_ref[pl.ds(base + keep + kv, T, stride=NKV), :] = v_new
            kc = ck_ref[pl.ds(base + kv, WINDOW, stride=NKV), :]
            vc = cv_ref[pl.ds(base + kv, WINDOW, stride=NKV), :]
            k_all = jnp.concatenate([kc, k_new, kpad], axis=0).astype(BF16)
            values.append(jnp.concatenate([vc, v_new, kpad], axis=0).astype(BF16))
            q_st = jnp.concatenate([q_s[rows, (kv * GQA_GROUP + g) * HEAD_DIM:(kv * GQA_GROUP + g + 1) * HEAD_DIM]
                                    for g in range(GQA_GROUP)], axis=0).astype(BF16)
            scores.append(_dot_nt(q_st, k_all))
        probs, norms = [], []
        for (b, kv), s in zip(chains, scores):
            s = jnp.where(mask, s * scale, -jnp.inf)
            m = jnp.maximum(jnp.max(s, axis=-1, keepdims=True), sink_cols[kv])
            p = jnp.exp(s - m)
            norms.append(jnp.sum(p, axis=-1, keepdims=True) + jnp.exp(sink_cols[kv] - m))
            probs.append(p.astype(BF16))
        outs = [_dot(p, v) for p, v in zip(probs, values)]
        for (b, kv), o, l in zip(chains, outs, norms):
            rows = pl.ds(pl.multiple_of(b * T, T), T)
            o = o * (1.0 / l)
            for g in range(GQA_GROUP):
                h = kv * GQA_GROUP + g
                o_s[rows, h * HEAD_DIM:(h + 1) * HEAD_DIM] = o[g * T:(g + 1) * T]
        return carry

    lax.fori_loop(0, SB // SEQ_UNROLL, per_group, 0)

    for h in range(SWA_HEADS):
        cols = slice(h * HEAD_DIM, (h + 1) * HEAD_DIM)
        gate = _silu(z_ref[:, O_SG + h * HEAD_DIM:O_SG + (h + 1) * HEAD_DIM])
        y_ref[:, cols] = (o_s[:, cols] * gate).astype(BF16)

    R = SR
    lane = lax.broadcasted_iota(jnp.int32, (1, GATE_LANES), 1)
    zg = zg_ref[...]
    gates = jnp.where(lane < LANE_F, zg, _log_sigmoid(zg))
    gates = jnp.where(lane < 2 * ML_HEADS, gates, 0.0)
    rr = lax.broadcasted_iota(jnp.int32, (R, R), 0)
    cc = lax.broadcasted_iota(jnp.int32, (R, R), 1)
    same_seq = (rr >> 3) == (cc >> 3)
    causal = same_seq & (rr >= cc)
    tri = jnp.where(causal, 1.0, 0.0).astype(BF16)
    csum = _exact_sel_dot(tri, gates)
    gates_t = gates.T
    csum_t = csum.T
    half_lo = lane < ML_DQK
    seq_of_col = lax.broadcasted_iota(jnp.int32, (SB, 1, R), 2) >> 3
    seq_id = lax.broadcasted_iota(jnp.int32, (SB, 1, R), 0)
    own_cols = seq_of_col == seq_id
    own_blk = ((lax.broadcasted_iota(jnp.int32, (R, SB * 128), 0) >> 3)
               == (lax.broadcasted_iota(jnp.int32, (R, SB * 128), 1) >> 7))
    k_t = [(z_ref[:, O_MK + p * 128:O_MK + (p + 1) * 128] * (ML_DQK ** -0.5)).T for p in range(2)]
    mrep = mrep_ref[...]
    tok3 = lax.broadcasted_iota(jnp.int32, (SB, T, 1), 1)
    mo_ref[...] = jnp.zeros_like(mo_ref)

    def per_seq_value(col):
        return jnp.max(col.reshape(SB, T, 1), axis=1, keepdims=True)

    def last_of_seq(col):
        c3 = jnp.where(tok3 == T - 1, col.reshape(SB, T, 1), -jnp.inf)
        return jnp.broadcast_to(jnp.max(c3, axis=1, keepdims=True), (SB, T, 1)).reshape(R, 1)

    for h in range(ML_HEADS):
        pair, half = divmod(h, 2)
        hmask = half_lo if half == 0 else jnp.logical_not(half_lo)
        q_pair = z_ref[:, O_MQ + pair * 128:O_MQ + (pair + 1) * 128]
        k_pair = z_ref[:, O_MK + pair * 128:O_MK + (pair + 1) * 128] * (ML_DQK ** -0.5)
        v_f = z_ref[:, O_MV + h * ML_DV:O_MV + (h + 1) * ML_DV]
        v_h = v_f.astype(BF16)
        q_h = jnp.where(hmask, q_pair, 0.0)
        q_hb = q_h.astype(BF16)
        k_pb = k_pair.astype(BF16)

        bt_c = csum[:, LANE_F + h:LANE_F + h + 1]
        it_c = gates[:, LANE_I + h:LANE_I + h + 1]
        bt_r = csum_t[LANE_F + h:LANE_F + h + 1, :]
        it_r = gates_t[LANE_I + h:LANE_I + h + 1, :]
        m_prev = mrep[:, h:h + 1]

        dlog = jnp.where(causal, bt_c - bt_r + it_r, -jnp.inf)
        inter = bt_c + m_prev
        m_t = jnp.maximum(inter, jnp.max(dlog, axis=-1, keepdims=True))
        w_intra = jnp.exp(dlog - m_t)
        w_state = jnp.exp(inter - m_t)
        a = w_intra * _dot_nt(q_hb, k_pb)

        ct_st = c_ref[:, 2 * pair:2 * pair + 2].reshape(SB * 128, ML_DV).astype(BF16)
        q_blk = jnp.where(own_blk, jnp.tile(q_h, (1, SB)), 0.0).astype(BF16)
        num_state = _dot(q_blk, ct_st)

        num = _dot(a.astype(BF16), v_h) + w_state * num_state
        n_pair = nrep_ref[:, pair * 128:(pair + 1) * 128]
        den = jnp.sum(a, axis=-1, keepdims=True) + w_state * jnp.sum(q_h * n_pair, axis=-1, keepdims=True)
        denom = jnp.maximum(jnp.abs(den), jnp.exp(-m_t))
        hid = num * (1.0 / denom)
        o_gate = _sigmoid(z_ref[:, O_MO + h * ML_DV:O_MO + (h + 1) * ML_DV])
        gate = _silu(z_ref[:, O_MG + h * ML_DV:O_MG + (h + 1) * ML_DV])
        col = SWA_W + h * ML_DV
        y_ref[:, col:col + ML_DV] = ((hid * o_gate) * gate).astype(BF16)

        m_new = last_of_seq(m_t)
        bt_last = last_of_seq(bt_c)
        w_s = jnp.exp(bt_last - bt_c + it_c - m_new)
        decay = jnp.exp(bt_last + m_prev - m_new)
        decay_seq = per_seq_value(decay)

        kt_h = k_t[pair][half * ML_DQK:(half + 1) * ML_DQK, :]
        lhs = jnp.where(own_cols, kt_h[None, :, :], 0.0).reshape(SB * ML_DQK, R).astype(BF16)
        upd = _dot(lhs, (v_f * w_s).astype(BF16)).reshape(SB, ML_DQK, ML_DV)
        co_ref[:, h] = decay_seq * c_ref[:, h] + upd

        kw = (k_pair * w_s).reshape(SB, T, 128)
        ksum = jnp.sum(kw, axis=1)
        n_old = nst_ref[:, h * ML_DQK:(h + 1) * ML_DQK]
        dec2 = decay_seq.reshape(SB, 1)
        no_ref[:, h * ML_DQK:(h + 1) * ML_DQK] = dec2 * n_old + ksum[:, half * ML_DQK:(half + 1) * ML_DQK]
        mo_ref[:, h:h + 1] = per_seq_value(m_new).reshape(SB, 1)


def _mix_sample(sinks, z, zg, cos, sin, ck, cv, cst, nst, nrep, mrep):
    nb = cst.shape[0]
    steps = nb // SB
    const = lambda i, s: (0, 0)
    cache_rows = SB * WINDOW * SWA_KV_HEADS
    return pl.pallas_call(
        _mix_sample_kernel,
        grid_spec=pltpu.PrefetchScalarGridSpec(
            num_scalar_prefetch=1,
            grid=(steps,),
            in_specs=[
                pl.BlockSpec((SR, D_MAIN), lambda i, s: (i, 0)),
                pl.BlockSpec((SR, GATE_LANES), lambda i, s: (i, 0)),
                pl.BlockSpec((SR, HEAD_DIM), const),
                pl.BlockSpec((SR, HEAD_DIM), const),
                pl.BlockSpec((cache_rows, HEAD_DIM), lambda i, s: (i, 0)),
                pl.BlockSpec((cache_rows, HEAD_DIM), lambda i, s: (i, 0)),
                pl.BlockSpec((SB, ML_HEADS, ML_DQK, ML_DV), lambda i, s: (i, 0, 0, 0)),
                pl.BlockSpec((SB, ML_QK_W), lambda i, s: (i, 0)),
                pl.BlockSpec((SR, ML_QK_W), lambda i, s: (i, 0)),
                pl.BlockSpec((SR, GATE_LANES), lambda i, s: (i, 0)),
            ],
            out_specs=[
                pl.BlockSpec((SR, SWA_W + ML_W), lambda i, s: (i, 0)),
                pl.BlockSpec((cache_rows, HEAD_DIM), lambda i, s: (i, 0)),
                pl.BlockSpec((cache_rows, HEAD_DIM), lambda i, s: (i, 0)),
                pl.BlockSpec((SB, ML_HEADS, ML_DQK, ML_DV), lambda i, s: (i, 0, 0, 0)),
                pl.BlockSpec((SB, ML_QK_W), lambda i, s: (i, 0)),
                pl.BlockSpec((SB, GATE_LANES), lambda i, s: (i, 0)),
            ],
            scratch_shapes=[
                pltpu.VMEM((SR, SWA_W), F32),
                pltpu.VMEM((SR, SWA_KV_W), F32),
                pltpu.VMEM((SR, SWA_W), F32),
            ],
        ),
        out_shape=[
            jax.ShapeDtypeStruct((nb * DEC_SEQ, SWA_W + ML_W), BF16),
            jax.ShapeDtypeStruct(ck.shape, F32),
            jax.ShapeDtypeStruct(cv.shape, F32),
            jax.ShapeDtypeStruct((nb, ML_HEADS, ML_DQK, ML_DV), F32),
            jax.ShapeDtypeStruct((nb, ML_QK_W), F32),
            jax.ShapeDtypeStruct((nb, GATE_LANES), F32),
        ],
        compiler_params=pltpu.CompilerParams(
            dimension_semantics=("arbitrary",), vmem_limit_bytes=VMEM_LIMIT),
        name="mix_sample",
    )(sinks, z, zg, cos, sin, ck, cv, cst, nst, nrep, mrep)


MB = 8
MR = MB * DEC_SEQ


def _mem_sample_kernel(cq_ref, cg_ref, mk_ref, mv_ref, y_ref, o_s):
    T = DEC_SEQ
    scale = HEAD_DIM ** -0.5
    zpad = jnp.zeros((T, HEAD_DIM), F32)

    seq_rows = MEM_LEN * MEM_HEADS

    def per_group(i, carry):
        chains = [(i * SEQ_UNROLL + j, h) for j in range(SEQ_UNROLL) for h in range(MEM_HEADS)]
        scores = []
        for b, h in chains:
            rows = pl.ds(pl.multiple_of(b * T, T), T)
            base = pl.multiple_of(b * seq_rows, seq_rows)
            kb = mk_ref[pl.ds(base + h, MEM_LEN, stride=MEM_HEADS), :].astype(BF16)
            q = jnp.concatenate([cq_ref[rows, h * HEAD_DIM:(h + 1) * HEAD_DIM], zpad], axis=0).astype(BF16)
            scores.append(_dot_nt(q, kb))
        probs, norms = [], []
        for s in scores:
            s = s * scale
            p = jnp.exp(s - jnp.max(s, axis=-1, keepdims=True))
            norms.append(jnp.sum(p, axis=-1, keepdims=True))
            probs.append(p.astype(BF16))
        outs = []
        for (b, h), p in zip(chains, probs):
            base = pl.multiple_of(b * seq_rows, seq_rows)
            vb = mv_ref[pl.ds(base + h, MEM_LEN, stride=MEM_HEADS), :].astype(BF16)
            outs.append(_dot(p, vb))
        for (b, h), o, l in zip(chains, outs, norms):
            rows = pl.ds(pl.multiple_of(b * T, T), T)
            o_s[rows, h * HEAD_DIM:(h + 1) * HEAD_DIM] = (o * (1.0 / l))[0:T]
        return carry

    lax.fori_loop(0, MB // SEQ_UNROLL, per_group, 0)
    y_ref[...] = (o_s[...] * _silu(cg_ref[...])).astype(BF16)


def _mem_sample(z, mk, mv):
    nb = z.shape[0] // DEC_SEQ
    cache_rows = MB * MEM_LEN * MEM_HEADS
    return pl.pallas_call(
        _mem_sample_kernel,
        grid=(nb // MB,),
        in_specs=[
            pl.BlockSpec((MR, MEM_W), lambda i: (i, O_CQ // MEM_W)),
            pl.BlockSpec((MR, MEM_W), lambda i: (i, O_CG // MEM_W)),
            pl.BlockSpec((cache_rows, HEAD_DIM), lambda i: (i, 0)),
            pl.BlockSpec((cache_rows, HEAD_DIM), lambda i: (i, 0)),
        ],
        out_specs=pl.BlockSpec((MR, MEM_W), lambda i: (i, 0)),
        out_shape=jax.ShapeDtypeStruct((nb * DEC_SEQ, MEM_W), BF16),
        scratch_shapes=[pltpu.VMEM((MR, MEM_W), F32)],
        compiler_params=pltpu.CompilerParams(
            dimension_semantics=("arbitrary",), vmem_limit_bytes=VMEM_LIMIT),
        name="mem_sample",
    )(z, z, mk, mv)


def _rope_cos_sin(pos):
    half = HEAD_DIM // 2
    inv = jnp.power(ROPE_THETA, -(jnp.arange(half, dtype=F32) * 2.0 / HEAD_DIM))
    ang = pos.astype(F32)[:, None] * inv[None, :]
    return jnp.cos(ang), jnp.sin(ang)


def _rope_tables(pos):
    cos, sin = _rope_cos_sin(pos)
    return jnp.concatenate([cos, cos], axis=-1), jnp.concatenate([-sin, sin], axis=-1)


def _rope_split_tables(n_chunks):
    ca, sa = _rope_cos_sin(jnp.arange(n_chunks, dtype=jnp.int32) * CHUNK)
    cb, sb = _rope_cos_sin(jnp.arange(CHUNK, dtype=jnp.int32))
    dup = lambda t: jnp.concatenate([t, t], axis=-1)
    sgn = lambda t: jnp.concatenate([-t, t], axis=-1)
    return jnp.stack([dup(ca), dup(sa)]), jnp.stack([dup(cb), dup(sb), sgn(cb), sgn(sb)])


def _relayout_in_proj(w_in, b_in):
    w_t = jnp.swapaxes(w_in, 0, 1)
    main_t = jnp.concatenate([w_t[_R_SQ:_R_MI], w_t[_R_MO:_R_END]], axis=0).astype(BF16)
    pad_t = jnp.zeros((GATE_LANES - 2 * ML_HEADS, D_MODEL), BF16)
    gate_t = jnp.concatenate([w_t[_R_MI:_R_MO].astype(BF16), pad_t], axis=0)
    b_main = jnp.concatenate([b_in[_R_SQ:_R_MI], b_in[_R_MO:_R_END]])[None, :]
    b_gate = jnp.pad(b_in[_R_MI:_R_MO], (0, GATE_LANES - 2 * ML_HEADS))[None, :]
    return jnp.swapaxes(main_t, 0, 1), b_main, jnp.swapaxes(gate_t, 0, 1), b_gate


def _layer(xp, xs, mem, ck, cv, c_st, n_st, m_st, cmk, cmv,
           g_norm, w_in, b_in, sinks, g_mem, w_mem_kv, w_out, g_final):
    bp, sp, _ = xp.shape
    bs, ts, _ = xs.shape
    xp2 = xp.reshape(bp * sp, D_MODEL)
    xs2 = xs.reshape(bs * ts, D_MODEL)
    w_main, b_main, w_gate, b_gate = _relayout_in_proj(w_in, b_in)
    g_norm2 = g_norm[None, :]
    w_out_b = w_out.astype(BF16)
    g_final2 = g_final[None, :]
    sinks_flat = sinks.reshape(SWA_HEADS)

    memkv = _memkv(mem.reshape(MEM_LEN, D_MODEL), g_mem[None, :], w_mem_kv.astype(BF16))
    mem_k = memkv[:, :MEM_W]
    mem_v = memkv[:, MEM_W:]
    rope_c, rope_i = _rope_split_tables(sp // CHUNK)
    out_p, pk, pv, ct, n_p, m_p = _prompt(sinks_flat, xp2, g_norm2, w_main, b_main, w_gate, b_gate,
                                           rope_c, rope_i, mem_k.astype(BF16), mem_v.astype(BF16),
                                           w_out_b, g_final2)

    zs, zgs = _proj(xs2, g_norm2, w_main, b_main, w_gate, b_gate)
    cos_s, sin_s = _rope_tables(PAST_LEN + jnp.arange(ts, dtype=jnp.int32))
    cos_s = jnp.tile(cos_s, (SB, 1))
    sin_s = jnp.tile(sin_s, (SB, 1))
    nrep = jnp.repeat(n_st.reshape(bs, ML_QK_W), ts, axis=0)
    m_pad = jnp.pad(m_st, ((0, 0), (0, GATE_LANES - ML_HEADS)))
    mrep = jnp.repeat(m_pad, ts, axis=0)
    ya, sk_o, sv_o, ct_o, n_o, m_o = _mix_sample(
        sinks_flat, zs, zgs, cos_s, sin_s,
        ck.reshape(bs * WINDOW * SWA_KV_HEADS, HEAD_DIM), cv.reshape(bs * WINDOW * SWA_KV_HEADS, HEAD_DIM),
        jnp.swapaxes(c_st, -1, -2), n_st.reshape(bs, ML_QK_W), nrep, mrep)
    yb = _mem_sample(zs, cmk.reshape(bs * MEM_LEN * MEM_HEADS, HEAD_DIM),
                     cmv.reshape(bs * MEM_LEN * MEM_HEADS, HEAD_DIM))
    out_s = _outproj(jnp.concatenate([ya, yb], axis=-1), xs2, w_out_b, g_final2)
    c_o = jnp.swapaxes(ct_o, -1, -2)

    p_state = (
        pk.reshape(bp, WINDOW, SWA_KV_HEADS, HEAD_DIM),
        pv.reshape(bp, WINDOW, SWA_KV_HEADS, HEAD_DIM),
        ct.reshape(ML_HEADS, ML_DQK, ML_DV).transpose(0, 2, 1)[None],
        n_p.reshape(bp, ML_HEADS, ML_DQK),
        m_p[:, :ML_HEADS],
        mem_k.reshape(bp, MEM_LEN, MEM_HEADS, HEAD_DIM),
        mem_v.reshape(bp, MEM_LEN, MEM_HEADS, HEAD_DIM),
    )
    s_state = (
        sk_o.reshape(bs, WINDOW, SWA_KV_HEADS, HEAD_DIM),
        sv_o.reshape(bs, WINDOW, SWA_KV_HEADS, HEAD_DIM),
        c_o,
        n_o.reshape(bs, ML_HEADS, ML_DQK),
        m_o[:, :ML_HEADS],
    )
    return out_p.reshape(bp, sp, D_MODEL), out_s.reshape(bs, ts, D_MODEL), p_state, s_state


def kernel(x_prompt, x_sample, mem_prompt, cache_swa_k, cache_swa_v, state_mlstm_C, state_mlstm_n,
           state_mlstm_m, cache_mem_k, cache_mem_v, g_norm, w_in, b_in, swa_sinks, g_mem, w_mem_kv,
           w_out, g_final):
    depth = g_norm.shape[0]
    assert depth == 1 and x_prompt.shape[0] == 1
    y_p, y_s, p_state, s_state = _layer(
        x_prompt, x_sample, mem_prompt, cache_swa_k[0], cache_swa_v[0], state_mlstm_C[0],
        state_mlstm_n[0], state_mlstm_m[0], cache_mem_k[0], cache_mem_v[0],
        g_norm[0], w_in[0], b_in[0], swa_sinks[0], g_mem[0], w_mem_kv[0], w_out[0], g_final)
    return (y_p, y_s) + tuple(s[None] for s in p_state) + tuple(s[None] for s in s_state)
```

```python
import functools
import math

import jax
import jax.numpy as jnp
from jax import lax
from jax.experimental import pallas as pl
from jax.experimental.pallas import tpu as pltpu

F32 = jnp.float32
BF16 = jnp.bfloat16

D_MODEL = 2048
HEAD_DIM = 128
SWA_HEADS = 8
SWA_KV_HEADS = 2
GQA_GROUP = 4
WINDOW = 128
ML_HEADS = 4
ML_DQK = 64
ML_DV = 128
MEM_HEADS = 4
MEM_LEN = 256
CHUNK = 128
ROPE_THETA = 10000.0
EPS = 1e-6
PAST_LEN = 16384
DEC_SEQ = 8

SWA_W = SWA_HEADS * HEAD_DIM
SWA_KV_W = SWA_KV_HEADS * HEAD_DIM
ML_W = ML_HEADS * ML_DV
ML_QK_W = ML_HEADS * ML_DQK
MEM_W = MEM_HEADS * HEAD_DIM
D_MIX = SWA_W + ML_W + MEM_W

_IN_WIDTHS = (SWA_W, SWA_KV_W, SWA_KV_W, SWA_W, ML_QK_W, ML_QK_W, ML_W, ML_HEADS, ML_HEADS, ML_W, ML_W, MEM_W, MEM_W)
_IN_OFFS = [0]
for _w in _IN_WIDTHS:
    _IN_OFFS.append(_IN_OFFS[-1] + _w)
(_R_SQ, _R_SK, _R_SV, _R_SG, _R_MQ, _R_MK, _R_MV, _R_MI, _R_MF, _R_MO, _R_MG, _R_CQ, _R_CG, _R_END) = _IN_OFFS

O_SQ = 0
O_SK = O_SQ + SWA_W
O_SV = O_SK + SWA_KV_W
O_SG = O_SV + SWA_KV_W
O_MQ = O_SG + SWA_W
O_MK = O_MQ + ML_QK_W
O_MV = O_MK + ML_QK_W
O_MO = O_MV + ML_W
O_MG = O_MO + ML_W
O_CQ = O_MG + ML_W
O_CG = O_CQ + MEM_W
D_MAIN = O_CG + MEM_W
W_A_COLS = O_MO
W_B_COLS = D_MAIN - W_A_COLS
GATE_LANES = 128
LANE_I = 0
LANE_F = ML_HEADS

VMEM_LIMIT = 56 * 1024 * 1024

_NT = (((1,), (1,)), ((), ()))


def _dot(a, b):
    return jnp.dot(a, b, preferred_element_type=F32)


def _dot_nt(a, b):
    return lax.dot_general(a, b, _NT, preferred_element_type=F32)


def _exact_sel_dot(sel_bf16, x):
    hi = x.astype(BF16)
    r1 = x - hi.astype(F32)
    mid = r1.astype(BF16)
    lo = (r1 - mid.astype(F32)).astype(BF16)
    return _dot(sel_bf16, hi) + _dot(sel_bf16, mid) + _dot(sel_bf16, lo)


def _silu(x):
    h = 0.5 * x
    return h + h * jnp.tanh(h)


def _sigmoid(x):
    return 0.5 + 0.5 * jnp.tanh(0.5 * x)


def _log_sigmoid(x):
    return jnp.minimum(x, 0.0) - jnp.log1p(jnp.exp(-jnp.abs(x)))


PROJ_TM = 1024
PROJ_TN = 512
NORM_ROWS = 256


def _proj_kernel(x_ref, g_ref, w_ref, b_ref, wg_ref, bg_ref, z_ref, zg_ref, u_ref):
    j = pl.program_id(1)

    @pl.when(j == 0)
    def _():
        g = g_ref[...]
        for r in range(PROJ_TM // NORM_ROWS):
            rows = pl.ds(r * NORM_ROWS, NORM_ROWS)
            xf = x_ref[rows, :]
            ms = jnp.mean(xf * xf, axis=-1, keepdims=True)
            u_ref[rows, :] = ((xf * lax.rsqrt(ms + EPS)) * g).astype(BF16)
        zg_ref[...] = _dot(u_ref[...], wg_ref[...]) + bg_ref[...]

    z_ref[...] = _dot(u_ref[...], w_ref[...]) + b_ref[...]


def _proj(x2d, g, w, b, wg, bg):
    n = x2d.shape[0]
    grid = (n // PROJ_TM, D_MAIN // PROJ_TN)
    return pl.pallas_call(
        _proj_kernel,
        grid=grid,
        in_specs=[
            pl.BlockSpec((PROJ_TM, D_MODEL), lambda i, j: (i, 0)),
            pl.BlockSpec((1, D_MODEL), lambda i, j: (0, 0)),
            pl.BlockSpec((D_MODEL, PROJ_TN), lambda i, j: (0, j)),
            pl.BlockSpec((1, PROJ_TN), lambda i, j: (0, j)),
            pl.BlockSpec((D_MODEL, GATE_LANES), lambda i, j: (0, 0)),
            pl.BlockSpec((1, GATE_LANES), lambda i, j: (0, 0)),
        ],
        out_specs=[
            pl.BlockSpec((PROJ_TM, PROJ_TN), lambda i, j: (i, j)),
            pl.BlockSpec((PROJ_TM, GATE_LANES), lambda i, j: (i, 0)),
        ],
        out_shape=[
            jax.ShapeDtypeStruct((n, D_MAIN), F32),
            jax.ShapeDtypeStruct((n, GATE_LANES), F32),
        ],
        scratch_shapes=[pltpu.VMEM((PROJ_TM, D_MODEL), BF16)],
        compiler_params=pltpu.CompilerParams(
            dimension_semantics=("arbitrary", "arbitrary"), vmem_limit_bytes=VMEM_LIMIT),
        name="proj",
    )(x2d, g, w, b, wg, bg)


WPREP_TN = 256


def _wprep_kernel(wt_ref, *refs):
    o_ref = refs[-1]
    o_ref[...] = wt_ref[...].T.astype(BF16)


def _wprep_part(w_t, dst, src_row0, dst_tile0, n_tiles):
    src = pl.BlockSpec((pl.Element(WPREP_TN), pl.Element(D_MODEL)),
                       lambda j: (pl.multiple_of(j * WPREP_TN + src_row0, 8), 0))
    in_specs, args, aliases = [src], [w_t], {}
    if dst is not None:
        in_specs.append(pl.BlockSpec(memory_space=pl.ANY))
        args.append(dst)
        aliases = {1: 0}
    return pl.pallas_call(
        _wprep_kernel,
        grid=(n_tiles,),
        in_specs=in_specs,
        out_specs=pl.BlockSpec((D_MODEL, WPREP_TN), lambda j: (0, j + dst_tile0)),
        out_shape=jax.ShapeDtypeStruct((D_MODEL, D_MAIN), BF16),
        input_output_aliases=aliases,
        compiler_params=pltpu.CompilerParams(
            dimension_semantics=("arbitrary",), vmem_limit_bytes=VMEM_LIMIT),
        name="wprep",
    )(*args)


def _wprep(w_t):
    tiles_a = W_A_COLS // WPREP_TN
    part = _wprep_part(w_t, None, 0, 0, tiles_a)
    return _wprep_part(w_t, part, _R_MO, tiles_a, (D_MAIN - W_A_COLS) // WPREP_TN)


def _wgate_kernel(wt_ref, o_ref):
    rows = jnp.concatenate(
        [wt_ref[...], jnp.zeros((GATE_LANES - 2 * ML_HEADS, D_MODEL), F32)], axis=0)
    o_ref[...] = rows.T.astype(BF16)


def _wgate(w_t):
    return pl.pallas_call(
        _wgate_kernel,
        grid=(1,),
        in_specs=[pl.BlockSpec((pl.Element(2 * ML_HEADS), pl.Element(D_MODEL)), lambda j: (_R_MI, 0))],
        out_specs=pl.BlockSpec((D_MODEL, GATE_LANES), lambda j: (0, 0)),
        out_shape=jax.ShapeDtypeStruct((D_MODEL, GATE_LANES), BF16),
        name="wgate",
    )(w_t)


MEMKV_TN = 256


def _memkv_kernel(mem_ref, g_ref, w_ref, o_ref):
    xf = mem_ref[...]
    ms = jnp.mean(xf * xf, axis=-1, keepdims=True)
    u = ((xf * lax.rsqrt(ms + EPS)) * g_ref[...]).astype(BF16)
    o_ref[...] = _dot(u, w_ref[...])


def _memkv(mem2d, g, w):
    return pl.pallas_call(
        _memkv_kernel,
        grid=(2 * MEM_W // MEMKV_TN,),
        in_specs=[
            pl.BlockSpec((MEM_LEN, D_MODEL), lambda j: (0, 0)),
            pl.BlockSpec((1, D_MODEL), lambda j: (0, 0)),
            pl.BlockSpec((D_MODEL, MEMKV_TN), lambda j: (0, j)),
        ],
        out_specs=pl.BlockSpec((MEM_LEN, MEMKV_TN), lambda j: (0, j)),
        out_shape=jax.ShapeDtypeStruct((MEM_LEN, 2 * MEM_W), F32),
        compiler_params=pltpu.CompilerParams(
            dimension_semantics=("arbitrary",), vmem_limit_bytes=VMEM_LIMIT),
        name="memkv",
    )(mem2d, g, w)


OUT_TN = 256


def _rope(x, cos, sin_signed):
    return x * cos + pltpu.roll(x, HEAD_DIM // 2, axis=1) * sin_signed


IN_TN = 256
SWA_POINTS = 3 * SWA_KV_HEADS
MEM_POINTS = 2 * MEM_HEADS
MIXER_WEIGHTS = ([1.0, 3.0, 1.0] * SWA_KV_HEADS + [1.0] * MEM_HEADS + [0.5] * MEM_HEADS
                 + [1.0] + [1.5] * ML_HEADS + [1.0] * ML_HEADS + [0.5] * ML_HEADS)


def _spread(tasks, weights):
    total = sum(weights)
    bounds = [0]
    acc = 0.0
    for w in weights:
        acc += w
        bounds.append(round(len(tasks) * acc / total))
    return [tasks[a:b] for a, b in zip(bounds[:-1], bounds[1:])]


def _zip_then_rest(a, b):
    k = min(len(a), len(b))
    return [t for pair in zip(a[:k], b[:k]) for t in pair] + a[k:] + b[k:]


def _prompt_kernel(sinks_ref, xin_ref, xres_ref, gn_ref, w_ref, b_ref, wg_ref, bg_ref,
                   rope_c_ref, rope_i_ref, mk_ref, mv_ref, wout_ref, gfin_ref,
                   out_ref, ko_ref, vo_ref, ct_ref, n_ref, m_ref,
                   kprev_ref, vprev_ref, u_ref, z_ref, zg_ref, zprev_ref, zgprev_ref,
                   y_ref, yprev_ref, acc_ref):
    s = pl.program_id(0)
    nc = pl.num_programs(0) - 2
    n_in = D_MAIN // IN_TN
    n_out = D_MODEL // OUT_TN

    def in_norm():
        xf = xin_ref[...]
        ms = jnp.mean(xf * xf, axis=-1, keepdims=True)
        u_ref[...] = ((xf * lax.rsqrt(ms + EPS)) * gn_ref[...]).astype(BF16)

    def in_tile(t):
        if t == n_in:
            zg_ref[...] = _dot(u_ref[...], wg_ref[...]) + bg_ref[...]
        else:
            cols = slice(t * IN_TN, (t + 1) * IN_TN)
            z_ref[:, cols] = _dot(u_ref[...], w_ref[:, cols]) + b_ref[:, cols]

    def in_rotate(lo, hi):
        zprev_ref[:, lo:hi] = z_ref[:, lo:hi]

    def in_rotate_rest():
        in_rotate(O_MQ, O_CQ)
        zgprev_ref[...] = zg_ref[...]

    def out_tile(t):
        cols = slice(t * OUT_TN, (t + 1) * OUT_TN)
        acc_ref[:, cols] = _dot(yprev_ref[...], wout_ref[:, cols]) + xres_ref[:, cols]

    def out_finish():
        acc = acc_ref[...]
        ms = jnp.mean(acc * acc, axis=-1, keepdims=True)
        out_ref[...] = (acc * lax.rsqrt(ms + EPS)) * gfin_ref[...]

    tile_of = lambda col: col // IN_TN
    in_order = (list(range(0, tile_of(O_MQ))) + list(range(tile_of(O_CQ), n_in))
                + list(range(tile_of(O_MQ), tile_of(O_CQ))) + [n_in])
    in_tasks = []
    for t in in_order:
        in_tasks.append((functools.partial(in_tile, t), 0))
        if t == tile_of(O_MQ) - 1:
            in_tasks.append((functools.partial(in_rotate, 0, O_MQ), SWA_POINTS))
        if t == n_in - 1:
            in_tasks.append((functools.partial(in_rotate, O_CQ, D_MAIN), SWA_POINTS + MEM_POINTS))
    out_tasks = [(functools.partial(out_tile, t), 0) for t in range(n_out)] + [(out_finish, 0)]

    def mix(tasks):
        groups = iter(_spread(tasks, MIXER_WEIGHTS))
        points_done = [0]

        def between():
            for emit, first_point in next(groups):
                assert points_done[0] >= first_point
                emit()
            points_done[0] += 1

        _prompt_mixers(s - 1, sinks_ref, zprev_ref, zgprev_ref, rope_c_ref, rope_i_ref, mk_ref, mv_ref,
                       y_ref, ko_ref, vo_ref, ct_ref, n_ref, m_ref, kprev_ref, vprev_ref, between)
        assert next(groups, None) is None

    def run(tasks):
        for emit, _ in tasks:
            emit()

    @pl.when(s == 0)
    def _():
        kprev_ref[...] = jnp.zeros_like(kprev_ref)
        vprev_ref[...] = jnp.zeros_like(vprev_ref)
        ct_ref[...] = jnp.zeros_like(ct_ref)
        n_ref[...] = jnp.zeros_like(n_ref)
        m_ref[...] = jnp.zeros_like(m_ref)
        in_norm()
        run(in_tasks)
        in_rotate_rest()

    @pl.when(s == 1)
    def _():
        in_norm()
        mix(in_tasks)
        in_rotate_rest()
        yprev_ref[...] = y_ref[...]

    @pl.when((s >= 2) & (s < nc))
    def _():
        in_norm()
        mix(out_tasks[:2] + _zip_then_rest(in_tasks, out_tasks[2:]))
        in_rotate_rest()
        yprev_ref[...] = y_ref[...]

    @pl.when(s == nc)
    def _():
        mix(out_tasks)
        yprev_ref[...] = y_ref[...]

    @pl.when(s == nc + 1)
    def _():
        run(out_tasks)


def _prompt_mixers(c, sinks_ref, z_ref, zg_ref, rope_c_ref, rope_i_ref, mk_ref, mv_ref,
                   y_ref, ko_ref, vo_ref, ct_ref, n_ref, m_ref, kprev_ref, vprev_ref, between):
    L = CHUNK
    scale = HEAD_DIM ** -0.5

    cc = rope_c_ref[0, pl.ds(c, 1), :]
    sc = rope_c_ref[1, pl.ds(c, 1), :]
    cos = cc * rope_i_ref[0] - sc * rope_i_ref[1]
    sin = sc * rope_i_ref[2] + cc * rope_i_ref[3]

    ri = lax.broadcasted_iota(jnp.int32, (GQA_GROUP * L, 2 * L), 0) & (L - 1)
    cj = lax.broadcasted_iota(jnp.int32, (GQA_GROUP * L, 2 * L), 1)
    j_low = jnp.where(c > 0, 0, L)
    band = (cj > ri) & (cj <= ri + L) & (cj >= j_low)
    for kv in range(SWA_KV_HEADS):
        between()
        k_new = _rope(z_ref[:, O_SK + kv * HEAD_DIM:O_SK + (kv + 1) * HEAD_DIM], cos, sin)
        v_new = z_ref[:, O_SV + kv * HEAD_DIM:O_SV + (kv + 1) * HEAD_DIM]
        ko_ref[:, kv * HEAD_DIM:(kv + 1) * HEAD_DIM] = k_new
        vo_ref[:, kv * HEAD_DIM:(kv + 1) * HEAD_DIM] = v_new
        k_new_b = k_new.astype(BF16)
        v_new_b = v_new.astype(BF16)
        kcat = jnp.concatenate([kprev_ref[:, kv * HEAD_DIM:(kv + 1) * HEAD_DIM], k_new_b], axis=0)
        vcat = jnp.concatenate([vprev_ref[:, kv * HEAD_DIM:(kv + 1) * HEAD_DIM], v_new_b], axis=0)
        kprev_ref[:, kv * HEAD_DIM:(kv + 1) * HEAD_DIM] = k_new_b
        vprev_ref[:, kv * HEAD_DIM:(kv + 1) * HEAD_DIM] = v_new_b
        qs = []
        sks = []
        for g in range(GQA_GROUP):
            h = kv * GQA_GROUP + g
            qs.append(_rope(z_ref[:, O_SQ + h * HEAD_DIM:O_SQ + (h + 1) * HEAD_DIM], cos, sin).astype(BF16))
            sks.append(jnp.full((L, 1), sinks_ref[h], F32))
        q_st = jnp.concatenate(qs, axis=0)
        sk = jnp.concatenate(sks, axis=0)
        s = _dot_nt(q_st, kcat)
        between()
        s = jnp.where(band, s * scale, -jnp.inf)
        m = jnp.maximum(jnp.max(s, axis=-1, keepdims=True), sk)
        p = jnp.exp(s - m)
        l = jnp.sum(p, axis=-1, keepdims=True) + jnp.exp(sk - m)
        o = _dot(p.astype(BF16), vcat)
        between()
        o = o * (1.0 / l)
        for g in range(GQA_GROUP):
            h = kv * GQA_GROUP + g
            gate = _silu(z_ref[:, O_SG + h * HEAD_DIM:O_SG + (h + 1) * HEAD_DIM])
            y_ref[:, h * HEAD_DIM:(h + 1) * HEAD_DIM] = (o[g * L:(g + 1) * L] * gate).astype(BF16)

    mem_s, mem_l, mem_o = [], [], []
    for h in range(MEM_HEADS):
        q = z_ref[:, O_CQ + h * HEAD_DIM:O_CQ + (h + 1) * HEAD_DIM].astype(BF16)
        mem_s.append(_dot_nt(q, mk_ref[:, h * HEAD_DIM:(h + 1) * HEAD_DIM]))
        between()
    for h in range(MEM_HEADS):
        s = mem_s[h] * scale
        m = jnp.max(s, axis=-1, keepdims=True)
        p = jnp.exp(s - m)
        mem_l.append(jnp.sum(p, axis=-1, keepdims=True))
        mem_o.append(_dot(p.astype(BF16), mv_ref[:, h * HEAD_DIM:(h + 1) * HEAD_DIM]))
        between()
    for h in range(MEM_HEADS):
        o = mem_o[h] * (1.0 / mem_l[h])
        gate = _silu(z_ref[:, O_CG + h * HEAD_DIM:O_CG + (h + 1) * HEAD_DIM])
        col = SWA_W + ML_W + h * HEAD_DIM
        y_ref[:, col:col + HEAD_DIM] = (o * gate).astype(BF16)

    lane = lax.broadcasted_iota(jnp.int32, (1, GATE_LANES), 1)
    zg = zg_ref[...]
    gates = jnp.where(lane < LANE_F, zg, _log_sigmoid(zg))
    gates = jnp.where(lane < 2 * ML_HEADS, gates, 0.0)
    rr = lax.broadcasted_iota(jnp.int32, (L, L), 0)
    cc = lax.broadcasted_iota(jnp.int32, (L, L), 1)
    causal = rr >= cc
    tri = jnp.where(causal, 1.0, 0.0).astype(BF16)
    csum = _exact_sel_dot(tri, gates)
    gates_t = gates.T
    csum_t = csum.T
    half_lo = lane < ML_DQK
    between()
    heads = []
    for h in range(ML_HEADS):
        pair, half = divmod(h, 2)
        hd = dict(pair=pair, half=half)
        hmask = half_lo if half == 0 else jnp.logical_not(half_lo)
        q_pair = z_ref[:, O_MQ + pair * 128:O_MQ + (pair + 1) * 128]
        hd["k_pair"] = z_ref[:, O_MK + pair * 128:O_MK + (pair + 1) * 128] * (ML_DQK ** -0.5)
        hd["v"] = z_ref[:, O_MV + h * ML_DV:O_MV + (h + 1) * ML_DV].astype(BF16)
        hd["q"] = jnp.where(hmask, q_pair, 0.0)
        q_hb = hd["q"].astype(BF16)
        hd["bt_c"] = csum[:, LANE_F + h:LANE_F + h + 1]
        hd["it_c"] = gates[:, LANE_I + h:LANE_I + h + 1]
        hd["m_prev"] = m_ref[:, h:h + 1]
        ct_pair = ct_ref[pair * 128:(pair + 1) * 128, :]
        hd["qk"] = _dot_nt(q_hb, hd["k_pair"].astype(BF16))
        hd["state_read"] = _dot(q_hb, ct_pair.astype(BF16))
        heads.append(hd)
        between()
    for h, hd in enumerate(heads):
        bt_r = csum_t[LANE_F + h:LANE_F + h + 1, :]
        it_r = gates_t[LANE_I + h:LANE_I + h + 1, :]
        dlog = jnp.where(causal, hd["bt_c"] - bt_r + it_r, -jnp.inf)
        inter = hd["bt_c"] + hd["m_prev"]
        hd["m_t"] = jnp.maximum(inter, jnp.max(dlog, axis=-1, keepdims=True))
        hd["w_state"] = jnp.exp(inter - hd["m_t"])
        a = jnp.exp(dlog - hd["m_t"]) * hd["qk"]
        hd["a_sum"] = jnp.sum(a, axis=-1, keepdims=True)
        hd["intra"] = _dot(a.astype(BF16), hd["v"])
        between()
    for h, hd in enumerate(heads):
        num = hd["intra"] + hd["w_state"] * hd["state_read"]
        n_pair = n_ref[:, hd["pair"] * 128:(hd["pair"] + 1) * 128]
        den = hd["a_sum"] + hd["w_state"] * jnp.sum(hd["q"] * n_pair, axis=-1, keepdims=True)
        denom = jnp.maximum(jnp.abs(den), jnp.exp(-hd["m_t"]))
        hid = num * (1.0 / denom)
        o_gate = _sigmoid(z_ref[:, O_MO + h * ML_DV:O_MO + (h + 1) * ML_DV])
        gate = _silu(z_ref[:, O_MG + h * ML_DV:O_MG + (h + 1) * ML_DV])
        col = SWA_W + h * ML_DV
        y_ref[:, col:col + ML_DV] = ((hid * o_gate) * gate).astype(BF16)

        hd["m_new"] = hd["m_t"][L - 1:L, :]
        bt_last = hd["bt_c"][L - 1:L, :]
        w_s = jnp.exp(bt_last - hd["bt_c"] + hd["it_c"] - hd["m_new"])
        hd["decay"] = jnp.exp(bt_last + hd["m_prev"] - hd["m_new"])
        kw = hd["k_pair"] * w_s
        hd["ksum"] = jnp.sum(kw, axis=0, keepdims=True)
        hd["upd"] = _dot(kw.T.astype(BF16), hd["v"])
        between()
    for h, hd in enumerate(heads):
        half, decay = hd["half"], hd["decay"]
        rows = slice(h * ML_DQK, (h + 1) * ML_DQK)
        ct_ref[rows, :] = decay * ct_ref[rows, :] + hd["upd"][half * ML_DQK:(half + 1) * ML_DQK, :]
        n_ref[:, rows] = decay * n_ref[:, rows] + hd["ksum"][:, half * ML_DQK:(half + 1) * ML_DQK]
        m_ref[:, h:h + 1] = hd["m_new"]


def _prompt(sinks, x2d, g_norm, w, b, wg, bg, rope_c, rope_i, mk, mv, w_out, g_final):
    n = x2d.shape[0]
    nc = n // CHUNK
    const = lambda c, s: (0, 0)
    resident = pl.Buffered(1)
    x_in = lambda c, s: (jnp.minimum(c, nc - 1), 0)
    x_res = lambda c, s: (jnp.clip(c - 2, 0, nc - 1), 0)
    return pl.pallas_call(
        _prompt_kernel,
        grid_spec=pltpu.PrefetchScalarGridSpec(
            num_scalar_prefetch=1,
            grid=(nc + 2,),
            in_specs=[
                pl.BlockSpec((CHUNK, D_MODEL), x_in),
                pl.BlockSpec((CHUNK, D_MODEL), x_res),
                pl.BlockSpec((1, D_MODEL), const, pipeline_mode=resident),
                pl.BlockSpec((D_MODEL, D_MAIN), const, pipeline_mode=resident),
                pl.BlockSpec((1, D_MAIN), const, pipeline_mode=resident),
                pl.BlockSpec((D_MODEL, GATE_LANES), const, pipeline_mode=resident),
                pl.BlockSpec((1, GATE_LANES), const, pipeline_mode=resident),
                pl.BlockSpec((2, nc, HEAD_DIM), lambda c, s: (0, 0, 0), pipeline_mode=resident),
                pl.BlockSpec((4, CHUNK, HEAD_DIM), lambda c, s: (0, 0, 0), pipeline_mode=resident),
                pl.BlockSpec((MEM_LEN, MEM_W), const, pipeline_mode=resident),
                pl.BlockSpec((MEM_LEN, MEM_W), const, pipeline_mode=resident),
                pl.BlockSpec((D_MIX, D_MODEL), const, pipeline_mode=resident),
                pl.BlockSpec((1, D_MODEL), const, pipeline_mode=resident),
            ],
            out_specs=[
                pl.BlockSpec((CHUNK, D_MODEL), x_res),
                pl.BlockSpec((CHUNK, SWA_KV_W), const),
                pl.BlockSpec((CHUNK, SWA_KV_W), const),
                pl.BlockSpec((ML_QK_W, ML_DV), const),
                pl.BlockSpec((1, ML_QK_W), const),
                pl.BlockSpec((1, GATE_LANES), const),
            ],
            scratch_shapes=[
                pltpu.VMEM((CHUNK, SWA_KV_W), BF16),
                pltpu.VMEM((CHUNK, SWA_KV_W), BF16),
                pltpu.VMEM((CHUNK, D_MODEL), BF16),
                pltpu.VMEM((CHUNK, D_MAIN), F32),
                pltpu.VMEM((CHUNK, GATE_LANES), F32),
                pltpu.VMEM((CHUNK, D_MAIN), F32),
                pltpu.VMEM((CHUNK, GATE_LANES), F32),
                pltpu.VMEM((CHUNK, D_MIX), BF16),
                pltpu.VMEM((CHUNK, D_MIX), BF16),
                pltpu.VMEM((CHUNK, D_MODEL), F32),
            ],
        ),
        out_shape=[
            jax.ShapeDtypeStruct((n, D_MODEL), F32),
            jax.ShapeDtypeStruct((CHUNK, SWA_KV_W), F32),
            jax.ShapeDtypeStruct((CHUNK, SWA_KV_W), F32),
            jax.ShapeDtypeStruct((ML_QK_W, ML_DV), F32),
            jax.ShapeDtypeStruct((1, ML_QK_W), F32),
            jax.ShapeDtypeStruct((1, GATE_LANES), F32),
        ],
        compiler_params=pltpu.CompilerParams(
            dimension_semantics=("arbitrary",), vmem_limit_bytes=VMEM_LIMIT),
        name="prompt",
    )(sinks, x2d, x2d, g_norm, w, b, wg, bg, rope_c, rope_i, mk, mv, w_out, g_final)


OUT_TM = 512


def _outproj_kernel(y_ref, x_ref, w_ref, g_ref, o_ref):
    acc = _dot(y_ref[...], w_ref[...]) + x_ref[...]
    ms = jnp.mean(acc * acc, axis=-1, keepdims=True)
    o_ref[...] = (acc * lax.rsqrt(ms + EPS)) * g_ref[...]


def _outproj(y, x2d, w, g):
    n = x2d.shape[0]
    return pl.pallas_call(
        _outproj_kernel,
        grid=(n // OUT_TM,),
        in_specs=[
            pl.BlockSpec((OUT_TM, D_MIX), lambda i: (i, 0)),
            pl.BlockSpec((OUT_TM, D_MODEL), lambda i: (i, 0)),
            pl.BlockSpec((D_MIX, D_MODEL), lambda i: (0, 0)),
            pl.BlockSpec((1, D_MODEL), lambda i: (0, 0)),
        ],
        out_specs=pl.BlockSpec((OUT_TM, D_MODEL), lambda i: (i, 0)),
        out_shape=jax.ShapeDtypeStruct((n, D_MODEL), F32),
        compiler_params=pltpu.CompilerParams(
            dimension_semantics=("arbitrary",), vmem_limit_bytes=VMEM_LIMIT),
        name="outproj",
    )(y, x2d, w, g)


SB = 16
SR = SB * DEC_SEQ
SEQ_UNROLL = 4


def _mix_sample_kernel(sinks_ref, z_ref, zg_ref, cos_ref, sin_ref, ck_ref, cv_ref,
                       c_ref, nst_ref, nrep_ref, mrep_ref,
                       y_ref, ko_ref, vo_ref, co_ref, no_ref, mo_ref,
                       q_s, k_s, o_s):
    T = DEC_SEQ
    scale = HEAD_DIM ** -0.5
    cos = cos_ref[...]
    sin = sin_ref[...]

    for h in range(SWA_HEADS):
        q_s[:, h * HEAD_DIM:(h + 1) * HEAD_DIM] = _rope(
            z_ref[:, O_SQ + h * HEAD_DIM:O_SQ + (h + 1) * HEAD_DIM], cos, sin)
    for kv in range(SWA_KV_HEADS):
        k_s[:, kv * HEAD_DIM:(kv + 1) * HEAD_DIM] = _rope(
            z_ref[:, O_SK + kv * HEAD_DIM:O_SK + (kv + 1) * HEAD_DIM], cos, sin)

    KP = 2 * WINDOW
    rt = lax.broadcasted_iota(jnp.int32, (GQA_GROUP * T, KP), 0) & (T - 1)
    cj = lax.broadcasted_iota(jnp.int32, (GQA_GROUP * T, KP), 1)
    mask = (cj > rt) & (cj <= rt + WINDOW)
    kpad = jnp.zeros((KP - WINDOW - T, HEAD_DIM), F32)
    NKV = SWA_KV_HEADS
    SEQ_ROWS = WINDOW * NKV

    keep = SEQ_ROWS - T * NKV
    sink_cols = [jnp.concatenate([jnp.full((T, 1), sinks_ref[kv * GQA_GROUP + g], F32)
                                  for g in range(GQA_GROUP)], axis=0) for kv in range(NKV)]

    def per_group(i, carry):
        chains = [(i * SEQ_UNROLL + j, kv) for j in range(SEQ_UNROLL) for kv in range(NKV)]
        scores, values = [], []
        for b, kv in chains:
            rows = pl.ds(pl.multiple_of(b * T, T), T)
            base = pl.multiple_of(b * SEQ_ROWS, SEQ_ROWS)
            cols = slice(kv * HEAD_DIM, (kv + 1) * HEAD_DIM)
            k_new = k_s[rows, cols]
            v_new = z_ref[rows, O_SV + kv * HEAD_DIM:O_SV + (kv + 1) * HEAD_DIM]
            if kv == 0:
                ko_ref[pl.ds(base, keep), :] = ck_ref[pl.ds(base + T * NKV, keep), :]
                vo_ref[pl.ds(base, keep), :] = cv_ref[pl.ds(base + T * NKV, keep), :]
            ko_ref[pl.ds(base + keep + kv, T, stride=NKV), :] = k_new
            vo_ref[pl.ds(base + keep + kv, T, stride=NKV), :] = v_new
            kc = ck_ref[pl.ds(base + kv, WINDOW, stride=NKV), :]
            vc = cv_ref[pl.ds(base + kv, WINDOW, stride=NKV), :]
            k_all = jnp.concatenate([kc, k_new, kpad], axis=0).astype(BF16)
            values.append(jnp.concatenate([vc, v_new, kpad], axis=0).astype(BF16))
            q_st = jnp.concatenate([q_s[rows, (kv * GQA_GROUP + g) * HEAD_DIM:(kv * GQA_GROUP + g + 1) * HEAD_DIM]
                                    for g in range(GQA_GROUP)], axis=0).astype(BF16)
            scores.append(_dot_nt(q_st, k_all))
        probs, norms = [], []
        for (b, kv), s in zip(chains, scores):
            s = jnp.where(mask, s * scale, -jnp.inf)
            m = jnp.maximum(jnp.max(s, axis=-1, keepdims=True), sink_cols[kv])
            p = jnp.exp(s - m)
            norms.append(jnp.sum(p, axis=-1, keepdims=True) + jnp.exp(sink_cols[kv] - m))
            probs.append(p.astype(BF16))
        outs = [_dot(p, v) for p, v in zip(probs, values)]
        for (b, kv), o, l in zip(chains, outs, norms):
            rows = pl.ds(pl.multiple_of(b * T, T), T)
            o = o * (1.0 / l)
            for g in range(GQA_GROUP):
                h = kv * GQA_GROUP + g
                o_s[rows, h * HEAD_DIM:(h + 1) * HEAD_DIM] = o[g * T:(g + 1) * T]
        return carry

    lax.fori_loop(0, SB // SEQ_UNROLL, per_group, 0)

    for h in range(SWA_HEADS):
        cols = slice(h * HEAD_DIM, (h + 1) * HEAD_DIM)
        gate = _silu(z_ref[:, O_SG + h * HEAD_DIM:O_SG + (h + 1) * HEAD_DIM])
        y_ref[:, cols] = (o_s[:, cols] * gate).astype(BF16)

    R = SR
    lane = lax.broadcasted_iota(jnp.int32, (1, GATE_LANES), 1)
    zg = zg_ref[...]
    gates = jnp.where(lane < LANE_F, zg, _log_sigmoid(zg))
    gates = jnp.where(lane < 2 * ML_HEADS, gates, 0.0)
    rr = lax.broadcasted_iota(jnp.int32, (R, R), 0)
    cc = lax.broadcasted_iota(jnp.int32, (R, R), 1)
    same_seq = (rr >> 3) == (cc >> 3)
    causal = same_seq & (rr >= cc)
    tri = jnp.where(causal, 1.0, 0.0).astype(BF16)
    csum = _exact_sel_dot(tri, gates)
    gates_t = gates.T
    csum_t = csum.T
    half_lo = lane < ML_DQK
    seq_of_col = lax.broadcasted_iota(jnp.int32, (SB, 1, R), 2) >> 3
    seq_id = lax.broadcasted_iota(jnp.int32, (SB, 1, R), 0)
    own_cols = seq_of_col == seq_id
    own_blk = ((lax.broadcasted_iota(jnp.int32, (R, SB * 128), 0) >> 3)
               == (lax.broadcasted_iota(jnp.int32, (R, SB * 128), 1) >> 7))
    k_t = [(z_ref[:, O_MK + p * 128:O_MK + (p + 1) * 128] * (ML_DQK ** -0.5)).T for p in range(2)]
    mrep = mrep_ref[...]
    tok3 = lax.broadcasted_iota(jnp.int32, (SB, T, 1), 1)
    mo_ref[...] = jnp.zeros_like(mo_ref)

    def per_seq_value(col):
        return jnp.max(col.reshape(SB, T, 1), axis=1, keepdims=True)

    def last_of_seq(col):
        c3 = jnp.where(tok3 == T - 1, col.reshape(SB, T, 1), -jnp.inf)
        return jnp.broadcast_to(jnp.max(c3, axis=1, keepdims=True), (SB, T, 1)).reshape(R, 1)

    for h in range(ML_HEADS):
        pair, half = divmod(h, 2)
        hmask = half_lo if half == 0 else jnp.logical_not(half_lo)
        q_pair = z_ref[:, O_MQ + pair * 128:O_MQ + (pair + 1) * 128]
        k_pair = z_ref[:, O_MK + pair * 128:O_MK + (pair + 1) * 128] * (ML_DQK ** -0.5)
        v_f = z_ref[:, O_MV + h * ML_DV:O_MV + (h + 1) * ML_DV]
        v_h = v_f.astype(BF16)
        q_h = jnp.where(hmask, q_pair, 0.0)
        q_hb = q_h.astype(BF16)
        k_pb = k_pair.astype(BF16)

        bt_c = csum[:, LANE_F + h:LANE_F + h + 1]
        it_c = gates[:, LANE_I + h:LANE_I + h + 1]
        bt_r = csum_t[LANE_F + h:LANE_F + h + 1, :]
        it_r = gates_t[LANE_I + h:LANE_I + h + 1, :]
        m_prev = mrep[:, h:h + 1]

        dlog = jnp.where(causal, bt_c - bt_r + it_r, -jnp.inf)
        inter = bt_c + m_prev
        m_t = jnp.maximum(inter, jnp.max(dlog, axis=-1, keepdims=True))
        w_intra = jnp.exp(dlog - m_t)
        w_state = jnp.exp(inter - m_t)
        a = w_intra * _dot_nt(q_hb, k_pb)

        ct_st = c_ref[:, 2 * pair:2 * pair + 2].reshape(SB * 128, ML_DV).astype(BF16)
        q_blk = jnp.where(own_blk, jnp.tile(q_h, (1, SB)), 0.0).astype(BF16)
        num_state = _dot(q_blk, ct_st)

        num = _dot(a.astype(BF16), v_h) + w_state * num_state
        n_pair = nrep_ref[:, pair * 128:(pair + 1) * 128]
        den = jnp.sum(a, axis=-1, keepdims=True) + w_state * jnp.sum(q_h * n_pair, axis=-1, keepdims=True)
        denom = jnp.maximum(jnp.abs(den), jnp.exp(-m_t))
        hid = num * (1.0 / denom)
        o_gate = _sigmoid(z_ref[:, O_MO + h * ML_DV:O_MO + (h + 1) * ML_DV])
        gate = _silu(z_ref[:, O_MG + h * ML_DV:O_MG + (h + 1) * ML_DV])
        col = SWA_W + h * ML_DV
        y_ref[:, col:col + ML_DV] = ((hid * o_gate) * gate).astype(BF16)

        m_new = last_of_seq(m_t)
        bt_last = last_of_seq(bt_c)
        w_s = jnp.exp(bt_last - bt_c + it_c - m_new)
        decay = jnp.exp(bt_last + m_prev - m_new)
        decay_seq = per_seq_value(decay)

        kt_h = k_t[pair][half * ML_DQK:(half + 1) * ML_DQK, :]
        lhs = jnp.where(own_cols, kt_h[None, :, :], 0.0).reshape(SB * ML_DQK, R).astype(BF16)
        upd = _dot(lhs, (v_f * w_s).astype(BF16)).reshape(SB, ML_DQK, ML_DV)
        co_ref[:, h] = decay_seq * c_ref[:, h] + upd

        kw = (k_pair * w_s).reshape(SB, T, 128)
        ksum = jnp.sum(kw, axis=1)
        n_old = nst_ref[:, h * ML_DQK:(h + 1) * ML_DQK]
        dec2 = decay_seq.reshape(SB, 1)
        no_ref[:, h * ML_DQK:(h + 1) * ML_DQK] = dec2 * n_old + ksum[:, half * ML_DQK:(half + 1) * ML_DQK]
        mo_ref[:, h:h + 1] = per_seq_value(m_new).reshape(SB, 1)


def _mix_sample(sinks, z, zg, cos, sin, ck, cv, cst, nst, nrep, mrep):
    nb = cst.shape[0]
    steps = nb // SB
    const = lambda i, s: (0, 0)
    cache_rows = SB * WINDOW * SWA_KV_HEADS
    return pl.pallas_call(
        _mix_sample_kernel,
        grid_spec=pltpu.PrefetchScalarGridSpec(
            num_scalar_prefetch=1,
            grid=(steps,),
            in_specs=[
                pl.BlockSpec((SR, D_MAIN), lambda i, s: (i, 0)),
                pl.BlockSpec((SR, GATE_LANES), lambda i, s: (i, 0)),
                pl.BlockSpec((SR, HEAD_DIM), const),
                pl.BlockSpec((SR, HEAD_DIM), const),
                pl.BlockSpec((cache_rows, HEAD_DIM), lambda i, s: (i, 0)),
                pl.BlockSpec((cache_rows, HEAD_DIM), lambda i, s: (i, 0)),
                pl.BlockSpec((SB, ML_HEADS, ML_DQK, ML_DV), lambda i, s: (i, 0, 0, 0)),
                pl.BlockSpec((SB, ML_QK_W), lambda i, s: (i, 0)),
                pl.BlockSpec((SR, ML_QK_W), lambda i, s: (i, 0)),
                pl.BlockSpec((SR, GATE_LANES), lambda i, s: (i, 0)),
            ],
            out_specs=[
                pl.BlockSpec((SR, SWA_W + ML_W), lambda i, s: (i, 0)),
                pl.BlockSpec((cache_rows, HEAD_DIM), lambda i, s: (i, 0)),
                pl.BlockSpec((cache_rows, HEAD_DIM), lambda i, s: (i, 0)),
                pl.BlockSpec((SB, ML_HEADS, ML_DQK, ML_DV), lambda i, s: (i, 0, 0, 0)),
                pl.BlockSpec((SB, ML_QK_W), lambda i, s: (i, 0)),
                pl.BlockSpec((SB, GATE_LANES), lambda i, s: (i, 0)),
            ],
            scratch_shapes=[
                pltpu.VMEM((SR, SWA_W), F32),
                pltpu.VMEM((SR, SWA_KV_W), F32),
                pltpu.VMEM((SR, SWA_W), F32),
            ],
        ),
        out_shape=[
            jax.ShapeDtypeStruct((nb * DEC_SEQ, SWA_W + ML_W), BF16),
            jax.ShapeDtypeStruct(ck.shape, F32),
            jax.ShapeDtypeStruct(cv.shape, F32),
            jax.ShapeDtypeStruct((nb, ML_HEADS, ML_DQK, ML_DV), F32),
            jax.ShapeDtypeStruct((nb, ML_QK_W), F32),
            jax.ShapeDtypeStruct((nb, GATE_LANES), F32),
        ],
        compiler_params=pltpu.CompilerParams(
            dimension_semantics=("arbitrary",), vmem_limit_bytes=VMEM_LIMIT),
        name="mix_sample",
    )(sinks, z, zg, cos, sin, ck, cv, cst, nst, nrep, mrep)


MB = 8
MR = MB * DEC_SEQ


def _mem_sample_kernel(cq_ref, cg_ref, mk_ref, mv_ref, y_ref, o_s):
    T = DEC_SEQ
    scale = HEAD_DIM ** -0.5
    zpad = jnp.zeros((T, HEAD_DIM), F32)

    seq_rows = MEM_LEN * MEM_HEADS

    def per_group(i, carry):
        chains = [(i * SEQ_UNROLL + j, h) for j in range(SEQ_UNROLL) for h in range(MEM_HEADS)]
        scores = []
        for b, h in chains:
            rows = pl.ds(pl.multiple_of(b * T, T), T)
            base = pl.multiple_of(b * seq_rows, seq_rows)
            kb = mk_ref[pl.ds(base + h, MEM_LEN, stride=MEM_HEADS), :].astype(BF16)
            q = jnp.concatenate([cq_ref[rows, h * HEAD_DIM:(h + 1) * HEAD_DIM], zpad], axis=0).astype(BF16)
            scores.append(_dot_nt(q, kb))
        probs, norms = [], []
        for s in scores:
            s = s * scale
            p = jnp.exp(s - jnp.max(s, axis=-1, keepdims=True))
            norms.append(jnp.sum(p, axis=-1, keepdims=True))
            probs.append(p.astype(BF16))
        outs = []
        for (b, h), p in zip(chains, probs):
            base = pl.multiple_of(b * seq_rows, seq_rows)
            vb = mv_ref[pl.ds(base + h, MEM_LEN, stride=MEM_HEADS), :].astype(BF16)
            outs.append(_dot(p, vb))
        for (b, h), o, l in zip(chains, outs, norms):
            rows = pl.ds(pl.multiple_of(b * T, T), T)
            o_s[rows, h * HEAD_DIM:(h + 1) * HEAD_DIM] = (o * (1.0 / l))[0:T]
        return carry

    lax.fori_loop(0, MB // SEQ_UNROLL, per_group, 0)
    y_ref[...] = (o_s[...] * _silu(cg_ref[...])).astype(BF16)


def _mem_sample(z, mk, mv):
    nb = z.shape[0] // DEC_SEQ
    cache_rows = MB * MEM_LEN * MEM_HEADS
    return pl.pallas_call(
        _mem_sample_kernel,
        grid=(nb // MB,),
        in_specs=[
            pl.BlockSpec((MR, MEM_W), lambda i: (i, O_CQ // MEM_W)),
            pl.BlockSpec((MR, MEM_W), lambda i: (i, O_CG // MEM_W)),
            pl.BlockSpec((cache_rows, HEAD_DIM), lambda i: (i, 0)),
            pl.BlockSpec((cache_rows, HEAD_DIM), lambda i: (i, 0)),
        ],
        out_specs=pl.BlockSpec((MR, MEM_W), lambda i: (i, 0)),
        out_shape=jax.ShapeDtypeStruct((nb * DEC_SEQ, MEM_W), BF16),
        scratch_shapes=[pltpu.VMEM((MR, MEM_W), F32)],
        compiler_params=pltpu.CompilerParams(
            dimension_semantics=("arbitrary",), vmem_limit_bytes=VMEM_LIMIT),
        name="mem_sample",
    )(z, z, mk, mv)


def _rope_cos_sin(pos):
    half = HEAD_DIM // 2
    inv = jnp.power(ROPE_THETA, -(jnp.arange(half, dtype=F32) * 2.0 / HEAD_DIM))
    ang = pos.astype(F32)[:, None] * inv[None, :]
    return jnp.cos(ang), jnp.sin(ang)


def _rope_tables(pos):
    cos, sin = _rope_cos_sin(pos)
    return jnp.concatenate([cos, cos], axis=-1), jnp.concatenate([-sin, sin], axis=-1)


def _rope_split_tables(n_chunks):
    ca, sa = _rope_cos_sin(jnp.arange(n_chunks, dtype=jnp.int32) * CHUNK)
    cb, sb = _rope_cos_sin(jnp.arange(CHUNK, dtype=jnp.int32))
    dup = lambda t: jnp.concatenate([t, t], axis=-1)
    sgn = lambda t: jnp.concatenate([-t, t], axis=-1)
    return jnp.stack([dup(ca), dup(sa)]), jnp.stack([dup(cb), dup(sb), sgn(cb), sgn(sb)])


def _relayout_in_proj(w_in, b_in):
    w_t = jnp.swapaxes(w_in, 0, 1)
    w_main = _wprep(w_t)
    w_gate = _wgate(w_t)
    b_main = jnp.concatenate([b_in[_R_SQ:_R_MI], b_in[_R_MO:_R_END]])[None, :]
    b_gate = jnp.pad(b_in[_R_MI:_R_MO], (0, GATE_LANES - 2 * ML_HEADS))[None, :]
    return w_main, b_main, w_gate, b_gate


def _layer(xp, xs, mem, ck, cv, c_st, n_st, m_st, cmk, cmv,
           g_norm, w_in, b_in, sinks, g_mem, w_mem_kv, w_out, g_final):
    bp, sp, _ = xp.shape
    bs, ts, _ = xs.shape
    xp2 = xp.reshape(bp * sp, D_MODEL)
    xs2 = xs.reshape(bs * ts, D_MODEL)
    w_main, b_main, w_gate, b_gate = _relayout_in_proj(w_in, b_in)
    g_norm2 = g_norm[None, :]
    w_out_b = w_out.astype(BF16)
    g_final2 = g_final[None, :]
    sinks_flat = sinks.reshape(SWA_HEADS)

    memkv = _memkv(mem.reshape(MEM_LEN, D_MODEL), g_mem[None, :], w_mem_kv.astype(BF16))
    mem_k = memkv[:, :MEM_W]
    mem_v = memkv[:, MEM_W:]
    rope_c, rope_i = _rope_split_tables(sp // CHUNK)
    out_p, pk, pv, ct, n_p, m_p = _prompt(sinks_flat, xp2, g_norm2, w_main, b_main, w_gate, b_gate,
                                           rope_c, rope_i, mem_k.astype(BF16), mem_v.astype(BF16),
                                           w_out_b, g_final2)

    zs, zgs = _proj(xs2, g_norm2, w_main, b_main, w_gate, b_gate)
    cos_s, sin_s = _rope_tables(PAST_LEN + jnp.arange(ts, dtype=jnp.int32))
    cos_s = jnp.tile(cos_s, (SB, 1))
    sin_s = jnp.tile(sin_s, (SB, 1))
    nrep = jnp.repeat(n_st.reshape(bs, ML_QK_W), ts, axis=0)
    m_pad = jnp.pad(m_st, ((0, 0), (0, GATE_LANES - ML_HEADS)))
    mrep = jnp.repeat(m_pad, ts, axis=0)
    ya, sk_o, sv_o, ct_o, n_o, m_o = _mix_sample(
        sinks_flat, zs, zgs, cos_s, sin_s,
        ck.reshape(bs * WINDOW * SWA_KV_HEADS, HEAD_DIM), cv.reshape(bs * WINDOW * SWA_KV_HEADS, HEAD_DIM),
        jnp.swapaxes(c_st, -1, -2), n_st.reshape(bs, ML_QK_W), nrep, mrep)
    yb = _mem_sample(zs, cmk.reshape(bs * MEM_LEN * MEM_HEADS, HEAD_DIM),
                     cmv.reshape(bs * MEM_LEN * MEM_HEADS, HEAD_DIM))
    out_s = _outproj(jnp.concatenate([ya, yb], axis=-1), xs2, w_out_b, g_final2)
    c_o = jnp.swapaxes(ct_o, -1, -2)

    p_state = (
        pk.reshape(bp, WINDOW, SWA_KV_HEADS, HEAD_DIM),
        pv.reshape(bp, WINDOW, SWA_KV_HEADS, HEAD_DIM),
        ct.reshape(ML_HEADS, ML_DQK, ML_DV).transpose(0, 2, 1)[None],
        n_p.reshape(bp, ML_HEADS, ML_DQK),
        m_p[:, :ML_HEADS],
        mem_k.reshape(bp, MEM_LEN, MEM_HEADS, HEAD_DIM),
        mem_v.reshape(bp, MEM_LEN, MEM_HEADS, HEAD_DIM),
    )
    s_state = (
        sk_o.reshape(bs, WINDOW, SWA_KV_HEADS, HEAD_DIM),
        sv_o.reshape(bs, WINDOW, SWA_KV_HEADS, HEAD_DIM),
        c_o,
        n_o.reshape(bs, ML_HEADS, ML_DQK),
        m_o[:, :ML_HEADS],
    )
    return out_p.reshape(bp, sp, D_MODEL), out_s.reshape(bs, ts, D_MODEL), p_state, s_state


def kernel(x_prompt, x_sample, mem_prompt, cache_swa_k, cache_swa_v, state_mlstm_C, state_mlstm_n,
           state_mlstm_m, cache_mem_k, cache_mem_v, g_norm, w_in, b_in, swa_sinks, g_mem, w_mem_kv,
           w_out, g_final):
    depth = g_norm.shape[0]
    assert depth == 1 and x_prompt.shape[0] == 1
    y_p, y_s, p_state, s_state = _layer(
        x_prompt, x_sample, mem_prompt, cache_swa_k[0], cache_swa_v[0], state_mlstm_C[0],
        state_mlstm_n[0], state_mlstm_m[0], cache_mem_k[0], cache_mem_v[0],
        g_norm[0], w_in[0], b_in[0], swa_sinks[0], g_mem[0], w_mem_kv[0], w_out[0], g_final)
    return (y_p, y_s) + tuple(s[None] for s in p_state) + tuple(s[None] for s in s_state)
```

```python
import functools
import math

import jax
import jax.numpy as jnp
from jax import lax
from jax.experimental import pallas as pl
from jax.experimental.pallas import tpu as pltpu

F32 = jnp.float32
BF16 = jnp.bfloat16

D_MODEL = 2048
HEAD_DIM = 128
SWA_HEADS = 8
SWA_KV_HEADS = 2
GQA_GROUP = 4
WINDOW = 128
ML_HEADS = 4
ML_DQK = 64
ML_DV = 128
MEM_HEADS = 4
MEM_LEN = 256
CHUNK = 128
ROPE_THETA = 10000.0
EPS = 1e-6
PAST_LEN = 16384
DEC_SEQ = 8

SWA_W = SWA_HEADS * HEAD_DIM
SWA_KV_W = SWA_KV_HEADS * HEAD_DIM
ML_W = ML_HEADS * ML_DV
ML_QK_W = ML_HEADS * ML_DQK
MEM_W = MEM_HEADS * HEAD_DIM
D_MIX = SWA_W + ML_W + MEM_W

_IN_WIDTHS = (SWA_W, SWA_KV_W, SWA_KV_W, SWA_W, ML_QK_W, ML_QK_W, ML_W, ML_HEADS, ML_HEADS, ML_W, ML_W, MEM_W, MEM_W)
_IN_OFFS = [0]
for _w in _IN_WIDTHS:
    _IN_OFFS.append(_IN_OFFS[-1] + _w)
(_R_SQ, _R_SK, _R_SV, _R_SG, _R_MQ, _R_MK, _R_MV, _R_MI, _R_MF, _R_MO, _R_MG, _R_CQ, _R_CG, _R_END) = _IN_OFFS

O_SQ = 0
O_SK = O_SQ + SWA_W
O_SV = O_SK + SWA_KV_W
O_SG = O_SV + SWA_KV_W
O_MQ = O_SG + SWA_W
O_MK = O_MQ + ML_QK_W
O_MV = O_MK + ML_QK_W
O_MO = O_MV + ML_W
O_MG = O_MO + ML_W
O_CQ = O_MG + ML_W
O_CG = O_CQ + MEM_W
D_MAIN = O_CG + MEM_W
W_A_COLS = O_MO
W_B_COLS = D_MAIN - W_A_COLS
GATE_LANES = 128
LANE_I = 0
LANE_F = ML_HEADS

VMEM_LIMIT = 56 * 1024 * 1024

_NT = (((1,), (1,)), ((), ()))


def _dot(a, b):
    return jnp.dot(a, b, preferred_element_type=F32)


def _dot_nt(a, b):
    return lax.dot_general(a, b, _NT, preferred_element_type=F32)


def _exact_sel_dot(sel_bf16, x):
    hi = x.astype(BF16)
    r1 = x - hi.astype(F32)
    mid = r1.astype(BF16)
    lo = (r1 - mid.astype(F32)).astype(BF16)
    return _dot(sel_bf16, hi) + _dot(sel_bf16, mid) + _dot(sel_bf16, lo)


def _silu(x):
    h = 0.5 * x
    return h + h * jnp.tanh(h)


def _sigmoid(x):
    return 0.5 + 0.5 * jnp.tanh(0.5 * x)


def _log_sigmoid(x):
    return jnp.minimum(x, 0.0) - jnp.log1p(jnp.exp(-jnp.abs(x)))


PROJ_TM = 1024
PROJ_TN = 512
NORM_ROWS = 256


def _proj_kernel(x_ref, g_ref, w_ref, b_ref, wg_ref, bg_ref, z_ref, zg_ref, u_ref):
    j = pl.program_id(1)

    @pl.when(j == 0)
    def _():
        g = g_ref[...]
        for r in range(PROJ_TM // NORM_ROWS):
            rows = pl.ds(r * NORM_ROWS, NORM_ROWS)
            xf = x_ref[rows, :]
            ms = jnp.mean(xf * xf, axis=-1, keepdims=True)
            u_ref[rows, :] = ((xf * lax.rsqrt(ms + EPS)) * g).astype(BF16)
        zg_ref[...] = _dot(u_ref[...], wg_ref[...]) + bg_ref[...]

    z_ref[...] = _dot(u_ref[...], w_ref[...]) + b_ref[...]


def _proj(x2d, g, w, b, wg, bg):
    n = x2d.shape[0]
    grid = (n // PROJ_TM, D_MAIN // PROJ_TN)
    return pl.pallas_call(
        _proj_kernel,
        grid=grid,
        in_specs=[
            pl.BlockSpec((PROJ_TM, D_MODEL), lambda i, j: (i, 0)),
            pl.BlockSpec((1, D_MODEL), lambda i, j: (0, 0)),
            pl.BlockSpec((D_MODEL, PROJ_TN), lambda i, j: (0, j)),
            pl.BlockSpec((1, PROJ_TN), lambda i, j: (0, j)),
            pl.BlockSpec((D_MODEL, GATE_LANES), lambda i, j: (0, 0)),
            pl.BlockSpec((1, GATE_LANES), lambda i, j: (0, 0)),
        ],
        out_specs=[
            pl.BlockSpec((PROJ_TM, PROJ_TN), lambda i, j: (i, j)),
            pl.BlockSpec((PROJ_TM, GATE_LANES), lambda i, j: (i, 0)),
        ],
        out_shape=[
            jax.ShapeDtypeStruct((n, D_MAIN), F32),
            jax.ShapeDtypeStruct((n, GATE_LANES), F32),
        ],
        scratch_shapes=[pltpu.VMEM((PROJ_TM, D_MODEL), BF16)],
        compiler_params=pltpu.CompilerParams(
            dimension_semantics=("arbitrary", "arbitrary"), vmem_limit_bytes=VMEM_LIMIT),
        name="proj",
    )(x2d, g, w, b, wg, bg)


WPREP_TN = 256


def _wprep_kernel(wt_ref, *refs):
    o_ref = refs[-1]
    o_ref[...] = wt_ref[...].T.astype(BF16)


def _wprep_part(w_t, dst, src_row0, dst_tile0, n_tiles):
    src = pl.BlockSpec((pl.Element(WPREP_TN), pl.Element(D_MODEL)),
                       lambda j: (pl.multiple_of(j * WPREP_TN + src_row0, 8), 0))
    in_specs, args, aliases = [src], [w_t], {}
    if dst is not None:
        in_specs.append(pl.BlockSpec(memory_space=pl.ANY))
        args.append(dst)
        aliases = {1: 0}
    return pl.pallas_call(
        _wprep_kernel,
        grid=(n_tiles,),
        in_specs=in_specs,
        out_specs=pl.BlockSpec((D_MODEL, WPREP_TN), lambda j: (0, j + dst_tile0)),
        out_shape=jax.ShapeDtypeStruct((D_MODEL, D_MAIN), BF16),
        input_output_aliases=aliases,
        compiler_params=pltpu.CompilerParams(
            dimension_semantics=("arbitrary",), vmem_limit_bytes=VMEM_LIMIT),
        name="wprep",
    )(*args)


def _wprep(w_t):
    tiles_a = W_A_COLS // WPREP_TN
    part = _wprep_part(w_t, None, 0, 0, tiles_a)
    return _wprep_part(w_t, part, _R_MO, tiles_a, (D_MAIN - W_A_COLS) // WPREP_TN)


def _wgate_kernel(wt_ref, o_ref):
    rows = jnp.concatenate(
        [wt_ref[...], jnp.zeros((GATE_LANES - 2 * ML_HEADS, D_MODEL), F32)], axis=0)
    o_ref[...] = rows.T.astype(BF16)


def _wgate(w_t):
    return pl.pallas_call(
        _wgate_kernel,
        grid=(1,),
        in_specs=[pl.BlockSpec((pl.Element(2 * ML_HEADS), pl.Element(D_MODEL)), lambda j: (_R_MI, 0))],
        out_specs=pl.BlockSpec((D_MODEL, GATE_LANES), lambda j: (0, 0)),
        out_shape=jax.ShapeDtypeStruct((D_MODEL, GATE_LANES), BF16),
        name="wgate",
    )(w_t)


MEMKV_TN = 256


def _memkv_kernel(mem_ref, g_ref, w_ref, o_ref):
    xf = mem_ref[...]
    ms = jnp.mean(xf * xf, axis=-1, keepdims=True)
    u = ((xf * lax.rsqrt(ms + EPS)) * g_ref[...]).astype(BF16)
    o_ref[...] = _dot(u, w_ref[...])


def _memkv(mem2d, g, w):
    return pl.pallas_call(
        _memkv_kernel,
        grid=(2 * MEM_W // MEMKV_TN,),
        in_specs=[
            pl.BlockSpec((MEM_LEN, D_MODEL), lambda j: (0, 0)),
            pl.BlockSpec((1, D_MODEL), lambda j: (0, 0)),
            pl.BlockSpec((D_MODEL, MEMKV_TN), lambda j: (0, j)),
        ],
        out_specs=pl.BlockSpec((MEM_LEN, MEMKV_TN), lambda j: (0, j)),
        out_shape=jax.ShapeDtypeStruct((MEM_LEN, 2 * MEM_W), F32),
        compiler_params=pltpu.CompilerParams(
            dimension_semantics=("arbitrary",), vmem_limit_bytes=VMEM_LIMIT),
        name="memkv",
    )(mem2d, g, w)


OUT_TN = 256


def _rope(x, cos, sin_signed):
    return x * cos + pltpu.roll(x, HEAD_DIM // 2, axis=1) * sin_signed


IN_TN = 256
SWA_POINTS = 3 * SWA_KV_HEADS
MEM_POINTS = 2 * MEM_HEADS
MIXER_WEIGHTS = ([1.0, 3.0, 1.0] * SWA_KV_HEADS + [1.0] * MEM_HEADS + [0.5] * MEM_HEADS
                 + [1.0] + [1.5] * ML_HEADS + [1.0] * ML_HEADS + [0.5] * ML_HEADS)


def _spread(tasks, weights):
    total = sum(weights)
    bounds = [0]
    acc = 0.0
    for w in weights:
        acc += w
        bounds.append(round(len(tasks) * acc / total))
    return [tasks[a:b] for a, b in zip(bounds[:-1], bounds[1:])]


def _zip_then_rest(a, b):
    k = min(len(a), len(b))
    return [t for pair in zip(a[:k], b[:k]) for t in pair] + a[k:] + b[k:]


def _prompt_kernel(sinks_ref, xin_ref, xres_ref, gn_ref, w_ref, b_ref, wg_ref, bg_ref,
                   rope_c_ref, rope_i_ref, mk_ref, mv_ref, wout_ref, gfin_ref,
                   out_ref, ko_ref, vo_ref, ct_ref, n_ref, m_ref,
                   kprev_ref, vprev_ref, u_ref, z_ref, zg_ref, zprev_ref, zgprev_ref,
                   y_ref, yprev_ref, acc_ref):
    s = pl.program_id(0)
    nc = pl.num_programs(0) - 2
    n_in = D_MAIN // IN_TN
    n_out = D_MODEL // OUT_TN

    def in_norm():
        xf = xin_ref[...]
        ms = jnp.mean(xf * xf, axis=-1, keepdims=True)
        u_ref[...] = ((xf * lax.rsqrt(ms + EPS)) * gn_ref[...]).astype(BF16)

    def in_tile(t):
        if t == n_in:
            zg_ref[...] = _dot(u_ref[...], wg_ref[...]) + bg_ref[...]
        else:
            cols = slice(t * IN_TN, (t + 1) * IN_TN)
            z_ref[:, cols] = _dot(u_ref[...], w_ref[:, cols]) + b_ref[:, cols]

    def in_rotate(lo, hi):
        zprev_ref[:, lo:hi] = z_ref[:, lo:hi]

    def in_rotate_rest():
        in_rotate(O_MQ, O_CQ)
        zgprev_ref[...] = zg_ref[...]

    def out_tile(t):
        cols = slice(t * OUT_TN, (t + 1) * OUT_TN)
        acc_ref[:, cols] = _dot(yprev_ref[...], wout_ref[:, cols]) + xres_ref[:, cols]

    def out_finish():
        acc = acc_ref[...]
        ms = jnp.mean(acc * acc, axis=-1, keepdims=True)
        out_ref[...] = (acc * lax.rsqrt(ms + EPS)) * gfin_ref[...]

    tile_of = lambda col: col // IN_TN
    in_order = (list(range(0, tile_of(O_MQ))) + list(range(tile_of(O_CQ), n_in))
                + list(range(tile_of(O_MQ), tile_of(O_CQ))) + [n_in])
    in_tasks = []
    for t in in_order:
        in_tasks.append((functools.partial(in_tile, t), 0))
        if t == tile_of(O_MQ) - 1:
            in_tasks.append((functools.partial(in_rotate, 0, O_MQ), SWA_POINTS))
        if t == n_in - 1:
            in_tasks.append((functools.partial(in_rotate, O_CQ, D_MAIN), SWA_POINTS + MEM_POINTS))
    out_tasks = [(functools.partial(out_tile, t), 0) for t in range(n_out)] + [(out_finish, 0)]

    def mix(tasks):
        groups = iter(_spread(tasks, MIXER_WEIGHTS))
        points_done = [0]

        def between():
            for emit, first_point in next(groups):
                assert points_done[0] >= first_point
                emit()
            points_done[0] += 1

        _prompt_mixers(s - 1, sinks_ref, zprev_ref, zgprev_ref, rope_c_ref, rope_i_ref, mk_ref, mv_ref,
                       y_ref, ko_ref, vo_ref, ct_ref, n_ref, m_ref, kprev_ref, vprev_ref, between)
        assert next(groups, None) is None

    def run(tasks):
        for emit, _ in tasks:
            emit()

    @pl.when(s == 0)
    def _():
        kprev_ref[...] = jnp.zeros_like(kprev_ref)
        vprev_ref[...] = jnp.zeros_like(vprev_ref)
        ct_ref[...] = jnp.zeros_like(ct_ref)
        n_ref[...] = jnp.zeros_like(n_ref)
        m_ref[...] = jnp.zeros_like(m_ref)
        in_norm()
        run(in_tasks)
        in_rotate_rest()

    @pl.when(s == 1)
    def _():
        in_norm()
        mix(in_tasks)
        in_rotate_rest()
        yprev_ref[...] = y_ref[...]

    @pl.when((s >= 2) & (s < nc))
    def _():
        in_norm()
        mix(out_tasks[:4] + _zip_then_rest(in_tasks, out_tasks[4:]))
        in_rotate_rest()
        yprev_ref[...] = y_ref[...]

    @pl.when(s == nc)
    def _():
        mix(out_tasks)
        yprev_ref[...] = y_ref[...]

    @pl.when(s == nc + 1)
    def _():
        run(out_tasks)


def _prompt_mixers(c, sinks_ref, z_ref, zg_ref, rope_c_ref, rope_i_ref, mk_ref, mv_ref,
                   y_ref, ko_ref, vo_ref, ct_ref, n_ref, m_ref, kprev_ref, vprev_ref, between):
    L = CHUNK
    scale = HEAD_DIM ** -0.5

    cc = rope_c_ref[0, pl.ds(c, 1), :]
    sc = rope_c_ref[1, pl.ds(c, 1), :]
    cos = cc * rope_i_ref[0] - sc * rope_i_ref[1]
    sin = sc * rope_i_ref[2] + cc * rope_i_ref[3]

    ri = lax.broadcasted_iota(jnp.int32, (GQA_GROUP * L, 2 * L), 0) & (L - 1)
    cj = lax.broadcasted_iota(jnp.int32, (GQA_GROUP * L, 2 * L), 1)
    j_low = jnp.where(c > 0, 0, L)
    band = (cj > ri) & (cj <= ri + L) & (cj >= j_low)
    for kv in range(SWA_KV_HEADS):
        between()
        k_new = _rope(z_ref[:, O_SK + kv * HEAD_DIM:O_SK + (kv + 1) * HEAD_DIM], cos, sin)
        v_new = z_ref[:, O_SV + kv * HEAD_DIM:O_SV + (kv + 1) * HEAD_DIM]
        ko_ref[:, kv * HEAD_DIM:(kv + 1) * HEAD_DIM] = k_new
        vo_ref[:, kv * HEAD_DIM:(kv + 1) * HEAD_DIM] = v_new
        k_new_b = k_new.astype(BF16)
        v_new_b = v_new.astype(BF16)
        kcat = jnp.concatenate([kprev_ref[:, kv * HEAD_DIM:(kv + 1) * HEAD_DIM], k_new_b], axis=0)
        vcat = jnp.concatenate([vprev_ref[:, kv * HEAD_DIM:(kv + 1) * HEAD_DIM], v_new_b], axis=0)
        kprev_ref[:, kv * HEAD_DIM:(kv + 1) * HEAD_DIM] = k_new_b
        vprev_ref[:, kv * HEAD_DIM:(kv + 1) * HEAD_DIM] = v_new_b
        qs = []
        sks = []
        for g in range(GQA_GROUP):
            h = kv * GQA_GROUP + g
            qs.append(_rope(z_ref[:, O_SQ + h * HEAD_DIM:O_SQ + (h + 1) * HEAD_DIM], cos, sin).astype(BF16))
            sks.append(jnp.full((L, 1), sinks_ref[h], F32))
        q_st = jnp.concatenate(qs, axis=0)
        sk = jnp.concatenate(sks, axis=0)
        s = _dot_nt(q_st, kcat)
        between()
        s = jnp.where(band, s * scale, -jnp.inf)
        m = jnp.maximum(jnp.max(s, axis=-1, keepdims=True), sk)
        p = jnp.exp(s - m)
        l = jnp.sum(p, axis=-1, keepdims=True) + jnp.exp(sk - m)
        o = _dot(p.astype(BF16), vcat)
        between()
        o = o * (1.0 / l)
        for g in range(GQA_GROUP):
            h = kv * GQA_GROUP + g
            gate = _silu(z_ref[:, O_SG + h * HEAD_DIM:O_SG + (h + 1) * HEAD_DIM])
            y_ref[:, h * HEAD_DIM:(h + 1) * HEAD_DIM] = (o[g * L:(g + 1) * L] * gate).astype(BF16)

    mem_s, mem_l, mem_o = [], [], []
    for h in range(MEM_HEADS):
        q = z_ref[:, O_CQ + h * HEAD_DIM:O_CQ + (h + 1) * HEAD_DIM].astype(BF16)
        mem_s.append(_dot_nt(q, mk_ref[:, h * HEAD_DIM:(h + 1) * HEAD_DIM]))
        between()
    for h in range(MEM_HEADS):
        s = mem_s[h] * scale
        m = jnp.max(s, axis=-1, keepdims=True)
        p = jnp.exp(s - m)
        mem_l.append(jnp.sum(p, axis=-1, keepdims=True))
        mem_o.append(_dot(p.astype(BF16), mv_ref[:, h * HEAD_DIM:(h + 1) * HEAD_DIM]))
        between()
    for h in range(MEM_HEADS):
        o = mem_o[h] * (1.0 / mem_l[h])
        gate = _silu(z_ref[:, O_CG + h * HEAD_DIM:O_CG + (h + 1) * HEAD_DIM])
        col = SWA_W + ML_W + h * HEAD_DIM
        y_ref[:, col:col + HEAD_DIM] = (o * gate).astype(BF16)

    lane = lax.broadcasted_iota(jnp.int32, (1, GATE_LANES), 1)
    zg = zg_ref[...]
    gates = jnp.where(lane < LANE_F, zg, _log_sigmoid(zg))
    gates = jnp.where(lane < 2 * ML_HEADS, gates, 0.0)
    rr = lax.broadcasted_iota(jnp.int32, (L, L), 0)
    cc = lax.broadcasted_iota(jnp.int32, (L, L), 1)
    causal = rr >= cc
    tri = jnp.where(causal, 1.0, 0.0).astype(BF16)
    csum = _exact_sel_dot(tri, gates)
    gates_t = gates.T
    csum_t = csum.T
    half_lo = lane < ML_DQK
    between()
    heads = []
    for h in range(ML_HEADS):
        pair, half = divmod(h, 2)
        hd = dict(pair=pair, half=half)
        hmask = half_lo if half == 0 else jnp.logical_not(half_lo)
        q_pair = z_ref[:, O_MQ + pair * 128:O_MQ + (pair + 1) * 128]
        hd["k_pair"] = z_ref[:, O_MK + pair * 128:O_MK + (pair + 1) * 128] * (ML_DQK ** -0.5)
        hd["v"] = z_ref[:, O_MV + h * ML_DV:O_MV + (h + 1) * ML_DV].astype(BF16)
        hd["q"] = jnp.where(hmask, q_pair, 0.0)
        q_hb = hd["q"].astype(BF16)
        hd["bt_c"] = csum[:, LANE_F + h:LANE_F + h + 1]
        hd["it_c"] = gates[:, LANE_I + h:LANE_I + h + 1]
        hd["m_prev"] = m_ref[:, h:h + 1]
        ct_pair = ct_ref[pair * 128:(pair + 1) * 128, :]
        hd["qk"] = _dot_nt(q_hb, hd["k_pair"].astype(BF16))
        hd["state_read"] = _dot(q_hb, ct_pair.astype(BF16))
        heads.append(hd)
        between()
    for h, hd in enumerate(heads):
        bt_r = csum_t[LANE_F + h:LANE_F + h + 1, :]
        it_r = gates_t[LANE_I + h:LANE_I + h + 1, :]
        dlog = jnp.where(causal, hd["bt_c"] - bt_r + it_r, -jnp.inf)
        inter = hd["bt_c"] + hd["m_prev"]
        hd["m_t"] = jnp.maximum(inter, jnp.max(dlog, axis=-1, keepdims=True))
        hd["w_state"] = jnp.exp(inter - hd["m_t"])
        a = jnp.exp(dlog - hd["m_t"]) * hd["qk"]
        hd["a_sum"] = jnp.sum(a, axis=-1, keepdims=True)
        hd["intra"] = _dot(a.astype(BF16), hd["v"])
        between()
    for h, hd in enumerate(heads):
        num = hd["intra"] + hd["w_state"] * hd["state_read"]
        n_pair = n_ref[:, hd["pair"] * 128:(hd["pair"] + 1) * 128]
        den = hd["a_sum"] + hd["w_state"] * jnp.sum(hd["q"] * n_pair, axis=-1, keepdims=True)
        denom = jnp.maximum(jnp.abs(den), jnp.exp(-hd["m_t"]))
        hid = num * (1.0 / denom)
        o_gate = _sigmoid(z_ref[:, O_MO + h * ML_DV:O_MO + (h + 1) * ML_DV])
        gate = _silu(z_ref[:, O_MG + h * ML_DV:O_MG + (h + 1) * ML_DV])
        col = SWA_W + h * ML_DV
        y_ref[:, col:col + ML_DV] = ((hid * o_gate) * gate).astype(BF16)

        hd["m_new"] = hd["m_t"][L - 1:L, :]
        bt_last = hd["bt_c"][L - 1:L, :]
        w_s = jnp.exp(bt_last - hd["bt_c"] + hd["it_c"] - hd["m_new"])
        hd["decay"] = jnp.exp(bt_last + hd["m_prev"] - hd["m_new"])
        kw = hd["k_pair"] * w_s
        hd["ksum"] = jnp.sum(kw, axis=0, keepdims=True)
        hd["upd"] = _dot(kw.T.astype(BF16), hd["v"])
        between()
    for h, hd in enumerate(heads):
        half, decay = hd["half"], hd["decay"]
        rows = slice(h * ML_DQK, (h + 1) * ML_DQK)
        ct_ref[rows, :] = decay * ct_ref[rows, :] + hd["upd"][half * ML_DQK:(half + 1) * ML_DQK, :]
        n_ref[:, rows] = decay * n_ref[:, rows] + hd["ksum"][:, half * ML_DQK:(half + 1) * ML_DQK]
        m_ref[:, h:h + 1] = hd["m_new"]


def _prompt(sinks, x2d, g_norm, w, b, wg, bg, rope_c, rope_i, mk, mv, w_out, g_final):
    n = x2d.shape[0]
    nc = n // CHUNK
    const = lambda c, s: (0, 0)
    resident = pl.Buffered(1)
    x_in = lambda c, s: (jnp.minimum(c, nc - 1), 0)
    x_res = lambda c, s: (jnp.clip(c - 2, 0, nc - 1), 0)
    return pl.pallas_call(
        _prompt_kernel,
        grid_spec=pltpu.PrefetchScalarGridSpec(
            num_scalar_prefetch=1,
            grid=(nc + 2,),
            in_specs=[
                pl.BlockSpec((CHUNK, D_MODEL), x_in),
                pl.BlockSpec((CHUNK, D_MODEL), x_res),
                pl.BlockSpec((1, D_MODEL), const, pipeline_mode=resident),
                pl.BlockSpec((D_MODEL, D_MAIN), const, pipeline_mode=resident),
                pl.BlockSpec((1, D_MAIN), const, pipeline_mode=resident),
                pl.BlockSpec((D_MODEL, GATE_LANES), const, pipeline_mode=resident),
                pl.BlockSpec((1, GATE_LANES), const, pipeline_mode=resident),
                pl.BlockSpec((2, nc, HEAD_DIM), lambda c, s: (0, 0, 0), pipeline_mode=resident),
                pl.BlockSpec((4, CHUNK, HEAD_DIM), lambda c, s: (0, 0, 0), pipeline_mode=resident),
                pl.BlockSpec((MEM_LEN, MEM_W), const, pipeline_mode=resident),
                pl.BlockSpec((MEM_LEN, MEM_W), const, pipeline_mode=resident),
                pl.BlockSpec((D_MIX, D_MODEL), const, pipeline_mode=resident),
                pl.BlockSpec((1, D_MODEL), const, pipeline_mode=resident),
            ],
            out_specs=[
                pl.BlockSpec((CHUNK, D_MODEL), x_res),
                pl.BlockSpec((CHUNK, SWA_KV_W), const),
                pl.BlockSpec((CHUNK, SWA_KV_W), const),
                pl.BlockSpec((ML_QK_W, ML_DV), const),
                pl.BlockSpec((1, ML_QK_W), const),
                pl.BlockSpec((1, GATE_LANES), const),
            ],
            scratch_shapes=[
                pltpu.VMEM((CHUNK, SWA_KV_W), BF16),
                pltpu.VMEM((CHUNK, SWA_KV_W), BF16),
                pltpu.VMEM((CHUNK, D_MODEL), BF16),
                pltpu.VMEM((CHUNK, D_MAIN), F32),
                pltpu.VMEM((CHUNK, GATE_LANES), F32),
                pltpu.VMEM((CHUNK, D_MAIN), F32),
                pltpu.VMEM((CHUNK, GATE_LANES), F32),
                pltpu.VMEM((CHUNK, D_MIX), BF16),
                pltpu.VMEM((CHUNK, D_MIX), BF16),
                pltpu.VMEM((CHUNK, D_MODEL), F32),
            ],
        ),
        out_shape=[
            jax.ShapeDtypeStruct((n, D_MODEL), F32),
            jax.ShapeDtypeStruct((CHUNK, SWA_KV_W), F32),
            jax.ShapeDtypeStruct((CHUNK, SWA_KV_W), F32),
            jax.ShapeDtypeStruct((ML_QK_W, ML_DV), F32),
            jax.ShapeDtypeStruct((1, ML_QK_W), F32),
            jax.ShapeDtypeStruct((1, GATE_LANES), F32),
        ],
        compiler_params=pltpu.CompilerParams(
            dimension_semantics=("arbitrary",), vmem_limit_bytes=VMEM_LIMIT),
        name="prompt",
    )(sinks, x2d, x2d, g_norm, w, b, wg, bg, rope_c, rope_i, mk, mv, w_out, g_final)


OUT_TM = 512


def _outproj_kernel(y_ref, x_ref, w_ref, g_ref, o_ref):
    acc = _dot(y_ref[...], w_ref[...]) + x_ref[...]
    ms = jnp.mean(acc * acc, axis=-1, keepdims=True)
    o_ref[...] = (acc * lax.rsqrt(ms + EPS)) * g_ref[...]


def _outproj(y, x2d, w, g):
    n = x2d.shape[0]
    return pl.pallas_call(
        _outproj_kernel,
        grid=(n // OUT_TM,),
        in_specs=[
            pl.BlockSpec((OUT_TM, D_MIX), lambda i: (i, 0)),
            pl.BlockSpec((OUT_TM, D_MODEL), lambda i: (i, 0)),
            pl.BlockSpec((D_MIX, D_MODEL), lambda i: (0, 0)),
            pl.BlockSpec((1, D_MODEL), lambda i: (0, 0)),
        ],
        out_specs=pl.BlockSpec((OUT_TM, D_MODEL), lambda i: (i, 0)),
        out_shape=jax.ShapeDtypeStruct((n, D_MODEL), F32),
        compiler_params=pltpu.CompilerParams(
            dimension_semantics=("arbitrary",), vmem_limit_bytes=VMEM_LIMIT),
        name="outproj",
    )(y, x2d, w, g)


SB = 16
SR = SB * DEC_SEQ
SEQ_UNROLL = 4
SWA_UNROLL = 8


def _mix_sample_kernel(sinks_ref, z_ref, zg_ref, cos_ref, sin_ref, ck_ref, cv_ref,
                       c_ref, nst_ref, nrep_ref, mrep_ref,
                       y_ref, ko_ref, vo_ref, co_ref, no_ref, mo_ref,
                       q_s, k_s, o_s):
    T = DEC_SEQ
    scale = HEAD_DIM ** -0.5
    cos = cos_ref[...]
    sin = sin_ref[...]

    for h in range(SWA_HEADS):
        q_s[:, h * HEAD_DIM:(h + 1) * HEAD_DIM] = _rope(
            z_ref[:, O_SQ + h * HEAD_DIM:O_SQ + (h + 1) * HEAD_DIM], cos, sin)
    for kv in range(SWA_KV_HEADS):
        k_s[:, kv * HEAD_DIM:(kv + 1) * HEAD_DIM] = _rope(
            z_ref[:, O_SK + kv * HEAD_DIM:O_SK + (kv + 1) * HEAD_DIM], cos, sin)

    KP = 2 * WINDOW
    rt = lax.broadcasted_iota(jnp.int32, (GQA_GROUP * T, KP), 0) & (T - 1)
    cj = lax.broadcasted_iota(jnp.int32, (GQA_GROUP * T, KP), 1)
    mask = (cj > rt) & (cj <= rt + WINDOW)
    kpad = jnp.zeros((KP - WINDOW - T, HEAD_DIM), F32)
    NKV = SWA_KV_HEADS
    SEQ_ROWS = WINDOW * NKV

    keep = SEQ_ROWS - T * NKV
    sink_cols = [jnp.concatenate([jnp.full((T, 1), sinks_ref[kv * GQA_GROUP + g], F32)
                                  for g in range(GQA_GROUP)], axis=0) for kv in range(NKV)]

    def per_group(i, carry):
        chains = [(i * SWA_UNROLL + j, kv) for j in range(SWA_UNROLL) for kv in range(NKV)]
        scores, values = [], []
        for b, kv in chains:
            rows = pl.ds(pl.multiple_of(b * T, T), T)
            base = pl.multiple_of(b * SEQ_ROWS, SEQ_ROWS)
            cols = slice(kv * HEAD_DIM, (kv + 1) * HEAD_DIM)
            k_new = k_s[rows, cols]
            v_new = z_ref[rows, O_SV + kv * HEAD_DIM:O_SV + (kv + 1) * HEAD_DIM]
            if kv == 0:
                ko_ref[pl.ds(base, keep), :] = ck_ref[pl.ds(base + T * NKV, keep), :]
                vo_ref[pl.ds(base, keep), :] = cv_ref[pl.ds(base + T * NKV, keep), :]
            ko_ref[pl.ds(base + keep + kv, T, stride=NKV), :] = k_new
            vo_ref[pl.ds(base + keep + kv, T, stride=NKV), :] = v_new
            kc = ck_ref[pl.ds(base + kv, WINDOW, stride=NKV), :]
            vc = cv_ref[pl.ds(base + kv, WINDOW, stride=NKV), :]
            k_all = jnp.concatenate([kc, k_new, kpad], axis=0).astype(BF16)
            values.append(jnp.concatenate([vc, v_new, kpad], axis=0).astype(BF16))
            q_st = jnp.concatenate([q_s[rows, (kv * GQA_GROUP + g) * HEAD_DIM:(kv * GQA_GROUP + g + 1) * HEAD_DIM]
                                    for g in range(GQA_GROUP)], axis=0).astype(BF16)
            scores.append(_dot_nt(q_st, k_all))
        probs, norms = [], []
        for (b, kv), s in zip(chains, scores):
            s = jnp.where(mask, s * scale, -jnp.inf)
            m = jnp.maximum(jnp.max(s, axis=-1, keepdims=True), sink_cols[kv])
            p = jnp.exp(s - m)
            norms.append(jnp.sum(p, axis=-1, keepdims=True) + jnp.exp(sink_cols[kv] - m))
            probs.append(p.astype(BF16))
        outs = [_dot(p, v) for p, v in zip(probs, values)]
        for (b, kv), o, l in zip(chains, outs, norms):
            rows = pl.ds(pl.multiple_of(b * T, T), T)
            o = o * (1.0 / l)
            for g in range(GQA_GROUP):
                h = kv * GQA_GROUP + g
                o_s[rows, h * HEAD_DIM:(h + 1) * HEAD_DIM] = o[g * T:(g + 1) * T]
        return carry

    lax.fori_loop(0, SB // SWA_UNROLL, per_group, 0)

    for h in range(SWA_HEADS):
        cols = slice(h * HEAD_DIM, (h + 1) * HEAD_DIM)
        gate = _silu(z_ref[:, O_SG + h * HEAD_DIM:O_SG + (h + 1) * HEAD_DIM])
        y_ref[:, cols] = (o_s[:, cols] * gate).astype(BF16)
    y_ref[:, SWA_W + ML_W:] = jnp.zeros((SR, MEM_W), BF16)

    R = SR
    lane = lax.broadcasted_iota(jnp.int32, (1, GATE_LANES), 1)
    zg = zg_ref[...]
    gates = jnp.where(lane < LANE_F, zg, _log_sigmoid(zg))
    gates = jnp.where(lane < 2 * ML_HEADS, gates, 0.0)
    rr = lax.broadcasted_iota(jnp.int32, (R, R), 0)
    cc = lax.broadcasted_iota(jnp.int32, (R, R), 1)
    same_seq = (rr >> 3) == (cc >> 3)
    causal = same_seq & (rr >= cc)
    tri = jnp.where(causal, 1.0, 0.0).astype(BF16)
    csum = _exact_sel_dot(tri, gates)
    gates_t = gates.T
    csum_t = csum.T
    half_lo = lane < ML_DQK
    seq_of_col = lax.broadcasted_iota(jnp.int32, (SB, 1, R), 2) >> 3
    seq_id = lax.broadcasted_iota(jnp.int32, (SB, 1, R), 0)
    own_cols = seq_of_col == seq_id
    own_blk = ((lax.broadcasted_iota(jnp.int32, (R, SB * 128), 0) >> 3)
               == (lax.broadcasted_iota(jnp.int32, (R, SB * 128), 1) >> 7))
    k_t = [(z_ref[:, O_MK + p * 128:O_MK + (p + 1) * 128] * (ML_DQK ** -0.5)).T for p in range(2)]
    mrep = mrep_ref[...]
    tok3 = lax.broadcasted_iota(jnp.int32, (SB, T, 1), 1)
    mo_ref[...] = jnp.zeros_like(mo_ref)

    def per_seq_value(col):
        return jnp.max(col.reshape(SB, T, 1), axis=1, keepdims=True)

    def last_of_seq(col):
        c3 = jnp.where(tok3 == T - 1, col.reshape(SB, T, 1), -jnp.inf)
        return jnp.broadcast_to(jnp.max(c3, axis=1, keepdims=True), (SB, T, 1)).reshape(R, 1)

    for h in range(ML_HEADS):
        pair, half = divmod(h, 2)
        hmask = half_lo if half == 0 else jnp.logical_not(half_lo)
        q_pair = z_ref[:, O_MQ + pair * 128:O_MQ + (pair + 1) * 128]
        k_pair = z_ref[:, O_MK + pair * 128:O_MK + (pair + 1) * 128] * (ML_DQK ** -0.5)
        v_f = z_ref[:, O_MV + h * ML_DV:O_MV + (h + 1) * ML_DV]
        v_h = v_f.astype(BF16)
        q_h = jnp.where(hmask, q_pair, 0.0)
        q_hb = q_h.astype(BF16)
        k_pb = k_pair.astype(BF16)

        bt_c = csum[:, LANE_F + h:LANE_F + h + 1]
        it_c = gates[:, LANE_I + h:LANE_I + h + 1]
        bt_r = csum_t[LANE_F + h:LANE_F + h + 1, :]
        it_r = gates_t[LANE_I + h:LANE_I + h + 1, :]
        m_prev = mrep[:, h:h + 1]

        dlog = jnp.where(causal, bt_c - bt_r + it_r, -jnp.inf)
        inter = bt_c + m_prev
        m_t = jnp.maximum(inter, jnp.max(dlog, axis=-1, keepdims=True))
        w_intra = jnp.exp(dlog - m_t)
        w_state = jnp.exp(inter - m_t)
        a = w_intra * _dot_nt(q_hb, k_pb)

        ct_st = c_ref[:, 2 * pair:2 * pair + 2].reshape(SB * 128, ML_DV).astype(BF16)
        q_blk = jnp.where(own_blk, jnp.tile(q_h, (1, SB)), 0.0).astype(BF16)
        num_state = _dot(q_blk, ct_st)

        num = _dot(a.astype(BF16), v_h) + w_state * num_state
        n_pair = nrep_ref[:, pair * 128:(pair + 1) * 128]
        den = jnp.sum(a, axis=-1, keepdims=True) + w_state * jnp.sum(q_h * n_pair, axis=-1, keepdims=True)
        denom = jnp.maximum(jnp.abs(den), jnp.exp(-m_t))
        hid = num * (1.0 / denom)
        o_gate = _sigmoid(z_ref[:, O_MO + h * ML_DV:O_MO + (h + 1) * ML_DV])
        gate = _silu(z_ref[:, O_MG + h * ML_DV:O_MG + (h + 1) * ML_DV])
        col = SWA_W + h * ML_DV
        y_ref[:, col:col + ML_DV] = ((hid * o_gate) * gate).astype(BF16)

        m_new = last_of_seq(m_t)
        bt_last = last_of_seq(bt_c)
        w_s = jnp.exp(bt_last - bt_c + it_c - m_new)
        decay = jnp.exp(bt_last + m_prev - m_new)
        decay_seq = per_seq_value(decay)

        kt_h = k_t[pair][half * ML_DQK:(half + 1) * ML_DQK, :]
        lhs = jnp.where(own_cols, kt_h[None, :, :], 0.0).reshape(SB * ML_DQK, R).astype(BF16)
        upd = _dot(lhs, (v_f * w_s).astype(BF16)).reshape(SB, ML_DQK, ML_DV)
        co_ref[:, h] = decay_seq * c_ref[:, h] + upd

        kw = (k_pair * w_s).reshape(SB, T, 128)
        ksum = jnp.sum(kw, axis=1)
        n_old = nst_ref[:, h * ML_DQK:(h + 1) * ML_DQK]
        dec2 = decay_seq.reshape(SB, 1)
        no_ref[:, h * ML_DQK:(h + 1) * ML_DQK] = dec2 * n_old + ksum[:, half * ML_DQK:(half + 1) * ML_DQK]
        mo_ref[:, h:h + 1] = per_seq_value(m_new).reshape(SB, 1)


def _mix_sample(sinks, z, zg, cos, sin, ck, cv, cst, nst, nrep, mrep):
    nb = cst.shape[0]
    steps = nb // SB
    const = lambda i, s: (0, 0)
    cache_rows = SB * WINDOW * SWA_KV_HEADS
    return pl.pallas_call(
        _mix_sample_kernel,
        grid_spec=pltpu.PrefetchScalarGridSpec(
            num_scalar_prefetch=1,
            grid=(steps,),
            in_specs=[
                pl.BlockSpec((SR, D_MAIN), lambda i, s: (i, 0)),
                pl.BlockSpec((SR, GATE_LANES), lambda i, s: (i, 0)),
                pl.BlockSpec((SR, HEAD_DIM), const),
                pl.BlockSpec((SR, HEAD_DIM), const),
                pl.BlockSpec((cache_rows, HEAD_DIM), lambda i, s: (i, 0)),
                pl.BlockSpec((cache_rows, HEAD_DIM), lambda i, s: (i, 0)),
                pl.BlockSpec((SB, ML_HEADS, ML_DQK, ML_DV), lambda i, s: (i, 0, 0, 0)),
                pl.BlockSpec((SB, ML_QK_W), lambda i, s: (i, 0)),
                pl.BlockSpec((SR, ML_QK_W), lambda i, s: (i, 0)),
                pl.BlockSpec((SR, GATE_LANES), lambda i, s: (i, 0)),
            ],
            out_specs=[
                pl.BlockSpec((SR, D_MIX), lambda i, s: (i, 0)),
                pl.BlockSpec((cache_rows, HEAD_DIM), lambda i, s: (i, 0)),
                pl.BlockSpec((cache_rows, HEAD_DIM), lambda i, s: (i, 0)),
                pl.BlockSpec((SB, ML_HEADS, ML_DQK, ML_DV), lambda i, s: (i, 0, 0, 0)),
                pl.BlockSpec((SB, ML_QK_W), lambda i, s: (i, 0)),
                pl.BlockSpec((SB, GATE_LANES), lambda i, s: (i, 0)),
            ],
            scratch_shapes=[
                pltpu.VMEM((SR, SWA_W), F32),
                pltpu.VMEM((SR, SWA_KV_W), F32),
                pltpu.VMEM((SR, SWA_W), F32),
            ],
        ),
        out_shape=[
            jax.ShapeDtypeStruct((nb * DEC_SEQ, D_MIX), BF16),
            jax.ShapeDtypeStruct(ck.shape, F32),
            jax.ShapeDtypeStruct(cv.shape, F32),
            jax.ShapeDtypeStruct((nb, ML_HEADS, ML_DQK, ML_DV), F32),
            jax.ShapeDtypeStruct((nb, ML_QK_W), F32),
            jax.ShapeDtypeStruct((nb, GATE_LANES), F32),
        ],
        compiler_params=pltpu.CompilerParams(
            dimension_semantics=("arbitrary",), vmem_limit_bytes=VMEM_LIMIT),
        name="mix_sample",
    )(sinks, z, zg, cos, sin, ck, cv, cst, nst, nrep, mrep)


MB = 8
MR = MB * DEC_SEQ


def _mem_sample_kernel(cq_ref, cg_ref, mk_ref, mv_ref, y_in_ref, y_ref, o_s):
    del y_in_ref
    T = DEC_SEQ
    scale = HEAD_DIM ** -0.5
    zpad = jnp.zeros((T, HEAD_DIM), F32)

    seq_rows = MEM_LEN * MEM_HEADS

    def per_group(i, carry):
        chains = [(i * SEQ_UNROLL + j, h) for j in range(SEQ_UNROLL) for h in range(MEM_HEADS)]
        scores = []
        for b, h in chains:
            rows = pl.ds(pl.multiple_of(b * T, T), T)
            base = pl.multiple_of(b * seq_rows, seq_rows)
            kb = mk_ref[pl.ds(base + h, MEM_LEN, stride=MEM_HEADS), :].astype(BF16)
            q = jnp.concatenate([cq_ref[rows, h * HEAD_DIM:(h + 1) * HEAD_DIM], zpad], axis=0).astype(BF16)
            scores.append(_dot_nt(q, kb))
        probs, norms = [], []
        for s in scores:
            s = s * scale
            p = jnp.exp(s - jnp.max(s, axis=-1, keepdims=True))
            norms.append(jnp.sum(p, axis=-1, keepdims=True))
            probs.append(p.astype(BF16))
        outs = []
        for (b, h), p in zip(chains, probs):
            base = pl.multiple_of(b * seq_rows, seq_rows)
            vb = mv_ref[pl.ds(base + h, MEM_LEN, stride=MEM_HEADS), :].astype(BF16)
            outs.append(_dot(p, vb))
        for (b, h), o, l in zip(chains, outs, norms):
            rows = pl.ds(pl.multiple_of(b * T, T), T)
            o_s[rows, h * HEAD_DIM:(h + 1) * HEAD_DIM] = (o * (1.0 / l))[0:T]
        return carry

    lax.fori_loop(0, MB // SEQ_UNROLL, per_group, 0)
    y_ref[...] = (o_s[...] * _silu(cg_ref[...])).astype(BF16)


def _mem_sample(z, mk, mv, y):
    nb = z.shape[0] // DEC_SEQ
    cache_rows = MB * MEM_LEN * MEM_HEADS
    return pl.pallas_call(
        _mem_sample_kernel,
        grid=(nb // MB,),
        in_specs=[
            pl.BlockSpec((MR, MEM_W), lambda i: (i, O_CQ // MEM_W)),
            pl.BlockSpec((MR, MEM_W), lambda i: (i, O_CG // MEM_W)),
            pl.BlockSpec((cache_rows, HEAD_DIM), lambda i: (i, 0)),
            pl.BlockSpec((cache_rows, HEAD_DIM), lambda i: (i, 0)),
            pl.BlockSpec(memory_space=pl.ANY),
        ],
        out_specs=pl.BlockSpec((MR, MEM_W), lambda i: (i, (SWA_W + ML_W) // MEM_W)),
        out_shape=jax.ShapeDtypeStruct(y.shape, y.dtype),
        input_output_aliases={4: 0},
        scratch_shapes=[pltpu.VMEM((MR, MEM_W), F32)],
        compiler_params=pltpu.CompilerParams(
            dimension_semantics=("arbitrary",), vmem_limit_bytes=VMEM_LIMIT),
        name="mem_sample",
    )(z, z, mk, mv, y)


def _rope_cos_sin(pos):
    half = HEAD_DIM // 2
    inv = jnp.power(ROPE_THETA, -(jnp.arange(half, dtype=F32) * 2.0 / HEAD_DIM))
    ang = pos.astype(F32)[:, None] * inv[None, :]
    return jnp.cos(ang), jnp.sin(ang)


def _rope_tables(pos):
    cos, sin = _rope_cos_sin(pos)
    return jnp.concatenate([cos, cos], axis=-1), jnp.concatenate([-sin, sin], axis=-1)


def _rope_split_tables(n_chunks):
    ca, sa = _rope_cos_sin(jnp.arange(n_chunks, dtype=jnp.int32) * CHUNK)
    cb, sb = _rope_cos_sin(jnp.arange(CHUNK, dtype=jnp.int32))
    dup = lambda t: jnp.concatenate([t, t], axis=-1)
    sgn = lambda t: jnp.concatenate([-t, t], axis=-1)
    return jnp.stack([dup(ca), dup(sa)]), jnp.stack([dup(cb), dup(sb), sgn(cb), sgn(sb)])


def _relayout_in_proj(w_in, b_in):
    w_t = jnp.swapaxes(w_in, 0, 1)
    w_main = _wprep(w_t)
    w_gate = _wgate(w_t)
    b_main = jnp.concatenate([b_in[_R_SQ:_R_MI], b_in[_R_MO:_R_END]])[None, :]
    b_gate = jnp.pad(b_in[_R_MI:_R_MO], (0, GATE_LANES - 2 * ML_HEADS))[None, :]
    return w_main, b_main, w_gate, b_gate


def _layer(xp, xs, mem, ck, cv, c_st, n_st, m_st, cmk, cmv,
           g_norm, w_in, b_in, sinks, g_mem, w_mem_kv, w_out, g_final):
    bp, sp, _ = xp.shape
    bs, ts, _ = xs.shape
    xp2 = xp.reshape(bp * sp, D_MODEL)
    xs2 = xs.reshape(bs * ts, D_MODEL)
    w_main, b_main, w_gate, b_gate = _relayout_in_proj(w_in, b_in)
    g_norm2 = g_norm[None, :]
    w_out_b = w_out.astype(BF16)
    g_final2 = g_final[None, :]
    sinks_flat = sinks.reshape(SWA_HEADS)

    memkv = _memkv(mem.reshape(MEM_LEN, D_MODEL), g_mem[None, :], w_mem_kv.astype(BF16))
    mem_k = memkv[:, :MEM_W]
    mem_v = memkv[:, MEM_W:]
    rope_c, rope_i = _rope_split_tables(sp // CHUNK)
    out_p, pk, pv, ct, n_p, m_p = _prompt(sinks_flat, xp2, g_norm2, w_main, b_main, w_gate, b_gate,
                                           rope_c, rope_i, mem_k.astype(BF16), mem_v.astype(BF16),
                                           w_out_b, g_final2)

    zs, zgs = _proj(xs2, g_norm2, w_main, b_main, w_gate, b_gate)
    cos_s, sin_s = _rope_tables(PAST_LEN + jnp.arange(ts, dtype=jnp.int32))
    cos_s = jnp.tile(cos_s, (SB, 1))
    sin_s = jnp.tile(sin_s, (SB, 1))
    nrep = jnp.repeat(n_st.reshape(bs, ML_QK_W), ts, axis=0)
    m_pad = jnp.pad(m_st, ((0, 0), (0, GATE_LANES - ML_HEADS)))
    mrep = jnp.repeat(m_pad, ts, axis=0)
    ya, sk_o, sv_o, ct_o, n_o, m_o = _mix_sample(
        sinks_flat, zs, zgs, cos_s, sin_s,
        ck.reshape(bs * WINDOW * SWA_KV_HEADS, HEAD_DIM), cv.reshape(bs * WINDOW * SWA_KV_HEADS, HEAD_DIM),
        jnp.swapaxes(c_st, -1, -2), n_st.reshape(bs, ML_QK_W), nrep, mrep)
    ys = _mem_sample(zs, cmk.reshape(bs * MEM_LEN * MEM_HEADS, HEAD_DIM),
                     cmv.reshape(bs * MEM_LEN * MEM_HEADS, HEAD_DIM), ya)
    out_s = _outproj(ys, xs2, w_out_b, g_final2)
    c_o = jnp.swapaxes(ct_o, -1, -2)

    p_state = (
        pk.reshape(bp, WINDOW, SWA_KV_HEADS, HEAD_DIM),
        pv.reshape(bp, WINDOW, SWA_KV_HEADS, HEAD_DIM),
        ct.reshape(ML_HEADS, ML_DQK, ML_DV).transpose(0, 2, 1)[None],
        n_p.reshape(bp, ML_HEADS, ML_DQK),
        m_p[:, :ML_HEADS],
        mem_k.reshape(bp, MEM_LEN, MEM_HEADS, HEAD_DIM),
        mem_v.reshape(bp, MEM_LEN, MEM_HEADS, HEAD_DIM),
    )
    s_state = (
        sk_o.reshape(bs, WINDOW, SWA_KV_HEADS, HEAD_DIM),
        sv_o.reshape(bs, WINDOW, SWA_KV_HEADS, HEAD_DIM),
        c_o,
        n_o.reshape(bs, ML_HEADS, ML_DQK),
        m_o[:, :ML_HEADS],
    )
    return out_p.reshape(bp, sp, D_MODEL), out_s.reshape(bs, ts, D_MODEL), p_state, s_state


def kernel(x_prompt, x_sample, mem_prompt, cache_swa_k, cache_swa_v, state_mlstm_C, state_mlstm_n,
           state_mlstm_m, cache_mem_k, cache_mem_v, g_norm, w_in, b_in, swa_sinks, g_mem, w_mem_kv,
           w_out, g_final):
    depth = g_norm.shape[0]
    assert depth == 1 and x_prompt.shape[0] == 1
    y_p, y_s, p_state, s_state = _layer(
        x_prompt, x_sample, mem_prompt, cache_swa_k[0], cache_swa_v[0], state_mlstm_C[0],
        state_mlstm_n[0], state_mlstm_m[0], cache_mem_k[0], cache_mem_v[0],
        g_norm[0], w_in[0], b_in[0], swa_sinks[0], g_mem[0], w_mem_kv[0], w_out[0], g_final)
    return (y_p, y_s) + tuple(s[None] for s in p_state) + tuple(s[None] for s in s_state)
```

```python
import functools
import math

import jax
import jax.numpy as jnp
from jax import lax
from jax.experimental import pallas as pl
from jax.experimental.pallas import tpu as pltpu

F32 = jnp.float32
BF16 = jnp.bfloat16

D_MODEL = 2048
HEAD_DIM = 128
SWA_HEADS = 8
SWA_KV_HEADS = 2
GQA_GROUP = 4
WINDOW = 128
ML_HEADS = 4
ML_DQK = 64
ML_DV = 128
MEM_HEADS = 4
MEM_LEN = 256
CHUNK = 128
ROPE_THETA = 10000.0
EPS = 1e-6
PAST_LEN = 16384
DEC_SEQ = 8

SWA_W = SWA_HEADS * HEAD_DIM
SWA_KV_W = SWA_KV_HEADS * HEAD_DIM
ML_W = ML_HEADS * ML_DV
ML_QK_W = ML_HEADS * ML_DQK
MEM_W = MEM_HEADS * HEAD_DIM
D_MIX = SWA_W + ML_W + MEM_W

_IN_WIDTHS = (SWA_W, SWA_KV_W, SWA_KV_W, SWA_W, ML_QK_W, ML_QK_W, ML_W, ML_HEADS, ML_HEADS, ML_W, ML_W, MEM_W, MEM_W)
_IN_OFFS = [0]
for _w in _IN_WIDTHS:
    _IN_OFFS.append(_IN_OFFS[-1] + _w)
(_R_SQ, _R_SK, _R_SV, _R_SG, _R_MQ, _R_MK, _R_MV, _R_MI, _R_MF, _R_MO, _R_MG, _R_CQ, _R_CG, _R_END) = _IN_OFFS

O_SQ = 0
O_SK = O_SQ + SWA_W
O_SV = O_SK + SWA_KV_W
O_SG = O_SV + SWA_KV_W
O_MQ = O_SG + SWA_W
O_MK = O_MQ + ML_QK_W
O_MV = O_MK + ML_QK_W
O_MO = O_MV + ML_W
O_MG = O_MO + ML_W
O_CQ = O_MG + ML_W
O_CG = O_CQ + MEM_W
D_MAIN = O_CG + MEM_W
W_A_COLS = O_MO
W_B_COLS = D_MAIN - W_A_COLS
GATE_LANES = 128
LANE_I = 0
LANE_F = ML_HEADS

VMEM_LIMIT = 58 * 1024 * 1024

_NT = (((1,), (1,)), ((), ()))


def _dot(a, b):
    return jnp.dot(a, b, preferred_element_type=F32)


def _dot_nt(a, b):
    return lax.dot_general(a, b, _NT, preferred_element_type=F32)


def _exact_sel_dot(sel_bf16, x):
    hi = x.astype(BF16)
    r1 = x - hi.astype(F32)
    mid = r1.astype(BF16)
    lo = (r1 - mid.astype(F32)).astype(BF16)
    return _dot(sel_bf16, hi) + _dot(sel_bf16, mid) + _dot(sel_bf16, lo)


def _silu(x):
    h = 0.5 * x
    return h + h * jnp.tanh(h)


def _sigmoid(x):
    return 0.5 + 0.5 * jnp.tanh(0.5 * x)


def _log_sigmoid(x):
    return jnp.minimum(x, 0.0) - jnp.log1p(jnp.exp(-jnp.abs(x)))


PROJ_TM = 1024
PROJ_TN = 512
NORM_ROWS = 256


def _proj_kernel(x_ref, g_ref, w_ref, b_ref, wg_ref, bg_ref, z_ref, zg_ref, u_ref):
    j = pl.program_id(1)

    @pl.when(j == 0)
    def _():
        g = g_ref[...]
        for r in range(PROJ_TM // NORM_ROWS):
            rows = pl.ds(r * NORM_ROWS, NORM_ROWS)
            xf = x_ref[rows, :]
            ms = jnp.mean(xf * xf, axis=-1, keepdims=True)
            u_ref[rows, :] = ((xf * lax.rsqrt(ms + EPS)) * g).astype(BF16)
        zg_ref[...] = _dot(u_ref[...], wg_ref[...]) + bg_ref[...]

    z_ref[...] = _dot(u_ref[...], w_ref[...]) + b_ref[...]


def _proj(x2d, g, w, b, wg, bg):
    n = x2d.shape[0]
    grid = (n // PROJ_TM, D_MAIN // PROJ_TN)
    return pl.pallas_call(
        _proj_kernel,
        grid=grid,
        in_specs=[
            pl.BlockSpec((PROJ_TM, D_MODEL), lambda i, j: (i, 0)),
            pl.BlockSpec((1, D_MODEL), lambda i, j: (0, 0)),
            pl.BlockSpec((D_MODEL, PROJ_TN), lambda i, j: (0, j)),
            pl.BlockSpec((1, PROJ_TN), lambda i, j: (0, j)),
            pl.BlockSpec((D_MODEL, GATE_LANES), lambda i, j: (0, 0)),
            pl.BlockSpec((1, GATE_LANES), lambda i, j: (0, 0)),
        ],
        out_specs=[
            pl.BlockSpec((PROJ_TM, PROJ_TN), lambda i, j: (i, j)),
            pl.BlockSpec((PROJ_TM, GATE_LANES), lambda i, j: (i, 0)),
        ],
        out_shape=[
            jax.ShapeDtypeStruct((n, D_MAIN), F32),
            jax.ShapeDtypeStruct((n, GATE_LANES), F32),
        ],
        scratch_shapes=[pltpu.VMEM((PROJ_TM, D_MODEL), BF16)],
        compiler_params=pltpu.CompilerParams(
            dimension_semantics=("arbitrary", "arbitrary"), vmem_limit_bytes=VMEM_LIMIT),
        name="proj",
    )(x2d, g, w, b, wg, bg)


WPREP_TN = 256


def _wprep_kernel(wt_ref, *refs):
    o_ref = refs[-1]
    o_ref[...] = wt_ref[...].T.astype(BF16)


def _wprep_part(w_t, dst, src_row0, dst_tile0, n_tiles):
    src = pl.BlockSpec((pl.Element(WPREP_TN), pl.Element(D_MODEL)),
                       lambda j: (pl.multiple_of(j * WPREP_TN + src_row0, 8), 0))
    in_specs, args, aliases = [src], [w_t], {}
    if dst is not None:
        in_specs.append(pl.BlockSpec(memory_space=pl.ANY))
        args.append(dst)
        aliases = {1: 0}
    return pl.pallas_call(
        _wprep_kernel,
        grid=(n_tiles,),
        in_specs=in_specs,
        out_specs=pl.BlockSpec((D_MODEL, WPREP_TN), lambda j: (0, j + dst_tile0)),
        out_shape=jax.ShapeDtypeStruct((D_MODEL, D_MAIN), BF16),
        input_output_aliases=aliases,
        compiler_params=pltpu.CompilerParams(
            dimension_semantics=("arbitrary",), vmem_limit_bytes=VMEM_LIMIT),
        name="wprep",
    )(*args)


def _wprep(w_t):
    tiles_a = W_A_COLS // WPREP_TN
    part = _wprep_part(w_t, None, 0, 0, tiles_a)
    return _wprep_part(w_t, part, _R_MO, tiles_a, (D_MAIN - W_A_COLS) // WPREP_TN)


def _wgate_kernel(wt_ref, o_ref):
    rows = jnp.concatenate(
        [wt_ref[...], jnp.zeros((GATE_LANES - 2 * ML_HEADS, D_MODEL), F32)], axis=0)
    o_ref[...] = rows.T.astype(BF16)


def _wgate(w_t):
    return pl.pallas_call(
        _wgate_kernel,
        grid=(1,),
        in_specs=[pl.BlockSpec((pl.Element(2 * ML_HEADS), pl.Element(D_MODEL)), lambda j: (_R_MI, 0))],
        out_specs=pl.BlockSpec((D_MODEL, GATE_LANES), lambda j: (0, 0)),
        out_shape=jax.ShapeDtypeStruct((D_MODEL, GATE_LANES), BF16),
        name="wgate",
    )(w_t)


MEMKV_TN = 256


def _memkv_kernel(mem_ref, g_ref, w_ref, o_ref):
    xf = mem_ref[...]
    ms = jnp.mean(xf * xf, axis=-1, keepdims=True)
    u = ((xf * lax.rsqrt(ms + EPS)) * g_ref[...]).astype(BF16)
    o_ref[...] = _dot(u, w_ref[...])


def _memkv(mem2d, g, w):
    return pl.pallas_call(
        _memkv_kernel,
        grid=(2 * MEM_W // MEMKV_TN,),
        in_specs=[
            pl.BlockSpec((MEM_LEN, D_MODEL), lambda j: (0, 0)),
            pl.BlockSpec((1, D_MODEL), lambda j: (0, 0)),
            pl.BlockSpec((D_MODEL, MEMKV_TN), lambda j: (0, j)),
        ],
        out_specs=pl.BlockSpec((MEM_LEN, MEMKV_TN), lambda j: (0, j)),
        out_shape=jax.ShapeDtypeStruct((MEM_LEN, 2 * MEM_W), F32),
        compiler_params=pltpu.CompilerParams(
            dimension_semantics=("arbitrary",), vmem_limit_bytes=VMEM_LIMIT),
        name="memkv",
    )(mem2d, g, w)


OUT_TN = 256


def _rope(x, cos, sin_signed):
    return x * cos + pltpu.roll(x, HEAD_DIM // 2, axis=1) * sin_signed


IN_TN = 256
SWA_POINTS = 3 * SWA_KV_HEADS
MEM_POINTS = 2 * MEM_HEADS
MIXER_WEIGHTS = ([1.0, 3.0, 1.0] * SWA_KV_HEADS + [1.0] * MEM_HEADS + [0.5] * MEM_HEADS
                 + [1.0] + [1.5] * ML_HEADS + [1.0] * ML_HEADS + [0.5] * ML_HEADS)


def _spread(tasks, weights):
    total = sum(weights)
    bounds = [0]
    acc = 0.0
    for w in weights:
        acc += w
        bounds.append(round(len(tasks) * acc / total))
    return [tasks[a:b] for a, b in zip(bounds[:-1], bounds[1:])]


def _zip_then_rest(a, b):
    k = min(len(a), len(b))
    return [t for pair in zip(a[:k], b[:k]) for t in pair] + a[k:] + b[k:]


def _prompt_kernel(*refs):
    pair = pl.program_id(0)
    n_pairs = pl.num_programs(0) - 1

    carried = refs[_PROMPT_CARRIED]

    @pl.when(pair == 0)
    def _():
        for ref in carried:
            ref[...] = jnp.zeros_like(ref)
        _prompt_step(0, 0, True, False, False, *refs)
        _prompt_step(1, 1, True, True, False, *refs)

    @pl.when((pair > 0) & (pair < n_pairs))
    def _():
        _prompt_step(2 * pair, 0, True, True, True, *refs)
        _prompt_step(2 * pair + 1, 1, True, True, True, *refs)

    @pl.when(pair == n_pairs)
    def _():
        _prompt_step(2 * pair, 0, False, True, True, *refs)
        _prompt_step(2 * pair + 1, 1, False, False, True, *refs)


_PROMPT_CARRIED = slice(17, 22)


def _prompt_step(s, half, do_in, do_mix, do_out,
                 sinks_ref, xin2_ref, xres2_ref, gn_ref, w_ref, b_ref, wg_ref, bg_ref,
                 rope_c_ref, rope_i_ref, mk_ref, mv_ref, wout_ref, gfin_ref,
                 out2_ref, ko_ref, vo_ref, ct_ref, n_ref, m_ref,
                 kprev_ref, vprev_ref, u_ref, z_ref, zg_ref, zprev_ref, zgprev_ref,
                 y_ref, yprev_ref):
    rows = slice(half * CHUNK, (half + 1) * CHUNK)
    xin_ref = xin2_ref.at[rows]
    xres_ref = xres2_ref.at[rows]
    out_ref = out2_ref.at[rows]
    n_in = D_MAIN // IN_TN
    n_out = D_MODEL // OUT_TN

    def in_norm():
        xf = xin_ref[...]
        ms = jnp.mean(xf * xf, axis=-1, keepdims=True)
        u_ref[...] = ((xf * lax.rsqrt(ms + EPS)) * gn_ref[...]).astype(BF16)

    def in_tile(t):
        if t == n_in:
            zg_ref[...] = _dot(u_ref[...], wg_ref[...]) + bg_ref[...]
        else:
            cols = slice(t * IN_TN, (t + 1) * IN_TN)
            z_ref[:, cols] = _dot(u_ref[...], w_ref[:, cols]) + b_ref[:, cols]

    def in_rotate(lo, hi):
        zprev_ref[:, lo:hi] = z_ref[:, lo:hi]

    def in_rotate_rest():
        in_rotate(O_MQ, O_CQ)
        zgprev_ref[...] = zg_ref[...]

    def out_tile(t):
        cols = slice(t * OUT_TN, (t + 1) * OUT_TN)
        out_ref[:, cols] = _dot(yprev_ref[...], wout_ref[:, cols]) + xres_ref[:, cols]

    def out_finish():
        acc = out_ref[...]
        ms = jnp.mean(acc * acc, axis=-1, keepdims=True)
        out_ref[...] = (acc * lax.rsqrt(ms + EPS)) * gfin_ref[...]

    tile_of = lambda col: col // IN_TN
    in_order = (list(range(0, tile_of(O_MQ))) + list(range(tile_of(O_CQ), n_in))
                + list(range(tile_of(O_MQ), tile_of(O_CQ))) + [n_in])
    in_tasks = []
    for t in in_order:
        in_tasks.append((functools.partial(in_tile, t), 0))
        if t == tile_of(O_MQ) - 1:
            in_tasks.append((functools.partial(in_rotate, 0, O_MQ), SWA_POINTS))
        if t == n_in - 1:
            in_tasks.append((functools.partial(in_rotate, O_CQ, D_MAIN), SWA_POINTS + MEM_POINTS))
    out_tasks = [(functools.partial(out_tile, t), 0) for t in range(n_out)] + [(out_finish, 0)]

    def mix(tasks):
        groups = iter(_spread(tasks, MIXER_WEIGHTS))
        points_done = [0]

        def between():
            for emit, first_point in next(groups):
                assert points_done[0] >= first_point
                emit()
            points_done[0] += 1

        _prompt_mixers(s - 1, sinks_ref, zprev_ref, zgprev_ref, rope_c_ref, rope_i_ref, mk_ref, mv_ref,
                       y_ref, ko_ref, vo_ref, ct_ref, n_ref, m_ref, kprev_ref, vprev_ref, between)
        assert next(groups, None) is None

    def run(tasks):
        for emit, _ in tasks:
            emit()

    if do_in:
        in_norm()
    if do_in and do_out:
        tasks = out_tasks[:4] + _zip_then_rest(in_tasks, out_tasks[4:])
    else:
        tasks = in_tasks if do_in else out_tasks
    if do_mix:
        mix(tasks)
    else:
        run(tasks)
    if do_in:
        in_rotate_rest()
    if do_mix:
        yprev_ref[...] = y_ref[...]


def _prompt_mixers(c, sinks_ref, z_ref, zg_ref, rope_c_ref, rope_i_ref, mk_ref, mv_ref,
                   y_ref, ko_ref, vo_ref, ct_ref, n_ref, m_ref, kprev_ref, vprev_ref, between):
    L = CHUNK
    scale = HEAD_DIM ** -0.5

    cc = rope_c_ref[0, pl.ds(c, 1), :]
    sc = rope_c_ref[1, pl.ds(c, 1), :]
    cos = cc * rope_i_ref[0] - sc * rope_i_ref[1]
    sin = sc * rope_i_ref[2] + cc * rope_i_ref[3]

    ri = lax.broadcasted_iota(jnp.int32, (GQA_GROUP * L, 2 * L), 0) & (L - 1)
    cj = lax.broadcasted_iota(jnp.int32, (GQA_GROUP * L, 2 * L), 1)
    j_low = jnp.where(c > 0, 0, L)
    band = (cj > ri) & (cj <= ri + L) & (cj >= j_low)
    for kv in range(SWA_KV_HEADS):
        between()
        k_new = _rope(z_ref[:, O_SK + kv * HEAD_DIM:O_SK + (kv + 1) * HEAD_DIM], cos, sin)
        v_new = z_ref[:, O_SV + kv * HEAD_DIM:O_SV + (kv + 1) * HEAD_DIM]
        ko_ref[:, kv * HEAD_DIM:(kv + 1) * HEAD_DIM] = k_new
        vo_ref[:, kv * HEAD_DIM:(kv + 1) * HEAD_DIM] = v_new
        k_new_b = k_new.astype(BF16)
        v_new_b = v_new.astype(BF16)
        kcat = jnp.concatenate([kprev_ref[:, kv * HEAD_DIM:(kv + 1) * HEAD_DIM], k_new_b], axis=0)
        vcat = jnp.concatenate([vprev_ref[:, kv * HEAD_DIM:(kv + 1) * HEAD_DIM], v_new_b], axis=0)
        kprev_ref[:, kv * HEAD_DIM:(kv + 1) * HEAD_DIM] = k_new_b
        vprev_ref[:, kv * HEAD_DIM:(kv + 1) * HEAD_DIM] = v_new_b
        qs = []
        sks = []
        for g in range(GQA_GROUP):
            h = kv * GQA_GROUP + g
            qs.append(_rope(z_ref[:, O_SQ + h * HEAD_DIM:O_SQ + (h + 1) * HEAD_DIM], cos, sin).astype(BF16))
            sks.append(jnp.full((L, 1), sinks_ref[h], F32))
        q_st = jnp.concatenate(qs, axis=0)
        sk = jnp.concatenate(sks, axis=0)
        s = _dot_nt(q_st, kcat)
        between()
        s = jnp.where(band, s * scale, -jnp.inf)
        m = jnp.maximum(jnp.max(s, axis=-1, keepdims=True), sk)
        p = jnp.exp(s - m)
        l = jnp.sum(p, axis=-1, keepdims=True) + jnp.exp(sk - m)
        o = _dot(p.astype(BF16), vcat)
        between()
        o = o * (1.0 / l)
        for g in range(GQA_GROUP):
            h = kv * GQA_GROUP + g
            gate = _silu(z_ref[:, O_SG + h * HEAD_DIM:O_SG + (h + 1) * HEAD_DIM])
            y_ref[:, h * HEAD_DIM:(h + 1) * HEAD_DIM] = (o[g * L:(g + 1) * L] * gate).astype(BF16)

    mem_s, mem_l, mem_o = [], [], []
    for h in range(MEM_HEADS):
        q = z_ref[:, O_CQ + h * HEAD_DIM:O_CQ + (h + 1) * HEAD_DIM].astype(BF16)
        mem_s.append(_dot_nt(q, mk_ref[:, h * HEAD_DIM:(h + 1) * HEAD_DIM]))
        between()
    for h in range(MEM_HEADS):
        s = mem_s[h] * scale
        m = jnp.max(s, axis=-1, keepdims=True)
        p = jnp.exp(s - m)
        mem_l.append(jnp.sum(p, axis=-1, keepdims=True))
        mem_o.append(_dot(p.astype(BF16), mv_ref[:, h * HEAD_DIM:(h + 1) * HEAD_DIM]))
        between()
    for h in range(MEM_HEADS):
        o = mem_o[h] * (1.0 / mem_l[h])
        gate = _silu(z_ref[:, O_CG + h * HEAD_DIM:O_CG + (h + 1) * HEAD_DIM])
        col = SWA_W + ML_W + h * HEAD_DIM
        y_ref[:, col:col + HEAD_DIM] = (o * gate).astype(BF16)

    lane = lax.broadcasted_iota(jnp.int32, (1, GATE_LANES), 1)
    zg = zg_ref[...]
    gates = jnp.where(lane < LANE_F, zg, _log_sigmoid(zg))
    gates = jnp.where(lane < 2 * ML_HEADS, gates, 0.0)
    rr = lax.broadcasted_iota(jnp.int32, (L, L), 0)
    cc = lax.broadcasted_iota(jnp.int32, (L, L), 1)
    causal = rr >= cc
    tri = jnp.where(causal, 1.0, 0.0).astype(BF16)
    csum = _exact_sel_dot(tri, gates)
    gates_t = gates.T
    csum_t = csum.T
    half_lo = lane < ML_DQK
    between()
    heads = []
    for h in range(ML_HEADS):
        pair, half = divmod(h, 2)
        hd = dict(pair=pair, half=half)
        hmask = half_lo if half == 0 else jnp.logical_not(half_lo)
        q_pair = z_ref[:, O_MQ + pair * 128:O_MQ + (pair + 1) * 128]
        hd["k_pair"] = z_ref[:, O_MK + pair * 128:O_MK + (pair + 1) * 128] * (ML_DQK ** -0.5)
        hd["v"] = z_ref[:, O_MV + h * ML_DV:O_MV + (h + 1) * ML_DV].astype(BF16)
        hd["q"] = jnp.where(hmask, q_pair, 0.0)
        q_hb = hd["q"].astype(BF16)
        hd["bt_c"] = csum[:, LANE_F + h:LANE_F + h + 1]
        hd["it_c"] = gates[:, LANE_I + h:LANE_I + h + 1]
        hd["m_prev"] = m_ref[:, h:h + 1]
        ct_pair = ct_ref[pair * 128:(pair + 1) * 128, :]
        hd["qk"] = _dot_nt(q_hb, hd["k_pair"].astype(BF16))
        hd["state_read"] = _dot(q_hb, ct_pair.astype(BF16))
        heads.append(hd)
        between()
    for h, hd in enumerate(heads):
        bt_r = csum_t[LANE_F + h:LANE_F + h + 1, :]
        it_r = gates_t[LANE_I + h:LANE_I + h + 1, :]
        dlog = jnp.where(causal, hd["bt_c"] - bt_r + it_r, -jnp.inf)
        inter = hd["bt_c"] + hd["m_prev"]
        hd["m_t"] = jnp.maximum(inter, jnp.max(dlog, axis=-1, keepdims=True))
        hd["w_state"] = jnp.exp(inter - hd["m_t"])
        a = jnp.exp(dlog - hd["m_t"]) * hd["qk"]
        hd["a_sum"] = jnp.sum(a, axis=-1, keepdims=True)
        hd["intra"] = _dot(a.astype(BF16), hd["v"])
        between()
    for h, hd in enumerate(heads):
        num = hd["intra"] + hd["w_state"] * hd["state_read"]
        n_pair = n_ref[:, hd["pair"] * 128:(hd["pair"] + 1) * 128]
        den = hd["a_sum"] + hd["w_state"] * jnp.sum(hd["q"] * n_pair, axis=-1, keepdims=True)
        denom = jnp.maximum(jnp.abs(den), jnp.exp(-hd["m_t"]))
        hid = num * (1.0 / denom)
        o_gate = _sigmoid(z_ref[:, O_MO + h * ML_DV:O_MO + (h + 1) * ML_DV])
        gate = _silu(z_ref[:, O_MG + h * ML_DV:O_MG + (h + 1) * ML_DV])
        col = SWA_W + h * ML_DV
        y_ref[:, col:col + ML_DV] = ((hid * o_gate) * gate).astype(BF16)

        hd["m_new"] = hd["m_t"][L - 1:L, :]
        bt_last = hd["bt_c"][L - 1:L, :]
        w_s = jnp.exp(bt_last - hd["bt_c"] + hd["it_c"] - hd["m_new"])
        hd["decay"] = jnp.exp(bt_last + hd["m_prev"] - hd["m_new"])
        kw = hd["k_pair"] * w_s
        hd["ksum"] = jnp.sum(kw, axis=0, keepdims=True)
        hd["upd"] = _dot(kw.T.astype(BF16), hd["v"])
        between()
    for h, hd in enumerate(heads):
        half, decay = hd["half"], hd["decay"]
        rows = slice(h * ML_DQK, (h + 1) * ML_DQK)
        ct_ref[rows, :] = decay * ct_ref[rows, :] + hd["upd"][half * ML_DQK:(half + 1) * ML_DQK, :]
        n_ref[:, rows] = decay * n_ref[:, rows] + hd["ksum"][:, half * ML_DQK:(half + 1) * ML_DQK]
        m_ref[:, h:h + 1] = hd["m_new"]


def _prompt(sinks, x2d, g_norm, w, b, wg, bg, rope_c, rope_i, mk, mv, w_out, g_final):
    n = x2d.shape[0]
    nc = n // CHUNK
    const = lambda c, s: (0, 0)
    resident = pl.Buffered(1)
    assert nc % 2 == 0
    n_pairs = nc // 2
    x_in = lambda p, s: (jnp.minimum(p, n_pairs - 1), 0)
    x_res = lambda p, s: (jnp.clip(p - 1, 0, n_pairs - 1), 0)
    return pl.pallas_call(
        _prompt_kernel,
        grid_spec=pltpu.PrefetchScalarGridSpec(
            num_scalar_prefetch=1,
            grid=(n_pairs + 1,),
            in_specs=[
                pl.BlockSpec((2 * CHUNK, D_MODEL), x_in),
                pl.BlockSpec((2 * CHUNK, D_MODEL), x_res),
                pl.BlockSpec((1, D_MODEL), const, pipeline_mode=resident),
                pl.BlockSpec((D_MODEL, D_MAIN), const, pipeline_mode=resident),
                pl.BlockSpec((1, D_MAIN), const, pipeline_mode=resident),
                pl.BlockSpec((D_MODEL, GATE_LANES), const, pipeline_mode=resident),
                pl.BlockSpec((1, GATE_LANES), const, pipeline_mode=resident),
                pl.BlockSpec((2, nc, HEAD_DIM), lambda c, s: (0, 0, 0), pipeline_mode=resident),
                pl.BlockSpec((4, CHUNK, HEAD_DIM), lambda c, s: (0, 0, 0), pipeline_mode=resident),
                pl.BlockSpec((MEM_LEN, MEM_W), const, pipeline_mode=resident),
                pl.BlockSpec((MEM_LEN, MEM_W), const, pipeline_mode=resident),
                pl.BlockSpec((D_MIX, D_MODEL), const, pipeline_mode=resident),
                pl.BlockSpec((1, D_MODEL), const, pipeline_mode=resident),
            ],
            out_specs=[
                pl.BlockSpec((2 * CHUNK, D_MODEL), x_res),
                pl.BlockSpec((CHUNK, SWA_KV_W), const),
                pl.BlockSpec((CHUNK, SWA_KV_W), const),
                pl.BlockSpec((ML_QK_W, ML_DV), const),
                pl.BlockSpec((1, ML_QK_W), const),
                pl.BlockSpec((1, GATE_LANES), const),
            ],
            scratch_shapes=[
                pltpu.VMEM((CHUNK, SWA_KV_W), BF16),
                pltpu.VMEM((CHUNK, SWA_KV_W), BF16),
                pltpu.VMEM((CHUNK, D_MODEL), BF16),
                pltpu.VMEM((CHUNK, D_MAIN), F32),
                pltpu.VMEM((CHUNK, GATE_LANES), F32),
                pltpu.VMEM((CHUNK, D_MAIN), F32),
                pltpu.VMEM((CHUNK, GATE_LANES), F32),
                pltpu.VMEM((CHUNK, D_MIX), BF16),
                pltpu.VMEM((CHUNK, D_MIX), BF16),
            ],
        ),
        out_shape=[
            jax.ShapeDtypeStruct((n, D_MODEL), F32),
            jax.ShapeDtypeStruct((CHUNK, SWA_KV_W), F32),
            jax.ShapeDtypeStruct((CHUNK, SWA_KV_W), F32),
            jax.ShapeDtypeStruct((ML_QK_W, ML_DV), F32),
            jax.ShapeDtypeStruct((1, ML_QK_W), F32),
            jax.ShapeDtypeStruct((1, GATE_LANES), F32),
        ],
        compiler_params=pltpu.CompilerParams(
            dimension_semantics=("arbitrary",), vmem_limit_bytes=VMEM_LIMIT),
        name="prompt",
    )(sinks, x2d, x2d, g_norm, w, b, wg, bg, rope_c, rope_i, mk, mv, w_out, g_final)


OUT_TM = 512


def _outproj_kernel(y_ref, x_ref, w_ref, g_ref, o_ref):
    acc = _dot(y_ref[...], w_ref[...]) + x_ref[...]
    ms = jnp.mean(acc * acc, axis=-1, keepdims=True)
    o_ref[...] = (acc * lax.rsqrt(ms + EPS)) * g_ref[...]


def _outproj(y, x2d, w, g):
    n = x2d.shape[0]
    return pl.pallas_call(
        _outproj_kernel,
        grid=(n // OUT_TM,),
        in_specs=[
            pl.BlockSpec((OUT_TM, D_MIX), lambda i: (i, 0)),
            pl.BlockSpec((OUT_TM, D_MODEL), lambda i: (i, 0)),
            pl.BlockSpec((D_MIX, D_MODEL), lambda i: (0, 0)),
            pl.BlockSpec((1, D_MODEL), lambda i: (0, 0)),
        ],
        out_specs=pl.BlockSpec((OUT_TM, D_MODEL), lambda i: (i, 0)),
        out_shape=jax.ShapeDtypeStruct((n, D_MODEL), F32),
        compiler_params=pltpu.CompilerParams(
            dimension_semantics=("arbitrary",), vmem_limit_bytes=VMEM_LIMIT),
        name="outproj",
    )(y, x2d, w, g)


SB = 16
SR = SB * DEC_SEQ
SEQ_UNROLL = 4
SWA_UNROLL = 8


def _mix_sample_kernel(sinks_ref, z_ref, zg_ref, cos_ref, sin_ref, ck_ref, cv_ref,
                       c_ref, nst_ref, nrep_ref, mrep_ref,
                       y_ref, ko_ref, vo_ref, co_ref, no_ref, mo_ref,
                       q_s, k_s, o_s):
    T = DEC_SEQ
    scale = HEAD_DIM ** -0.5
    cos = cos_ref[...]
    sin = sin_ref[...]

    for h in range(SWA_HEADS):
        q_s[:, h * HEAD_DIM:(h + 1) * HEAD_DIM] = _rope(
            z_ref[:, O_SQ + h * HEAD_DIM:O_SQ + (h + 1) * HEAD_DIM], cos, sin)
    for kv in range(SWA_KV_HEADS):
        k_s[:, kv * HEAD_DIM:(kv + 1) * HEAD_DIM] = _rope(
            z_ref[:, O_SK + kv * HEAD_DIM:O_SK + (kv + 1) * HEAD_DIM], cos, sin)

    KP = 2 * WINDOW
    rt = lax.broadcasted_iota(jnp.int32, (GQA_GROUP * T, KP), 0) & (T - 1)
    cj = lax.broadcasted_iota(jnp.int32, (GQA_GROUP * T, KP), 1)
    mask = (cj > rt) & (cj <= rt + WINDOW)
    kpad = jnp.zeros((KP - WINDOW - T, HEAD_DIM), F32)
    NKV = SWA_KV_HEADS
    SEQ_ROWS = WINDOW * NKV

    keep = SEQ_ROWS - T * NKV
    sink_cols = [jnp.concatenate([jnp.full((T, 1), sinks_ref[kv * GQA_GROUP + g], F32)
                                  for g in range(GQA_GROUP)], axis=0) for kv in range(NKV)]

    def per_group(i, carry):
        chains = [(i * SWA_UNROLL + j, kv) for j in range(SWA_UNROLL) for kv in range(NKV)]
        scores, values = [], []
        for b, kv in chains:
            rows = pl.ds(pl.multiple_of(b * T, T), T)
            base = pl.multiple_of(b * SEQ_ROWS, SEQ_ROWS)
            cols = slice(kv * HEAD_DIM, (kv + 1) * HEAD_DIM)
            k_new = k_s[rows, cols]
            v_new = z_ref[rows, O_SV + kv * HEAD_DIM:O_SV + (kv + 1) * HEAD_DIM]
            if kv == 0:
                ko_ref[pl.ds(base, keep), :] = ck_ref[pl.ds(base + T * NKV, keep), :]
                vo_ref[pl.ds(base, keep), :] = cv_ref[pl.ds(base + T * NKV, keep), :]
            ko_ref[pl.ds(base + keep + kv, T, stride=NKV), :] = k_new
            vo_ref[pl.ds(base + keep + kv, T, stride=NKV), :] = v_new
            kc = ck_ref[pl.ds(base + kv, WINDOW, stride=NKV), :]
            vc = cv_ref[pl.ds(base + kv, WINDOW, stride=NKV), :]
            k_all = jnp.concatenate([kc, k_new, kpad], axis=0).astype(BF16)
            values.append(jnp.concatenate([vc, v_new, kpad], axis=0).astype(BF16))
            q_st = jnp.concatenate([q_s[rows, (kv * GQA_GROUP + g) * HEAD_DIM:(kv * GQA_GROUP + g + 1) * HEAD_DIM]
                                    for g in range(GQA_GROUP)], axis=0).astype(BF16)
            scores.append(_dot_nt(q_st, k_all))
        probs, norms = [], []
        for (b, kv), s in zip(chains, scores):
            s = jnp.where(mask, s * scale, -jnp.inf)
            m = jnp.maximum(jnp.max(s, axis=-1, keepdims=True), sink_cols[kv])
            p = jnp.exp(s - m)
            norms.append(jnp.sum(p, axis=-1, keepdims=True) + jnp.exp(sink_cols[kv] - m))
            probs.append(p.astype(BF16))
        outs = [_dot(p, v) for p, v in zip(probs, values)]
        for (b, kv), o, l in zip(chains, outs, norms):
            rows = pl.ds(pl.multiple_of(b * T, T), T)
            o = o * (1.0 / l)
            for g in range(GQA_GROUP):
                h = kv * GQA_GROUP + g
                o_s[rows, h * HEAD_DIM:(h + 1) * HEAD_DIM] = o[g * T:(g + 1) * T]
        return carry

    lax.fori_loop(0, SB // SWA_UNROLL, per_group, 0)

    for h in range(SWA_HEADS):
        cols = slice(h * HEAD_DIM, (h + 1) * HEAD_DIM)
        gate = _silu(z_ref[:, O_SG + h * HEAD_DIM:O_SG + (h + 1) * HEAD_DIM])
        y_ref[:, cols] = (o_s[:, cols] * gate).astype(BF16)
    y_ref[:, SWA_W + ML_W:] = jnp.zeros((SR, MEM_W), BF16)

    R = SR
    lane = lax.broadcasted_iota(jnp.int32, (1, GATE_LANES), 1)
    zg = zg_ref[...]
    gates = jnp.where(lane < LANE_F, zg, _log_sigmoid(zg))
    gates = jnp.where(lane < 2 * ML_HEADS, gates, 0.0)
    rr = lax.broadcasted_iota(jnp.int32, (R, R), 0)
    cc = lax.broadcasted_iota(jnp.int32, (R, R), 1)
    same_seq = (rr >> 3) == (cc >> 3)
    causal = same_seq & (rr >= cc)
    tri = jnp.where(causal, 1.0, 0.0).astype(BF16)
    csum = _exact_sel_dot(tri, gates)
    gates_t = gates.T
    csum_t = csum.T
    half_lo = lane < ML_DQK
    seq_of_col = lax.broadcasted_iota(jnp.int32, (SB, 1, R), 2) >> 3
    seq_id = lax.broadcasted_iota(jnp.int32, (SB, 1, R), 0)
    own_cols = seq_of_col == seq_id
    own_blk = ((lax.broadcasted_iota(jnp.int32, (R, SB * 128), 0) >> 3)
               == (lax.broadcasted_iota(jnp.int32, (R, SB * 128), 1) >> 7))
    k_t = [(z_ref[:, O_MK + p * 128:O_MK + (p + 1) * 128] * (ML_DQK ** -0.5)).T for p in range(2)]
    mrep = mrep_ref[...]
    tok3 = lax.broadcasted_iota(jnp.int32, (SB, T, 1), 1)
    mo_ref[...] = jnp.zeros_like(mo_ref)

    def per_seq_value(col):
        return jnp.max(col.reshape(SB, T, 1), axis=1, keepdims=True)

    def last_of_seq(col):
        c3 = jnp.where(tok3 == T - 1, col.reshape(SB, T, 1), -jnp.inf)
        return jnp.broadcast_to(jnp.max(c3, axis=1, keepdims=True), (SB, T, 1)).reshape(R, 1)

    for h in range(ML_HEADS):
        pair, half = divmod(h, 2)
        hmask = half_lo if half == 0 else jnp.logical_not(half_lo)
        q_pair = z_ref[:, O_MQ + pair * 128:O_MQ + (pair + 1) * 128]
        k_pair = z_ref[:, O_MK + pair * 128:O_MK + (pair + 1) * 128] * (ML_DQK ** -0.5)
        v_f = z_ref[:, O_MV + h * ML_DV:O_MV + (h + 1) * ML_DV]
        v_h = v_f.astype(BF16)
        q_h = jnp.where(hmask, q_pair, 0.0)
        q_hb = q_h.astype(BF16)
        k_pb = k_pair.astype(BF16)

        bt_c = csum[:, LANE_F + h:LANE_F + h + 1]
        it_c = gates[:, LANE_I + h:LANE_I + h + 1]
        bt_r = csum_t[LANE_F + h:LANE_F + h + 1, :]
        it_r = gates_t[LANE_I + h:LANE_I + h + 1, :]
        m_prev = mrep[:, h:h + 1]

        dlog = jnp.where(causal, bt_c - bt_r + it_r, -jnp.inf)
        inter = bt_c + m_prev
        m_t = jnp.maximum(inter, jnp.max(dlog, axis=-1, keepdims=True))
        w_intra = jnp.exp(dlog - m_t)
        w_state = jnp.exp(inter - m_t)
        a = w_intra * _dot_nt(q_hb, k_pb)

        ct_st = c_ref[:, 2 * pair:2 * pair + 2].reshape(SB * 128, ML_DV).astype(BF16)
        q_blk = jnp.where(own_blk, jnp.tile(q_h, (1, SB)), 0.0).astype(BF16)
        num_state = _dot(q_blk, ct_st)

        num = _dot(a.astype(BF16), v_h) + w_state * num_state
        n_pair = nrep_ref[:, pair * 128:(pair + 1) * 128]
        den = jnp.sum(a, axis=-1, keepdims=True) + w_state * jnp.sum(q_h * n_pair, axis=-1, keepdims=True)
        denom = jnp.maximum(jnp.abs(den), jnp.exp(-m_t))
        hid = num * (1.0 / denom)
        o_gate = _sigmoid(z_ref[:, O_MO + h * ML_DV:O_MO + (h + 1) * ML_DV])
        gate = _silu(z_ref[:, O_MG + h * ML_DV:O_MG + (h + 1) * ML_DV])
        col = SWA_W + h * ML_DV
        y_ref[:, col:col + ML_DV] = ((hid * o_gate) * gate).astype(BF16)

        m_new = last_of_seq(m_t)
        bt_last = last_of_seq(bt_c)
        w_s = jnp.exp(bt_last - bt_c + it_c - m_new)
        decay = jnp.exp(bt_last + m_prev - m_new)
        decay_seq = per_seq_value(decay)

        kt_h = k_t[pair][half * ML_DQK:(half + 1) * ML_DQK, :]
        lhs = jnp.where(own_cols, kt_h[None, :, :], 0.0).reshape(SB * ML_DQK, R).astype(BF16)
        upd = _dot(lhs, (v_f * w_s).astype(BF16)).reshape(SB, ML_DQK, ML_DV)
        co_ref[:, h] = decay_seq * c_ref[:, h] + upd

        kw = (k_pair * w_s).reshape(SB, T, 128)
        ksum = jnp.sum(kw, axis=1)
        n_old = nst_ref[:, h * ML_DQK:(h + 1) * ML_DQK]
        dec2 = decay_seq.reshape(SB, 1)
        no_ref[:, h * ML_DQK:(h + 1) * ML_DQK] = dec2 * n_old + ksum[:, half * ML_DQK:(half + 1) * ML_DQK]
        mo_ref[:, h:h + 1] = per_seq_value(m_new).reshape(SB, 1)


def _mix_sample(sinks, z, zg, cos, sin, ck, cv, cst, nst, nrep, mrep):
    nb = cst.shape[0]
    steps = nb // SB
    const = lambda i, s: (0, 0)
    cache_rows = SB * WINDOW * SWA_KV_HEADS
    return pl.pallas_call(
        _mix_sample_kernel,
        grid_spec=pltpu.PrefetchScalarGridSpec(
            num_scalar_prefetch=1,
            grid=(steps,),
            in_specs=[
                pl.BlockSpec((SR, D_MAIN), lambda i, s: (i, 0)),
                pl.BlockSpec((SR, GATE_LANES), lambda i, s: (i, 0)),
                pl.BlockSpec((SR, HEAD_DIM), const),
                pl.BlockSpec((SR, HEAD_DIM), const),
                pl.BlockSpec((cache_rows, HEAD_DIM), lambda i, s: (i, 0)),
                pl.BlockSpec((cache_rows, HEAD_DIM), lambda i, s: (i, 0)),
                pl.BlockSpec((SB, ML_HEADS, ML_DQK, ML_DV), lambda i, s: (i, 0, 0, 0)),
                pl.BlockSpec((SB, ML_QK_W), lambda i, s: (i, 0)),
                pl.BlockSpec((SR, ML_QK_W), lambda i, s: (i, 0)),
                pl.BlockSpec((SR, GATE_LANES), lambda i, s: (i, 0)),
            ],
            out_specs=[
                pl.BlockSpec((SR, D_MIX), lambda i, s: (i, 0)),
                pl.BlockSpec((cache_rows, HEAD_DIM), lambda i, s: (i, 0)),
                pl.BlockSpec((cache_rows, HEAD_DIM), lambda i, s: (i, 0)),
                pl.BlockSpec((SB, ML_HEADS, ML_DQK, ML_DV), lambda i, s: (i, 0, 0, 0)),
                pl.BlockSpec((SB, ML_QK_W), lambda i, s: (i, 0)),
                pl.BlockSpec((SB, GATE_LANES), lambda i, s: (i, 0)),
            ],
            scratch_shapes=[
                pltpu.VMEM((SR, SWA_W), F32),
                pltpu.VMEM((SR, SWA_KV_W), F32),
                pltpu.VMEM((SR, SWA_W), F32),
            ],
        ),
        out_shape=[
            jax.ShapeDtypeStruct((nb * DEC_SEQ, D_MIX), BF16),
            jax.ShapeDtypeStruct(ck.shape, F32),
            jax.ShapeDtypeStruct(cv.shape, F32),
            jax.ShapeDtypeStruct((nb, ML_HEADS, ML_DQK, ML_DV), F32),
            jax.ShapeDtypeStruct((nb, ML_QK_W), F32),
            jax.ShapeDtypeStruct((nb, GATE_LANES), F32),
        ],
        compiler_params=pltpu.CompilerParams(
            dimension_semantics=("arbitrary",), vmem_limit_bytes=VMEM_LIMIT),
        name="mix_sample",
    )(sinks, z, zg, cos, sin, ck, cv, cst, nst, nrep, mrep)


MB = 8
MR = MB * DEC_SEQ


def _mem_sample_kernel(cq_ref, cg_ref, mk_ref, mv_ref, y_in_ref, y_ref, o_s):
    del y_in_ref
    T = DEC_SEQ
    scale = HEAD_DIM ** -0.5
    zpad = jnp.zeros((T, HEAD_DIM), F32)

    seq_rows = MEM_LEN * MEM_HEADS

    def per_group(i, carry):
        chains = [(i * SEQ_UNROLL + j, h) for j in range(SEQ_UNROLL) for h in range(MEM_HEADS)]
        scores = []
        for b, h in chains:
            rows = pl.ds(pl.multiple_of(b * T, T), T)
            base = pl.multiple_of(b * seq_rows, seq_rows)
            kb = mk_ref[pl.ds(base + h, MEM_LEN, stride=MEM_HEADS), :].astype(BF16)
            q = jnp.concatenate([cq_ref[rows, h * HEAD_DIM:(h + 1) * HEAD_DIM], zpad], axis=0).astype(BF16)
            scores.append(_dot_nt(q, kb))
        probs, norms = [], []
        for s in scores:
            s = s * scale
            p = jnp.exp(s - jnp.max(s, axis=-1, keepdims=True))
            norms.append(jnp.sum(p, axis=-1, keepdims=True))
            probs.append(p.astype(BF16))
        outs = []
        for (b, h), p in zip(chains, probs):
            base = pl.multiple_of(b * seq_rows, seq_rows)
            vb = mv_ref[pl.ds(base + h, MEM_LEN, stride=MEM_HEADS), :].astype(BF16)
            outs.append(_dot(p, vb))
        for (b, h), o, l in zip(chains, outs, norms):
            rows = pl.ds(pl.multiple_of(b * T, T), T)
            o_s[rows, h * HEAD_DIM:(h + 1) * HEAD_DIM] = (o * (1.0 / l))[0:T]
        return carry

    lax.fori_loop(0, MB // SEQ_UNROLL, per_group, 0)
    y_ref[...] = (o_s[...] * _silu(cg_ref[...])).astype(BF16)


def _mem_sample(z, mk, mv, y):
    nb = z.shape[0] // DEC_SEQ
    cache_rows = MB * MEM_LEN * MEM_HEADS
    return pl.pallas_call(
        _mem_sample_kernel,
        grid=(nb // MB,),
        in_specs=[
            pl.BlockSpec((MR, MEM_W), lambda i: (i, O_CQ // MEM_W)),
            pl.BlockSpec((MR, MEM_W), lambda i: (i, O_CG // MEM_W)),
            pl.BlockSpec((cache_rows, HEAD_DIM), lambda i: (i, 0)),
            pl.BlockSpec((cache_rows, HEAD_DIM), lambda i: (i, 0)),
            pl.BlockSpec(memory_space=pl.ANY),
        ],
        out_specs=pl.BlockSpec((MR, MEM_W), lambda i: (i, (SWA_W + ML_W) // MEM_W)),
        out_shape=jax.ShapeDtypeStruct(y.shape, y.dtype),
        input_output_aliases={4: 0},
        scratch_shapes=[pltpu.VMEM((MR, MEM_W), F32)],
        compiler_params=pltpu.CompilerParams(
            dimension_semantics=("arbitrary",), vmem_limit_bytes=VMEM_LIMIT),
        name="mem_sample",
    )(z, z, mk, mv, y)


def _rope_cos_sin(pos):
    half = HEAD_DIM // 2
    inv = jnp.power(ROPE_THETA, -(jnp.arange(half, dtype=F32) * 2.0 / HEAD_DIM))
    ang = pos.astype(F32)[:, None] * inv[None, :]
    return jnp.cos(ang), jnp.sin(ang)


def _rope_tables(pos):
    cos, sin = _rope_cos_sin(pos)
    return jnp.concatenate([cos, cos], axis=-1), jnp.concatenate([-sin, sin], axis=-1)


def _rope_split_tables(n_chunks):
    ca, sa = _rope_cos_sin(jnp.arange(n_chunks, dtype=jnp.int32) * CHUNK)
    cb, sb = _rope_cos_sin(jnp.arange(CHUNK, dtype=jnp.int32))
    dup = lambda t: jnp.concatenate([t, t], axis=-1)
    sgn = lambda t: jnp.concatenate([-t, t], axis=-1)
    return jnp.stack([dup(ca), dup(sa)]), jnp.stack([dup(cb), dup(sb), sgn(cb), sgn(sb)])


def _relayout_in_proj(w_in, b_in):
    w_t = jnp.swapaxes(w_in, 0, 1)
    w_main = _wprep(w_t)
    w_gate = _wgate(w_t)
    b_main = jnp.concatenate([b_in[_R_SQ:_R_MI], b_in[_R_MO:_R_END]])[None, :]
    b_gate = jnp.pad(b_in[_R_MI:_R_MO], (0, GATE_LANES - 2 * ML_HEADS))[None, :]
    return w_main, b_main, w_gate, b_gate


def _layer(xp, xs, mem, ck, cv, c_st, n_st, m_st, cmk, cmv,
           g_norm, w_in, b_in, sinks, g_mem, w_mem_kv, w_out, g_final):
    bp, sp, _ = xp.shape
    bs, ts, _ = xs.shape
    xp2 = xp.reshape(bp * sp, D_MODEL)
    xs2 = xs.reshape(bs * ts, D_MODEL)
    w_main, b_main, w_gate, b_gate = _relayout_in_proj(w_in, b_in)
    g_norm2 = g_norm[None, :]
    w_out_b = w_out.astype(BF16)
    g_final2 = g_final[None, :]
    sinks_flat = sinks.reshape(SWA_HEADS)

    memkv = _memkv(mem.reshape(MEM_LEN, D_MODEL), g_mem[None, :], w_mem_kv.astype(BF16))
    mem_k = memkv[:, :MEM_W]
    mem_v = memkv[:, MEM_W:]
    rope_c, rope_i = _rope_split_tables(sp // CHUNK)
    out_p, pk, pv, ct, n_p, m_p = _prompt(sinks_flat, xp2, g_norm2, w_main, b_main, w_gate, b_gate,
                                           rope_c, rope_i, mem_k.astype(BF16), mem_v.astype(BF16),
                                           w_out_b, g_final2)

    zs, zgs = _proj(xs2, g_norm2, w_main, b_main, w_gate, b_gate)
    cos_s, sin_s = _rope_tables(PAST_LEN + jnp.arange(ts, dtype=jnp.int32))
    cos_s = jnp.tile(cos_s, (SB, 1))
    sin_s = jnp.tile(sin_s, (SB, 1))
    nrep = jnp.repeat(n_st.reshape(bs, ML_QK_W), ts, axis=0)
    m_pad = jnp.pad(m_st, ((0, 0), (0, GATE_LANES - ML_HEADS)))
    mrep = jnp.repeat(m_pad, ts, axis=0)
    ya, sk_o, sv_o, ct_o, n_o, m_o = _mix_sample(
        sinks_flat, zs, zgs, cos_s, sin_s,
        ck.reshape(bs * WINDOW * SWA_KV_HEADS, HEAD_DIM), cv.reshape(bs * WINDOW * SWA_KV_HEADS, HEAD_DIM),
        jnp.swapaxes(c_st, -1, -2), n_st.reshape(bs, ML_QK_W), nrep, mrep)
    ys = _mem_sample(zs, cmk.reshape(bs * MEM_LEN * MEM_HEADS, HEAD_DIM),
                     cmv.reshape(bs * MEM_LEN * MEM_HEADS, HEAD_DIM), ya)
    out_s = _outproj(ys, xs2, w_out_b, g_final2)
    c_o = jnp.swapaxes(ct_o, -1, -2)

    p_state = (
        pk.reshape(bp, WINDOW, SWA_KV_HEADS, HEAD_DIM),
        pv.reshape(bp, WINDOW, SWA_KV_HEADS, HEAD_DIM),
        ct.reshape(ML_HEADS, ML_DQK, ML_DV).transpose(0, 2, 1)[None],
        n_p.reshape(bp, ML_HEADS, ML_DQK),
        m_p[:, :ML_HEADS],
        mem_k.reshape(bp, MEM_LEN, MEM_HEADS, HEAD_DIM),
        mem_v.reshape(bp, MEM_LEN, MEM_HEADS, HEAD_DIM),
    )
    s_state = (
        sk_o.reshape(bs, WINDOW, SWA_KV_HEADS, HEAD_DIM),
        sv_o.reshape(bs, WINDOW, SWA_KV_HEADS, HEAD_DIM),
        c_o,
        n_o.reshape(bs, ML_HEADS, ML_DQK),
        m_o[:, :ML_HEADS],
    )
    return out_p.reshape(bp, sp, D_MODEL), out_s.reshape(bs, ts, D_MODEL), p_state, s_state


def kernel(x_prompt, x_sample, mem_prompt, cache_swa_k, cache_swa_v, state_mlstm_C, state_mlstm_n,
           state_mlstm_m, cache_mem_k, cache_mem_v, g_norm, w_in, b_in, swa_sinks, g_mem, w_mem_kv,
           w_out, g_final):
    depth = g_norm.shape[0]
    assert depth == 1 and x_prompt.shape[0] == 1
    y_p, y_s, p_state, s_state = _layer(
        x_prompt, x_sample, mem_prompt, cache_swa_k[0], cache_swa_v[0], state_mlstm_C[0],
        state_mlstm_n[0], state_mlstm_m[0], cache_mem_k[0], cache_mem_v[0],
        g_norm[0], w_in[0], b_in[0], swa_sinks[0], g_mem[0], w_mem_kv[0], w_out[0], g_final)
    return (y_p, y_s) + tuple(s[None] for s in p_state) + tuple(s[None] for s in s_state)
```

```python
import functools

import jax
import jax.numpy as jnp
from jax import lax
from jax.experimental import pallas as pl
from jax.experimental.pallas import tpu as pltpu

F32 = jnp.float32
BF16 = jnp.bfloat16

D_MODEL = 2048
HEAD_DIM = 128
SWA_HEADS = 8
SWA_KV_HEADS = 2
GQA_GROUP = 4
WINDOW = 128
ML_HEADS = 4
ML_DQK = 64
ML_DV = 128
MEM_HEADS = 4
MEM_LEN = 256
CHUNK = 128
ROPE_THETA = 10000.0
EPS = 1e-6
PAST_LEN = 16384
DEC_SEQ = 8

SWA_W = SWA_HEADS * HEAD_DIM
SWA_KV_W = SWA_KV_HEADS * HEAD_DIM
ML_W = ML_HEADS * ML_DV
ML_QK_W = ML_HEADS * ML_DQK
MEM_W = MEM_HEADS * HEAD_DIM
D_MIX = SWA_W + ML_W + MEM_W

_IN_WIDTHS = (SWA_W, SWA_KV_W, SWA_KV_W, SWA_W, ML_QK_W, ML_QK_W, ML_W, ML_HEADS, ML_HEADS, ML_W, ML_W, MEM_W, MEM_W)
_IN_OFFS = [0]
for _w in _IN_WIDTHS:
    _IN_OFFS.append(_IN_OFFS[-1] + _w)
(_R_SQ, _R_SK, _R_SV, _R_SG, _R_MQ, _R_MK, _R_MV, _R_MI, _R_MF, _R_MO, _R_MG, _R_CQ, _R_CG, _R_END) = _IN_OFFS

O_SQ = 0
O_SK = O_SQ + SWA_W
O_SV = O_SK + SWA_KV_W
O_SG = O_SV + SWA_KV_W
O_MQ = O_SG + SWA_W
O_MK = O_MQ + ML_QK_W
O_MV = O_MK + ML_QK_W
O_MO = O_MV + ML_W
O_MG = O_MO + ML_W
O_CQ = O_MG + ML_W
O_CG = O_CQ + MEM_W
D_MAIN = O_CG + MEM_W
W_A_COLS = O_MO
GATE_LANES = 128
LANE_I = 0
LANE_F = ML_HEADS

VMEM_LIMIT = 58 * 1024 * 1024

_NT = (((1,), (1,)), ((), ()))


def _dot(a, b):
    return jnp.dot(a, b, preferred_element_type=F32)


def _dot_nt(a, b):
    return lax.dot_general(a, b, _NT, preferred_element_type=F32)


def _exact_sel_dot(sel_bf16, x):
    hi = x.astype(BF16)
    r1 = x - hi.astype(F32)
    mid = r1.astype(BF16)
    lo = (r1 - mid.astype(F32)).astype(BF16)
    return _dot(sel_bf16, hi) + _dot(sel_bf16, mid) + _dot(sel_bf16, lo)


def _silu(x):
    h = 0.5 * x
    return h + h * jnp.tanh(h)


def _sigmoid(x):
    return 0.5 + 0.5 * jnp.tanh(0.5 * x)


def _log_sigmoid(x):
    return jnp.minimum(x, 0.0) - jnp.log1p(jnp.exp(-jnp.abs(x)))


PROJ_TM = 1024
PROJ_TN = 512
NORM_ROWS = 256


def _proj_kernel(x_ref, g_ref, w_ref, b_ref, wg_ref, bg_ref, z_ref, zg_ref, u_ref):
    j = pl.program_id(1)

    @pl.when(j == 0)
    def _():
        g = g_ref[...]
        for r in range(PROJ_TM // NORM_ROWS):
            rows = pl.ds(r * NORM_ROWS, NORM_ROWS)
            xf = x_ref[rows, :]
            ms = jnp.mean(xf * xf, axis=-1, keepdims=True)
            u_ref[rows, :] = ((xf * lax.rsqrt(ms + EPS)) * g).astype(BF16)
        zg_ref[...] = _dot(u_ref[...], wg_ref[...]) + bg_ref[...]

    z_ref[...] = _dot(u_ref[...], w_ref[...]) + b_ref[...]


def _proj(x2d, g, w, b, wg, bg):
    n = x2d.shape[0]
    grid = (n // PROJ_TM, D_MAIN // PROJ_TN)
    return pl.pallas_call(
        _proj_kernel,
        grid=grid,
        in_specs=[
            pl.BlockSpec((PROJ_TM, D_MODEL), lambda i, j: (i, 0)),
            pl.BlockSpec((1, D_MODEL), lambda i, j: (0, 0)),
            pl.BlockSpec((D_MODEL, PROJ_TN), lambda i, j: (0, j)),
            pl.BlockSpec((1, PROJ_TN), lambda i, j: (0, j)),
            pl.BlockSpec((D_MODEL, GATE_LANES), lambda i, j: (0, 0)),
            pl.BlockSpec((1, GATE_LANES), lambda i, j: (0, 0)),
        ],
        out_specs=[
            pl.BlockSpec((PROJ_TM, PROJ_TN), lambda i, j: (i, j)),
            pl.BlockSpec((PROJ_TM, GATE_LANES), lambda i, j: (i, 0)),
        ],
        out_shape=[
            jax.ShapeDtypeStruct((n, D_MAIN), F32),
            jax.ShapeDtypeStruct((n, GATE_LANES), F32),
        ],
        scratch_shapes=[pltpu.VMEM((PROJ_TM, D_MODEL), BF16)],
        compiler_params=pltpu.CompilerParams(
            dimension_semantics=("arbitrary", "arbitrary"), vmem_limit_bytes=VMEM_LIMIT),
        name="proj",
    )(x2d, g, w, b, wg, bg)


WPREP_TN = 256


def _wprep_kernel(wt_ref, *refs):
    o_ref = refs[-1]
    o_ref[...] = wt_ref[...].T.astype(BF16)


def _wprep_part(w_t, dst, src_row0, dst_tile0, n_tiles):
    src = pl.BlockSpec((pl.Element(WPREP_TN), pl.Element(D_MODEL)),
                       lambda j: (pl.multiple_of(j * WPREP_TN + src_row0, 8), 0))
    in_specs, args, aliases = [src], [w_t], {}
    if dst is not None:
        in_specs.append(pl.BlockSpec(memory_space=pl.ANY))
        args.append(dst)
        aliases = {1: 0}
    return pl.pallas_call(
        _wprep_kernel,
        grid=(n_tiles,),
        in_specs=in_specs,
        out_specs=pl.BlockSpec((D_MODEL, WPREP_TN), lambda j: (0, j + dst_tile0)),
        out_shape=jax.ShapeDtypeStruct((D_MODEL, D_MAIN), BF16),
        input_output_aliases=aliases,
        compiler_params=pltpu.CompilerParams(
            dimension_semantics=("arbitrary",), vmem_limit_bytes=VMEM_LIMIT),
        name="wprep",
    )(*args)


def _wprep(w_t):
    tiles_a = W_A_COLS // WPREP_TN
    part = _wprep_part(w_t, None, 0, 0, tiles_a)
    return _wprep_part(w_t, part, _R_MO, tiles_a, (D_MAIN - W_A_COLS) // WPREP_TN)


def _wgate_kernel(wt_ref, o_ref):
    rows = jnp.concatenate(
        [wt_ref[...], jnp.zeros((GATE_LANES - 2 * ML_HEADS, D_MODEL), F32)], axis=0)
    o_ref[...] = rows.T.astype(BF16)


def _wgate(w_t):
    return pl.pallas_call(
        _wgate_kernel,
        grid=(1,),
        in_specs=[pl.BlockSpec((pl.Element(2 * ML_HEADS), pl.Element(D_MODEL)), lambda j: (_R_MI, 0))],
        out_specs=pl.BlockSpec((D_MODEL, GATE_LANES), lambda j: (0, 0)),
        out_shape=jax.ShapeDtypeStruct((D_MODEL, GATE_LANES), BF16),
        name="wgate",
    )(w_t)


MEMKV_TN = 256


def _memkv_kernel(mem_ref, g_ref, w_ref, o_ref):
    xf = mem_ref[...]
    ms = jnp.mean(xf * xf, axis=-1, keepdims=True)
    u = ((xf * lax.rsqrt(ms + EPS)) * g_ref[...]).astype(BF16)
    o_ref[...] = _dot(u, w_ref[...])


def _memkv(mem2d, g, w):
    return pl.pallas_call(
        _memkv_kernel,
        grid=(2 * MEM_W // MEMKV_TN,),
        in_specs=[
            pl.BlockSpec((MEM_LEN, D_MODEL), lambda j: (0, 0)),
            pl.BlockSpec((1, D_MODEL), lambda j: (0, 0)),
            pl.BlockSpec((D_MODEL, MEMKV_TN), lambda j: (0, j)),
        ],
        out_specs=pl.BlockSpec((MEM_LEN, MEMKV_TN), lambda j: (0, j)),
        out_shape=jax.ShapeDtypeStruct((MEM_LEN, 2 * MEM_W), F32),
        compiler_params=pltpu.CompilerParams(
            dimension_semantics=("arbitrary",), vmem_limit_bytes=VMEM_LIMIT),
        name="memkv",
    )(mem2d, g, w)


OUT_TN = 256


def _rope(x, cos, sin_signed):
    return x * cos + pltpu.roll(x, HEAD_DIM // 2, axis=1) * sin_signed


IN_TN = 256
SWA_POINTS = 3 * SWA_KV_HEADS
MEM_POINTS = 2 * MEM_HEADS
MIXER_WEIGHTS = ([1.0, 3.0, 1.0] * SWA_KV_HEADS + [1.0] * MEM_HEADS + [0.5] * MEM_HEADS
                 + [1.0] + [1.5] * ML_HEADS + [1.0] * ML_HEADS + [0.5] * ML_HEADS)


def _spread(tasks, weights):
    total = sum(weights)
    bounds = [0]
    acc = 0.0
    for w in weights:
        acc += w
        bounds.append(round(len(tasks) * acc / total))
    return [tasks[a:b] for a, b in zip(bounds[:-1], bounds[1:])]


def _zip_then_rest(a, b):
    k = min(len(a), len(b))
    return [t for pair in zip(a[:k], b[:k]) for t in pair] + a[k:] + b[k:]


def _prompt_kernel(*refs):
    pair = pl.program_id(0)
    n_pairs = pl.num_programs(0) - 1

    carried = refs[_PROMPT_CARRIED]

    @pl.when(pair == 0)
    def _():
        for ref in carried:
            ref[...] = jnp.zeros_like(ref)
        _prompt_step(0, 0, True, False, False, *refs)
        _prompt_step(1, 1, True, True, False, *refs)

    @pl.when((pair > 0) & (pair < n_pairs))
    def _():
        _prompt_step(2 * pair, 0, True, True, True, *refs)
        _prompt_step(2 * pair + 1, 1, True, True, True, *refs)

    @pl.when(pair == n_pairs)
    def _():
        _prompt_step(2 * pair, 0, False, True, True, *refs)
        _prompt_step(2 * pair + 1, 1, False, False, True, *refs)


_PROMPT_CARRIED = slice(17, 22)


def _prompt_step(s, half, do_in, do_mix, do_out,
                 sinks_ref, xin2_ref, xres2_ref, gn_ref, w_ref, b_ref, wg_ref, bg_ref,
                 rope_c_ref, rope_i_ref, mk_ref, mv_ref, wout_ref, gfin_ref,
                 out2_ref, ko_ref, vo_ref, ct_ref, n_ref, m_ref,
                 kprev_ref, vprev_ref, u_ref, z_ref, zg_ref, zprev_ref, zgprev_ref,
                 y_ref, yprev_ref):
    rows = slice(half * CHUNK, (half + 1) * CHUNK)
    xin_ref = xin2_ref.at[rows]
    xres_ref = xres2_ref.at[rows]
    out_ref = out2_ref.at[rows]
    n_in = D_MAIN // IN_TN
    n_out = D_MODEL // OUT_TN

    def in_norm():
        xf = xin_ref[...]
        ms = jnp.mean(xf * xf, axis=-1, keepdims=True)
        u_ref[...] = ((xf * lax.rsqrt(ms + EPS)) * gn_ref[...]).astype(BF16)

    def in_tile(t):
        if t == n_in:
            zg_ref[...] = _dot(u_ref[...], wg_ref[...]) + bg_ref[...]
        else:
            cols = slice(t * IN_TN, (t + 1) * IN_TN)
            z_ref[:, cols] = _dot(u_ref[...], w_ref[:, cols]) + b_ref[:, cols]

    def in_rotate(lo, hi):
        zprev_ref[:, lo:hi] = z_ref[:, lo:hi]

    def in_rotate_rest():
        in_rotate(O_MQ, O_CQ)
        zgprev_ref[...] = zg_ref[...]

    def out_tile(t):
        cols = slice(t * OUT_TN, (t + 1) * OUT_TN)
        out_ref[:, cols] = _dot(yprev_ref[...], wout_ref[:, cols]) + xres_ref[:, cols]

    def out_finish():
        acc = out_ref[...]
        ms = jnp.mean(acc * acc, axis=-1, keepdims=True)
        out_ref[...] = (acc * lax.rsqrt(ms + EPS)) * gfin_ref[...]

    tile_of = lambda col: col // IN_TN
    in_order = (list(range(0, tile_of(O_MQ))) + list(range(tile_of(O_CQ), n_in))
                + list(range(tile_of(O_MQ), tile_of(O_CQ))) + [n_in])
    in_tasks = []
    for t in in_order:
        in_tasks.append((functools.partial(in_tile, t), 0))
        if t == tile_of(O_MQ) - 1:
            in_tasks.append((functools.partial(in_rotate, 0, O_MQ), SWA_POINTS))
        if t == n_in - 1:
            in_tasks.append((functools.partial(in_rotate, O_CQ, D_MAIN), SWA_POINTS + MEM_POINTS))
    out_tasks = [(functools.partial(out_tile, t), 0) for t in range(n_out)] + [(out_finish, 0)]

    def mix(tasks):
        groups = iter(_spread(tasks, MIXER_WEIGHTS))
        points_done = [0]

        def between():
            for emit, first_point in next(groups):
                assert points_done[0] >= first_point
                emit()
            points_done[0] += 1

        _prompt_mixers(s - 1, sinks_ref, zprev_ref, zgprev_ref, rope_c_ref, rope_i_ref, mk_ref, mv_ref,
                       y_ref, ko_ref, vo_ref, ct_ref, n_ref, m_ref, kprev_ref, vprev_ref, between)
        assert next(groups, None) is None

    def run(tasks):
        for emit, _ in tasks:
            emit()

    if do_in:
        in_norm()
    if do_in and do_out:
        tasks = out_tasks[:4] + _zip_then_rest(in_tasks, out_tasks[4:])
    else:
        tasks = in_tasks if do_in else out_tasks
    if do_mix:
        mix(tasks)
    else:
        run(tasks)
    if do_in:
        in_rotate_rest()
    if do_mix:
        yprev_ref[...] = y_ref[...]


def _prompt_mixers(c, sinks_ref, z_ref, zg_ref, rope_c_ref, rope_i_ref, mk_ref, mv_ref,
                   y_ref, ko_ref, vo_ref, ct_ref, n_ref, m_ref, kprev_ref, vprev_ref, between):
    L = CHUNK
    scale = HEAD_DIM ** -0.5

    cc = rope_c_ref[0, pl.ds(c, 1), :]
    sc = rope_c_ref[1, pl.ds(c, 1), :]
    cos = cc * rope_i_ref[0] - sc * rope_i_ref[1]
    sin = sc * rope_i_ref[2] + cc * rope_i_ref[3]

    ri = lax.broadcasted_iota(jnp.int32, (GQA_GROUP * L, 2 * L), 0) & (L - 1)
    cj = lax.broadcasted_iota(jnp.int32, (GQA_GROUP * L, 2 * L), 1)
    j_low = jnp.where(c > 0, 0, L)
    band = (cj > ri) & (cj <= ri + L) & (cj >= j_low)
    for kv in range(SWA_KV_HEADS):
        between()
        k_new = _rope(z_ref[:, O_SK + kv * HEAD_DIM:O_SK + (kv + 1) * HEAD_DIM], cos, sin)
        v_new = z_ref[:, O_SV + kv * HEAD_DIM:O_SV + (kv + 1) * HEAD_DIM]
        ko_ref[:, kv * HEAD_DIM:(kv + 1) * HEAD_DIM] = k_new
        vo_ref[:, kv * HEAD_DIM:(kv + 1) * HEAD_DIM] = v_new
        k_new_b = k_new.astype(BF16)
        v_new_b = v_new.astype(BF16)
        kcat = jnp.concatenate([kprev_ref[:, kv * HEAD_DIM:(kv + 1) * HEAD_DIM], k_new_b], axis=0)
        vcat = jnp.concatenate([vprev_ref[:, kv * HEAD_DIM:(kv + 1) * HEAD_DIM], v_new_b], axis=0)
        kprev_ref[:, kv * HEAD_DIM:(kv + 1) * HEAD_DIM] = k_new_b
        vprev_ref[:, kv * HEAD_DIM:(kv + 1) * HEAD_DIM] = v_new_b
        qs = []
        sks = []
        for g in range(GQA_GROUP):
            h = kv * GQA_GROUP + g
            qs.append(_rope(z_ref[:, O_SQ + h * HEAD_DIM:O_SQ + (h + 1) * HEAD_DIM], cos, sin).astype(BF16))
            sks.append(jnp.full((L, 1), sinks_ref[h], F32))
        q_st = jnp.concatenate(qs, axis=0)
        sk = jnp.concatenate(sks, axis=0)
        s = _dot_nt(q_st, kcat)
        between()
        s = jnp.where(band, s * scale, -jnp.inf)
        m = jnp.maximum(jnp.max(s, axis=-1, keepdims=True), sk)
        p = jnp.exp(s - m)
        l = jnp.sum(p, axis=-1, keepdims=True) + jnp.exp(sk - m)
        o = _dot(p.astype(BF16), vcat)
        between()
        o = o * (1.0 / l)
        for g in range(GQA_GROUP):
            h = kv * GQA_GROUP + g
            gate = _silu(z_ref[:, O_SG + h * HEAD_DIM:O_SG + (h + 1) * HEAD_DIM])
            y_ref[:, h * HEAD_DIM:(h + 1) * HEAD_DIM] = (o[g * L:(g + 1) * L] * gate).astype(BF16)

    mem_s, mem_l, mem_o = [], [], []
    for h in range(MEM_HEADS):
        q = z_ref[:, O_CQ + h * HEAD_DIM:O_CQ + (h + 1) * HEAD_DIM].astype(BF16)
        mem_s.append(_dot_nt(q, mk_ref[:, h * HEAD_DIM:(h + 1) * HEAD_DIM]))
        between()
    for h in range(MEM_HEADS):
        s = mem_s[h] * scale
        m = jnp.max(s, axis=-1, keepdims=True)
        p = jnp.exp(s - m)
        mem_l.append(jnp.sum(p, axis=-1, keepdims=True))
        mem_o.append(_dot(p.astype(BF16), mv_ref[:, h * HEAD_DIM:(h + 1) * HEAD_DIM]))
        between()
    for h in range(MEM_HEADS):
        o = mem_o[h] * (1.0 / mem_l[h])
        gate = _silu(z_ref[:, O_CG + h * HEAD_DIM:O_CG + (h + 1) * HEAD_DIM])
        col = SWA_W + ML_W + h * HEAD_DIM
        y_ref[:, col:col + HEAD_DIM] = (o * gate).astype(BF16)

    lane = lax.broadcasted_iota(jnp.int32, (1, GATE_LANES), 1)
    zg = zg_ref[...]
    gates = jnp.where(lane < LANE_F, zg, _log_sigmoid(zg))
    gates = jnp.where(lane < 2 * ML_HEADS, gates, 0.0)
    rr = lax.broadcasted_iota(jnp.int32, (L, L), 0)
    cc = lax.broadcasted_iota(jnp.int32, (L, L), 1)
    causal = rr >= cc
    tri = jnp.where(causal, 1.0, 0.0).astype(BF16)
    csum = _exact_sel_dot(tri, gates)
    gates_t = gates.T
    csum_t = csum.T
    half_lo = lane < ML_DQK
    between()
    heads = []
    for h in range(ML_HEADS):
        pair, half = divmod(h, 2)
        hd = dict(pair=pair, half=half)
        hmask = half_lo if half == 0 else jnp.logical_not(half_lo)
        q_pair = z_ref[:, O_MQ + pair * 128:O_MQ + (pair + 1) * 128]
        hd["k_pair"] = z_ref[:, O_MK + pair * 128:O_MK + (pair + 1) * 128] * (ML_DQK ** -0.5)
        hd["v"] = z_ref[:, O_MV + h * ML_DV:O_MV + (h + 1) * ML_DV].astype(BF16)
        hd["q"] = jnp.where(hmask, q_pair, 0.0)
        q_hb = hd["q"].astype(BF16)
        hd["bt_c"] = csum[:, LANE_F + h:LANE_F + h + 1]
        hd["it_c"] = gates[:, LANE_I + h:LANE_I + h + 1]
        hd["m_prev"] = m_ref[:, h:h + 1]
        ct_pair = ct_ref[pair * 128:(pair + 1) * 128, :]
        hd["qk"] = _dot_nt(q_hb, hd["k_pair"].astype(BF16))
        hd["state_read"] = _dot(q_hb, ct_pair.astype(BF16))
        heads.append(hd)
        between()
    for h, hd in enumerate(heads):
        bt_r = csum_t[LANE_F + h:LANE_F + h + 1, :]
        it_r = gates_t[LANE_I + h:LANE_I + h + 1, :]
        dlog = jnp.where(causal, hd["bt_c"] - bt_r + it_r, -jnp.inf)
        inter = hd["bt_c"] + hd["m_prev"]
        hd["m_t"] = jnp.maximum(inter, jnp.max(dlog, axis=-1, keepdims=True))
        hd["w_state"] = jnp.exp(inter - hd["m_t"])
        a = jnp.exp(dlog - hd["m_t"]) * hd["qk"]
        hd["a_sum"] = jnp.sum(a, axis=-1, keepdims=True)
        hd["intra"] = _dot(a.astype(BF16), hd["v"])
        between()
    for h, hd in enumerate(heads):
        num = hd["intra"] + hd["w_state"] * hd["state_read"]
        n_pair = n_ref[:, hd["pair"] * 128:(hd["pair"] + 1) * 128]
        den = hd["a_sum"] + hd["w_state"] * jnp.sum(hd["q"] * n_pair, axis=-1, keepdims=True)
        denom = jnp.maximum(jnp.abs(den), jnp.exp(-hd["m_t"]))
        hid = num * (1.0 / denom)
        o_gate = _sigmoid(z_ref[:, O_MO + h * ML_DV:O_MO + (h + 1) * ML_DV])
        gate = _silu(z_ref[:, O_MG + h * ML_DV:O_MG + (h + 1) * ML_DV])
        col = SWA_W + h * ML_DV
        y_ref[:, col:col + ML_DV] = ((hid * o_gate) * gate).astype(BF16)

        hd["m_new"] = hd["m_t"][L - 1:L, :]
        bt_last = hd["bt_c"][L - 1:L, :]
        w_s = jnp.exp(bt_last - hd["bt_c"] + hd["it_c"] - hd["m_new"])
        hd["decay"] = jnp.exp(bt_last + hd["m_prev"] - hd["m_new"])
        kw = hd["k_pair"] * w_s
        hd["ksum"] = jnp.sum(kw, axis=0, keepdims=True)
        hd["upd"] = _dot(kw.T.astype(BF16), hd["v"])
        between()
    for h, hd in enumerate(heads):
        half, decay = hd["half"], hd["decay"]
        rows = slice(h * ML_DQK, (h + 1) * ML_DQK)
        ct_ref[rows, :] = decay * ct_ref[rows, :] + hd["upd"][half * ML_DQK:(half + 1) * ML_DQK, :]
        n_ref[:, rows] = decay * n_ref[:, rows] + hd["ksum"][:, half * ML_DQK:(half + 1) * ML_DQK]
        m_ref[:, h:h + 1] = hd["m_new"]


def _prompt(sinks, x2d, g_norm, w, b, wg, bg, rope_c, rope_i, mk, mv, w_out, g_final):
    n = x2d.shape[0]
    nc = n // CHUNK
    const = lambda c, s: (0, 0)
    resident = pl.Buffered(1)
    assert nc % 2 == 0
    n_pairs = nc // 2
    x_in = lambda p, s: (jnp.minimum(p, n_pairs - 1), 0)
    x_res = lambda p, s: (jnp.clip(p - 1, 0, n_pairs - 1), 0)
    return pl.pallas_call(
        _prompt_kernel,
        grid_spec=pltpu.PrefetchScalarGridSpec(
            num_scalar_prefetch=1,
            grid=(n_pairs + 1,),
            in_specs=[
                pl.BlockSpec((2 * CHUNK, D_MODEL), x_in),
                pl.BlockSpec((2 * CHUNK, D_MODEL), x_res),
                pl.BlockSpec((1, D_MODEL), const, pipeline_mode=resident),
                pl.BlockSpec((D_MODEL, D_MAIN), const, pipeline_mode=resident),
                pl.BlockSpec((1, D_MAIN), const, pipeline_mode=resident),
                pl.BlockSpec((D_MODEL, GATE_LANES), const, pipeline_mode=resident),
                pl.BlockSpec((1, GATE_LANES), const, pipeline_mode=resident),
                pl.BlockSpec((2, nc, HEAD_DIM), lambda c, s: (0, 0, 0), pipeline_mode=resident),
                pl.BlockSpec((4, CHUNK, HEAD_DIM), lambda c, s: (0, 0, 0), pipeline_mode=resident),
                pl.BlockSpec((MEM_LEN, MEM_W), const, pipeline_mode=resident),
                pl.BlockSpec((MEM_LEN, MEM_W), const, pipeline_mode=resident),
                pl.BlockSpec((D_MIX, D_MODEL), const, pipeline_mode=resident),
                pl.BlockSpec((1, D_MODEL), const, pipeline_mode=resident),
            ],
            out_specs=[
                pl.BlockSpec((2 * CHUNK, D_MODEL), x_res),
                pl.BlockSpec((CHUNK, SWA_KV_W), const),
                pl.BlockSpec((CHUNK, SWA_KV_W), const),
                pl.BlockSpec((ML_QK_W, ML_DV), const),
                pl.BlockSpec((1, ML_QK_W), const),
                pl.BlockSpec((1, GATE_LANES), const),
            ],
            scratch_shapes=[
                pltpu.VMEM((CHUNK, SWA_KV_W), BF16),
                pltpu.VMEM((CHUNK, SWA_KV_W), BF16),
                pltpu.VMEM((CHUNK, D_MODEL), BF16),
                pltpu.VMEM((CHUNK, D_MAIN), F32),
                pltpu.VMEM((CHUNK, GATE_LANES), F32),
                pltpu.VMEM((CHUNK, D_MAIN), F32),
                pltpu.VMEM((CHUNK, GATE_LANES), F32),
                pltpu.VMEM((CHUNK, D_MIX), BF16),
                pltpu.VMEM((CHUNK, D_MIX), BF16),
            ],
        ),
        out_shape=[
            jax.ShapeDtypeStruct((n, D_MODEL), F32),
            jax.ShapeDtypeStruct((CHUNK, SWA_KV_W), F32),
            jax.ShapeDtypeStruct((CHUNK, SWA_KV_W), F32),
            jax.ShapeDtypeStruct((ML_QK_W, ML_DV), F32),
            jax.ShapeDtypeStruct((1, ML_QK_W), F32),
            jax.ShapeDtypeStruct((1, GATE_LANES), F32),
        ],
        compiler_params=pltpu.CompilerParams(
            dimension_semantics=("arbitrary",), vmem_limit_bytes=VMEM_LIMIT),
        name="prompt",
    )(sinks, x2d, x2d, g_norm, w, b, wg, bg, rope_c, rope_i, mk, mv, w_out, g_final)


OUT_TM = 512


def _outproj_kernel(y_ref, x_ref, w_ref, g_ref, o_ref):
    acc = _dot(y_ref[...], w_ref[...]) + x_ref[...]
    ms = jnp.mean(acc * acc, axis=-1, keepdims=True)
    o_ref[...] = (acc * lax.rsqrt(ms + EPS)) * g_ref[...]


def _outproj(y, x2d, w, g):
    n = x2d.shape[0]
    return pl.pallas_call(
        _outproj_kernel,
        grid=(n // OUT_TM,),
        in_specs=[
            pl.BlockSpec((OUT_TM, D_MIX), lambda i: (i, 0)),
            pl.BlockSpec((OUT_TM, D_MODEL), lambda i: (i, 0)),
            pl.BlockSpec((D_MIX, D_MODEL), lambda i: (0, 0)),
            pl.BlockSpec((1, D_MODEL), lambda i: (0, 0)),
        ],
        out_specs=pl.BlockSpec((OUT_TM, D_MODEL), lambda i: (i, 0)),
        out_shape=jax.ShapeDtypeStruct((n, D_MODEL), F32),
        compiler_params=pltpu.CompilerParams(
            dimension_semantics=("arbitrary",), vmem_limit_bytes=VMEM_LIMIT),
        name="outproj",
    )(y, x2d, w, g)


SB = 16
SR = SB * DEC_SEQ
SEQ_UNROLL = 4
SWA_UNROLL = 8


def _mix_sample_kernel(sinks_ref, z_ref, zg_ref, cos_ref, sin_ref, ck_ref, cv_ref,
                       c_ref, nst_ref, nrep_ref, mrep_ref,
                       y_ref, ko_ref, vo_ref, co_ref, no_ref, mo_ref,
                       q_s, k_s, o_s):
    T = DEC_SEQ
    scale = HEAD_DIM ** -0.5
    cos = cos_ref[...]
    sin = sin_ref[...]

    for h in range(SWA_HEADS):
        q_s[:, h * HEAD_DIM:(h + 1) * HEAD_DIM] = _rope(
            z_ref[:, O_SQ + h * HEAD_DIM:O_SQ + (h + 1) * HEAD_DIM], cos, sin)
    for kv in range(SWA_KV_HEADS):
        k_s[:, kv * HEAD_DIM:(kv + 1) * HEAD_DIM] = _rope(
            z_ref[:, O_SK + kv * HEAD_DIM:O_SK + (kv + 1) * HEAD_DIM], cos, sin)

    KP = 2 * WINDOW
    rt = lax.broadcasted_iota(jnp.int32, (GQA_GROUP * T, KP), 0) & (T - 1)
    cj = lax.broadcasted_iota(jnp.int32, (GQA_GROUP * T, KP), 1)
    mask = (cj > rt) & (cj <= rt + WINDOW)
    kpad = jnp.zeros((KP - WINDOW - T, HEAD_DIM), F32)
    NKV = SWA_KV_HEADS
    SEQ_ROWS = WINDOW * NKV

    keep = SEQ_ROWS - T * NKV
    sink_cols = [jnp.concatenate([jnp.full((T, 1), sinks_ref[kv * GQA_GROUP + g], F32)
                                  for g in range(GQA_GROUP)], axis=0) for kv in range(NKV)]

    def per_group(i, carry):
        chains = [(i * SWA_UNROLL + j, kv) for j in range(SWA_UNROLL) for kv in range(NKV)]
        scores, values = [], []
        for b, kv in chains:
            rows = pl.ds(pl.multiple_of(b * T, T), T)
            base = pl.multiple_of(b * SEQ_ROWS, SEQ_ROWS)
            cols = slice(kv * HEAD_DIM, (kv + 1) * HEAD_DIM)
            k_new = k_s[rows, cols]
            v_new = z_ref[rows, O_SV + kv * HEAD_DIM:O_SV + (kv + 1) * HEAD_DIM]
            if kv == 0:
                ko_ref[pl.ds(base, keep), :] = ck_ref[pl.ds(base + T * NKV, keep), :]
                vo_ref[pl.ds(base, keep), :] = cv_ref[pl.ds(base + T * NKV, keep), :]
            ko_ref[pl.ds(base + keep + kv, T, stride=NKV), :] = k_new
            vo_ref[pl.ds(base + keep + kv, T, stride=NKV), :] = v_new
            kc = ck_ref[pl.ds(base + kv, WINDOW, stride=NKV), :]
            vc = cv_ref[pl.ds(base + kv, WINDOW, stride=NKV), :]
            k_all = jnp.concatenate([kc, k_new, kpad], axis=0).astype(BF16)
            values.append(jnp.concatenate([vc, v_new, kpad], axis=0).astype(BF16))
            q_st = jnp.concatenate([q_s[rows, (kv * GQA_GROUP + g) * HEAD_DIM:(kv * GQA_GROUP + g + 1) * HEAD_DIM]
                                    for g in range(GQA_GROUP)], axis=0).astype(BF16)
            scores.append(_dot_nt(q_st, k_all))
        probs, norms = [], []
        for (b, kv), s in zip(chains, scores):
            s = jnp.where(mask, s * scale, -jnp.inf)
            m = jnp.maximum(jnp.max(s, axis=-1, keepdims=True), sink_cols[kv])
            p = jnp.exp(s - m)
            norms.append(jnp.sum(p, axis=-1, keepdims=True) + jnp.exp(sink_cols[kv] - m))
            probs.append(p.astype(BF16))
        outs = [_dot(p, v) for p, v in zip(probs, values)]
        for (b, kv), o, l in zip(chains, outs, norms):
            rows = pl.ds(pl.multiple_of(b * T, T), T)
            o = o * (1.0 / l)
            for g in range(GQA_GROUP):
                h = kv * GQA_GROUP + g
                o_s[rows, h * HEAD_DIM:(h + 1) * HEAD_DIM] = o[g * T:(g + 1) * T]
        return carry

    lax.fori_loop(0, SB // SWA_UNROLL, per_group, 0)

    for h in range(SWA_HEADS):
        cols = slice(h * HEAD_DIM, (h + 1) * HEAD_DIM)
        gate = _silu(z_ref[:, O_SG + h * HEAD_DIM:O_SG + (h + 1) * HEAD_DIM])
        y_ref[:, cols] = (o_s[:, cols] * gate).astype(BF16)
    y_ref[:, SWA_W + ML_W:] = jnp.zeros((SR, MEM_W), BF16)

    R = SR
    lane = lax.broadcasted_iota(jnp.int32, (1, GATE_LANES), 1)
    zg = zg_ref[...]
    gates = jnp.where(lane < LANE_F, zg, _log_sigmoid(zg))
    gates = jnp.where(lane < 2 * ML_HEADS, gates, 0.0)
    rr = lax.broadcasted_iota(jnp.int32, (R, R), 0)
    cc = lax.broadcasted_iota(jnp.int32, (R, R), 1)
    same_seq = (rr >> 3) == (cc >> 3)
    causal = same_seq & (rr >= cc)
    tri = jnp.where(causal, 1.0, 0.0).astype(BF16)
    csum = _exact_sel_dot(tri, gates)
    gates_t = gates.T
    csum_t = csum.T
    half_lo = lane < ML_DQK
    seq_of_col = lax.broadcasted_iota(jnp.int32, (SB, 1, R), 2) >> 3
    seq_id = lax.broadcasted_iota(jnp.int32, (SB, 1, R), 0)
    own_cols = seq_of_col == seq_id
    own_blk = ((lax.broadcasted_iota(jnp.int32, (R, SB * 128), 0) >> 3)
               == (lax.broadcasted_iota(jnp.int32, (R, SB * 128), 1) >> 7))
    k_t = [(z_ref[:, O_MK + p * 128:O_MK + (p + 1) * 128] * (ML_DQK ** -0.5)).T for p in range(2)]
    mrep = mrep_ref[...]
    tok3 = lax.broadcasted_iota(jnp.int32, (SB, T, 1), 1)
    mo_ref[...] = jnp.zeros_like(mo_ref)

    def per_seq_value(col):
        return jnp.max(col.reshape(SB, T, 1), axis=1, keepdims=True)

    def last_of_seq(col):
        c3 = jnp.where(tok3 == T - 1, col.reshape(SB, T, 1), -jnp.inf)
        return jnp.broadcast_to(jnp.max(c3, axis=1, keepdims=True), (SB, T, 1)).reshape(R, 1)

    for h in range(ML_HEADS):
        pair, half = divmod(h, 2)
        hmask = half_lo if half == 0 else jnp.logical_not(half_lo)
        q_pair = z_ref[:, O_MQ + pair * 128:O_MQ + (pair + 1) * 128]
        k_pair = z_ref[:, O_MK + pair * 128:O_MK + (pair + 1) * 128] * (ML_DQK ** -0.5)
        v_f = z_ref[:, O_MV + h * ML_DV:O_MV + (h + 1) * ML_DV]
        v_h = v_f.astype(BF16)
        q_h = jnp.where(hmask, q_pair, 0.0)
        q_hb = q_h.astype(BF16)
        k_pb = k_pair.astype(BF16)

        bt_c = csum[:, LANE_F + h:LANE_F + h + 1]
        it_c = gates[:, LANE_I + h:LANE_I + h + 1]
        bt_r = csum_t[LANE_F + h:LANE_F + h + 1, :]
        it_r = gates_t[LANE_I + h:LANE_I + h + 1, :]
        m_prev = mrep[:, h:h + 1]

        dlog = jnp.where(causal, bt_c - bt_r + it_r, -jnp.inf)
        inter = bt_c + m_prev
        m_t = jnp.maximum(inter, jnp.max(dlog, axis=-1, keepdims=True))
        w_intra = jnp.exp(dlog - m_t)
        w_state = jnp.exp(inter - m_t)
        a = w_intra * _dot_nt(q_hb, k_pb)

        ct_st = c_ref[:, 2 * pair:2 * pair + 2].reshape(SB * 128, ML_DV).astype(BF16)
        q_blk = jnp.where(own_blk, jnp.tile(q_h, (1, SB)), 0.0).astype(BF16)
        num_state = _dot(q_blk, ct_st)

        num = _dot(a.astype(BF16), v_h) + w_state * num_state
        n_pair = nrep_ref[:, pair * 128:(pair + 1) * 128]
        den = jnp.sum(a, axis=-1, keepdims=True) + w_state * jnp.sum(q_h * n_pair, axis=-1, keepdims=True)
        denom = jnp.maximum(jnp.abs(den), jnp.exp(-m_t))
        hid = num * (1.0 / denom)
        o_gate = _sigmoid(z_ref[:, O_MO + h * ML_DV:O_MO + (h + 1) * ML_DV])
        gate = _silu(z_ref[:, O_MG + h * ML_DV:O_MG + (h + 1) * ML_DV])
        col = SWA_W + h * ML_DV
        y_ref[:, col:col + ML_DV] = ((hid * o_gate) * gate).astype(BF16)

        m_new = last_of_seq(m_t)
        bt_last = last_of_seq(bt_c)
        w_s = jnp.exp(bt_last - bt_c + it_c - m_new)
        decay = jnp.exp(bt_last + m_prev - m_new)
        decay_seq = per_seq_value(decay)

        kt_h = k_t[pair][half * ML_DQK:(half + 1) * ML_DQK, :]
        lhs = jnp.where(own_cols, kt_h[None, :, :], 0.0).reshape(SB * ML_DQK, R).astype(BF16)
        upd = _dot(lhs, (v_f * w_s).astype(BF16)).reshape(SB, ML_DQK, ML_DV)
        co_ref[:, h] = decay_seq * c_ref[:, h] + upd

        kw = (k_pair * w_s).reshape(SB, T, 128)
        ksum = jnp.sum(kw, axis=1)
        n_old = nst_ref[:, h * ML_DQK:(h + 1) * ML_DQK]
        dec2 = decay_seq.reshape(SB, 1)
        no_ref[:, h * ML_DQK:(h + 1) * ML_DQK] = dec2 * n_old + ksum[:, half * ML_DQK:(half + 1) * ML_DQK]
        mo_ref[:, h:h + 1] = per_seq_value(m_new).reshape(SB, 1)


def _mix_sample(sinks, z, zg, cos, sin, ck, cv, cst, nst, nrep, mrep):
    nb = cst.shape[0]
    steps = nb // SB
    const = lambda i, s: (0, 0)
    cache_rows = SB * WINDOW * SWA_KV_HEADS
    return pl.pallas_call(
        _mix_sample_kernel,
        grid_spec=pltpu.PrefetchScalarGridSpec(
            num_scalar_prefetch=1,
            grid=(steps,),
            in_specs=[
                pl.BlockSpec((SR, D_MAIN), lambda i, s: (i, 0)),
                pl.BlockSpec((SR, GATE_LANES), lambda i, s: (i, 0)),
                pl.BlockSpec((SR, HEAD_DIM), const),
                pl.BlockSpec((SR, HEAD_DIM), const),
                pl.BlockSpec((cache_rows, HEAD_DIM), lambda i, s: (i, 0)),
                pl.BlockSpec((cache_rows, HEAD_DIM), lambda i, s: (i, 0)),
                pl.BlockSpec((SB, ML_HEADS, ML_DQK, ML_DV), lambda i, s: (i, 0, 0, 0)),
                pl.BlockSpec((SB, ML_QK_W), lambda i, s: (i, 0)),
                pl.BlockSpec((SR, ML_QK_W), lambda i, s: (i, 0)),
                pl.BlockSpec((SR, GATE_LANES), lambda i, s: (i, 0)),
            ],
            out_specs=[
                pl.BlockSpec((SR, D_MIX), lambda i, s: (i, 0)),
                pl.BlockSpec((cache_rows, HEAD_DIM), lambda i, s: (i, 0)),
                pl.BlockSpec((cache_rows, HEAD_DIM), lambda i, s: (i, 0)),
                pl.BlockSpec((SB, ML_HEADS, ML_DQK, ML_DV), lambda i, s: (i, 0, 0, 0)),
                pl.BlockSpec((SB, ML_QK_W), lambda i, s: (i, 0)),
                pl.BlockSpec((SB, GATE_LANES), lambda i, s: (i, 0)),
            ],
            scratch_shapes=[
                pltpu.VMEM((SR, SWA_W), F32),
                pltpu.VMEM((SR, SWA_KV_W), F32),
                pltpu.VMEM((SR, SWA_W), F32),
            ],
        ),
        out_shape=[
            jax.ShapeDtypeStruct((nb * DEC_SEQ, D_MIX), BF16),
            jax.ShapeDtypeStruct(ck.shape, F32),
            jax.ShapeDtypeStruct(cv.shape, F32),
            jax.ShapeDtypeStruct((nb, ML_HEADS, ML_DQK, ML_DV), F32),
            jax.ShapeDtypeStruct((nb, ML_QK_W), F32),
            jax.ShapeDtypeStruct((nb, GATE_LANES), F32),
        ],
        compiler_params=pltpu.CompilerParams(
            dimension_semantics=("arbitrary",), vmem_limit_bytes=VMEM_LIMIT),
        name="mix_sample",
    )(sinks, z, zg, cos, sin, ck, cv, cst, nst, nrep, mrep)


MB = 8
MR = MB * DEC_SEQ


def _mem_sample_kernel(cq_ref, cg_ref, mk_ref, mv_ref, y_in_ref, y_ref, o_s):
    del y_in_ref
    T = DEC_SEQ
    scale = HEAD_DIM ** -0.5
    zpad = jnp.zeros((T, HEAD_DIM), F32)

    seq_rows = MEM_LEN * MEM_HEADS

    def per_group(i, carry):
        chains = [(i * SEQ_UNROLL + j, h) for j in range(SEQ_UNROLL) for h in range(MEM_HEADS)]
        scores = []
        for b, h in chains:
            rows = pl.ds(pl.multiple_of(b * T, T), T)
            base = pl.multiple_of(b * seq_rows, seq_rows)
            kb = mk_ref[pl.ds(base + h, MEM_LEN, stride=MEM_HEADS), :].astype(BF16)
            q = jnp.concatenate([cq_ref[rows, h * HEAD_DIM:(h + 1) * HEAD_DIM], zpad], axis=0).astype(BF16)
            scores.append(_dot_nt(q, kb))
        probs, norms = [], []
        for s in scores:
            s = s * scale
            p = jnp.exp(s - jnp.max(s, axis=-1, keepdims=True))
            norms.append(jnp.sum(p, axis=-1, keepdims=True))
            probs.append(p.astype(BF16))
        outs = []
        for (b, h), p in zip(chains, probs):
            base = pl.multiple_of(b * seq_rows, seq_rows)
            vb = mv_ref[pl.ds(base + h, MEM_LEN, stride=MEM_HEADS), :].astype(BF16)
            outs.append(_dot(p, vb))
        for (b, h), o, l in zip(chains, outs, norms):
            rows = pl.ds(pl.multiple_of(b * T, T), T)
            o_s[rows, h * HEAD_DIM:(h + 1) * HEAD_DIM] = (o * (1.0 / l))[0:T]
        return carry

    lax.fori_loop(0, MB // SEQ_UNROLL, per_group, 0)
    y_ref[...] = (o_s[...] * _silu(cg_ref[...])).astype(BF16)


def _mem_sample(z, mk, mv, y):
    nb = z.shape[0] // DEC_SEQ
    cache_rows = MB * MEM_LEN * MEM_HEADS
    return pl.pallas_call(
        _mem_sample_kernel,
        grid=(nb // MB,),
        in_specs=[
            pl.BlockSpec((MR, MEM_W), lambda i: (i, O_CQ // MEM_W)),
            pl.BlockSpec((MR, MEM_W), lambda i: (i, O_CG // MEM_W)),
            pl.BlockSpec((cache_rows, HEAD_DIM), lambda i: (i, 0)),
            pl.BlockSpec((cache_rows, HEAD_DIM), lambda i: (i, 0)),
            pl.BlockSpec(memory_space=pl.ANY),
        ],
        out_specs=pl.BlockSpec((MR, MEM_W), lambda i: (i, (SWA_W + ML_W) // MEM_W)),
        out_shape=jax.ShapeDtypeStruct(y.shape, y.dtype),
        input_output_aliases={4: 0},
        scratch_shapes=[pltpu.VMEM((MR, MEM_W), F32)],
        compiler_params=pltpu.CompilerParams(
            dimension_semantics=("arbitrary",), vmem_limit_bytes=VMEM_LIMIT),
        name="mem_sample",
    )(z, z, mk, mv, y)


def _rope_cos_sin(pos):
    half = HEAD_DIM // 2
    inv = jnp.power(ROPE_THETA, -(jnp.arange(half, dtype=F32) * 2.0 / HEAD_DIM))
    ang = pos.astype(F32)[:, None] * inv[None, :]
    return jnp.cos(ang), jnp.sin(ang)


def _rope_tables(pos):
    cos, sin = _rope_cos_sin(pos)
    return jnp.concatenate([cos, cos], axis=-1), jnp.concatenate([-sin, sin], axis=-1)


def _rope_split_tables(n_chunks):
    ca, sa = _rope_cos_sin(jnp.arange(n_chunks, dtype=jnp.int32) * CHUNK)
    cb, sb = _rope_cos_sin(jnp.arange(CHUNK, dtype=jnp.int32))
    dup = lambda t: jnp.concatenate([t, t], axis=-1)
    sgn = lambda t: jnp.concatenate([-t, t], axis=-1)
    return jnp.stack([dup(ca), dup(sa)]), jnp.stack([dup(cb), dup(sb), sgn(cb), sgn(sb)])


def _relayout_in_proj(w_in, b_in):
    w_t = jnp.swapaxes(w_in, 0, 1)
    w_main = _wprep(w_t)
    w_gate = _wgate(w_t)
    b_main = jnp.concatenate([b_in[_R_SQ:_R_MI], b_in[_R_MO:_R_END]])[None, :]
    b_gate = jnp.pad(b_in[_R_MI:_R_MO], (0, GATE_LANES - 2 * ML_HEADS))[None, :]
    return w_main, b_main, w_gate, b_gate


def _layer(xp, xs, mem, ck, cv, c_st, n_st, m_st, cmk, cmv,
           g_norm, w_in, b_in, sinks, g_mem, w_mem_kv, w_out, g_final):
    bp, sp, _ = xp.shape
    bs, ts, _ = xs.shape
    xp2 = xp.reshape(bp * sp, D_MODEL)
    xs2 = xs.reshape(bs * ts, D_MODEL)
    w_main, b_main, w_gate, b_gate = _relayout_in_proj(w_in, b_in)
    g_norm2 = g_norm[None, :]
    w_out_b = w_out.astype(BF16)
    g_final2 = g_final[None, :]
    sinks_flat = sinks.reshape(SWA_HEADS)

    memkv = _memkv(mem.reshape(MEM_LEN, D_MODEL), g_mem[None, :], w_mem_kv.astype(BF16))
    mem_k = memkv[:, :MEM_W]
    mem_v = memkv[:, MEM_W:]
    rope_c, rope_i = _rope_split_tables(sp // CHUNK)
    out_p, pk, pv, ct, n_p, m_p = _prompt(sinks_flat, xp2, g_norm2, w_main, b_main, w_gate, b_gate,
                                           rope_c, rope_i, mem_k.astype(BF16), mem_v.astype(BF16),
                                           w_out_b, g_final2)

    zs, zgs = _proj(xs2, g_norm2, w_main, b_main, w_gate, b_gate)
    cos_s, sin_s = _rope_tables(PAST_LEN + jnp.arange(ts, dtype=jnp.int32))
    cos_s = jnp.tile(cos_s, (SB, 1))
    sin_s = jnp.tile(sin_s, (SB, 1))
    nrep = jnp.repeat(n_st.reshape(bs, ML_QK_W), ts, axis=0)
    m_pad = jnp.pad(m_st, ((0, 0), (0, GATE_LANES - ML_HEADS)))
    mrep = jnp.repeat(m_pad, ts, axis=0)
    ya, sk_o, sv_o, ct_o, n_o, m_o = _mix_sample(
        sinks_flat, zs, zgs, cos_s, sin_s,
        ck.reshape(bs * WINDOW * SWA_KV_HEADS, HEAD_DIM), cv.reshape(bs * WINDOW * SWA_KV_HEADS, HEAD_DIM),
        jnp.swapaxes(c_st, -1, -2), n_st.reshape(bs, ML_QK_W), nrep, mrep)
    ys = _mem_sample(zs, cmk.reshape(bs * MEM_LEN * MEM_HEADS, HEAD_DIM),
                     cmv.reshape(bs * MEM_LEN * MEM_HEADS, HEAD_DIM), ya)
    out_s = _outproj(ys, xs2, w_out_b, g_final2)
    c_o = jnp.swapaxes(ct_o, -1, -2)

    p_state = (
        pk.reshape(bp, WINDOW, SWA_KV_HEADS, HEAD_DIM),
        pv.reshape(bp, WINDOW, SWA_KV_HEADS, HEAD_DIM),
        ct.reshape(ML_HEADS, ML_DQK, ML_DV).transpose(0, 2, 1)[None],
        n_p.reshape(bp, ML_HEADS, ML_DQK),
        m_p[:, :ML_HEADS],
        mem_k.reshape(bp, MEM_LEN, MEM_HEADS, HEAD_DIM),
        mem_v.reshape(bp, MEM_LEN, MEM_HEADS, HEAD_DIM),
    )
    s_state = (
        sk_o.reshape(bs, WINDOW, SWA_KV_HEADS, HEAD_DIM),
        sv_o.reshape(bs, WINDOW, SWA_KV_HEADS, HEAD_DIM),
        c_o,
        n_o.reshape(bs, ML_HEADS, ML_DQK),
        m_o[:, :ML_HEADS],
    )
    return out_p.reshape(bp, sp, D_MODEL), out_s.reshape(bs, ts, D_MODEL), p_state, s_state


def kernel(x_prompt, x_sample, mem_prompt, cache_swa_k, cache_swa_v, state_mlstm_C, state_mlstm_n,
           state_mlstm_m, cache_mem_k, cache_mem_v, g_norm, w_in, b_in, swa_sinks, g_mem, w_mem_kv,
           w_out, g_final):
    depth = g_norm.shape[0]
    assert depth == 1 and x_prompt.shape[0] == 1
    y_p, y_s, p_state, s_state = _layer(
        x_prompt, x_sample, mem_prompt, cache_swa_k[0], cache_swa_v[0], state_mlstm_C[0],
        state_mlstm_n[0], state_mlstm_m[0], cache_mem_k[0], cache_mem_v[0],
        g_norm[0], w_in[0], b_in[0], swa_sinks[0], g_mem[0], w_mem_kv[0], w_out[0], g_final)
    return (y_p, y_s) + tuple(s[None] for s in p_state) + tuple(s[None] for s in s_state)
```

```python
import functools

import jax
import jax.numpy as jnp
from jax import lax
from jax.experimental import pallas as pl
from jax.experimental.pallas import tpu as pltpu

F32 = jnp.float32
BF16 = jnp.bfloat16

D_MODEL = 2048
HEAD_DIM = 128
SWA_HEADS = 8
SWA_KV_HEADS = 2
GQA_GROUP = 4
WINDOW = 128
ML_HEADS = 4
ML_DQK = 64
ML_DV = 128
MEM_HEADS = 4
MEM_LEN = 256
CHUNK = 128
ROPE_THETA = 10000.0
EPS = 1e-6
PAST_LEN = 16384
DEC_SEQ = 8

SWA_W = SWA_HEADS * HEAD_DIM
SWA_KV_W = SWA_KV_HEADS * HEAD_DIM
ML_W = ML_HEADS * ML_DV
ML_QK_W = ML_HEADS * ML_DQK
MEM_W = MEM_HEADS * HEAD_DIM
D_MIX = SWA_W + ML_W + MEM_W

_IN_WIDTHS = (SWA_W, SWA_KV_W, SWA_KV_W, SWA_W, ML_QK_W, ML_QK_W, ML_W, ML_HEADS, ML_HEADS, ML_W, ML_W, MEM_W, MEM_W)
_IN_OFFS = [0]
for _w in _IN_WIDTHS:
    _IN_OFFS.append(_IN_OFFS[-1] + _w)
(_R_SQ, _R_SK, _R_SV, _R_SG, _R_MQ, _R_MK, _R_MV, _R_MI, _R_MF, _R_MO, _R_MG, _R_CQ, _R_CG, _R_END) = _IN_OFFS

O_SQ = 0
O_SK = O_SQ + SWA_W
O_SV = O_SK + SWA_KV_W
O_SG = O_SV + SWA_KV_W
O_MQ = O_SG + SWA_W
O_MK = O_MQ + ML_QK_W
O_MV = O_MK + ML_QK_W
O_MO = O_MV + ML_W
O_MG = O_MO + ML_W
O_CQ = O_MG + ML_W
O_CG = O_CQ + MEM_W
D_MAIN = O_CG + MEM_W
W_A_COLS = O_MO
GATE_LANES = 128
LANE_I = 0
LANE_F = ML_HEADS

VMEM_LIMIT = 58 * 1024 * 1024

_NT = (((1,), (1,)), ((), ()))


def _dot(a, b):
    return jnp.dot(a, b, preferred_element_type=F32)


def _dot_nt(a, b):
    return lax.dot_general(a, b, _NT, preferred_element_type=F32)


def _exact_sel_dot(sel_bf16, x):
    hi = x.astype(BF16)
    r1 = x - hi.astype(F32)
    mid = r1.astype(BF16)
    lo = (r1 - mid.astype(F32)).astype(BF16)
    return _dot(sel_bf16, hi) + _dot(sel_bf16, mid) + _dot(sel_bf16, lo)


def _silu(x):
    h = 0.5 * x
    return h + h * jnp.tanh(h)


def _sigmoid(x):
    return 0.5 + 0.5 * jnp.tanh(0.5 * x)


def _log_sigmoid(x):
    return jnp.minimum(x, 0.0) - jnp.log1p(jnp.exp(-jnp.abs(x)))


PROJ_TM = 1024
PROJ_TN = 512
NORM_ROWS = 256


def _proj_kernel(x_ref, g_ref, w_ref, b_ref, wg_ref, bg_ref, z_ref, zg_ref, u_ref):
    j = pl.program_id(1)

    @pl.when(j == 0)
    def _():
        g = g_ref[...]
        for r in range(PROJ_TM // NORM_ROWS):
            rows = pl.ds(r * NORM_ROWS, NORM_ROWS)
            xf = x_ref[rows, :]
            ms = jnp.mean(xf * xf, axis=-1, keepdims=True)
            u_ref[rows, :] = ((xf * lax.rsqrt(ms + EPS)) * g).astype(BF16)
        zg_ref[...] = _dot(u_ref[...], wg_ref[...]) + bg_ref[...]

    z_ref[...] = _dot(u_ref[...], w_ref[...]) + b_ref[...]


def _proj(x2d, g, w, b, wg, bg):
    n = x2d.shape[0]
    grid = (n // PROJ_TM, D_MAIN // PROJ_TN)
    return pl.pallas_call(
        _proj_kernel,
        grid=grid,
        in_specs=[
            pl.BlockSpec((PROJ_TM, D_MODEL), lambda i, j: (i, 0)),
            pl.BlockSpec((1, D_MODEL), lambda i, j: (0, 0)),
            pl.BlockSpec((D_MODEL, PROJ_TN), lambda i, j: (0, j)),
            pl.BlockSpec((1, PROJ_TN), lambda i, j: (0, j)),
            pl.BlockSpec((D_MODEL, GATE_LANES), lambda i, j: (0, 0)),
            pl.BlockSpec((1, GATE_LANES), lambda i, j: (0, 0)),
        ],
        out_specs=[
            pl.BlockSpec((PROJ_TM, PROJ_TN), lambda i, j: (i, j)),
            pl.BlockSpec((PROJ_TM, GATE_LANES), lambda i, j: (i, 0)),
        ],
        out_shape=[
            jax.ShapeDtypeStruct((n, D_MAIN), F32),
            jax.ShapeDtypeStruct((n, GATE_LANES), F32),
        ],
        scratch_shapes=[pltpu.VMEM((PROJ_TM, D_MODEL), BF16)],
        compiler_params=pltpu.CompilerParams(
            dimension_semantics=("arbitrary", "arbitrary"), vmem_limit_bytes=VMEM_LIMIT),
        name="proj",
    )(x2d, g, w, b, wg, bg)


WPREP_TN = 256


def _wprep_kernel(wt_ref, o_ref):
    o_ref[...] = wt_ref[...].T.astype(BF16)


def _wprep(w_t):
    tiles_a = W_A_COLS // WPREP_TN
    gate_rows = _R_MO - W_A_COLS

    def src_row(j):
        return pl.multiple_of(j * WPREP_TN + jnp.where(j >= tiles_a, gate_rows, 0), 8)

    return pl.pallas_call(
        _wprep_kernel,
        grid=(D_MAIN // WPREP_TN,),
        in_specs=[pl.BlockSpec((pl.Element(WPREP_TN), pl.Element(D_MODEL)), lambda j: (src_row(j), 0))],
        out_specs=pl.BlockSpec((D_MODEL, WPREP_TN), lambda j: (0, j)),
        out_shape=jax.ShapeDtypeStruct((D_MODEL, D_MAIN), BF16),
        compiler_params=pltpu.CompilerParams(
            dimension_semantics=("arbitrary",), vmem_limit_bytes=VMEM_LIMIT),
        name="wprep",
    )(w_t)


def _wgate_kernel(wt_ref, o_ref):
    rows = jnp.concatenate(
        [wt_ref[...], jnp.zeros((GATE_LANES - 2 * ML_HEADS, D_MODEL), F32)], axis=0)
    o_ref[...] = rows.T.astype(BF16)


def _wgate(w_t):
    return pl.pallas_call(
        _wgate_kernel,
        grid=(1,),
        in_specs=[pl.BlockSpec((pl.Element(2 * ML_HEADS), pl.Element(D_MODEL)), lambda j: (_R_MI, 0))],
        out_specs=pl.BlockSpec((D_MODEL, GATE_LANES), lambda j: (0, 0)),
        out_shape=jax.ShapeDtypeStruct((D_MODEL, GATE_LANES), BF16),
        name="wgate",
    )(w_t)


MEMKV_TN = 256


def _memkv_kernel(mem_ref, g_ref, w_ref, o_ref):
    xf = mem_ref[...]
    ms = jnp.mean(xf * xf, axis=-1, keepdims=True)
    u = ((xf * lax.rsqrt(ms + EPS)) * g_ref[...]).astype(BF16)
    o_ref[...] = _dot(u, w_ref[...])


def _memkv(mem2d, g, w):
    return pl.pallas_call(
        _memkv_kernel,
        grid=(2 * MEM_W // MEMKV_TN,),
        in_specs=[
            pl.BlockSpec((MEM_LEN, D_MODEL), lambda j: (0, 0)),
            pl.BlockSpec((1, D_MODEL), lambda j: (0, 0)),
            pl.BlockSpec((D_MODEL, MEMKV_TN), lambda j: (0, j)),
        ],
        out_specs=pl.BlockSpec((MEM_LEN, MEMKV_TN), lambda j: (0, j)),
        out_shape=jax.ShapeDtypeStruct((MEM_LEN, 2 * MEM_W), F32),
        compiler_params=pltpu.CompilerParams(
            dimension_semantics=("arbitrary",), vmem_limit_bytes=VMEM_LIMIT),
        name="memkv",
    )(mem2d, g, w)


OUT_TN = 256


def _rope(x, cos, sin_signed):
    return x * cos + pltpu.roll(x, HEAD_DIM // 2, axis=1) * sin_signed


IN_TN = 256
SWA_POINTS = 3 * SWA_KV_HEADS
MEM_POINTS = 2 * MEM_HEADS
MIXER_WEIGHTS = ([1.0, 3.0, 1.0] * SWA_KV_HEADS + [1.0] * MEM_HEADS + [0.5] * MEM_HEADS
                 + [1.0] + [1.5] * ML_HEADS + [1.0] * ML_HEADS + [0.5] * ML_HEADS)


def _spread(tasks, weights):
    total = sum(weights)
    bounds = [0]
    acc = 0.0
    for w in weights:
        acc += w
        bounds.append(round(len(tasks) * acc / total))
    return [tasks[a:b] for a, b in zip(bounds[:-1], bounds[1:])]


def _zip_then_rest(a, b):
    k = min(len(a), len(b))
    return [t for pair in zip(a[:k], b[:k]) for t in pair] + a[k:] + b[k:]


def _prompt_kernel(*refs):
    pair = pl.program_id(0)
    n_pairs = pl.num_programs(0) - 1

    carried = refs[_PROMPT_CARRIED]

    @pl.when(pair == 0)
    def _():
        for ref in carried:
            ref[...] = jnp.zeros_like(ref)
        _prompt_step(0, 0, True, False, False, *refs)
        _prompt_step(1, 1, True, True, False, *refs)

    @pl.when((pair > 0) & (pair < n_pairs))
    def _():
        _prompt_step(2 * pair, 0, True, True, True, *refs)
        _prompt_step(2 * pair + 1, 1, True, True, True, *refs)

    @pl.when(pair == n_pairs)
    def _():
        _prompt_step(2 * pair, 0, False, True, True, *refs)
        _prompt_step(2 * pair + 1, 1, False, False, True, *refs)


_PROMPT_CARRIED = slice(17, 22)


def _prompt_step(s, half, do_in, do_mix, do_out,
                 sinks_ref, xin2_ref, xres2_ref, gn_ref, w_ref, b_ref, wg_ref, bg_ref,
                 rope_c_ref, rope_i_ref, mk_ref, mv_ref, wout_ref, gfin_ref,
                 out2_ref, ko_ref, vo_ref, ct_ref, n_ref, m_ref,
                 kprev_ref, vprev_ref, u_ref, z_ref, zg_ref, zprev_ref, zgprev_ref,
                 y_ref, yprev_ref):
    rows = slice(half * CHUNK, (half + 1) * CHUNK)
    xin_ref = xin2_ref.at[rows]
    xres_ref = xres2_ref.at[rows]
    out_ref = out2_ref.at[rows]
    n_in = D_MAIN // IN_TN
    n_out = D_MODEL // OUT_TN

    def in_norm():
        xf = xin_ref[...]
        ms = jnp.mean(xf * xf, axis=-1, keepdims=True)
        u_ref[...] = ((xf * lax.rsqrt(ms + EPS)) * gn_ref[...]).astype(BF16)

    def in_tile(t):
        if t == n_in:
            zg_ref[...] = _dot(u_ref[...], wg_ref[...]) + bg_ref[...]
        else:
            cols = slice(t * IN_TN, (t + 1) * IN_TN)
            z_ref[:, cols] = _dot(u_ref[...], w_ref[:, cols]) + b_ref[:, cols]

    def in_rotate(lo, hi):
        zprev_ref[:, lo:hi] = z_ref[:, lo:hi]

    def in_rotate_rest():
        in_rotate(O_MQ, O_CQ)
        zgprev_ref[...] = zg_ref[...]

    def out_tile(t):
        cols = slice(t * OUT_TN, (t + 1) * OUT_TN)
        out_ref[:, cols] = _dot(yprev_ref[...], wout_ref[:, cols]) + xres_ref[:, cols]

    def out_finish():
        acc = out_ref[...]
        ms = jnp.mean(acc * acc, axis=-1, keepdims=True)
        out_ref[...] = (acc * lax.rsqrt(ms + EPS)) * gfin_ref[...]

    tile_of = lambda col: col // IN_TN
    in_order = (list(range(0, tile_of(O_MQ))) + list(range(tile_of(O_CQ), n_in))
                + list(range(tile_of(O_MQ), tile_of(O_CQ))) + [n_in])
    in_tasks = []
    for t in in_order:
        in_tasks.append((functools.partial(in_tile, t), 0))
        if t == tile_of(O_MQ) - 1:
            in_tasks.append((functools.partial(in_rotate, 0, O_MQ), SWA_POINTS))
        if t == n_in - 1:
            in_tasks.append((functools.partial(in_rotate, O_CQ, D_MAIN), SWA_POINTS + MEM_POINTS))
    out_tasks = [(functools.partial(out_tile, t), 0) for t in range(n_out)] + [(out_finish, 0)]

    def mix(tasks):
        groups = iter(_spread(tasks, MIXER_WEIGHTS))
        points_done = [0]

        def between():
            for emit, first_point in next(groups):
                assert points_done[0] >= first_point
                emit()
            points_done[0] += 1

        _prompt_mixers(s - 1, sinks_ref, zprev_ref, zgprev_ref, rope_c_ref, rope_i_ref, mk_ref, mv_ref,
                       y_ref, ko_ref, vo_ref, ct_ref, n_ref, m_ref, kprev_ref, vprev_ref, between)
        assert next(groups, None) is None

    def run(tasks):
        for emit, _ in tasks:
            emit()

    if do_in:
        in_norm()
    if do_in and do_out:
        tasks = out_tasks[:4] + _zip_then_rest(in_tasks, out_tasks[4:])
    else:
        tasks = in_tasks if do_in else out_tasks
    if do_mix:
        mix(tasks)
    else:
        run(tasks)
    if do_in:
        in_rotate_rest()
    if do_mix:
        yprev_ref[...] = y_ref[...]


def _prompt_mixers(c, sinks_ref, z_ref, zg_ref, rope_c_ref, rope_i_ref, mk_ref, mv_ref,
                   y_ref, ko_ref, vo_ref, ct_ref, n_ref, m_ref, kprev_ref, vprev_ref, between):
    L = CHUNK
    scale = HEAD_DIM ** -0.5

    cc = rope_c_ref[0, pl.ds(c, 1), :]
    sc = rope_c_ref[1, pl.ds(c, 1), :]
    cos = cc * rope_i_ref[0] - sc * rope_i_ref[1]
    sin = sc * rope_i_ref[2] + cc * rope_i_ref[3]

    ri = lax.broadcasted_iota(jnp.int32, (GQA_GROUP * L, 2 * L), 0) & (L - 1)
    cj = lax.broadcasted_iota(jnp.int32, (GQA_GROUP * L, 2 * L), 1)
    j_low = jnp.where(c > 0, 0, L)
    band = (cj > ri) & (cj <= ri + L) & (cj >= j_low)
    for kv in range(SWA_KV_HEADS):
        between()
        k_new = _rope(z_ref[:, O_SK + kv * HEAD_DIM:O_SK + (kv + 1) * HEAD_DIM], cos, sin)
        v_new = z_ref[:, O_SV + kv * HEAD_DIM:O_SV + (kv + 1) * HEAD_DIM]
        ko_ref[:, kv * HEAD_DIM:(kv + 1) * HEAD_DIM] = k_new
        vo_ref[:, kv * HEAD_DIM:(kv + 1) * HEAD_DIM] = v_new
        k_new_b = k_new.astype(BF16)
        v_new_b = v_new.astype(BF16)
        kcat = jnp.concatenate([kprev_ref[:, kv * HEAD_DIM:(kv + 1) * HEAD_DIM], k_new_b], axis=0)
        vcat = jnp.concatenate([vprev_ref[:, kv * HEAD_DIM:(kv + 1) * HEAD_DIM], v_new_b], axis=0)
        kprev_ref[:, kv * HEAD_DIM:(kv + 1) * HEAD_DIM] = k_new_b
        vprev_ref[:, kv * HEAD_DIM:(kv + 1) * HEAD_DIM] = v_new_b
        qs = []
        sks = []
        for g in range(GQA_GROUP):
            h = kv * GQA_GROUP + g
            qs.append(_rope(z_ref[:, O_SQ + h * HEAD_DIM:O_SQ + (h + 1) * HEAD_DIM], cos, sin).astype(BF16))
            sks.append(jnp.full((L, 1), sinks_ref[h], F32))
        q_st = jnp.concatenate(qs, axis=0)
        sk = jnp.concatenate(sks, axis=0)
        s = _dot_nt(q_st, kcat)
        between()
        s = jnp.where(band, s * scale, -jnp.inf)
        m = jnp.maximum(jnp.max(s, axis=-1, keepdims=True), sk)
        p = jnp.exp(s - m)
        l = jnp.sum(p, axis=-1, keepdims=True) + jnp.exp(sk - m)
        o = _dot(p.astype(BF16), vcat)
        between()
        o = o * (1.0 / l)
        for g in range(GQA_GROUP):
            h = kv * GQA_GROUP + g
            gate = _silu(z_ref[:, O_SG + h * HEAD_DIM:O_SG + (h + 1) * HEAD_DIM])
            y_ref[:, h * HEAD_DIM:(h + 1) * HEAD_DIM] = (o[g * L:(g + 1) * L] * gate).astype(BF16)

    mem_s, mem_l, mem_o = [], [], []
    for h in range(MEM_HEADS):
        q = z_ref[:, O_CQ + h * HEAD_DIM:O_CQ + (h + 1) * HEAD_DIM].astype(BF16)
        mem_s.append(_dot_nt(q, mk_ref[:, h * HEAD_DIM:(h + 1) * HEAD_DIM]))
        between()
    for h in range(MEM_HEADS):
        s = mem_s[h] * scale
        m = jnp.max(s, axis=-1, keepdims=True)
        p = jnp.exp(s - m)
        mem_l.append(jnp.sum(p, axis=-1, keepdims=True))
        mem_o.append(_dot(p.astype(BF16), mv_ref[:, h * HEAD_DIM:(h + 1) * HEAD_DIM]))
        between()
    for h in range(MEM_HEADS):
        o = mem_o[h] * (1.0 / mem_l[h])
        gate = _silu(z_ref[:, O_CG + h * HEAD_DIM:O_CG + (h + 1) * HEAD_DIM])
        col = SWA_W + ML_W + h * HEAD_DIM
        y_ref[:, col:col + HEAD_DIM] = (o * gate).astype(BF16)

    lane = lax.broadcasted_iota(jnp.int32, (1, GATE_LANES), 1)
    zg = zg_ref[...]
    gates = jnp.where(lane < LANE_F, zg, _log_sigmoid(zg))
    gates = jnp.where(lane < 2 * ML_HEADS, gates, 0.0)
    rr = lax.broadcasted_iota(jnp.int32, (L, L), 0)
    cc = lax.broadcasted_iota(jnp.int32, (L, L), 1)
    causal = rr >= cc
    tri = jnp.where(causal, 1.0, 0.0).astype(BF16)
    csum = _exact_sel_dot(tri, gates)
    gates_t = gates.T
    csum_t = csum.T
    half_lo = lane < ML_DQK
    between()
    heads = []
    for h in range(ML_HEADS):
        pair, half = divmod(h, 2)
        hd = dict(pair=pair, half=half)
        hmask = half_lo if half == 0 else jnp.logical_not(half_lo)
        q_pair = z_ref[:, O_MQ + pair * 128:O_MQ + (pair + 1) * 128]
        hd["k_pair"] = z_ref[:, O_MK + pair * 128:O_MK + (pair + 1) * 128] * (ML_DQK ** -0.5)
        hd["v"] = z_ref[:, O_MV + h * ML_DV:O_MV + (h + 1) * ML_DV].astype(BF16)
        hd["q"] = jnp.where(hmask, q_pair, 0.0)
        q_hb = hd["q"].astype(BF16)
        hd["bt_c"] = csum[:, LANE_F + h:LANE_F + h + 1]
        hd["it_c"] = gates[:, LANE_I + h:LANE_I + h + 1]
        hd["m_prev"] = m_ref[:, h:h + 1]
        ct_pair = ct_ref[pair * 128:(pair + 1) * 128, :]
        hd["qk"] = _dot_nt(q_hb, hd["k_pair"].astype(BF16))
        hd["state_read"] = _dot(q_hb, ct_pair.astype(BF16))
        heads.append(hd)
        between()
    for h, hd in enumerate(heads):
        bt_r = csum_t[LANE_F + h:LANE_F + h + 1, :]
        it_r = gates_t[LANE_I + h:LANE_I + h + 1, :]
        dlog = jnp.where(causal, hd["bt_c"] - bt_r + it_r, -jnp.inf)
        inter = hd["bt_c"] + hd["m_prev"]
        hd["m_t"] = jnp.maximum(inter, jnp.max(dlog, axis=-1, keepdims=True))
        hd["w_state"] = jnp.exp(inter - hd["m_t"])
        a = jnp.exp(dlog - hd["m_t"]) * hd["qk"]
        hd["a_sum"] = jnp.sum(a, axis=-1, keepdims=True)
        hd["intra"] = _dot(a.astype(BF16), hd["v"])
        between()
    for h, hd in enumerate(heads):
        num = hd["intra"] + hd["w_state"] * hd["state_read"]
        n_pair = n_ref[:, hd["pair"] * 128:(hd["pair"] + 1) * 128]
        den = hd["a_sum"] + hd["w_state"] * jnp.sum(hd["q"] * n_pair, axis=-1, keepdims=True)
        denom = jnp.maximum(jnp.abs(den), jnp.exp(-hd["m_t"]))
        hid = num * (1.0 / denom)
        o_gate = _sigmoid(z_ref[:, O_MO + h * ML_DV:O_MO + (h + 1) * ML_DV])
        gate = _silu(z_ref[:, O_MG + h * ML_DV:O_MG + (h + 1) * ML_DV])
        col = SWA_W + h * ML_DV
        y_ref[:, col:col + ML_DV] = ((hid * o_gate) * gate).astype(BF16)

        hd["m_new"] = hd["m_t"][L - 1:L, :]
        bt_last = hd["bt_c"][L - 1:L, :]
        w_s = jnp.exp(bt_last - hd["bt_c"] + hd["it_c"] - hd["m_new"])
        hd["decay"] = jnp.exp(bt_last + hd["m_prev"] - hd["m_new"])
        kw = hd["k_pair"] * w_s
        hd["ksum"] = jnp.sum(kw, axis=0, keepdims=True)
        hd["upd"] = _dot(kw.T.astype(BF16), hd["v"])
        between()
    for h, hd in enumerate(heads):
        half, decay = hd["half"], hd["decay"]
        rows = slice(h * ML_DQK, (h + 1) * ML_DQK)
        ct_ref[rows, :] = decay * ct_ref[rows, :] + hd["upd"][half * ML_DQK:(half + 1) * ML_DQK, :]
        n_ref[:, rows] = decay * n_ref[:, rows] + hd["ksum"][:, half * ML_DQK:(half + 1) * ML_DQK]
        m_ref[:, h:h + 1] = hd["m_new"]


def _prompt(sinks, x2d, g_norm, w, b, wg, bg, rope_c, rope_i, mk, mv, w_out, g_final):
    n = x2d.shape[0]
    nc = n // CHUNK
    const = lambda c, s: (0, 0)
    resident = pl.Buffered(1)
    assert nc % 2 == 0
    n_pairs = nc // 2
    x_in = lambda p, s: (jnp.minimum(p, n_pairs - 1), 0)
    x_res = lambda p, s: (jnp.clip(p - 1, 0, n_pairs - 1), 0)
    return pl.pallas_call(
        _prompt_kernel,
        grid_spec=pltpu.PrefetchScalarGridSpec(
            num_scalar_prefetch=1,
            grid=(n_pairs + 1,),
            in_specs=[
                pl.BlockSpec((2 * CHUNK, D_MODEL), x_in),
                pl.BlockSpec((2 * CHUNK, D_MODEL), x_res),
                pl.BlockSpec((1, D_MODEL), const, pipeline_mode=resident),
                pl.BlockSpec((D_MODEL, D_MAIN), const, pipeline_mode=resident),
                pl.BlockSpec((1, D_MAIN), const, pipeline_mode=resident),
                pl.BlockSpec((D_MODEL, GATE_LANES), const, pipeline_mode=resident),
                pl.BlockSpec((1, GATE_LANES), const, pipeline_mode=resident),
                pl.BlockSpec((2, nc, HEAD_DIM), lambda c, s: (0, 0, 0), pipeline_mode=resident),
                pl.BlockSpec((4, CHUNK, HEAD_DIM), lambda c, s: (0, 0, 0), pipeline_mode=resident),
                pl.BlockSpec((MEM_LEN, MEM_W), const, pipeline_mode=resident),
                pl.BlockSpec((MEM_LEN, MEM_W), const, pipeline_mode=resident),
                pl.BlockSpec((D_MIX, D_MODEL), const, pipeline_mode=resident),
                pl.BlockSpec((1, D_MODEL), const, pipeline_mode=resident),
            ],
            out_specs=[
                pl.BlockSpec((2 * CHUNK, D_MODEL), x_res),
                pl.BlockSpec((CHUNK, SWA_KV_W), const),
                pl.BlockSpec((CHUNK, SWA_KV_W), const),
                pl.BlockSpec((ML_QK_W, ML_DV), const),
                pl.BlockSpec((1, ML_QK_W), const),
                pl.BlockSpec((1, GATE_LANES), const),
            ],
            scratch_shapes=[
                pltpu.VMEM((CHUNK, SWA_KV_W), BF16),
                pltpu.VMEM((CHUNK, SWA_KV_W), BF16),
                pltpu.VMEM((CHUNK, D_MODEL), BF16),
                pltpu.VMEM((CHUNK, D_MAIN), F32),
                pltpu.VMEM((CHUNK, GATE_LANES), F32),
                pltpu.VMEM((CHUNK, D_MAIN), F32),
                pltpu.VMEM((CHUNK, GATE_LANES), F32),
                pltpu.VMEM((CHUNK, D_MIX), BF16),
                pltpu.VMEM((CHUNK, D_MIX), BF16),
            ],
        ),
        out_shape=[
            jax.ShapeDtypeStruct((n, D_MODEL), F32),
            jax.ShapeDtypeStruct((CHUNK, SWA_KV_W), F32),
            jax.ShapeDtypeStruct((CHUNK, SWA_KV_W), F32),
            jax.ShapeDtypeStruct((ML_QK_W, ML_DV), F32),
            jax.ShapeDtypeStruct((1, ML_QK_W), F32),
            jax.ShapeDtypeStruct((1, GATE_LANES), F32),
        ],
        compiler_params=pltpu.CompilerParams(
            dimension_semantics=("arbitrary",), vmem_limit_bytes=VMEM_LIMIT),
        name="prompt",
    )(sinks, x2d, x2d, g_norm, w, b, wg, bg, rope_c, rope_i, mk, mv, w_out, g_final)


OUT_TM = 512


def _outproj_kernel(y_ref, x_ref, w_ref, g_ref, o_ref):
    acc = _dot(y_ref[...], w_ref[...]) + x_ref[...]
    ms = jnp.mean(acc * acc, axis=-1, keepdims=True)
    o_ref[...] = (acc * lax.rsqrt(ms + EPS)) * g_ref[...]


def _outproj(y, x2d, w, g):
    n = x2d.shape[0]
    return pl.pallas_call(
        _outproj_kernel,
        grid=(n // OUT_TM,),
        in_specs=[
            pl.BlockSpec((OUT_TM, D_MIX), lambda i: (i, 0)),
            pl.BlockSpec((OUT_TM, D_MODEL), lambda i: (i, 0)),
            pl.BlockSpec((D_MIX, D_MODEL), lambda i: (0, 0)),
            pl.BlockSpec((1, D_MODEL), lambda i: (0, 0)),
        ],
        out_specs=pl.BlockSpec((OUT_TM, D_MODEL), lambda i: (i, 0)),
        out_shape=jax.ShapeDtypeStruct((n, D_MODEL), F32),
        compiler_params=pltpu.CompilerParams(
            dimension_semantics=("arbitrary",), vmem_limit_bytes=VMEM_LIMIT),
        name="outproj",
    )(y, x2d, w, g)


SB = 16
SR = SB * DEC_SEQ
SEQ_UNROLL = 4
SWA_UNROLL = 8


def _mix_sample_kernel(sinks_ref, z_ref, zg_ref, cos_ref, sin_ref, ck_ref, cv_ref,
                       c_ref, nst_ref, nrep_ref, mrep_ref,
                       y_ref, ko_ref, vo_ref, co_ref, no_ref, mo_ref,
                       q_s, k_s, o_s):
    T = DEC_SEQ
    scale = HEAD_DIM ** -0.5
    cos = cos_ref[...]
    sin = sin_ref[...]

    for h in range(SWA_HEADS):
        q_s[:, h * HEAD_DIM:(h + 1) * HEAD_DIM] = _rope(
            z_ref[:, O_SQ + h * HEAD_DIM:O_SQ + (h + 1) * HEAD_DIM], cos, sin)
    for kv in range(SWA_KV_HEADS):
        k_s[:, kv * HEAD_DIM:(kv + 1) * HEAD_DIM] = _rope(
            z_ref[:, O_SK + kv * HEAD_DIM:O_SK + (kv + 1) * HEAD_DIM], cos, sin)

    KP = 2 * WINDOW
    rt = lax.broadcasted_iota(jnp.int32, (GQA_GROUP * T, KP), 0) & (T - 1)
    cj = lax.broadcasted_iota(jnp.int32, (GQA_GROUP * T, KP), 1)
    mask = (cj > rt) & (cj <= rt + WINDOW)
    kpad = jnp.zeros((KP - WINDOW - T, HEAD_DIM), F32)
    NKV = SWA_KV_HEADS
    SEQ_ROWS = WINDOW * NKV

    keep = SEQ_ROWS - T * NKV
    sink_cols = [jnp.concatenate([jnp.full((T, 1), sinks_ref[kv * GQA_GROUP + g], F32)
                                  for g in range(GQA_GROUP)], axis=0) for kv in range(NKV)]

    def per_group(i, carry):
        chains = [(i * SWA_UNROLL + j, kv) for j in range(SWA_UNROLL) for kv in range(NKV)]
        scores, values = [], []
        for b, kv in chains:
            rows = pl.ds(pl.multiple_of(b * T, T), T)
            base = pl.multiple_of(b * SEQ_ROWS, SEQ_ROWS)
            cols = slice(kv * HEAD_DIM, (kv + 1) * HEAD_DIM)
            k_new = k_s[rows, cols]
            v_new = z_ref[rows, O_SV + kv * HEAD_DIM:O_SV + (kv + 1) * HEAD_DIM]
            if kv == 0:
                ko_ref[pl.ds(base, keep), :] = ck_ref[pl.ds(base + T * NKV, keep), :]
                vo_ref[pl.ds(base, keep), :] = cv_ref[pl.ds(base + T * NKV, keep), :]
            ko_ref[pl.ds(base + keep + kv, T, stride=NKV), :] = k_new
            vo_ref[pl.ds(base + keep + kv, T, stride=NKV), :] = v_new
            kc = ck_ref[pl.ds(base + kv, WINDOW, stride=NKV), :]
            vc = cv_ref[pl.ds(base + kv, WINDOW, stride=NKV), :]
            k_all = jnp.concatenate([kc, k_new, kpad], axis=0).astype(BF16)
            values.append(jnp.concatenate([vc, v_new, kpad], axis=0).astype(BF16))
            q_st = jnp.concatenate([q_s[rows, (kv * GQA_GROUP + g) * HEAD_DIM:(kv * GQA_GROUP + g + 1) * HEAD_DIM]
                                    for g in range(GQA_GROUP)], axis=0).astype(BF16)
            scores.append(_dot_nt(q_st, k_all))
        probs, norms = [], []
        for (b, kv), s in zip(chains, scores):
            s = jnp.where(mask, s * scale, -jnp.inf)
            m = jnp.maximum(jnp.max(s, axis=-1, keepdims=True), sink_cols[kv])
            p = jnp.exp(s - m)
            norms.append(jnp.sum(p, axis=-1, keepdims=True) + jnp.exp(sink_cols[kv] - m))
            probs.append(p.astype(BF16))
        outs = [_dot(p, v) for p, v in zip(probs, values)]
        for (b, kv), o, l in zip(chains, outs, norms):
            rows = pl.ds(pl.multiple_of(b * T, T), T)
            o = o * (1.0 / l)
            for g in range(GQA_GROUP):
                h = kv * GQA_GROUP + g
                o_s[rows, h * HEAD_DIM:(h + 1) * HEAD_DIM] = o[g * T:(g + 1) * T]
        return carry

    lax.fori_loop(0, SB // SWA_UNROLL, per_group, 0)

    for h in range(SWA_HEADS):
        cols = slice(h * HEAD_DIM, (h + 1) * HEAD_DIM)
        gate = _silu(z_ref[:, O_SG + h * HEAD_DIM:O_SG + (h + 1) * HEAD_DIM])
        y_ref[:, cols] = (o_s[:, cols] * gate).astype(BF16)
    y_ref[:, SWA_W + ML_W:] = jnp.zeros((SR, MEM_W), BF16)

    R = SR
    lane = lax.broadcasted_iota(jnp.int32, (1, GATE_LANES), 1)
    zg = zg_ref[...]
    gates = jnp.where(lane < LANE_F, zg, _log_sigmoid(zg))
    gates = jnp.where(lane < 2 * ML_HEADS, gates, 0.0)
    rr = lax.broadcasted_iota(jnp.int32, (R, R), 0)
    cc = lax.broadcasted_iota(jnp.int32, (R, R), 1)
    same_seq = (rr >> 3) == (cc >> 3)
    causal = same_seq & (rr >= cc)
    tri = jnp.where(causal, 1.0, 0.0).astype(BF16)
    csum = _exact_sel_dot(tri, gates)
    gates_t = gates.T
    csum_t = csum.T
    half_lo = lane < ML_DQK
    seq_of_col = lax.broadcasted_iota(jnp.int32, (SB, 1, R), 2) >> 3
    seq_id = lax.broadcasted_iota(jnp.int32, (SB, 1, R), 0)
    own_cols = seq_of_col == seq_id
    own_blk = ((lax.broadcasted_iota(jnp.int32, (R, SB * 128), 0) >> 3)
               == (lax.broadcasted_iota(jnp.int32, (R, SB * 128), 1) >> 7))
    k_t = [(z_ref[:, O_MK + p * 128:O_MK + (p + 1) * 128] * (ML_DQK ** -0.5)).T for p in range(2)]
    mrep = mrep_ref[...]
    tok3 = lax.broadcasted_iota(jnp.int32, (SB, T, 1), 1)
    mo_ref[...] = jnp.zeros_like(mo_ref)

    def per_seq_value(col):
        return jnp.max(col.reshape(SB, T, 1), axis=1, keepdims=True)

    def last_of_seq(col):
        c3 = jnp.where(tok3 == T - 1, col.reshape(SB, T, 1), -jnp.inf)
        return jnp.broadcast_to(jnp.max(c3, axis=1, keepdims=True), (SB, T, 1)).reshape(R, 1)

    for h in range(ML_HEADS):
        pair, half = divmod(h, 2)
        hmask = half_lo if half == 0 else jnp.logical_not(half_lo)
        q_pair = z_ref[:, O_MQ + pair * 128:O_MQ + (pair + 1) * 128]
        k_pair = z_ref[:, O_MK + pair * 128:O_MK + (pair + 1) * 128] * (ML_DQK ** -0.5)
        v_f = z_ref[:, O_MV + h * ML_DV:O_MV + (h + 1) * ML_DV]
        v_h = v_f.astype(BF16)
        q_h = jnp.where(hmask, q_pair, 0.0)
        q_hb = q_h.astype(BF16)
        k_pb = k_pair.astype(BF16)

        bt_c = csum[:, LANE_F + h:LANE_F + h + 1]
        it_c = gates[:, LANE_I + h:LANE_I + h + 1]
        bt_r = csum_t[LANE_F + h:LANE_F + h + 1, :]
        it_r = gates_t[LANE_I + h:LANE_I + h + 1, :]
        m_prev = mrep[:, h:h + 1]

        dlog = jnp.where(causal, bt_c - bt_r + it_r, -jnp.inf)
        inter = bt_c + m_prev
        m_t = jnp.maximum(inter, jnp.max(dlog, axis=-1, keepdims=True))
        w_intra = jnp.exp(dlog - m_t)
        w_state = jnp.exp(inter - m_t)
        a = w_intra * _dot_nt(q_hb, k_pb)

        ct_st = c_ref[:, 2 * pair:2 * pair + 2].reshape(SB * 128, ML_DV).astype(BF16)
        q_blk = jnp.where(own_blk, jnp.tile(q_h, (1, SB)), 0.0).astype(BF16)
        num_state = _dot(q_blk, ct_st)

        num = _dot(a.astype(BF16), v_h) + w_state * num_state
        n_pair = nrep_ref[:, pair * 128:(pair + 1) * 128]
        den = jnp.sum(a, axis=-1, keepdims=True) + w_state * jnp.sum(q_h * n_pair, axis=-1, keepdims=True)
        denom = jnp.maximum(jnp.abs(den), jnp.exp(-m_t))
        hid = num * (1.0 / denom)
        o_gate = _sigmoid(z_ref[:, O_MO + h * ML_DV:O_MO + (h + 1) * ML_DV])
        gate = _silu(z_ref[:, O_MG + h * ML_DV:O_MG + (h + 1) * ML_DV])
        col = SWA_W + h * ML_DV
        y_ref[:, col:col + ML_DV] = ((hid * o_gate) * gate).astype(BF16)

        m_new = last_of_seq(m_t)
        bt_last = last_of_seq(bt_c)
        w_s = jnp.exp(bt_last - bt_c + it_c - m_new)
        decay = jnp.exp(bt_last + m_prev - m_new)
        decay_seq = per_seq_value(decay)

        kt_h = k_t[pair][half * ML_DQK:(half + 1) * ML_DQK, :]
        lhs = jnp.where(own_cols, kt_h[None, :, :], 0.0).reshape(SB * ML_DQK, R).astype(BF16)
        upd = _dot(lhs, (v_f * w_s).astype(BF16)).reshape(SB, ML_DQK, ML_DV)
        co_ref[:, h] = decay_seq * c_ref[:, h] + upd

        kw = (k_pair * w_s).reshape(SB, T, 128)
        ksum = jnp.sum(kw, axis=1)
        n_old = nst_ref[:, h * ML_DQK:(h + 1) * ML_DQK]
        dec2 = decay_seq.reshape(SB, 1)
        no_ref[:, h * ML_DQK:(h + 1) * ML_DQK] = dec2 * n_old + ksum[:, half * ML_DQK:(half + 1) * ML_DQK]
        mo_ref[:, h:h + 1] = per_seq_value(m_new).reshape(SB, 1)


def _mix_sample(sinks, z, zg, cos, sin, ck, cv, cst, nst, nrep, mrep):
    nb = cst.shape[0]
    steps = nb // SB
    const = lambda i, s: (0, 0)
    cache_rows = SB * WINDOW * SWA_KV_HEADS
    return pl.pallas_call(
        _mix_sample_kernel,
        grid_spec=pltpu.PrefetchScalarGridSpec(
            num_scalar_prefetch=1,
            grid=(steps,),
            in_specs=[
                pl.BlockSpec((SR, D_MAIN), lambda i, s: (i, 0)),
                pl.BlockSpec((SR, GATE_LANES), lambda i, s: (i, 0)),
                pl.BlockSpec((SR, HEAD_DIM), const),
                pl.BlockSpec((SR, HEAD_DIM), const),
                pl.BlockSpec((cache_rows, HEAD_DIM), lambda i, s: (i, 0)),
                pl.BlockSpec((cache_rows, HEAD_DIM), lambda i, s: (i, 0)),
                pl.BlockSpec((SB, ML_HEADS, ML_DQK, ML_DV), lambda i, s: (i, 0, 0, 0)),
                pl.BlockSpec((SB, ML_QK_W), lambda i, s: (i, 0)),
                pl.BlockSpec((SR, ML_QK_W), lambda i, s: (i, 0)),
                pl.BlockSpec((SR, GATE_LANES), lambda i, s: (i, 0)),
            ],
            out_specs=[
                pl.BlockSpec((SR, D_MIX), lambda i, s: (i, 0)),
                pl.BlockSpec((cache_rows, HEAD_DIM), lambda i, s: (i, 0)),
                pl.BlockSpec((cache_rows, HEAD_DIM), lambda i, s: (i, 0)),
                pl.BlockSpec((SB, ML_HEADS, ML_DQK, ML_DV), lambda i, s: (i, 0, 0, 0)),
                pl.BlockSpec((SB, ML_QK_W), lambda i, s: (i, 0)),
                pl.BlockSpec((SB, GATE_LANES), lambda i, s: (i, 0)),
            ],
            scratch_shapes=[
                pltpu.VMEM((SR, SWA_W), F32),
                pltpu.VMEM((SR, SWA_KV_W), F32),
                pltpu.VMEM((SR, SWA_W), F32),
            ],
        ),
        out_shape=[
            jax.ShapeDtypeStruct((nb * DEC_SEQ, D_MIX), BF16),
            jax.ShapeDtypeStruct(ck.shape, F32),
            jax.ShapeDtypeStruct(cv.shape, F32),
            jax.ShapeDtypeStruct((nb, ML_HEADS, ML_DQK, ML_DV), F32),
            jax.ShapeDtypeStruct((nb, ML_QK_W), F32),
            jax.ShapeDtypeStruct((nb, GATE_LANES), F32),
        ],
        compiler_params=pltpu.CompilerParams(
            dimension_semantics=("arbitrary",), vmem_limit_bytes=VMEM_LIMIT),
        name="mix_sample",
    )(sinks, z, zg, cos, sin, ck, cv, cst, nst, nrep, mrep)


MB = 8
MR = MB * DEC_SEQ


def _mem_sample_kernel(cq_ref, cg_ref, mk_ref, mv_ref, y_in_ref, y_ref, o_s):
    del y_in_ref
    T = DEC_SEQ
    scale = HEAD_DIM ** -0.5
    zpad = jnp.zeros((T, HEAD_DIM), F32)

    seq_rows = MEM_LEN * MEM_HEADS

    def per_group(i, carry):
        chains = [(i * SEQ_UNROLL + j, h) for j in range(SEQ_UNROLL) for h in range(MEM_HEADS)]
        scores = []
        for b, h in chains:
            rows = pl.ds(pl.multiple_of(b * T, T), T)
            base = pl.multiple_of(b * seq_rows, seq_rows)
            kb = mk_ref[pl.ds(base + h, MEM_LEN, stride=MEM_HEADS), :].astype(BF16)
            q = jnp.concatenate([cq_ref[rows, h * HEAD_DIM:(h + 1) * HEAD_DIM], zpad], axis=0).astype(BF16)
            scores.append(_dot_nt(q, kb))
        probs, norms = [], []
        for s in scores:
            s = s * scale
            p = jnp.exp(s - jnp.max(s, axis=-1, keepdims=True))
            norms.append(jnp.sum(p, axis=-1, keepdims=True))
            probs.append(p.astype(BF16))
        outs = []
        for (b, h), p in zip(chains, probs):
            base = pl.multiple_of(b * seq_rows, seq_rows)
            vb = mv_ref[pl.ds(base + h, MEM_LEN, stride=MEM_HEADS), :].astype(BF16)
            outs.append(_dot(p, vb))
        for (b, h), o, l in zip(chains, outs, norms):
            rows = pl.ds(pl.multiple_of(b * T, T), T)
            o_s[rows, h * HEAD_DIM:(h + 1) * HEAD_DIM] = (o * (1.0 / l))[0:T]
        return carry

    lax.fori_loop(0, MB // SEQ_UNROLL, per_group, 0)
    y_ref[...] = (o_s[...] * _silu(cg_ref[...])).astype(BF16)


def _mem_sample(z, mk, mv, y):
    nb = z.shape[0] // DEC_SEQ
    cache_rows = MB * MEM_LEN * MEM_HEADS
    return pl.pallas_call(
        _mem_sample_kernel,
        grid=(nb // MB,),
        in_specs=[
            pl.BlockSpec((MR, MEM_W), lambda i: (i, O_CQ // MEM_W)),
            pl.BlockSpec((MR, MEM_W), lambda i: (i, O_CG // MEM_W)),
            pl.BlockSpec((cache_rows, HEAD_DIM), lambda i: (i, 0)),
            pl.BlockSpec((cache_rows, HEAD_DIM), lambda i: (i, 0)),
            pl.BlockSpec(memory_space=pl.ANY),
        ],
        out_specs=pl.BlockSpec((MR, MEM_W), lambda i: (i, (SWA_W + ML_W) // MEM_W)),
        out_shape=jax.ShapeDtypeStruct(y.shape, y.dtype),
        input_output_aliases={4: 0},
        scratch_shapes=[pltpu.VMEM((MR, MEM_W), F32)],
        compiler_params=pltpu.CompilerParams(
            dimension_semantics=("arbitrary",), vmem_limit_bytes=VMEM_LIMIT),
        name="mem_sample",
    )(z, z, mk, mv, y)


def _rope_cos_sin(pos):
    half = HEAD_DIM // 2
    inv = jnp.power(ROPE_THETA, -(jnp.arange(half, dtype=F32) * 2.0 / HEAD_DIM))
    ang = pos.astype(F32)[:, None] * inv[None, :]
    return jnp.cos(ang), jnp.sin(ang)


def _rope_tables(pos):
    cos, sin = _rope_cos_sin(pos)
    return jnp.concatenate([cos, cos], axis=-1), jnp.concatenate([-sin, sin], axis=-1)


def _rope_split_tables(n_chunks):
    ca, sa = _rope_cos_sin(jnp.arange(n_chunks, dtype=jnp.int32) * CHUNK)
    cb, sb = _rope_cos_sin(jnp.arange(CHUNK, dtype=jnp.int32))
    dup = lambda t: jnp.concatenate([t, t], axis=-1)
    sgn = lambda t: jnp.concatenate([-t, t], axis=-1)
    return jnp.stack([dup(ca), dup(sa)]), jnp.stack([dup(cb), dup(sb), sgn(cb), sgn(sb)])


def _relayout_in_proj(w_in, b_in):
    w_t = jnp.swapaxes(w_in, 0, 1)
    w_main = _wprep(w_t)
    w_gate = _wgate(w_t)
    b_main = jnp.concatenate([b_in[_R_SQ:_R_MI], b_in[_R_MO:_R_END]])[None, :]
    b_gate = jnp.pad(b_in[_R_MI:_R_MO], (0, GATE_LANES - 2 * ML_HEADS))[None, :]
    return w_main, b_main, w_gate, b_gate


def _layer(xp, xs, mem, ck, cv, c_st, n_st, m_st, cmk, cmv,
           g_norm, w_in, b_in, sinks, g_mem, w_mem_kv, w_out, g_final):
    bp, sp, _ = xp.shape
    bs, ts, _ = xs.shape
    xp2 = xp.reshape(bp * sp, D_MODEL)
    xs2 = xs.reshape(bs * ts, D_MODEL)
    w_main, b_main, w_gate, b_gate = _relayout_in_proj(w_in, b_in)
    g_norm2 = g_norm[None, :]
    w_out_b = w_out.astype(BF16)
    g_final2 = g_final[None, :]
    sinks_flat = sinks.reshape(SWA_HEADS)

    memkv = _memkv(mem.reshape(MEM_LEN, D_MODEL), g_mem[None, :], w_mem_kv.astype(BF16))
    mem_k = memkv[:, :MEM_W]
    mem_v = memkv[:, MEM_W:]
    rope_c, rope_i = _rope_split_tables(sp // CHUNK)
    out_p, pk, pv, ct, n_p, m_p = _prompt(sinks_flat, xp2, g_norm2, w_main, b_main, w_gate, b_gate,
                                           rope_c, rope_i, mem_k.astype(BF16), mem_v.astype(BF16),
                                           w_out_b, g_final2)

    zs, zgs = _proj(xs2, g_norm2, w_main, b_main, w_gate, b_gate)
    cos_s, sin_s = _rope_tables(PAST_LEN + jnp.arange(ts, dtype=jnp.int32))
    cos_s = jnp.tile(cos_s, (SB, 1))
    sin_s = jnp.tile(sin_s, (SB, 1))
    nrep = jnp.repeat(n_st.reshape(bs, ML_QK_W), ts, axis=0)
    m_pad = jnp.pad(m_st, ((0, 0), (0, GATE_LANES - ML_HEADS)))
    mrep = jnp.repeat(m_pad, ts, axis=0)
    ya, sk_o, sv_o, ct_o, n_o, m_o = _mix_sample(
        sinks_flat, zs, zgs, cos_s, sin_s,
        ck.reshape(bs * WINDOW * SWA_KV_HEADS, HEAD_DIM), cv.reshape(bs * WINDOW * SWA_KV_HEADS, HEAD_DIM),
        jnp.swapaxes(c_st, -1, -2), n_st.reshape(bs, ML_QK_W), nrep, mrep)
    ys = _mem_sample(zs, cmk.reshape(bs * MEM_LEN * MEM_HEADS, HEAD_DIM),
                     cmv.reshape(bs * MEM_LEN * MEM_HEADS, HEAD_DIM), ya)
    out_s = _outproj(ys, xs2, w_out_b, g_final2)
    c_o = jnp.swapaxes(ct_o, -1, -2)

    p_state = (
        pk.reshape(bp, WINDOW, SWA_KV_HEADS, HEAD_DIM),
        pv.reshape(bp, WINDOW, SWA_KV_HEADS, HEAD_DIM),
        ct.reshape(ML_HEADS, ML_DQK, ML_DV).transpose(0, 2, 1)[None],
        n_p.reshape(bp, ML_HEADS, ML_DQK),
        m_p[:, :ML_HEADS],
        mem_k.reshape(bp, MEM_LEN, MEM_HEADS, HEAD_DIM),
        mem_v.reshape(bp, MEM_LEN, MEM_HEADS, HEAD_DIM),
    )
    s_state = (
        sk_o.reshape(bs, WINDOW, SWA_KV_HEADS, HEAD_DIM),
        sv_o.reshape(bs, WINDOW, SWA_KV_HEADS, HEAD_DIM),
        c_o,
        n_o.reshape(bs, ML_HEADS, ML_DQK),
        m_o[:, :ML_HEADS],
    )
    return out_p.reshape(bp, sp, D_MODEL), out_s.reshape(bs, ts, D_MODEL), p_state, s_state


def kernel(x_prompt, x_sample, mem_prompt, cache_swa_k, cache_swa_v, state_mlstm_C, state_mlstm_n,
           state_mlstm_m, cache_mem_k, cache_mem_v, g_norm, w_in, b_in, swa_sinks, g_mem, w_mem_kv,
           w_out, g_final):
    depth = g_norm.shape[0]
    assert depth == 1 and x_prompt.shape[0] == 1
    y_p, y_s, p_state, s_state = _layer(
        x_prompt, x_sample, mem_prompt, cache_swa_k[0], cache_swa_v[0], state_mlstm_C[0],
        state_mlstm_n[0], state_mlstm_m[0], cache_mem_k[0], cache_mem_v[0],
        g_norm[0], w_in[0], b_in[0], swa_sinks[0], g_mem[0], w_mem_kv[0], w_out[0], g_final)
    return (y_p, y_s) + tuple(s[None] for s in p_state) + tuple(s[None] for s in s_state)
```

```python
import functools

import jax
import jax.numpy as jnp
from jax import lax
from jax.experimental import pallas as pl
from jax.experimental.pallas import tpu as pltpu

F32 = jnp.float32
BF16 = jnp.bfloat16

D_MODEL = 2048
HEAD_DIM = 128
SWA_HEADS = 8
SWA_KV_HEADS = 2
GQA_GROUP = 4
WINDOW = 128
ML_HEADS = 4
ML_DQK = 64
ML_DV = 128
MEM_HEADS = 4
MEM_LEN = 256
CHUNK = 128
ROPE_THETA = 10000.0
EPS = 1e-6
PAST_LEN = 16384
DEC_SEQ = 8

SWA_W = SWA_HEADS * HEAD_DIM
SWA_KV_W = SWA_KV_HEADS * HEAD_DIM
ML_W = ML_HEADS * ML_DV
ML_QK_W = ML_HEADS * ML_DQK
MEM_W = MEM_HEADS * HEAD_DIM
D_MIX = SWA_W + ML_W + MEM_W

_IN_WIDTHS = (SWA_W, SWA_KV_W, SWA_KV_W, SWA_W, ML_QK_W, ML_QK_W, ML_W, ML_HEADS, ML_HEADS, ML_W, ML_W, MEM_W, MEM_W)
_IN_OFFS = [0]
for _w in _IN_WIDTHS:
    _IN_OFFS.append(_IN_OFFS[-1] + _w)
(_R_SQ, _R_SK, _R_SV, _R_SG, _R_MQ, _R_MK, _R_MV, _R_MI, _R_MF, _R_MO, _R_MG, _R_CQ, _R_CG, _R_END) = _IN_OFFS

O_SQ = 0
O_SK = O_SQ + SWA_W
O_SV = O_SK + SWA_KV_W
O_SG = O_SV + SWA_KV_W
O_MQ = O_SG + SWA_W
O_MK = O_MQ + ML_QK_W
O_MV = O_MK + ML_QK_W
O_MO = O_MV + ML_W
O_MG = O_MO + ML_W
O_CQ = O_MG + ML_W
O_CG = O_CQ + MEM_W
D_MAIN = O_CG + MEM_W
W_A_COLS = O_MO
GATE_LANES = 128
LANE_I = 0
LANE_F = ML_HEADS

VMEM_LIMIT = 58 * 1024 * 1024

_NT = (((1,), (1,)), ((), ()))


def _dot(a, b):
    return jnp.dot(a, b, preferred_element_type=F32)


def _dot_nt(a, b):
    return lax.dot_general(a, b, _NT, preferred_element_type=F32)


def _exact_sel_dot(sel_bf16, x):
    hi = x.astype(BF16)
    r1 = x - hi.astype(F32)
    mid = r1.astype(BF16)
    lo = (r1 - mid.astype(F32)).astype(BF16)
    return _dot(sel_bf16, hi) + _dot(sel_bf16, mid) + _dot(sel_bf16, lo)


def _silu(x):
    h = 0.5 * x
    return h + h * jnp.tanh(h)


def _sigmoid(x):
    return 0.5 + 0.5 * jnp.tanh(0.5 * x)


def _log_sigmoid(x):
    return jnp.minimum(x, 0.0) - jnp.log1p(jnp.exp(-jnp.abs(x)))


PROJ_TM = 1024
PROJ_TN = 512
NORM_ROWS = 256


def _proj_kernel(x_ref, g_ref, w_ref, b_ref, wg_ref, bg_ref, z_ref, zg_ref, u_ref):
    j = pl.program_id(1)

    @pl.when(j == 0)
    def _():
        g = g_ref[...]
        for r in range(PROJ_TM // NORM_ROWS):
            rows = pl.ds(r * NORM_ROWS, NORM_ROWS)
            xf = x_ref[rows, :]
            ms = jnp.mean(xf * xf, axis=-1, keepdims=True)
            u_ref[rows, :] = ((xf * lax.rsqrt(ms + EPS)) * g).astype(BF16)
        zg_ref[...] = _dot(u_ref[...], wg_ref[...]) + bg_ref[...]

    z_ref[...] = _dot(u_ref[...], w_ref[...]) + b_ref[...]


def _proj(x2d, g, w, b, wg, bg):
    n = x2d.shape[0]
    grid = (n // PROJ_TM, D_MAIN // PROJ_TN)
    return pl.pallas_call(
        _proj_kernel,
        grid=grid,
        in_specs=[
            pl.BlockSpec((PROJ_TM, D_MODEL), lambda i, j: (i, 0)),
            pl.BlockSpec((1, D_MODEL), lambda i, j: (0, 0)),
            pl.BlockSpec((D_MODEL, PROJ_TN), lambda i, j: (0, j)),
            pl.BlockSpec((1, PROJ_TN), lambda i, j: (0, j)),
            pl.BlockSpec((D_MODEL, GATE_LANES), lambda i, j: (0, 0)),
            pl.BlockSpec((1, GATE_LANES), lambda i, j: (0, 0)),
        ],
        out_specs=[
            pl.BlockSpec((PROJ_TM, PROJ_TN), lambda i, j: (i, j)),
            pl.BlockSpec((PROJ_TM, GATE_LANES), lambda i, j: (i, 0)),
        ],
        out_shape=[
            jax.ShapeDtypeStruct((n, D_MAIN), F32),
            jax.ShapeDtypeStruct((n, GATE_LANES), F32),
        ],
        scratch_shapes=[pltpu.VMEM((PROJ_TM, D_MODEL), BF16)],
        compiler_params=pltpu.CompilerParams(
            dimension_semantics=("arbitrary", "arbitrary"), vmem_limit_bytes=VMEM_LIMIT),
        name="proj",
    )(x2d, g, w, b, wg, bg)


WPREP_TN = 512


def _wprep_kernel(wt_ref, o_ref):
    o_ref[...] = wt_ref[...].T.astype(BF16)


def _wprep(w_t):
    tiles_a = W_A_COLS // WPREP_TN
    gate_rows = _R_MO - W_A_COLS

    def src_row(j):
        return pl.multiple_of(j * WPREP_TN + jnp.where(j >= tiles_a, gate_rows, 0), 8)

    return pl.pallas_call(
        _wprep_kernel,
        grid=(D_MAIN // WPREP_TN,),
        in_specs=[pl.BlockSpec((pl.Element(WPREP_TN), pl.Element(D_MODEL)), lambda j: (src_row(j), 0))],
        out_specs=pl.BlockSpec((D_MODEL, WPREP_TN), lambda j: (0, j)),
        out_shape=jax.ShapeDtypeStruct((D_MODEL, D_MAIN), BF16),
        compiler_params=pltpu.CompilerParams(
            dimension_semantics=("arbitrary",), vmem_limit_bytes=VMEM_LIMIT),
        name="wprep",
    )(w_t)


def _wgate_kernel(wt_ref, o_ref):
    rows = jnp.concatenate(
        [wt_ref[...], jnp.zeros((GATE_LANES - 2 * ML_HEADS, D_MODEL), F32)], axis=0)
    o_ref[...] = rows.T.astype(BF16)


def _wgate(w_t):
    return pl.pallas_call(
        _wgate_kernel,
        grid=(1,),
        in_specs=[pl.BlockSpec((pl.Element(2 * ML_HEADS), pl.Element(D_MODEL)), lambda j: (_R_MI, 0))],
        out_specs=pl.BlockSpec((D_MODEL, GATE_LANES), lambda j: (0, 0)),
        out_shape=jax.ShapeDtypeStruct((D_MODEL, GATE_LANES), BF16),
        name="wgate",
    )(w_t)


MEMKV_TN = 256


def _memkv_kernel(mem_ref, g_ref, w_ref, o_ref):
    xf = mem_ref[...]
    ms = jnp.mean(xf * xf, axis=-1, keepdims=True)
    u = ((xf * lax.rsqrt(ms + EPS)) * g_ref[...]).astype(BF16)
    o_ref[...] = _dot(u, w_ref[...])


def _memkv(mem2d, g, w):
    return pl.pallas_call(
        _memkv_kernel,
        grid=(2 * MEM_W // MEMKV_TN,),
        in_specs=[
            pl.BlockSpec((MEM_LEN, D_MODEL), lambda j: (0, 0)),
            pl.BlockSpec((1, D_MODEL), lambda j: (0, 0)),
            pl.BlockSpec((D_MODEL, MEMKV_TN), lambda j: (0, j)),
        ],
        out_specs=pl.BlockSpec((MEM_LEN, MEMKV_TN), lambda j: (0, j)),
        out_shape=jax.ShapeDtypeStruct((MEM_LEN, 2 * MEM_W), F32),
        compiler_params=pltpu.CompilerParams(
            dimension_semantics=("arbitrary",), vmem_limit_bytes=VMEM_LIMIT),
        name="memkv",
    )(mem2d, g, w)


OUT_TN = 256


def _rope(x, cos, sin_signed):
    return x * cos + pltpu.roll(x, HEAD_DIM // 2, axis=1) * sin_signed


IN_TN = 256
SWA_POINTS = 3 * SWA_KV_HEADS
MEM_POINTS = 2 * MEM_HEADS
MIXER_WEIGHTS = ([1.0, 3.0, 1.0] * SWA_KV_HEADS + [1.0] * MEM_HEADS + [0.5] * MEM_HEADS
                 + [1.0] + [1.5] * ML_HEADS + [1.0] * ML_HEADS + [0.5] * ML_HEADS)


def _spread(tasks, weights):
    total = sum(weights)
    bounds = [0]
    acc = 0.0
    for w in weights:
        acc += w
        bounds.append(round(len(tasks) * acc / total))
    return [tasks[a:b] for a, b in zip(bounds[:-1], bounds[1:])]


def _zip_then_rest(a, b):
    k = min(len(a), len(b))
    return [t for pair in zip(a[:k], b[:k]) for t in pair] + a[k:] + b[k:]


def _prompt_kernel(*refs):
    pair = pl.program_id(0)
    n_pairs = pl.num_programs(0) - 1

    carried = refs[_PROMPT_CARRIED]

    @pl.when(pair == 0)
    def _():
        for ref in carried:
            ref[...] = jnp.zeros_like(ref)
        _prompt_step(0, 0, True, False, False, *refs)
        _prompt_step(1, 1, True, True, False, *refs)

    @pl.when((pair > 0) & (pair < n_pairs))
    def _():
        _prompt_step(2 * pair, 0, True, True, True, *refs)
        _prompt_step(2 * pair + 1, 1, True, True, True, *refs)

    @pl.when(pair == n_pairs)
    def _():
        _prompt_step(2 * pair, 0, False, True, True, *refs)
        _prompt_step(2 * pair + 1, 1, False, False, True, *refs)


_PROMPT_CARRIED = slice(17, 22)


def _prompt_step(s, half, do_in, do_mix, do_out,
                 sinks_ref, xin2_ref, xres2_ref, gn_ref, w_ref, b_ref, wg_ref, bg_ref,
                 rope_c_ref, rope_i_ref, mk_ref, mv_ref, wout_ref, gfin_ref,
                 out2_ref, ko_ref, vo_ref, ct_ref, n_ref, m_ref,
                 kprev_ref, vprev_ref, u_ref, z_ref, zg_ref, zprev_ref, zgprev_ref,
                 y_ref, yprev_ref):
    rows = slice(half * CHUNK, (half + 1) * CHUNK)
    xin_ref = xin2_ref.at[rows]
    xres_ref = xres2_ref.at[rows]
    out_ref = out2_ref.at[rows]
    n_in = D_MAIN // IN_TN
    n_out = D_MODEL // OUT_TN

    def in_norm():
        xf = xin_ref[...]
        ms = jnp.mean(xf * xf, axis=-1, keepdims=True)
        u_ref[...] = ((xf * lax.rsqrt(ms + EPS)) * gn_ref[...]).astype(BF16)

    def in_tile(t):
        if t == n_in:
            zg_ref[...] = _dot(u_ref[...], wg_ref[...]) + bg_ref[...]
        else:
            cols = slice(t * IN_TN, (t + 1) * IN_TN)
            z_ref[:, cols] = _dot(u_ref[...], w_ref[:, cols]) + b_ref[:, cols]

    def in_rotate(lo, hi):
        zprev_ref[:, lo:hi] = z_ref[:, lo:hi]

    def in_rotate_rest():
        in_rotate(O_MQ, O_CQ)
        zgprev_ref[...] = zg_ref[...]

    def out_tile(t):
        cols = slice(t * OUT_TN, (t + 1) * OUT_TN)
        out_ref[:, cols] = _dot(yprev_ref[...], wout_ref[:, cols]) + xres_ref[:, cols]

    def out_finish():
        acc = out_ref[...]
        ms = jnp.mean(acc * acc, axis=-1, keepdims=True)
        out_ref[...] = (acc * lax.rsqrt(ms + EPS)) * gfin_ref[...]

    tile_of = lambda col: col // IN_TN
    in_order = (list(range(0, tile_of(O_MQ))) + list(range(tile_of(O_CQ), n_in))
                + list(range(tile_of(O_MQ), tile_of(O_CQ))) + [n_in])
    in_tasks = []
    for t in in_order:
        in_tasks.append((functools.partial(in_tile, t), 0))
        if t == tile_of(O_MQ) - 1:
            in_tasks.append((functools.partial(in_rotate, 0, O_MQ), SWA_POINTS))
        if t == n_in - 1:
            in_tasks.append((functools.partial(in_rotate, O_CQ, D_MAIN), SWA_POINTS + MEM_POINTS))
    out_tasks = [(functools.partial(out_tile, t), 0) for t in range(n_out)] + [(out_finish, 0)]

    def mix(tasks):
        groups = iter(_spread(tasks, MIXER_WEIGHTS))
        points_done = [0]

        def between():
            for emit, first_point in next(groups):
                assert points_done[0] >= first_point
                emit()
            points_done[0] += 1

        _prompt_mixers(s - 1, sinks_ref, zprev_ref, zgprev_ref, rope_c_ref, rope_i_ref, mk_ref, mv_ref,
                       y_ref, ko_ref, vo_ref, ct_ref, n_ref, m_ref, kprev_ref, vprev_ref, between)
        assert next(groups, None) is None

    def run(tasks):
        for emit, _ in tasks:
            emit()

    if do_in:
        in_norm()
    if do_in and do_out:
        tasks = out_tasks[:4] + _zip_then_rest(in_tasks, out_tasks[4:])
    else:
        tasks = in_tasks if do_in else out_tasks
    if do_mix:
        mix(tasks)
    else:
        run(tasks)
    if do_in:
        in_rotate_rest()
    if do_mix:
        yprev_ref[...] = y_ref[...]


def _prompt_mixers(c, sinks_ref, z_ref, zg_ref, rope_c_ref, rope_i_ref, mk_ref, mv_ref,
                   y_ref, ko_ref, vo_ref, ct_ref, n_ref, m_ref, kprev_ref, vprev_ref, between):
    L = CHUNK
    scale = HEAD_DIM ** -0.5

    cc = rope_c_ref[0, pl.ds(c, 1), :]
    sc = rope_c_ref[1, pl.ds(c, 1), :]
    cos = cc * rope_i_ref[0] - sc * rope_i_ref[1]
    sin = sc * rope_i_ref[2] + cc * rope_i_ref[3]

    ri = lax.broadcasted_iota(jnp.int32, (GQA_GROUP * L, 2 * L), 0) & (L - 1)
    cj = lax.broadcasted_iota(jnp.int32, (GQA_GROUP * L, 2 * L), 1)
    j_low = jnp.where(c > 0, 0, L)
    band = (cj > ri) & (cj <= ri + L) & (cj >= j_low)
    for kv in range(SWA_KV_HEADS):
        between()
        k_new = _rope(z_ref[:, O_SK + kv * HEAD_DIM:O_SK + (kv + 1) * HEAD_DIM], cos, sin)
        v_new = z_ref[:, O_SV + kv * HEAD_DIM:O_SV + (kv + 1) * HEAD_DIM]
        ko_ref[:, kv * HEAD_DIM:(kv + 1) * HEAD_DIM] = k_new
        vo_ref[:, kv * HEAD_DIM:(kv + 1) * HEAD_DIM] = v_new
        k_new_b = k_new.astype(BF16)
        v_new_b = v_new.astype(BF16)
        kcat = jnp.concatenate([kprev_ref[:, kv * HEAD_DIM:(kv + 1) * HEAD_DIM], k_new_b], axis=0)
        vcat = jnp.concatenate([vprev_ref[:, kv * HEAD_DIM:(kv + 1) * HEAD_DIM], v_new_b], axis=0)
        kprev_ref[:, kv * HEAD_DIM:(kv + 1) * HEAD_DIM] = k_new_b
        vprev_ref[:, kv * HEAD_DIM:(kv + 1) * HEAD_DIM] = v_new_b
        qs = []
        sks = []
        for g in range(GQA_GROUP):
            h = kv * GQA_GROUP + g
            qs.append(_rope(z_ref[:, O_SQ + h * HEAD_DIM:O_SQ + (h + 1) * HEAD_DIM], cos, sin).astype(BF16))
            sks.append(jnp.full((L, 1), sinks_ref[h], F32))
        q_st = jnp.concatenate(qs, axis=0)
        sk = jnp.concatenate(sks, axis=0)
        s = _dot_nt(q_st, kcat)
        between()
        s = jnp.where(band, s * scale, -jnp.inf)
        m = jnp.maximum(jnp.max(s, axis=-1, keepdims=True), sk)
        p = jnp.exp(s - m)
        l = jnp.sum(p, axis=-1, keepdims=True) + jnp.exp(sk - m)
        o = _dot(p.astype(BF16), vcat)
        between()
        o = o * (1.0 / l)
        for g in range(GQA_GROUP):
            h = kv * GQA_GROUP + g
            gate = _silu(z_ref[:, O_SG + h * HEAD_DIM:O_SG + (h + 1) * HEAD_DIM])
            y_ref[:, h * HEAD_DIM:(h + 1) * HEAD_DIM] = (o[g * L:(g + 1) * L] * gate).astype(BF16)

    mem_s, mem_l, mem_o = [], [], []
    for h in range(MEM_HEADS):
        q = z_ref[:, O_CQ + h * HEAD_DIM:O_CQ + (h + 1) * HEAD_DIM].astype(BF16)
        mem_s.append(_dot_nt(q, mk_ref[:, h * HEAD_DIM:(h + 1) * HEAD_DIM]))
        between()
    for h in range(MEM_HEADS):
        s = mem_s[h] * scale
        m = jnp.max(s, axis=-1, keepdims=True)
        p = jnp.exp(s - m)
        mem_l.append(jnp.sum(p, axis=-1, keepdims=True))
        mem_o.append(_dot(p.astype(BF16), mv_ref[:, h * HEAD_DIM:(h + 1) * HEAD_DIM]))
        between()
    for h in range(MEM_HEADS):
        o = mem_o[h] * (1.0 / mem_l[h])
        gate = _silu(z_ref[:, O_CG + h * HEAD_DIM:O_CG + (h + 1) * HEAD_DIM])
        col = SWA_W + ML_W + h * HEAD_DIM
        y_ref[:, col:col + HEAD_DIM] = (o * gate).astype(BF16)

    lane = lax.broadcasted_iota(jnp.int32, (1, GATE_LANES), 1)
    zg = zg_ref[...]
    gates = jnp.where(lane < LANE_F, zg, _log_sigmoid(zg))
    gates = jnp.where(lane < 2 * ML_HEADS, gates, 0.0)
    rr = lax.broadcasted_iota(jnp.int32, (L, L), 0)
    cc = lax.broadcasted_iota(jnp.int32, (L, L), 1)
    causal = rr >= cc
    tri = jnp.where(causal, 1.0, 0.0).astype(BF16)
    csum = _exact_sel_dot(tri, gates)
    gates_t = gates.T
    csum_t = csum.T
    half_lo = lane < ML_DQK
    between()
    heads = []
    for h in range(ML_HEADS):
        pair, half = divmod(h, 2)
        hd = dict(pair=pair, half=half)
        hmask = half_lo if half == 0 else jnp.logical_not(half_lo)
        q_pair = z_ref[:, O_MQ + pair * 128:O_MQ + (pair + 1) * 128]
        hd["k_pair"] = z_ref[:, O_MK + pair * 128:O_MK + (pair + 1) * 128] * (ML_DQK ** -0.5)
        hd["v"] = z_ref[:, O_MV + h * ML_DV:O_MV + (h + 1) * ML_DV].astype(BF16)
        hd["q"] = jnp.where(hmask, q_pair, 0.0)
        q_hb = hd["q"].astype(BF16)
        hd["bt_c"] = csum[:, LANE_F + h:LANE_F + h + 1]
        hd["it_c"] = gates[:, LANE_I + h:LANE_I + h + 1]
        hd["m_prev"] = m_ref[:, h:h + 1]
        ct_pair = ct_ref[pair * 128:(pair + 1) * 128, :]
        hd["qk"] = _dot_nt(q_hb, hd["k_pair"].astype(BF16))
        hd["state_read"] = _dot(q_hb, ct_pair.astype(BF16))
        heads.append(hd)
        between()
    for h, hd in enumerate(heads):
        bt_r = csum_t[LANE_F + h:LANE_F + h + 1, :]
        it_r = gates_t[LANE_I + h:LANE_I + h + 1, :]
        dlog = jnp.where(causal, hd["bt_c"] - bt_r + it_r, -jnp.inf)
        inter = hd["bt_c"] + hd["m_prev"]
        hd["m_t"] = jnp.maximum(inter, jnp.max(dlog, axis=-1, keepdims=True))
        hd["w_state"] = jnp.exp(inter - hd["m_t"])
        a = jnp.exp(dlog - hd["m_t"]) * hd["qk"]
        hd["a_sum"] = jnp.sum(a, axis=-1, keepdims=True)
        hd["intra"] = _dot(a.astype(BF16), hd["v"])
        between()
    for h, hd in enumerate(heads):
        num = hd["intra"] + hd["w_state"] * hd["state_read"]
        n_pair = n_ref[:, hd["pair"] * 128:(hd["pair"] + 1) * 128]
        den = hd["a_sum"] + hd["w_state"] * jnp.sum(hd["q"] * n_pair, axis=-1, keepdims=True)
        denom = jnp.maximum(jnp.abs(den), jnp.exp(-hd["m_t"]))
        hid = num * (1.0 / denom)
        o_gate = _sigmoid(z_ref[:, O_MO + h * ML_DV:O_MO + (h + 1) * ML_DV])
        gate = _silu(z_ref[:, O_MG + h * ML_DV:O_MG + (h + 1) * ML_DV])
        col = SWA_W + h * ML_DV
        y_ref[:, col:col + ML_DV] = ((hid * o_gate) * gate).astype(BF16)

        hd["m_new"] = hd["m_t"][L - 1:L, :]
        bt_last = hd["bt_c"][L - 1:L, :]
        w_s = jnp.exp(bt_last - hd["bt_c"] + hd["it_c"] - hd["m_new"])
        hd["decay"] = jnp.exp(bt_last + hd["m_prev"] - hd["m_new"])
        kw = hd["k_pair"] * w_s
        hd["ksum"] = jnp.sum(kw, axis=0, keepdims=True)
        hd["upd"] = _dot(kw.T.astype(BF16), hd["v"])
        between()
    for h, hd in enumerate(heads):
        half, decay = hd["half"], hd["decay"]
        rows = slice(h * ML_DQK, (h + 1) * ML_DQK)
        ct_ref[rows, :] = decay * ct_ref[rows, :] + hd["upd"][half * ML_DQK:(half + 1) * ML_DQK, :]
        n_ref[:, rows] = decay * n_ref[:, rows] + hd["ksum"][:, half * ML_DQK:(half + 1) * ML_DQK]
        m_ref[:, h:h + 1] = hd["m_new"]


def _prompt(sinks, x2d, g_norm, w, b, wg, bg, rope_c, rope_i, mk, mv, w_out, g_final):
    n = x2d.shape[0]
    nc = n // CHUNK
    const = lambda c, s: (0, 0)
    resident = pl.Buffered(1)
    assert nc % 2 == 0
    n_pairs = nc // 2
    x_in = lambda p, s: (jnp.minimum(p, n_pairs - 1), 0)
    x_res = lambda p, s: (jnp.clip(p - 1, 0, n_pairs - 1), 0)
    return pl.pallas_call(
        _prompt_kernel,
        grid_spec=pltpu.PrefetchScalarGridSpec(
            num_scalar_prefetch=1,
            grid=(n_pairs + 1,),
            in_specs=[
                pl.BlockSpec((2 * CHUNK, D_MODEL), x_in),
                pl.BlockSpec((2 * CHUNK, D_MODEL), x_res),
                pl.BlockSpec((1, D_MODEL), const, pipeline_mode=resident),
                pl.BlockSpec((D_MODEL, D_MAIN), const, pipeline_mode=resident),
                pl.BlockSpec((1, D_MAIN), const, pipeline_mode=resident),
                pl.BlockSpec((D_MODEL, GATE_LANES), const, pipeline_mode=resident),
                pl.BlockSpec((1, GATE_LANES), const, pipeline_mode=resident),
                pl.BlockSpec((2, nc, HEAD_DIM), lambda c, s: (0, 0, 0), pipeline_mode=resident),
                pl.BlockSpec((4, CHUNK, HEAD_DIM), lambda c, s: (0, 0, 0), pipeline_mode=resident),
                pl.BlockSpec((MEM_LEN, MEM_W), const, pipeline_mode=resident),
                pl.BlockSpec((MEM_LEN, MEM_W), const, pipeline_mode=resident),
                pl.BlockSpec((D_MIX, D_MODEL), const, pipeline_mode=resident),
                pl.BlockSpec((1, D_MODEL), const, pipeline_mode=resident),
            ],
            out_specs=[
                pl.BlockSpec((2 * CHUNK, D_MODEL), x_res),
                pl.BlockSpec((CHUNK, SWA_KV_W), const),
                pl.BlockSpec((CHUNK, SWA_KV_W), const),
                pl.BlockSpec((ML_QK_W, ML_DV), const),
                pl.BlockSpec((1, ML_QK_W), const),
                pl.BlockSpec((1, GATE_LANES), const),
            ],
            scratch_shapes=[
                pltpu.VMEM((CHUNK, SWA_KV_W), BF16),
                pltpu.VMEM((CHUNK, SWA_KV_W), BF16),
                pltpu.VMEM((CHUNK, D_MODEL), BF16),
                pltpu.VMEM((CHUNK, D_MAIN), F32),
                pltpu.VMEM((CHUNK, GATE_LANES), F32),
                pltpu.VMEM((CHUNK, D_MAIN), F32),
                pltpu.VMEM((CHUNK, GATE_LANES), F32),
                pltpu.VMEM((CHUNK, D_MIX), BF16),
                pltpu.VMEM((CHUNK, D_MIX), BF16),
            ],
        ),
        out_shape=[
            jax.ShapeDtypeStruct((n, D_MODEL), F32),
            jax.ShapeDtypeStruct((CHUNK, SWA_KV_W), F32),
            jax.ShapeDtypeStruct((CHUNK, SWA_KV_W), F32),
            jax.ShapeDtypeStruct((ML_QK_W, ML_DV), F32),
            jax.ShapeDtypeStruct((1, ML_QK_W), F32),
            jax.ShapeDtypeStruct((1, GATE_LANES), F32),
        ],
        compiler_params=pltpu.CompilerParams(
            dimension_semantics=("arbitrary",), vmem_limit_bytes=VMEM_LIMIT),
        name="prompt",
    )(sinks, x2d, x2d, g_norm, w, b, wg, bg, rope_c, rope_i, mk, mv, w_out, g_final)


OUT_TM = 512


def _outproj_kernel(y_ref, x_ref, w_ref, g_ref, o_ref):
    acc = _dot(y_ref[...], w_ref[...]) + x_ref[...]
    ms = jnp.mean(acc * acc, axis=-1, keepdims=True)
    o_ref[...] = (acc * lax.rsqrt(ms + EPS)) * g_ref[...]


def _outproj(y, x2d, w, g):
    n = x2d.shape[0]
    return pl.pallas_call(
        _outproj_kernel,
        grid=(n // OUT_TM,),
        in_specs=[
            pl.BlockSpec((OUT_TM, D_MIX), lambda i: (i, 0)),
            pl.BlockSpec((OUT_TM, D_MODEL), lambda i: (i, 0)),
            pl.BlockSpec((D_MIX, D_MODEL), lambda i: (0, 0)),
            pl.BlockSpec((1, D_MODEL), lambda i: (0, 0)),
        ],
        out_specs=pl.BlockSpec((OUT_TM, D_MODEL), lambda i: (i, 0)),
        out_shape=jax.ShapeDtypeStruct((n, D_MODEL), F32),
        compiler_params=pltpu.CompilerParams(
            dimension_semantics=("arbitrary",), vmem_limit_bytes=VMEM_LIMIT),
        name="outproj",
    )(y, x2d, w, g)


SB = 16
SR = SB * DEC_SEQ
SEQ_UNROLL = 4
SWA_UNROLL = 8


def _mix_sample_kernel(sinks_ref, z_ref, zg_ref, cos_ref, sin_ref, ck_ref, cv_ref,
                       c_ref, nst_ref, nrep_ref, mrep_ref,
                       y_ref, ko_ref, vo_ref, co_ref, no_ref, mo_ref,
                       q_s, k_s, o_s):
    T = DEC_SEQ
    scale = HEAD_DIM ** -0.5
    cos = cos_ref[...]
    sin = sin_ref[...]

    for h in range(SWA_HEADS):
        q_s[:, h * HEAD_DIM:(h + 1) * HEAD_DIM] = _rope(
            z_ref[:, O_SQ + h * HEAD_DIM:O_SQ + (h + 1) * HEAD_DIM], cos, sin)
    for kv in range(SWA_KV_HEADS):
        k_s[:, kv * HEAD_DIM:(kv + 1) * HEAD_DIM] = _rope(
            z_ref[:, O_SK + kv * HEAD_DIM:O_SK + (kv + 1) * HEAD_DIM], cos, sin)

    KP = 2 * WINDOW
    rt = lax.broadcasted_iota(jnp.int32, (GQA_GROUP * T, KP), 0) & (T - 1)
    cj = lax.broadcasted_iota(jnp.int32, (GQA_GROUP * T, KP), 1)
    mask = (cj > rt) & (cj <= rt + WINDOW)
    kpad = jnp.zeros((KP - WINDOW - T, HEAD_DIM), F32)
    NKV = SWA_KV_HEADS
    SEQ_ROWS = WINDOW * NKV

    keep = SEQ_ROWS - T * NKV
    sink_cols = [jnp.concatenate([jnp.full((T, 1), sinks_ref[kv * GQA_GROUP + g], F32)
                                  for g in range(GQA_GROUP)], axis=0) for kv in range(NKV)]

    def per_group(i, carry):
        chains = [(i * SWA_UNROLL + j, kv) for j in range(SWA_UNROLL) for kv in range(NKV)]
        scores, values = [], []
        for b, kv in chains:
            rows = pl.ds(pl.multiple_of(b * T, T), T)
            base = pl.multiple_of(b * SEQ_ROWS, SEQ_ROWS)
            cols = slice(kv * HEAD_DIM, (kv + 1) * HEAD_DIM)
            k_new = k_s[rows, cols]
            v_new = z_ref[rows, O_SV + kv * HEAD_DIM:O_SV + (kv + 1) * HEAD_DIM]
            if kv == 0:
                ko_ref[pl.ds(base, keep), :] = ck_ref[pl.ds(base + T * NKV, keep), :]
                vo_ref[pl.ds(base, keep), :] = cv_ref[pl.ds(base + T * NKV, keep), :]
            ko_ref[pl.ds(base + keep + kv, T, stride=NKV), :] = k_new
            vo_ref[pl.ds(base + keep + kv, T, stride=NKV), :] = v_new
            kc = ck_ref[pl.ds(base + kv, WINDOW, stride=NKV), :]
            vc = cv_ref[pl.ds(base + kv, WINDOW, stride=NKV), :]
            k_all = jnp.concatenate([kc, k_new, kpad], axis=0).astype(BF16)
            values.append(jnp.concatenate([vc, v_new, kpad], axis=0).astype(BF16))
            q_st = jnp.concatenate([q_s[rows, (kv * GQA_GROUP + g) * HEAD_DIM:(kv * GQA_GROUP + g + 1) * HEAD_DIM]
                                    for g in range(GQA_GROUP)], axis=0).astype(BF16)
            scores.append(_dot_nt(q_st, k_all))
        probs, norms = [], []
        for (b, kv), s in zip(chains, scores):
            s = jnp.where(mask, s * scale, -jnp.inf)
            m = jnp.maximum(jnp.max(s, axis=-1, keepdims=True), sink_cols[kv])
            p = jnp.exp(s - m)
            norms.append(jnp.sum(p, axis=-1, keepdims=True) + jnp.exp(sink_cols[kv] - m))
            probs.append(p.astype(BF16))
        outs = [_dot(p, v) for p, v in zip(probs, values)]
        for (b, kv), o, l in zip(chains, outs, norms):
            rows = pl.ds(pl.multiple_of(b * T, T), T)
            o = o * (1.0 / l)
            for g in range(GQA_GROUP):
                h = kv * GQA_GROUP + g
                o_s[rows, h * HEAD_DIM:(h + 1) * HEAD_DIM] = o[g * T:(g + 1) * T]
        return carry

    lax.fori_loop(0, SB // SWA_UNROLL, per_group, 0)

    for h in range(SWA_HEADS):
        cols = slice(h * HEAD_DIM, (h + 1) * HEAD_DIM)
        gate = _silu(z_ref[:, O_SG + h * HEAD_DIM:O_SG + (h + 1) * HEAD_DIM])
        y_ref[:, cols] = (o_s[:, cols] * gate).astype(BF16)
    y_ref[:, SWA_W + ML_W:] = jnp.zeros((SR, MEM_W), BF16)

    R = SR
    lane = lax.broadcasted_iota(jnp.int32, (1, GATE_LANES), 1)
    zg = zg_ref[...]
    gates = jnp.where(lane < LANE_F, zg, _log_sigmoid(zg))
    gates = jnp.where(lane < 2 * ML_HEADS, gates, 0.0)
    rr = lax.broadcasted_iota(jnp.int32, (R, R), 0)
    cc = lax.broadcasted_iota(jnp.int32, (R, R), 1)
    same_seq = (rr >> 3) == (cc >> 3)
    causal = same_seq & (rr >= cc)
    tri = jnp.where(causal, 1.0, 0.0).astype(BF16)
    csum = _exact_sel_dot(tri, gates)
    gates_t = gates.T
    csum_t = csum.T
    half_lo = lane < ML_DQK
    seq_of_col = lax.broadcasted_iota(jnp.int32, (SB, 1, R), 2) >> 3
    seq_id = lax.broadcasted_iota(jnp.int32, (SB, 1, R), 0)
    own_cols = seq_of_col == seq_id
    own_blk = ((lax.broadcasted_iota(jnp.int32, (R, SB * 128), 0) >> 3)
               == (lax.broadcasted_iota(jnp.int32, (R, SB * 128), 1) >> 7))
    k_t = [(z_ref[:, O_MK + p * 128:O_MK + (p + 1) * 128] * (ML_DQK ** -0.5)).T for p in range(2)]
    mrep = mrep_ref[...]
    tok3 = lax.broadcasted_iota(jnp.int32, (SB, T, 1), 1)
    mo_ref[...] = jnp.zeros_like(mo_ref)

    def per_seq_value(col):
        return jnp.max(col.reshape(SB, T, 1), axis=1, keepdims=True)

    def last_of_seq(col):
        c3 = jnp.where(tok3 == T - 1, col.reshape(SB, T, 1), -jnp.inf)
        return jnp.broadcast_to(jnp.max(c3, axis=1, keepdims=True), (SB, T, 1)).reshape(R, 1)

    for h in range(ML_HEADS):
        pair, half = divmod(h, 2)
        hmask = half_lo if half == 0 else jnp.logical_not(half_lo)
        q_pair = z_ref[:, O_MQ + pair * 128:O_MQ + (pair + 1) * 128]
        k_pair = z_ref[:, O_MK + pair * 128:O_MK + (pair + 1) * 128] * (ML_DQK ** -0.5)
        v_f = z_ref[:, O_MV + h * ML_DV:O_MV + (h + 1) * ML_DV]
        v_h = v_f.astype(BF16)
        q_h = jnp.where(hmask, q_pair, 0.0)
        q_hb = q_h.astype(BF16)
        k_pb = k_pair.astype(BF16)

        bt_c = csum[:, LANE_F + h:LANE_F + h + 1]
        it_c = gates[:, LANE_I + h:LANE_I + h + 1]
        bt_r = csum_t[LANE_F + h:LANE_F + h + 1, :]
        it_r = gates_t[LANE_I + h:LANE_I + h + 1, :]
        m_prev = mrep[:, h:h + 1]

        dlog = jnp.where(causal, bt_c - bt_r + it_r, -jnp.inf)
        inter = bt_c + m_prev
        m_t = jnp.maximum(inter, jnp.max(dlog, axis=-1, keepdims=True))
        w_intra = jnp.exp(dlog - m_t)
        w_state = jnp.exp(inter - m_t)
        a = w_intra * _dot_nt(q_hb, k_pb)

        ct_st = c_ref[:, 2 * pair:2 * pair + 2].reshape(SB * 128, ML_DV).astype(BF16)
        q_blk = jnp.where(own_blk, jnp.tile(q_h, (1, SB)), 0.0).astype(BF16)
        num_state = _dot(q_blk, ct_st)

        num = _dot(a.astype(BF16), v_h) + w_state * num_state
        n_pair = nrep_ref[:, pair * 128:(pair + 1) * 128]
        den = jnp.sum(a, axis=-1, keepdims=True) + w_state * jnp.sum(q_h * n_pair, axis=-1, keepdims=True)
        denom = jnp.maximum(jnp.abs(den), jnp.exp(-m_t))
        hid = num * (1.0 / denom)
        o_gate = _sigmoid(z_ref[:, O_MO + h * ML_DV:O_MO + (h + 1) * ML_DV])
        gate = _silu(z_ref[:, O_MG + h * ML_DV:O_MG + (h + 1) * ML_DV])
        col = SWA_W + h * ML_DV
        y_ref[:, col:col + ML_DV] = ((hid * o_gate) * gate).astype(BF16)

        m_new = last_of_seq(m_t)
        bt_last = last_of_seq(bt_c)
        w_s = jnp.exp(bt_last - bt_c + it_c - m_new)
        decay = jnp.exp(bt_last + m_prev - m_new)
        decay_seq = per_seq_value(decay)

        kt_h = k_t[pair][half * ML_DQK:(half + 1) * ML_DQK, :]
        lhs = jnp.where(own_cols, kt_h[None, :, :], 0.0).reshape(SB * ML_DQK, R).astype(BF16)
        upd = _dot(lhs, (v_f * w_s).astype(BF16)).reshape(SB, ML_DQK, ML_DV)
        co_ref[:, h] = decay_seq * c_ref[:, h] + upd

        kw = (k_pair * w_s).reshape(SB, T, 128)
        ksum = jnp.sum(kw, axis=1)
        n_old = nst_ref[:, h * ML_DQK:(h + 1) * ML_DQK]
        dec2 = decay_seq.reshape(SB, 1)
        no_ref[:, h * ML_DQK:(h + 1) * ML_DQK] = dec2 * n_old + ksum[:, half * ML_DQK:(half + 1) * ML_DQK]
        mo_ref[:, h:h + 1] = per_seq_value(m_new).reshape(SB, 1)


def _mix_sample(sinks, z, zg, cos, sin, ck, cv, cst, nst, nrep, mrep):
    nb = cst.shape[0]
    steps = nb // SB
    const = lambda i, s: (0, 0)
    cache_rows = SB * WINDOW * SWA_KV_HEADS
    return pl.pallas_call(
        _mix_sample_kernel,
        grid_spec=pltpu.PrefetchScalarGridSpec(
            num_scalar_prefetch=1,
            grid=(steps,),
            in_specs=[
                pl.BlockSpec((SR, D_MAIN), lambda i, s: (i, 0)),
                pl.BlockSpec((SR, GATE_LANES), lambda i, s: (i, 0)),
                pl.BlockSpec((SR, HEAD_DIM), const),
                pl.BlockSpec((SR, HEAD_DIM), const),
                pl.BlockSpec((cache_rows, HEAD_DIM), lambda i, s: (i, 0)),
                pl.BlockSpec((cache_rows, HEAD_DIM), lambda i, s: (i, 0)),
                pl.BlockSpec((SB, ML_HEADS, ML_DQK, ML_DV), lambda i, s: (i, 0, 0, 0)),
                pl.BlockSpec((SB, ML_QK_W), lambda i, s: (i, 0)),
                pl.BlockSpec((SR, ML_QK_W), lambda i, s: (i, 0)),
                pl.BlockSpec((SR, GATE_LANES), lambda i, s: (i, 0)),
            ],
            out_specs=[
                pl.BlockSpec((SR, D_MIX), lambda i, s: (i, 0)),
                pl.BlockSpec((cache_rows, HEAD_DIM), lambda i, s: (i, 0)),
                pl.BlockSpec((cache_rows, HEAD_DIM), lambda i, s: (i, 0)),
                pl.BlockSpec((SB, ML_HEADS, ML_DQK, ML_DV), lambda i, s: (i, 0, 0, 0)),
                pl.BlockSpec((SB, ML_QK_W), lambda i, s: (i, 0)),
                pl.BlockSpec((SB, GATE_LANES), lambda i, s: (i, 0)),
            ],
            scratch_shapes=[
                pltpu.VMEM((SR, SWA_W), F32),
                pltpu.VMEM((SR, SWA_KV_W), F32),
                pltpu.VMEM((SR, SWA_W), F32),
            ],
        ),
        out_shape=[
            jax.ShapeDtypeStruct((nb * DEC_SEQ, D_MIX), BF16),
            jax.ShapeDtypeStruct(ck.shape, F32),
            jax.ShapeDtypeStruct(cv.shape, F32),
            jax.ShapeDtypeStruct((nb, ML_HEADS, ML_DQK, ML_DV), F32),
            jax.ShapeDtypeStruct((nb, ML_QK_W), F32),
            jax.ShapeDtypeStruct((nb, GATE_LANES), F32),
        ],
        compiler_params=pltpu.CompilerParams(
            dimension_semantics=("arbitrary",), vmem_limit_bytes=VMEM_LIMIT),
        name="mix_sample",
    )(sinks, z, zg, cos, sin, ck, cv, cst, nst, nrep, mrep)


MB = 8
MR = MB * DEC_SEQ


def _mem_sample_kernel(cq_ref, cg_ref, mk_ref, mv_ref, y_in_ref, y_ref, o_s):
    del y_in_ref
    T = DEC_SEQ
    scale = HEAD_DIM ** -0.5
    zpad = jnp.zeros((T, HEAD_DIM), F32)

    seq_rows = MEM_LEN * MEM_HEADS

    def per_group(i, carry):
        chains = [(i * SEQ_UNROLL + j, h) for j in range(SEQ_UNROLL) for h in range(MEM_HEADS)]
        scores = []
        for b, h in chains:
            rows = pl.ds(pl.multiple_of(b * T, T), T)
            base = pl.multiple_of(b * seq_rows, seq_rows)
            kb = mk_ref[pl.ds(base + h, MEM_LEN, stride=MEM_HEADS), :].astype(BF16)
            q = jnp.concatenate([cq_ref[rows, h * HEAD_DIM:(h + 1) * HEAD_DIM], zpad], axis=0).astype(BF16)
            scores.append(_dot_nt(q, kb))
        probs, norms = [], []
        for s in scores:
            s = s * scale
            p = jnp.exp(s - jnp.max(s, axis=-1, keepdims=True))
            norms.append(jnp.sum(p, axis=-1, keepdims=True))
            probs.append(p.astype(BF16))
        outs = []
        for (b, h), p in zip(chains, probs):
            base = pl.multiple_of(b * seq_rows, seq_rows)
            vb = mv_ref[pl.ds(base + h, MEM_LEN, stride=MEM_HEADS), :].astype(BF16)
            outs.append(_dot(p, vb))
        for (b, h), o, l in zip(chains, outs, norms):
            rows = pl.ds(pl.multiple_of(b * T, T), T)
            o_s[rows, h * HEAD_DIM:(h + 1) * HEAD_DIM] = (o * (1.0 / l))[0:T]
        return carry

    lax.fori_loop(0, MB // SEQ_UNROLL, per_group, 0)
    y_ref[...] = (o_s[...] * _silu(cg_ref[...])).astype(BF16)


def _mem_sample(z, mk, mv, y):
    nb = z.shape[0] // DEC_SEQ
    cache_rows = MB * MEM_LEN * MEM_HEADS
    return pl.pallas_call(
        _mem_sample_kernel,
        grid=(nb // MB,),
        in_specs=[
            pl.BlockSpec((MR, MEM_W), lambda i: (i, O_CQ // MEM_W)),
            pl.BlockSpec((MR, MEM_W), lambda i: (i, O_CG // MEM_W)),
            pl.BlockSpec((cache_rows, HEAD_DIM), lambda i: (i, 0)),
            pl.BlockSpec((cache_rows, HEAD_DIM), lambda i: (i, 0)),
            pl.BlockSpec(memory_space=pl.ANY),
        ],
        out_specs=pl.BlockSpec((MR, MEM_W), lambda i: (i, (SWA_W + ML_W) // MEM_W)),
        out_shape=jax.ShapeDtypeStruct(y.shape, y.dtype),
        input_output_aliases={4: 0},
        scratch_shapes=[pltpu.VMEM((MR, MEM_W), F32)],
        compiler_params=pltpu.CompilerParams(
            dimension_semantics=("arbitrary",), vmem_limit_bytes=VMEM_LIMIT),
        name="mem_sample",
    )(z, z, mk, mv, y)


def _rope_cos_sin(pos):
    half = HEAD_DIM // 2
    inv = jnp.power(ROPE_THETA, -(jnp.arange(half, dtype=F32) * 2.0 / HEAD_DIM))
    ang = pos.astype(F32)[:, None] * inv[None, :]
    return jnp.cos(ang), jnp.sin(ang)


def _rope_tables(pos):
    cos, sin = _rope_cos_sin(pos)
    return jnp.concatenate([cos, cos], axis=-1), jnp.concatenate([-sin, sin], axis=-1)


def _rope_split_tables(n_chunks):
    ca, sa = _rope_cos_sin(jnp.arange(n_chunks, dtype=jnp.int32) * CHUNK)
    cb, sb = _rope_cos_sin(jnp.arange(CHUNK, dtype=jnp.int32))
    dup = lambda t: jnp.concatenate([t, t], axis=-1)
    sgn = lambda t: jnp.concatenate([-t, t], axis=-1)
    return jnp.stack([dup(ca), dup(sa)]), jnp.stack([dup(cb), dup(sb), sgn(cb), sgn(sb)])


def _relayout_in_proj(w_in, b_in):
    w_t = jnp.swapaxes(w_in, 0, 1)
    w_main = _wprep(w_t)
    w_gate = _wgate(w_t)
    b_main = jnp.concatenate([b_in[_R_SQ:_R_MI], b_in[_R_MO:_R_END]])[None, :]
    b_gate = jnp.pad(b_in[_R_MI:_R_MO], (0, GATE_LANES - 2 * ML_HEADS))[None, :]
    return w_main, b_main, w_gate, b_gate


def _layer(xp, xs, mem, ck, cv, c_st, n_st, m_st, cmk, cmv,
           g_norm, w_in, b_in, sinks, g_mem, w_mem_kv, w_out, g_final):
    bp, sp, _ = xp.shape
    bs, ts, _ = xs.shape
    xp2 = xp.reshape(bp * sp, D_MODEL)
    xs2 = xs.reshape(bs * ts, D_MODEL)
    w_main, b_main, w_gate, b_gate = _relayout_in_proj(w_in, b_in)
    g_norm2 = g_norm[None, :]
    w_out_b = w_out.astype(BF16)
    g_final2 = g_final[None, :]
    sinks_flat = sinks.reshape(SWA_HEADS)

    memkv = _memkv(mem.reshape(MEM_LEN, D_MODEL), g_mem[None, :], w_mem_kv.astype(BF16))
    mem_k = memkv[:, :MEM_W]
    mem_v = memkv[:, MEM_W:]
    rope_c, rope_i = _rope_split_tables(sp // CHUNK)
    out_p, pk, pv, ct, n_p, m_p = _prompt(sinks_flat, xp2, g_norm2, w_main, b_main, w_gate, b_gate,
                                           rope_c, rope_i, mem_k.astype(BF16), mem_v.astype(BF16),
                                           w_out_b, g_final2)

    zs, zgs = _proj(xs2, g_norm2, w_main, b_main, w_gate, b_gate)
    cos_s, sin_s = _rope_tables(PAST_LEN + jnp.arange(ts, dtype=jnp.int32))
    cos_s = jnp.tile(cos_s, (SB, 1))
    sin_s = jnp.tile(sin_s, (SB, 1))
    nrep = jnp.repeat(n_st.reshape(bs, ML_QK_W), ts, axis=0)
    m_pad = jnp.pad(m_st, ((0, 0), (0, GATE_LANES - ML_HEADS)))
    mrep = jnp.repeat(m_pad, ts, axis=0)
    ya, sk_o, sv_o, ct_o, n_o, m_o = _mix_sample(
        sinks_flat, zs, zgs, cos_s, sin_s,
        ck.reshape(bs * WINDOW * SWA_KV_HEADS, HEAD_DIM), cv.reshape(bs * WINDOW * SWA_KV_HEADS, HEAD_DIM),
        jnp.swapaxes(c_st, -1, -2), n_st.reshape(bs, ML_QK_W), nrep, mrep)
    ys = _mem_sample(zs, cmk.reshape(bs * MEM_LEN * MEM_HEADS, HEAD_DIM),
                     cmv.reshape(bs * MEM_LEN * MEM_HEADS, HEAD_DIM), ya)
    out_s = _outproj(ys, xs2, w_out_b, g_final2)
    c_o = jnp.swapaxes(ct_o, -1, -2)

    p_state = (
        pk.reshape(bp, WINDOW, SWA_KV_HEADS, HEAD_DIM),
        pv.reshape(bp, WINDOW, SWA_KV_HEADS, HEAD_DIM),
        ct.reshape(ML_HEADS, ML_DQK, ML_DV).transpose(0, 2, 1)[None],
        n_p.reshape(bp, ML_HEADS, ML_DQK),
        m_p[:, :ML_HEADS],
        mem_k.reshape(bp, MEM_LEN, MEM_HEADS, HEAD_DIM),
        mem_v.reshape(bp, MEM_LEN, MEM_HEADS, HEAD_DIM),
    )
    s_state = (
        sk_o.reshape(bs, WINDOW, SWA_KV_HEADS, HEAD_DIM),
        sv_o.reshape(bs, WINDOW, SWA_KV_HEADS, HEAD_DIM),
        c_o,
        n_o.reshape(bs, ML_HEADS, ML_DQK),
        m_o[:, :ML_HEADS],
    )
    return out_p.reshape(bp, sp, D_MODEL), out_s.reshape(bs, ts, D_MODEL), p_state, s_state


def kernel(x_prompt, x_sample, mem_prompt, cache_swa_k, cache_swa_v, state_mlstm_C, state_mlstm_n,
           state_mlstm_m, cache_mem_k, cache_mem_v, g_norm, w_in, b_in, swa_sinks, g_mem, w_mem_kv,
           w_out, g_final):
    depth = g_norm.shape[0]
    assert depth == 1 and x_prompt.shape[0] == 1
    y_p, y_s, p_state, s_state = _layer(
        x_prompt, x_sample, mem_prompt, cache_swa_k[0], cache_swa_v[0], state_mlstm_C[0],
        state_mlstm_n[0], state_mlstm_m[0], cache_mem_k[0], cache_mem_v[0],
        g_norm[0], w_in[0], b_in[0], swa_sinks[0], g_mem[0], w_mem_kv[0], w_out[0], g_final)
    return (y_p, y_s) + tuple(s[None] for s in p_state) + tuple(s[None] for s in s_state)
```

```python
import functools

import jax
import jax.numpy as jnp
from jax import lax
from jax.experimental import pallas as pl
from jax.experimental.pallas import tpu as pltpu

F32 = jnp.float32
BF16 = jnp.bfloat16

D_MODEL = 2048
HEAD_DIM = 128
SWA_HEADS = 8
SWA_KV_HEADS = 2
GQA_GROUP = 4
WINDOW = 128
ML_HEADS = 4
ML_DQK = 64
ML_DV = 128
MEM_HEADS = 4
MEM_LEN = 256
CHUNK = 128
ROPE_THETA = 10000.0
EPS = 1e-6
PAST_LEN = 16384
DEC_SEQ = 8

SWA_W = SWA_HEADS * HEAD_DIM
SWA_KV_W = SWA_KV_HEADS * HEAD_DIM
ML_W = ML_HEADS * ML_DV
ML_QK_W = ML_HEADS * ML_DQK
MEM_W = MEM_HEADS * HEAD_DIM
D_MIX = SWA_W + ML_W + MEM_W

_IN_WIDTHS = (SWA_W, SWA_KV_W, SWA_KV_W, SWA_W, ML_QK_W, ML_QK_W, ML_W, ML_HEADS, ML_HEADS, ML_W, ML_W, MEM_W, MEM_W)
_IN_OFFS = [0]
for _w in _IN_WIDTHS:
    _IN_OFFS.append(_IN_OFFS[-1] + _w)
(_R_SQ, _R_SK, _R_SV, _R_SG, _R_MQ, _R_MK, _R_MV, _R_MI, _R_MF, _R_MO, _R_MG, _R_CQ, _R_CG, _R_END) = _IN_OFFS

O_SQ = 0
O_SK = O_SQ + SWA_W
O_SV = O_SK + SWA_KV_W
O_SG = O_SV + SWA_KV_W
O_MQ = O_SG + SWA_W
O_MK = O_MQ + ML_QK_W
O_MV = O_MK + ML_QK_W
O_MO = O_MV + ML_W
O_MG = O_MO + ML_W
O_CQ = O_MG + ML_W
O_CG = O_CQ + MEM_W
D_MAIN = O_CG + MEM_W
W_A_COLS = O_MO
GATE_LANES = 128
LANE_I = 0
LANE_F = ML_HEADS

VMEM_LIMIT = 58 * 1024 * 1024

_NT = (((1,), (1,)), ((), ()))


def _dot(a, b):
    return jnp.dot(a, b, preferred_element_type=F32)


def _dot_nt(a, b):
    return lax.dot_general(a, b, _NT, preferred_element_type=F32)


def _exact_sel_dot(sel_bf16, x):
    hi = x.astype(BF16)
    r1 = x - hi.astype(F32)
    mid = r1.astype(BF16)
    lo = (r1 - mid.astype(F32)).astype(BF16)
    return _dot(sel_bf16, hi) + _dot(sel_bf16, mid) + _dot(sel_bf16, lo)


def _silu(x):
    h = 0.5 * x
    return h + h * jnp.tanh(h)


def _sigmoid(x):
    return 0.5 + 0.5 * jnp.tanh(0.5 * x)


def _log_sigmoid(x):
    return jnp.minimum(x, 0.0) - jnp.log1p(jnp.exp(-jnp.abs(x)))


PROJ_TM = 1024
PROJ_TN = 512
NORM_ROWS = 256


def _proj_kernel(x_ref, g_ref, w_ref, b_ref, wg_ref, bg_ref, z_ref, zg_ref, u_ref):
    j = pl.program_id(1)

    @pl.when(j == 0)
    def _():
        g = g_ref[...]
        for r in range(PROJ_TM // NORM_ROWS):
            rows = pl.ds(r * NORM_ROWS, NORM_ROWS)
            xf = x_ref[rows, :]
            ms = jnp.mean(xf * xf, axis=-1, keepdims=True)
            u_ref[rows, :] = ((xf * lax.rsqrt(ms + EPS)) * g).astype(BF16)
        zg_ref[...] = _dot(u_ref[...], wg_ref[...]) + bg_ref[...]

    z_ref[...] = _dot(u_ref[...], w_ref[...]) + b_ref[...]


def _proj(x2d, g, w, b, wg, bg):
    n = x2d.shape[0]
    grid = (n // PROJ_TM, D_MAIN // PROJ_TN)
    return pl.pallas_call(
        _proj_kernel,
        grid=grid,
        in_specs=[
            pl.BlockSpec((PROJ_TM, D_MODEL), lambda i, j: (i, 0)),
            pl.BlockSpec((1, D_MODEL), lambda i, j: (0, 0)),
            pl.BlockSpec((D_MODEL, PROJ_TN), lambda i, j: (0, j)),
            pl.BlockSpec((1, PROJ_TN), lambda i, j: (0, j)),
            pl.BlockSpec((D_MODEL, GATE_LANES), lambda i, j: (0, 0)),
            pl.BlockSpec((1, GATE_LANES), lambda i, j: (0, 0)),
        ],
        out_specs=[
            pl.BlockSpec((PROJ_TM, PROJ_TN), lambda i, j: (i, j)),
            pl.BlockSpec((PROJ_TM, GATE_LANES), lambda i, j: (i, 0)),
        ],
        out_shape=[
            jax.ShapeDtypeStruct((n, D_MAIN), F32),
            jax.ShapeDtypeStruct((n, GATE_LANES), F32),
        ],
        scratch_shapes=[pltpu.VMEM((PROJ_TM, D_MODEL), BF16)],
        compiler_params=pltpu.CompilerParams(
            dimension_semantics=("arbitrary", "arbitrary"), vmem_limit_bytes=VMEM_LIMIT),
        name="proj",
    )(x2d, g, w, b, wg, bg)


WPREP_TN = 512


def _wprep_kernel(wt_ref, o_ref):
    o_ref[...] = wt_ref[...].T.astype(BF16)


def _wprep(w_t):
    tiles_a = W_A_COLS // WPREP_TN
    gate_rows = _R_MO - W_A_COLS

    def src_row(j):
        return pl.multiple_of(j * WPREP_TN + jnp.where(j >= tiles_a, gate_rows, 0), 8)

    return pl.pallas_call(
        _wprep_kernel,
        grid=(D_MAIN // WPREP_TN,),
        in_specs=[pl.BlockSpec((pl.Element(WPREP_TN), pl.Element(D_MODEL)), lambda j: (src_row(j), 0))],
        out_specs=pl.BlockSpec((D_MODEL, WPREP_TN), lambda j: (0, j)),
        out_shape=jax.ShapeDtypeStruct((D_MODEL, D_MAIN), BF16),
        compiler_params=pltpu.CompilerParams(
            dimension_semantics=("arbitrary",), vmem_limit_bytes=VMEM_LIMIT),
        name="wprep",
    )(w_t)


def _wgate_kernel(wt_ref, o_ref):
    rows = jnp.concatenate(
        [wt_ref[...], jnp.zeros((GATE_LANES - 2 * ML_HEADS, D_MODEL), F32)], axis=0)
    o_ref[...] = rows.T.astype(BF16)


def _wgate(w_t):
    return pl.pallas_call(
        _wgate_kernel,
        grid=(1,),
        in_specs=[pl.BlockSpec((pl.Element(2 * ML_HEADS), pl.Element(D_MODEL)), lambda j: (_R_MI, 0))],
        out_specs=pl.BlockSpec((D_MODEL, GATE_LANES), lambda j: (0, 0)),
        out_shape=jax.ShapeDtypeStruct((D_MODEL, GATE_LANES), BF16),
        name="wgate",
    )(w_t)


MEMKV_TN = 256


def _memkv_kernel(mem_ref, g_ref, w_ref, o_ref):
    xf = mem_ref[...]
    ms = jnp.mean(xf * xf, axis=-1, keepdims=True)
    u = ((xf * lax.rsqrt(ms + EPS)) * g_ref[...]).astype(BF16)
    o_ref[...] = _dot(u, w_ref[...])


def _memkv(mem2d, g, w):
    return pl.pallas_call(
        _memkv_kernel,
        grid=(2 * MEM_W // MEMKV_TN,),
        in_specs=[
            pl.BlockSpec((MEM_LEN, D_MODEL), lambda j: (0, 0)),
            pl.BlockSpec((1, D_MODEL), lambda j: (0, 0)),
            pl.BlockSpec((D_MODEL, MEMKV_TN), lambda j: (0, j)),
        ],
        out_specs=pl.BlockSpec((MEM_LEN, MEMKV_TN), lambda j: (0, j)),
        out_shape=jax.ShapeDtypeStruct((MEM_LEN, 2 * MEM_W), F32),
        compiler_params=pltpu.CompilerParams(
            dimension_semantics=("arbitrary",), vmem_limit_bytes=VMEM_LIMIT),
        name="memkv",
    )(mem2d, g, w)


OUT_TN = 256


def _rope(x, cos, sin_signed):
    return x * cos + pltpu.roll(x, HEAD_DIM // 2, axis=1) * sin_signed


IN_TN = 256
SWA_POINTS = 3 * SWA_KV_HEADS
MEM_POINTS = 2 * MEM_HEADS
MIXER_WEIGHTS = ([1.0, 3.0, 1.0] * SWA_KV_HEADS + [1.0] * MEM_HEADS + [0.5] * MEM_HEADS
                 + [1.0] + [1.5] * ML_HEADS + [1.0] * ML_HEADS + [0.5] * ML_HEADS)


def _spread(tasks, weights):
    total = sum(weights)
    bounds = [0]
    acc = 0.0
    for w in weights:
        acc += w
        bounds.append(round(len(tasks) * acc / total))
    return [tasks[a:b] for a, b in zip(bounds[:-1], bounds[1:])]


def _zip_then_rest(a, b):
    k = min(len(a), len(b))
    return [t for pair in zip(a[:k], b[:k]) for t in pair] + a[k:] + b[k:]


def _prompt_kernel(*refs):
    pair = pl.program_id(0)
    n_pairs = pl.num_programs(0) - 1

    carried = refs[_PROMPT_CARRIED]

    @pl.when(pair == 0)
    def _():
        for ref in carried:
            ref[...] = jnp.zeros_like(ref)
        _prompt_step(0, 0, True, False, False, *refs)
        _prompt_step(1, 1, True, True, False, *refs)

    @pl.when((pair > 0) & (pair < n_pairs))
    def _():
        _prompt_step(2 * pair, 0, True, True, True, *refs)
        _prompt_step(2 * pair + 1, 1, True, True, True, *refs)

    @pl.when(pair == n_pairs)
    def _():
        _prompt_step(2 * pair, 0, False, True, True, *refs)
        _prompt_step(2 * pair + 1, 1, False, False, True, *refs)


_PROMPT_CARRIED = slice(17, 22)


def _prompt_step(s, half, do_in, do_mix, do_out,
                 sinks_ref, xin2_ref, xres2_ref, gn_ref, w_ref, b_ref, wg_ref, bg_ref,
                 rope_c_ref, rope_i_ref, mk_ref, mv_ref, wout_ref, gfin_ref,
                 out2_ref, ko_ref, vo_ref, ct_ref, n_ref, m_ref,
                 kprev_ref, vprev_ref, u_ref, z_ref, zg_ref, zprev_ref, zgprev_ref,
                 y_ref, yprev_ref):
    rows = slice(half * CHUNK, (half + 1) * CHUNK)
    xin_ref = xin2_ref.at[rows]
    xres_ref = xres2_ref.at[rows]
    out_ref = out2_ref.at[rows]
    n_in = D_MAIN // IN_TN
    n_out = D_MODEL // OUT_TN

    def in_norm():
        xf = xin_ref[...]
        ms = jnp.mean(xf * xf, axis=-1, keepdims=True)
        u_ref[...] = ((xf * lax.rsqrt(ms + EPS)) * gn_ref[...]).astype(BF16)

    def in_tile(t):
        if t == n_in:
            zg_ref[...] = _dot(u_ref[...], wg_ref[...]) + bg_ref[...]
        else:
            cols = slice(t * IN_TN, (t + 1) * IN_TN)
            z_ref[:, cols] = _dot(u_ref[...], w_ref[:, cols]) + b_ref[:, cols]

    def in_rotate(lo, hi):
        zprev_ref[:, lo:hi] = z_ref[:, lo:hi]

    def in_rotate_rest():
        in_rotate(O_MQ, O_CQ)
        zgprev_ref[...] = zg_ref[...]

    def out_tile(t):
        cols = slice(t * OUT_TN, (t + 1) * OUT_TN)
        out_ref[:, cols] = _dot(yprev_ref[...], wout_ref[:, cols]) + xres_ref[:, cols]

    def out_finish():
        acc = out_ref[...]
        ms = jnp.mean(acc * acc, axis=-1, keepdims=True)
        out_ref[...] = (acc * lax.rsqrt(ms + EPS)) * gfin_ref[...]

    tile_of = lambda col: col // IN_TN
    in_order = (list(range(0, tile_of(O_MQ))) + list(range(tile_of(O_CQ), n_in))
                + list(range(tile_of(O_MQ), tile_of(O_CQ))) + [n_in])
    in_tasks = []
    for t in in_order:
        in_tasks.append((functools.partial(in_tile, t), 0))
        if t == tile_of(O_MQ) - 1:
            in_tasks.append((functools.partial(in_rotate, 0, O_MQ), SWA_POINTS))
        if t == n_in - 1:
            in_tasks.append((functools.partial(in_rotate, O_CQ, D_MAIN), SWA_POINTS + MEM_POINTS))
    out_tasks = [(functools.partial(out_tile, t), 0) for t in range(n_out)] + [(out_finish, 0)]

    def mix(tasks):
        groups = iter(_spread(tasks, MIXER_WEIGHTS))
        points_done = [0]

        def between():
            for emit, first_point in next(groups):
                assert points_done[0] >= first_point
                emit()
            points_done[0] += 1

        _prompt_mixers(s - 1, sinks_ref, zprev_ref, zgprev_ref, rope_c_ref, rope_i_ref, mk_ref, mv_ref,
                       y_ref, ko_ref, vo_ref, ct_ref, n_ref, m_ref, kprev_ref, vprev_ref, between)
        assert next(groups, None) is None

    def run(tasks):
        for emit, _ in tasks:
            emit()

    if do_in:
        in_norm()
    if do_in and do_out:
        tasks = out_tasks[:4] + _zip_then_rest(in_tasks, out_tasks[4:])
    else:
        tasks = in_tasks if do_in else out_tasks
    if do_mix:
        mix(tasks)
    else:
        run(tasks)
    if do_in:
        in_rotate_rest()
    if do_mix:
        yprev_ref[...] = y_ref[...]


def _prompt_mixers(c, sinks_ref, z_ref, zg_ref, rope_c_ref, rope_i_ref, mk_ref, mv_ref,
                   y_ref, ko_ref, vo_ref, ct_ref, n_ref, m_ref, kprev_ref, vprev_ref, between):
    L = CHUNK
    scale = HEAD_DIM ** -0.5

    cc = rope_c_ref[0, pl.ds(c, 1), :]
    sc = rope_c_ref[1, pl.ds(c, 1), :]
    cos = cc * rope_i_ref[0] - sc * rope_i_ref[1]
    sin = sc * rope_i_ref[2] + cc * rope_i_ref[3]

    ri = lax.broadcasted_iota(jnp.int32, (GQA_GROUP * L, 2 * L), 0) & (L - 1)
    cj = lax.broadcasted_iota(jnp.int32, (GQA_GROUP * L, 2 * L), 1)
    j_low = jnp.where(c > 0, 0, L)
    band = (cj > ri) & (cj <= ri + L) & (cj >= j_low)
    for kv in range(SWA_KV_HEADS):
        between()
        k_new = _rope(z_ref[:, O_SK + kv * HEAD_DIM:O_SK + (kv + 1) * HEAD_DIM], cos, sin)
        v_new = z_ref[:, O_SV + kv * HEAD_DIM:O_SV + (kv + 1) * HEAD_DIM]
        ko_ref[:, kv * HEAD_DIM:(kv + 1) * HEAD_DIM] = k_new
        vo_ref[:, kv * HEAD_DIM:(kv + 1) * HEAD_DIM] = v_new
        k_new_b = k_new.astype(BF16)
        v_new_b = v_new.astype(BF16)
        kcat = jnp.concatenate([kprev_ref[:, kv * HEAD_DIM:(kv + 1) * HEAD_DIM], k_new_b], axis=0)
        vcat = jnp.concatenate([vprev_ref[:, kv * HEAD_DIM:(kv + 1) * HEAD_DIM], v_new_b], axis=0)
        kprev_ref[:, kv * HEAD_DIM:(kv + 1) * HEAD_DIM] = k_new_b
        vprev_ref[:, kv * HEAD_DIM:(kv + 1) * HEAD_DIM] = v_new_b
        qs = []
        sks = []
        for g in range(GQA_GROUP):
            h = kv * GQA_GROUP + g
            qs.append(_rope(z_ref[:, O_SQ + h * HEAD_DIM:O_SQ + (h + 1) * HEAD_DIM], cos, sin).astype(BF16))
            sks.append(jnp.full((L, 1), sinks_ref[h], F32))
        q_st = jnp.concatenate(qs, axis=0)
        sk = jnp.concatenate(sks, axis=0)
        s = _dot_nt(q_st, kcat)
        between()
        s = jnp.where(band, s * scale, -jnp.inf)
        m = jnp.maximum(jnp.max(s, axis=-1, keepdims=True), sk)
        p = jnp.exp(s - m)
        l = jnp.sum(p, axis=-1, keepdims=True) + jnp.exp(sk - m)
        o = _dot(p.astype(BF16), vcat)
        between()
        o = o * (1.0 / l)
        for g in range(GQA_GROUP):
            h = kv * GQA_GROUP + g
            gate = _silu(z_ref[:, O_SG + h * HEAD_DIM:O_SG + (h + 1) * HEAD_DIM])
            y_ref[:, h * HEAD_DIM:(h + 1) * HEAD_DIM] = (o[g * L:(g + 1) * L] * gate).astype(BF16)

    mem_s, mem_l, mem_o = [], [], []
    for h in range(MEM_HEADS):
        q = z_ref[:, O_CQ + h * HEAD_DIM:O_CQ + (h + 1) * HEAD_DIM].astype(BF16)
        mem_s.append(_dot_nt(q, mk_ref[:, h * HEAD_DIM:(h + 1) * HEAD_DIM]))
        between()
    for h in range(MEM_HEADS):
        s = mem_s[h] * scale
        m = jnp.max(s, axis=-1, keepdims=True)
        p = jnp.exp(s - m)
        mem_l.append(jnp.sum(p, axis=-1, keepdims=True))
        mem_o.append(_dot(p.astype(BF16), mv_ref[:, h * HEAD_DIM:(h + 1) * HEAD_DIM]))
        between()
    for h in range(MEM_HEADS):
        o = mem_o[h] * (1.0 / mem_l[h])
        gate = _silu(z_ref[:, O_CG + h * HEAD_DIM:O_CG + (h + 1) * HEAD_DIM])
        col = SWA_W + ML_W + h * HEAD_DIM
        y_ref[:, col:col + HEAD_DIM] = (o * gate).astype(BF16)

    lane = lax.broadcasted_iota(jnp.int32, (1, GATE_LANES), 1)
    zg = zg_ref[...]
    gates = jnp.where(lane < LANE_F, zg, _log_sigmoid(zg))
    gates = jnp.where(lane < 2 * ML_HEADS, gates, 0.0)
    rr = lax.broadcasted_iota(jnp.int32, (L, L), 0)
    cc = lax.broadcasted_iota(jnp.int32, (L, L), 1)
    causal = rr >= cc
    tri = jnp.where(causal, 1.0, 0.0).astype(BF16)
    csum = _exact_sel_dot(tri, gates)
    gates_t = gates.T
    csum_t = csum.T
    half_lo = lane < ML_DQK
    between()
    heads = []
    for h in range(ML_HEADS):
        pair, half = divmod(h, 2)
        hd = dict(pair=pair, half=half)
        hmask = half_lo if half == 0 else jnp.logical_not(half_lo)
        q_pair = z_ref[:, O_MQ + pair * 128:O_MQ + (pair + 1) * 128]
        hd["k_pair"] = z_ref[:, O_MK + pair * 128:O_MK + (pair + 1) * 128] * (ML_DQK ** -0.5)
        hd["v"] = z_ref[:, O_MV + h * ML_DV:O_MV + (h + 1) * ML_DV].astype(BF16)
        hd["q"] = jnp.where(hmask, q_pair, 0.0)
        q_hb = hd["q"].astype(BF16)
        hd["bt_c"] = csum[:, LANE_F + h:LANE_F + h + 1]
        hd["it_c"] = gates[:, LANE_I + h:LANE_I + h + 1]
        hd["m_prev"] = m_ref[:, h:h + 1]
        ct_pair = ct_ref[pair * 128:(pair + 1) * 128, :]
        hd["qk"] = _dot_nt(q_hb, hd["k_pair"].astype(BF16))
        hd["state_read"] = _dot(q_hb, ct_pair.astype(BF16))
        heads.append(hd)
        between()
    for h, hd in enumerate(heads):
        bt_r = csum_t[LANE_F + h:LANE_F + h + 1, :]
        it_r = gates_t[LANE_I + h:LANE_I + h + 1, :]
        dlog = jnp.where(causal, hd["bt_c"] - bt_r + it_r, -jnp.inf)
        inter = hd["bt_c"] + hd["m_prev"]
        hd["m_t"] = jnp.maximum(inter, jnp.max(dlog, axis=-1, keepdims=True))
        hd["w_state"] = jnp.exp(inter - hd["m_t"])
        a = jnp.exp(dlog - hd["m_t"]) * hd["qk"]
        hd["a_sum"] = jnp.sum(a, axis=-1, keepdims=True)
        hd["intra"] = _dot(a.astype(BF16), hd["v"])
        between()
    for h, hd in enumerate(heads):
        num = hd["intra"] + hd["w_state"] * hd["state_read"]
        n_pair = n_ref[:, hd["pair"] * 128:(hd["pair"] + 1) * 128]
        den = hd["a_sum"] + hd["w_state"] * jnp.sum(hd["q"] * n_pair, axis=-1, keepdims=True)
        denom = jnp.maximum(jnp.abs(den), jnp.exp(-hd["m_t"]))
        hid = num * (1.0 / denom)
        o_gate = _sigmoid(z_ref[:, O_MO + h * ML_DV:O_MO + (h + 1) * ML_DV])
        gate = _silu(z_ref[:, O_MG + h * ML_DV:O_MG + (h + 1) * ML_DV])
        col = SWA_W + h * ML_DV
        y_ref[:, col:col + ML_DV] = ((hid * o_gate) * gate).astype(BF16)

        hd["m_new"] = hd["m_t"][L - 1:L, :]
        bt_last = hd["bt_c"][L - 1:L, :]
        w_s = jnp.exp(bt_last - hd["bt_c"] + hd["it_c"] - hd["m_new"])
        hd["decay"] = jnp.exp(bt_last + hd["m_prev"] - hd["m_new"])
        kw = hd["k_pair"] * w_s
        hd["ksum"] = jnp.sum(kw, axis=0, keepdims=True)
        hd["upd"] = _dot(kw.T.astype(BF16), hd["v"])
        between()
    for h, hd in enumerate(heads):
        half, decay = hd["half"], hd["decay"]
        rows = slice(h * ML_DQK, (h + 1) * ML_DQK)
        ct_ref[rows, :] = decay * ct_ref[rows, :] + hd["upd"][half * ML_DQK:(half + 1) * ML_DQK, :]
        n_ref[:, rows] = decay * n_ref[:, rows] + hd["ksum"][:, half * ML_DQK:(half + 1) * ML_DQK]
        m_ref[:, h:h + 1] = hd["m_new"]


def _prompt(sinks, x2d, g_norm, w, b, wg, bg, rope_c, rope_i, mk, mv, w_out, g_final):
    n = x2d.shape[0]
    nc = n // CHUNK
    const = lambda c, s: (0, 0)
    resident = pl.Buffered(1)
    assert nc % 2 == 0
    n_pairs = nc // 2
    x_in = lambda p, s: (jnp.minimum(p, n_pairs - 1), 0)
    x_res = lambda p, s: (jnp.clip(p - 1, 0, n_pairs - 1), 0)
    return pl.pallas_call(
        _prompt_kernel,
        grid_spec=pltpu.PrefetchScalarGridSpec(
            num_scalar_prefetch=1,
            grid=(n_pairs + 1,),
            in_specs=[
                pl.BlockSpec((2 * CHUNK, D_MODEL), x_in),
                pl.BlockSpec((2 * CHUNK, D_MODEL), x_res),
                pl.BlockSpec((1, D_MODEL), const, pipeline_mode=resident),
                pl.BlockSpec((D_MODEL, D_MAIN), const, pipeline_mode=resident),
                pl.BlockSpec((1, D_MAIN), const, pipeline_mode=resident),
                pl.BlockSpec((D_MODEL, GATE_LANES), const, pipeline_mode=resident),
                pl.BlockSpec((1, GATE_LANES), const, pipeline_mode=resident),
                pl.BlockSpec((2, nc, HEAD_DIM), lambda c, s: (0, 0, 0), pipeline_mode=resident),
                pl.BlockSpec((4, CHUNK, HEAD_DIM), lambda c, s: (0, 0, 0), pipeline_mode=resident),
                pl.BlockSpec((MEM_LEN, MEM_W), const, pipeline_mode=resident),
                pl.BlockSpec((MEM_LEN, MEM_W), const, pipeline_mode=resident),
                pl.BlockSpec((D_MIX, D_MODEL), const, pipeline_mode=resident),
                pl.BlockSpec((1, D_MODEL), const, pipeline_mode=resident),
            ],
            out_specs=[
                pl.BlockSpec((2 * CHUNK, D_MODEL), x_res),
                pl.BlockSpec((CHUNK, SWA_KV_W), const),
                pl.BlockSpec((CHUNK, SWA_KV_W), const),
                pl.BlockSpec((ML_QK_W, ML_DV), const),
                pl.BlockSpec((1, ML_QK_W), const),
                pl.BlockSpec((1, GATE_LANES), const),
            ],
            scratch_shapes=[
                pltpu.VMEM((CHUNK, SWA_KV_W), BF16),
                pltpu.VMEM((CHUNK, SWA_KV_W), BF16),
                pltpu.VMEM((CHUNK, D_MODEL), BF16),
                pltpu.VMEM((CHUNK, D_MAIN), F32),
                pltpu.VMEM((CHUNK, GATE_LANES), F32),
                pltpu.VMEM((CHUNK, D_MAIN), F32),
                pltpu.VMEM((CHUNK, GATE_LANES), F32),
                pltpu.VMEM((CHUNK, D_MIX), BF16),
                pltpu.VMEM((CHUNK, D_MIX), BF16),
            ],
        ),
        out_shape=[
            jax.ShapeDtypeStruct((n, D_MODEL), F32),
            jax.ShapeDtypeStruct((CHUNK, SWA_KV_W), F32),
            jax.ShapeDtypeStruct((CHUNK, SWA_KV_W), F32),
            jax.ShapeDtypeStruct((ML_QK_W, ML_DV), F32),
            jax.ShapeDtypeStruct((1, ML_QK_W), F32),
            jax.ShapeDtypeStruct((1, GATE_LANES), F32),
        ],
        compiler_params=pltpu.CompilerParams(
            dimension_semantics=("arbitrary",), vmem_limit_bytes=VMEM_LIMIT),
        name="prompt",
    )(sinks, x2d, x2d, g_norm, w, b, wg, bg, rope_c, rope_i, mk, mv, w_out, g_final)


OUT_TM = 256


def _outproj_kernel(y_ref, x_ref, w_ref, g_ref, o_ref):
    acc = _dot(y_ref[...], w_ref[...]) + x_ref[...]
    ms = jnp.mean(acc * acc, axis=-1, keepdims=True)
    o_ref[...] = (acc * lax.rsqrt(ms + EPS)) * g_ref[...]


def _outproj(y, x2d, w, g):
    n = x2d.shape[0]
    return pl.pallas_call(
        _outproj_kernel,
        grid=(n // OUT_TM,),
        in_specs=[
            pl.BlockSpec((OUT_TM, D_MIX), lambda i: (i, 0)),
            pl.BlockSpec((OUT_TM, D_MODEL), lambda i: (i, 0)),
            pl.BlockSpec((D_MIX, D_MODEL), lambda i: (0, 0)),
            pl.BlockSpec((1, D_MODEL), lambda i: (0, 0)),
        ],
        out_specs=pl.BlockSpec((OUT_TM, D_MODEL), lambda i: (i, 0)),
        out_shape=jax.ShapeDtypeStruct((n, D_MODEL), F32),
        compiler_params=pltpu.CompilerParams(
            dimension_semantics=("arbitrary",), vmem_limit_bytes=VMEM_LIMIT),
        name="outproj",
    )(y, x2d, w, g)


SB = 16
SR = SB * DEC_SEQ
SEQ_UNROLL = 4
SWA_UNROLL = 8


def _mix_sample_kernel(sinks_ref, z_ref, zg_ref, cos_ref, sin_ref, ck_ref, cv_ref,
                       c_ref, nst_ref, nrep_ref, mrep_ref,
                       y_ref, ko_ref, vo_ref, co_ref, no_ref, mo_ref,
                       q_s, k_s, o_s):
    T = DEC_SEQ
    scale = HEAD_DIM ** -0.5
    cos = cos_ref[...]
    sin = sin_ref[...]

    for h in range(SWA_HEADS):
        q_s[:, h * HEAD_DIM:(h + 1) * HEAD_DIM] = _rope(
            z_ref[:, O_SQ + h * HEAD_DIM:O_SQ + (h + 1) * HEAD_DIM], cos, sin)
    for kv in range(SWA_KV_HEADS):
        k_s[:, kv * HEAD_DIM:(kv + 1) * HEAD_DIM] = _rope(
            z_ref[:, O_SK + kv * HEAD_DIM:O_SK + (kv + 1) * HEAD_DIM], cos, sin)

    KP = 2 * WINDOW
    rt = lax.broadcasted_iota(jnp.int32, (GQA_GROUP * T, KP), 0) & (T - 1)
    cj = lax.broadcasted_iota(jnp.int32, (GQA_GROUP * T, KP), 1)
    mask = (cj > rt) & (cj <= rt + WINDOW)
    kpad = jnp.zeros((KP - WINDOW - T, HEAD_DIM), F32)
    NKV = SWA_KV_HEADS
    SEQ_ROWS = WINDOW * NKV

    keep = SEQ_ROWS - T * NKV
    sink_cols = [jnp.concatenate([jnp.full((T, 1), sinks_ref[kv * GQA_GROUP + g], F32)
                                  for g in range(GQA_GROUP)], axis=0) for kv in range(NKV)]

    def per_group(i, carry):
        chains = [(i * SWA_UNROLL + j, kv) for j in range(SWA_UNROLL) for kv in range(NKV)]
        scores, values = [], []
        for b, kv in chains:
            rows = pl.ds(pl.multiple_of(b * T, T), T)
            base = pl.multiple_of(b * SEQ_ROWS, SEQ_ROWS)
            cols = slice(kv * HEAD_DIM, (kv + 1) * HEAD_DIM)
            k_new = k_s[rows, cols]
            v_new = z_ref[rows, O_SV + kv * HEAD_DIM:O_SV + (kv + 1) * HEAD_DIM]
            if kv == 0:
                ko_ref[pl.ds(base, keep), :] = ck_ref[pl.ds(base + T * NKV, keep), :]
                vo_ref[pl.ds(base, keep), :] = cv_ref[pl.ds(base + T * NKV, keep), :]
            ko_ref[pl.ds(base + keep + kv, T, stride=NKV), :] = k_new
            vo_ref[pl.ds(base + keep + kv, T, stride=NKV), :] = v_new
            kc = ck_ref[pl.ds(base + kv, WINDOW, stride=NKV), :]
            vc = cv_ref[pl.ds(base + kv, WINDOW, stride=NKV), :]
            k_all = jnp.concatenate([kc, k_new, kpad], axis=0).astype(BF16)
            values.append(jnp.concatenate([vc, v_new, kpad], axis=0).astype(BF16))
            q_st = jnp.concatenate([q_s[rows, (kv * GQA_GROUP + g) * HEAD_DIM:(kv * GQA_GROUP + g + 1) * HEAD_DIM]
                                    for g in range(GQA_GROUP)], axis=0).astype(BF16)
            scores.append(_dot_nt(q_st, k_all))
        probs, norms = [], []
        for (b, kv), s in zip(chains, scores):
            s = jnp.where(mask, s * scale, -jnp.inf)
            m = jnp.maximum(jnp.max(s, axis=-1, keepdims=True), sink_cols[kv])
            p = jnp.exp(s - m)
            norms.append(jnp.sum(p, axis=-1, keepdims=True) + jnp.exp(sink_cols[kv] - m))
            probs.append(p.astype(BF16))
        outs = [_dot(p, v) for p, v in zip(probs, values)]
        for (b, kv), o, l in zip(chains, outs, norms):
            rows = pl.ds(pl.multiple_of(b * T, T), T)
            o = o * (1.0 / l)
            for g in range(GQA_GROUP):
                h = kv * GQA_GROUP + g
                o_s[rows, h * HEAD_DIM:(h + 1) * HEAD_DIM] = o[g * T:(g + 1) * T]
        return carry

    lax.fori_loop(0, SB // SWA_UNROLL, per_group, 0)

    for h in range(SWA_HEADS):
        cols = slice(h * HEAD_DIM, (h + 1) * HEAD_DIM)
        gate = _silu(z_ref[:, O_SG + h * HEAD_DIM:O_SG + (h + 1) * HEAD_DIM])
        y_ref[:, cols] = (o_s[:, cols] * gate).astype(BF16)
    y_ref[:, SWA_W + ML_W:] = jnp.zeros((SR, MEM_W), BF16)

    R = SR
    lane = lax.broadcasted_iota(jnp.int32, (1, GATE_LANES), 1)
    zg = zg_ref[...]
    gates = jnp.where(lane < LANE_F, zg, _log_sigmoid(zg))
    gates = jnp.where(lane < 2 * ML_HEADS, gates, 0.0)
    rr = lax.broadcasted_iota(jnp.int32, (R, R), 0)
    cc = lax.broadcasted_iota(jnp.int32, (R, R), 1)
    same_seq = (rr >> 3) == (cc >> 3)
    causal = same_seq & (rr >= cc)
    tri = jnp.where(causal, 1.0, 0.0).astype(BF16)
    csum = _exact_sel_dot(tri, gates)
    gates_t = gates.T
    csum_t = csum.T
    half_lo = lane < ML_DQK
    seq_of_col = lax.broadcasted_iota(jnp.int32, (SB, 1, R), 2) >> 3
    seq_id = lax.broadcasted_iota(jnp.int32, (SB, 1, R), 0)
    own_cols = seq_of_col == seq_id
    own_blk = ((lax.broadcasted_iota(jnp.int32, (R, SB * 128), 0) >> 3)
               == (lax.broadcasted_iota(jnp.int32, (R, SB * 128), 1) >> 7))
    k_t = [(z_ref[:, O_MK + p * 128:O_MK + (p + 1) * 128] * (ML_DQK ** -0.5)).T for p in range(2)]
    mrep = mrep_ref[...]
    tok3 = lax.broadcasted_iota(jnp.int32, (SB, T, 1), 1)
    mo_ref[...] = jnp.zeros_like(mo_ref)

    def per_seq_value(col):
        return jnp.max(col.reshape(SB, T, 1), axis=1, keepdims=True)

    def last_of_seq(col):
        c3 = jnp.where(tok3 == T - 1, col.reshape(SB, T, 1), -jnp.inf)
        return jnp.broadcast_to(jnp.max(c3, axis=1, keepdims=True), (SB, T, 1)).reshape(R, 1)

    for h in range(ML_HEADS):
        pair, half = divmod(h, 2)
        hmask = half_lo if half == 0 else jnp.logical_not(half_lo)
        q_pair = z_ref[:, O_MQ + pair * 128:O_MQ + (pair + 1) * 128]
        k_pair = z_ref[:, O_MK + pair * 128:O_MK + (pair + 1) * 128] * (ML_DQK ** -0.5)
        v_f = z_ref[:, O_MV + h * ML_DV:O_MV + (h + 1) * ML_DV]
        v_h = v_f.astype(BF16)
        q_h = jnp.where(hmask, q_pair, 0.0)
        q_hb = q_h.astype(BF16)
        k_pb = k_pair.astype(BF16)

        bt_c = csum[:, LANE_F + h:LANE_F + h + 1]
        it_c = gates[:, LANE_I + h:LANE_I + h + 1]
        bt_r = csum_t[LANE_F + h:LANE_F + h + 1, :]
        it_r = gates_t[LANE_I + h:LANE_I + h + 1, :]
        m_prev = mrep[:, h:h + 1]

        dlog = jnp.where(causal, bt_c - bt_r + it_r, -jnp.inf)
        inter = bt_c + m_prev
        m_t = jnp.maximum(inter, jnp.max(dlog, axis=-1, keepdims=True))
        w_intra = jnp.exp(dlog - m_t)
        w_state = jnp.exp(inter - m_t)
        a = w_intra * _dot_nt(q_hb, k_pb)

        ct_st = c_ref[:, 2 * pair:2 * pair + 2].reshape(SB * 128, ML_DV).astype(BF16)
        q_blk = jnp.where(own_blk, jnp.tile(q_h, (1, SB)), 0.0).astype(BF16)
        num_state = _dot(q_blk, ct_st)

        num = _dot(a.astype(BF16), v_h) + w_state * num_state
        n_pair = nrep_ref[:, pair * 128:(pair + 1) * 128]
        den = jnp.sum(a, axis=-1, keepdims=True) + w_state * jnp.sum(q_h * n_pair, axis=-1, keepdims=True)
        denom = jnp.maximum(jnp.abs(den), jnp.exp(-m_t))
        hid = num * (1.0 / denom)
        o_gate = _sigmoid(z_ref[:, O_MO + h * ML_DV:O_MO + (h + 1) * ML_DV])
        gate = _silu(z_ref[:, O_MG + h * ML_DV:O_MG + (h + 1) * ML_DV])
        col = SWA_W + h * ML_DV
        y_ref[:, col:col + ML_DV] = ((hid * o_gate) * gate).astype(BF16)

        m_new = last_of_seq(m_t)
        bt_last = last_of_seq(bt_c)
        w_s = jnp.exp(bt_last - bt_c + it_c - m_new)
        decay = jnp.exp(bt_last + m_prev - m_new)
        decay_seq = per_seq_value(decay)

        kt_h = k_t[pair][half * ML_DQK:(half + 1) * ML_DQK, :]
        lhs = jnp.where(own_cols, kt_h[None, :, :], 0.0).reshape(SB * ML_DQK, R).astype(BF16)
        upd = _dot(lhs, (v_f * w_s).astype(BF16)).reshape(SB, ML_DQK, ML_DV)
        co_ref[:, h] = decay_seq * c_ref[:, h] + upd

        kw = (k_pair * w_s).reshape(SB, T, 128)
        ksum = jnp.sum(kw, axis=1)
        n_old = nst_ref[:, h * ML_DQK:(h + 1) * ML_DQK]
        dec2 = decay_seq.reshape(SB, 1)
        no_ref[:, h * ML_DQK:(h + 1) * ML_DQK] = dec2 * n_old + ksum[:, half * ML_DQK:(half + 1) * ML_DQK]
        mo_ref[:, h:h + 1] = per_seq_value(m_new).reshape(SB, 1)


def _mix_sample(sinks, z, zg, cos, sin, ck, cv, cst, nst, nrep, mrep):
    nb = cst.shape[0]
    steps = nb // SB
    const = lambda i, s: (0, 0)
    cache_rows = SB * WINDOW * SWA_KV_HEADS
    return pl.pallas_call(
        _mix_sample_kernel,
        grid_spec=pltpu.PrefetchScalarGridSpec(
            num_scalar_prefetch=1,
            grid=(steps,),
            in_specs=[
                pl.BlockSpec((SR, D_MAIN), lambda i, s: (i, 0)),
                pl.BlockSpec((SR, GATE_LANES), lambda i, s: (i, 0)),
                pl.BlockSpec((SR, HEAD_DIM), const),
                pl.BlockSpec((SR, HEAD_DIM), const),
                pl.BlockSpec((cache_rows, HEAD_DIM), lambda i, s: (i, 0)),
                pl.BlockSpec((cache_rows, HEAD_DIM), lambda i, s: (i, 0)),
                pl.BlockSpec((SB, ML_HEADS, ML_DQK, ML_DV), lambda i, s: (i, 0, 0, 0)),
                pl.BlockSpec((SB, ML_QK_W), lambda i, s: (i, 0)),
                pl.BlockSpec((SR, ML_QK_W), lambda i, s: (i, 0)),
                pl.BlockSpec((SR, GATE_LANES), lambda i, s: (i, 0)),
            ],
            out_specs=[
                pl.BlockSpec((SR, D_MIX), lambda i, s: (i, 0)),
                pl.BlockSpec((cache_rows, HEAD_DIM), lambda i, s: (i, 0)),
                pl.BlockSpec((cache_rows, HEAD_DIM), lambda i, s: (i, 0)),
                pl.BlockSpec((SB, ML_HEADS, ML_DQK, ML_DV), lambda i, s: (i, 0, 0, 0)),
                pl.BlockSpec((SB, ML_QK_W), lambda i, s: (i, 0)),
                pl.BlockSpec((SB, GATE_LANES), lambda i, s: (i, 0)),
            ],
            scratch_shapes=[
                pltpu.VMEM((SR, SWA_W), F32),
                pltpu.VMEM((SR, SWA_KV_W), F32),
                pltpu.VMEM((SR, SWA_W), F32),
            ],
        ),
        out_shape=[
            jax.ShapeDtypeStruct((nb * DEC_SEQ, D_MIX), BF16),
            jax.ShapeDtypeStruct(ck.shape, F32),
            jax.ShapeDtypeStruct(cv.shape, F32),
            jax.ShapeDtypeStruct((nb, ML_HEADS, ML_DQK, ML_DV), F32),
            jax.ShapeDtypeStruct((nb, ML_QK_W), F32),
            jax.ShapeDtypeStruct((nb, GATE_LANES), F32),
        ],
        compiler_params=pltpu.CompilerParams(
            dimension_semantics=("arbitrary",), vmem_limit_bytes=VMEM_LIMIT),
        name="mix_sample",
    )(sinks, z, zg, cos, sin, ck, cv, cst, nst, nrep, mrep)


MB = 16
MR = MB * DEC_SEQ


def _mem_sample_kernel(cq_ref, cg_ref, mk_ref, mv_ref, y_in_ref, y_ref, o_s):
    del y_in_ref
    T = DEC_SEQ
    scale = HEAD_DIM ** -0.5
    zpad = jnp.zeros((T, HEAD_DIM), F32)

    seq_rows = MEM_LEN * MEM_HEADS

    def per_group(i, carry):
        chains = [(i * SEQ_UNROLL + j, h) for j in range(SEQ_UNROLL) for h in range(MEM_HEADS)]
        scores = []
        for b, h in chains:
            rows = pl.ds(pl.multiple_of(b * T, T), T)
            base = pl.multiple_of(b * seq_rows, seq_rows)
            kb = mk_ref[pl.ds(base + h, MEM_LEN, stride=MEM_HEADS), :].astype(BF16)
            q = jnp.concatenate([cq_ref[rows, h * HEAD_DIM:(h + 1) * HEAD_DIM], zpad], axis=0).astype(BF16)
            scores.append(_dot_nt(q, kb))
        probs, norms = [], []
        for s in scores:
            s = s * scale
            p = jnp.exp(s - jnp.max(s, axis=-1, keepdims=True))
            norms.append(jnp.sum(p, axis=-1, keepdims=True))
            probs.append(p.astype(BF16))
        outs = []
        for (b, h), p in zip(chains, probs):
            base = pl.multiple_of(b * seq_rows, seq_rows)
            vb = mv_ref[pl.ds(base + h, MEM_LEN, stride=MEM_HEADS), :].astype(BF16)
            outs.append(_dot(p, vb))
        for (b, h), o, l in zip(chains, outs, norms):
            rows = pl.ds(pl.multiple_of(b * T, T), T)
            o_s[rows, h * HEAD_DIM:(h + 1) * HEAD_DIM] = (o * (1.0 / l))[0:T]
        return carry

    lax.fori_loop(0, MB // SEQ_UNROLL, per_group, 0)
    y_ref[...] = (o_s[...] * _silu(cg_ref[...])).astype(BF16)


def _mem_sample(z, mk, mv, y):
    nb = z.shape[0] // DEC_SEQ
    cache_rows = MB * MEM_LEN * MEM_HEADS
    return pl.pallas_call(
        _mem_sample_kernel,
        grid=(nb // MB,),
        in_specs=[
            pl.BlockSpec((MR, MEM_W), lambda i: (i, O_CQ // MEM_W)),
            pl.BlockSpec((MR, MEM_W), lambda i: (i, O_CG // MEM_W)),
            pl.BlockSpec((cache_rows, HEAD_DIM), lambda i: (i, 0)),
            pl.BlockSpec((cache_rows, HEAD_DIM), lambda i: (i, 0)),
            pl.BlockSpec(memory_space=pl.ANY),
        ],
        out_specs=pl.BlockSpec((MR, MEM_W), lambda i: (i, (SWA_W + ML_W) // MEM_W)),
        out_shape=jax.ShapeDtypeStruct(y.shape, y.dtype),
        input_output_aliases={4: 0},
        scratch_shapes=[pltpu.VMEM((MR, MEM_W), F32)],
        compiler_params=pltpu.CompilerParams(
            dimension_semantics=("arbitrary",), vmem_limit_bytes=VMEM_LIMIT),
        name="mem_sample",
    )(z, z, mk, mv, y)


def _rope_cos_sin(pos):
    half = HEAD_DIM // 2
    inv = jnp.power(ROPE_THETA, -(jnp.arange(half, dtype=F32) * 2.0 / HEAD_DIM))
    ang = pos.astype(F32)[:, None] * inv[None, :]
    return jnp.cos(ang), jnp.sin(ang)


def _rope_tables(pos):
    cos, sin = _rope_cos_sin(pos)
    return jnp.concatenate([cos, cos], axis=-1), jnp.concatenate([-sin, sin], axis=-1)


def _rope_split_tables(n_chunks):
    ca, sa = _rope_cos_sin(jnp.arange(n_chunks, dtype=jnp.int32) * CHUNK)
    cb, sb = _rope_cos_sin(jnp.arange(CHUNK, dtype=jnp.int32))
    dup = lambda t: jnp.concatenate([t, t], axis=-1)
    sgn = lambda t: jnp.concatenate([-t, t], axis=-1)
    return jnp.stack([dup(ca), dup(sa)]), jnp.stack([dup(cb), dup(sb), sgn(cb), sgn(sb)])


def _relayout_in_proj(w_in, b_in):
    w_t = jnp.swapaxes(w_in, 0, 1)
    w_main = _wprep(w_t)
    w_gate = _wgate(w_t)
    b_main = jnp.concatenate([b_in[_R_SQ:_R_MI], b_in[_R_MO:_R_END]])[None, :]
    b_gate = jnp.pad(b_in[_R_MI:_R_MO], (0, GATE_LANES - 2 * ML_HEADS))[None, :]
    return w_main, b_main, w_gate, b_gate


def _layer(xp, xs, mem, ck, cv, c_st, n_st, m_st, cmk, cmv,
           g_norm, w_in, b_in, sinks, g_mem, w_mem_kv, w_out, g_final):
    bp, sp, _ = xp.shape
    bs, ts, _ = xs.shape
    xp2 = xp.reshape(bp * sp, D_MODEL)
    xs2 = xs.reshape(bs * ts, D_MODEL)
    w_main, b_main, w_gate, b_gate = _relayout_in_proj(w_in, b_in)
    g_norm2 = g_norm[None, :]
    w_out_b = w_out.astype(BF16)
    g_final2 = g_final[None, :]
    sinks_flat = sinks.reshape(SWA_HEADS)

    memkv = _memkv(mem.reshape(MEM_LEN, D_MODEL), g_mem[None, :], w_mem_kv.astype(BF16))
    mem_k = memkv[:, :MEM_W]
    mem_v = memkv[:, MEM_W:]
    rope_c, rope_i = _rope_split_tables(sp // CHUNK)
    out_p, pk, pv, ct, n_p, m_p = _prompt(sinks_flat, xp2, g_norm2, w_main, b_main, w_gate, b_gate,
                                           rope_c, rope_i, mem_k.astype(BF16), mem_v.astype(BF16),
                                           w_out_b, g_final2)

    zs, zgs = _proj(xs2, g_norm2, w_main, b_main, w_gate, b_gate)
    cos_s, sin_s = _rope_tables(PAST_LEN + jnp.arange(ts, dtype=jnp.int32))
    cos_s = jnp.tile(cos_s, (SB, 1))
    sin_s = jnp.tile(sin_s, (SB, 1))
    nrep = jnp.repeat(n_st.reshape(bs, ML_QK_W), ts, axis=0)
    m_pad = jnp.pad(m_st, ((0, 0), (0, GATE_LANES - ML_HEADS)))
    mrep = jnp.repeat(m_pad, ts, axis=0)
    ya, sk_o, sv_o, ct_o, n_o, m_o = _mix_sample(
        sinks_flat, zs, zgs, cos_s, sin_s,
        ck.reshape(bs * WINDOW * SWA_KV_HEADS, HEAD_DIM), cv.reshape(bs * WINDOW * SWA_KV_HEADS, HEAD_DIM),
        jnp.swapaxes(c_st, -1, -2), n_st.reshape(bs, ML_QK_W), nrep, mrep)
    ys = _mem_sample(zs, cmk.reshape(bs * MEM_LEN * MEM_HEADS, HEAD_DIM),
                     cmv.reshape(bs * MEM_LEN * MEM_HEADS, HEAD_DIM), ya)
    out_s = _outproj(ys, xs2, w_out_b, g_final2)
    c_o = jnp.swapaxes(ct_o, -1, -2)

    p_state = (
        pk.reshape(bp, WINDOW, SWA_KV_HEADS, HEAD_DIM),
        pv.reshape(bp, WINDOW, SWA_KV_HEADS, HEAD_DIM),
        ct.reshape(ML_HEADS, ML_DQK, ML_DV).transpose(0, 2, 1)[None],
        n_p.reshape(bp, ML_HEADS, ML_DQK),
        m_p[:, :ML_HEADS],
        mem_k.reshape(bp, MEM_LEN, MEM_HEADS, HEAD_DIM),
        mem_v.reshape(bp, MEM_LEN, MEM_HEADS, HEAD_DIM),
    )
    s_state = (
        sk_o.reshape(bs, WINDOW, SWA_KV_HEADS, HEAD_DIM),
        sv_o.reshape(bs, WINDOW, SWA_KV_HEADS, HEAD_DIM),
        c_o,
        n_o.reshape(bs, ML_HEADS, ML_DQK),
        m_o[:, :ML_HEADS],
    )
    return out_p.reshape(bp, sp, D_MODEL), out_s.reshape(bs, ts, D_MODEL), p_state, s_state


def kernel(x_prompt, x_sample, mem_prompt, cache_swa_k, cache_swa_v, state_mlstm_C, state_mlstm_n,
           state_mlstm_m, cache_mem_k, cache_mem_v, g_norm, w_in, b_in, swa_sinks, g_mem, w_mem_kv,
           w_out, g_final):
    depth = g_norm.shape[0]
    assert depth == 1 and x_prompt.shape[0] == 1
    y_p, y_s, p_state, s_state = _layer(
        x_prompt, x_sample, mem_prompt, cache_swa_k[0], cache_swa_v[0], state_mlstm_C[0],
        state_mlstm_n[0], state_mlstm_m[0], cache_mem_k[0], cache_mem_v[0],
        g_norm[0], w_in[0], b_in[0], swa_sinks[0], g_mem[0], w_mem_kv[0], w_out[0], g_final)
    return (y_p, y_s) + tuple(s[None] for s in p_state) + tuple(s[None] for s in s_state)
```

```python
import functools

import jax
import jax.numpy as jnp
from jax import lax
from jax.experimental import pallas as pl
from jax.experimental.pallas import tpu as pltpu

F32 = jnp.float32
BF16 = jnp.bfloat16

D_MODEL = 2048
HEAD_DIM = 128
SWA_HEADS = 8
SWA_KV_HEADS = 2
GQA_GROUP = 4
WINDOW = 128
ML_HEADS = 4
ML_DQK = 64
ML_DV = 128
MEM_HEADS = 4
MEM_LEN = 256
CHUNK = 128
ROPE_THETA = 10000.0
EPS = 1e-6
PAST_LEN = 16384
DEC_SEQ = 8

SWA_W = SWA_HEADS * HEAD_DIM
SWA_KV_W = SWA_KV_HEADS * HEAD_DIM
ML_W = ML_HEADS * ML_DV
ML_QK_W = ML_HEADS * ML_DQK
MEM_W = MEM_HEADS * HEAD_DIM
D_MIX = SWA_W + ML_W + MEM_W

_IN_WIDTHS = (SWA_W, SWA_KV_W, SWA_KV_W, SWA_W, ML_QK_W, ML_QK_W, ML_W, ML_HEADS, ML_HEADS, ML_W, ML_W, MEM_W, MEM_W)
_IN_OFFS = [0]
for _w in _IN_WIDTHS:
    _IN_OFFS.append(_IN_OFFS[-1] + _w)
(_R_SQ, _R_SK, _R_SV, _R_SG, _R_MQ, _R_MK, _R_MV, _R_MI, _R_MF, _R_MO, _R_MG, _R_CQ, _R_CG, _R_END) = _IN_OFFS

O_SQ = 0
O_SK = O_SQ + SWA_W
O_SV = O_SK + SWA_KV_W
O_SG = O_SV + SWA_KV_W
O_MQ = O_SG + SWA_W
O_MK = O_MQ + ML_QK_W
O_MV = O_MK + ML_QK_W
O_MO = O_MV + ML_W
O_MG = O_MO + ML_W
O_CQ = O_MG + ML_W
O_CG = O_CQ + MEM_W
D_MAIN = O_CG + MEM_W
W_A_COLS = O_MO
GATE_LANES = 128
LANE_I = 0
LANE_F = ML_HEADS

VMEM_LIMIT = 58 * 1024 * 1024

_NT = (((1,), (1,)), ((), ()))


def _dot(a, b):
    return jnp.dot(a, b, preferred_element_type=F32)


def _dot_nt(a, b):
    return lax.dot_general(a, b, _NT, preferred_element_type=F32)


def _exact_sel_dot(sel_bf16, x):
    hi = x.astype(BF16)
    r1 = x - hi.astype(F32)
    mid = r1.astype(BF16)
    lo = (r1 - mid.astype(F32)).astype(BF16)
    return _dot(sel_bf16, hi) + _dot(sel_bf16, mid) + _dot(sel_bf16, lo)


def _silu(x):
    h = 0.5 * x
    return h + h * jnp.tanh(h)


def _sigmoid(x):
    return 0.5 + 0.5 * jnp.tanh(0.5 * x)


def _log_sigmoid(x):
    return jnp.minimum(x, 0.0) - jnp.log1p(jnp.exp(-jnp.abs(x)))


PROJ_TM = 1024
PROJ_TN = 512
NORM_ROWS = 256


def _proj_kernel(x_ref, g_ref, w_ref, b_ref, wg_ref, bg_ref, z_ref, zg_ref, u_ref):
    j = pl.program_id(1)

    @pl.when(j == 0)
    def _():
        g = g_ref[...]
        for r in range(PROJ_TM // NORM_ROWS):
            rows = pl.ds(r * NORM_ROWS, NORM_ROWS)
            xf = x_ref[rows, :]
            ms = jnp.mean(xf * xf, axis=-1, keepdims=True)
            u_ref[rows, :] = ((xf * lax.rsqrt(ms + EPS)) * g).astype(BF16)
        zg_ref[...] = _dot(u_ref[...], wg_ref[...]) + bg_ref[...]

    z_ref[...] = _dot(u_ref[...], w_ref[...]) + b_ref[...]


def _proj(x2d, g, w, b, wg, bg):
    n = x2d.shape[0]
    grid = (n // PROJ_TM, D_MAIN // PROJ_TN)
    return pl.pallas_call(
        _proj_kernel,
        grid=grid,
        in_specs=[
            pl.BlockSpec((PROJ_TM, D_MODEL), lambda i, j: (i, 0)),
            pl.BlockSpec((1, D_MODEL), lambda i, j: (0, 0)),
            pl.BlockSpec((D_MODEL, PROJ_TN), lambda i, j: (0, j)),
            pl.BlockSpec((1, PROJ_TN), lambda i, j: (0, j)),
            pl.BlockSpec((D_MODEL, GATE_LANES), lambda i, j: (0, 0)),
            pl.BlockSpec((1, GATE_LANES), lambda i, j: (0, 0)),
        ],
        out_specs=[
            pl.BlockSpec((PROJ_TM, PROJ_TN), lambda i, j: (i, j)),
            pl.BlockSpec((PROJ_TM, GATE_LANES), lambda i, j: (i, 0)),
        ],
        out_shape=[
            jax.ShapeDtypeStruct((n, D_MAIN), F32),
            jax.ShapeDtypeStruct((n, GATE_LANES), F32),
        ],
        scratch_shapes=[pltpu.VMEM((PROJ_TM, D_MODEL), BF16)],
        compiler_params=pltpu.CompilerParams(
            dimension_semantics=("arbitrary", "arbitrary"), vmem_limit_bytes=VMEM_LIMIT),
        name="proj",
    )(x2d, g, w, b, wg, bg)


WPREP_TN = 512


def _wprep_kernel(wt_ref, o_ref):
    o_ref[...] = wt_ref[...].T.astype(BF16)


def _wprep(w_t):
    tiles_a = W_A_COLS // WPREP_TN
    gate_rows = _R_MO - W_A_COLS

    def src_row(j):
        return pl.multiple_of(j * WPREP_TN + jnp.where(j >= tiles_a, gate_rows, 0), 8)

    return pl.pallas_call(
        _wprep_kernel,
        grid=(D_MAIN // WPREP_TN,),
        in_specs=[pl.BlockSpec((pl.Element(WPREP_TN), pl.Element(D_MODEL)), lambda j: (src_row(j), 0))],
        out_specs=pl.BlockSpec((D_MODEL, WPREP_TN), lambda j: (0, j)),
        out_shape=jax.ShapeDtypeStruct((D_MODEL, D_MAIN), BF16),
        compiler_params=pltpu.CompilerParams(
            dimension_semantics=("arbitrary",), vmem_limit_bytes=VMEM_LIMIT),
        name="wprep",
    )(w_t)


def _wgate_kernel(wt_ref, o_ref):
    rows = jnp.concatenate(
        [wt_ref[...], jnp.zeros((GATE_LANES - 2 * ML_HEADS, D_MODEL), F32)], axis=0)
    o_ref[...] = rows.T.astype(BF16)


def _wgate(w_t):
    return pl.pallas_call(
        _wgate_kernel,
        grid=(1,),
        in_specs=[pl.BlockSpec((pl.Element(2 * ML_HEADS), pl.Element(D_MODEL)), lambda j: (_R_MI, 0))],
        out_specs=pl.BlockSpec((D_MODEL, GATE_LANES), lambda j: (0, 0)),
        out_shape=jax.ShapeDtypeStruct((D_MODEL, GATE_LANES), BF16),
        name="wgate",
    )(w_t)


MEMKV_TN = 256


def _memkv_kernel(mem_ref, g_ref, w_ref, o_ref):
    xf = mem_ref[...]
    ms = jnp.mean(xf * xf, axis=-1, keepdims=True)
    u = ((xf * lax.rsqrt(ms + EPS)) * g_ref[...]).astype(BF16)
    o_ref[...] = _dot(u, w_ref[...])


def _memkv(mem2d, g, w):
    return pl.pallas_call(
        _memkv_kernel,
        grid=(2 * MEM_W // MEMKV_TN,),
        in_specs=[
            pl.BlockSpec((MEM_LEN, D_MODEL), lambda j: (0, 0)),
            pl.BlockSpec((1, D_MODEL), lambda j: (0, 0)),
            pl.BlockSpec((D_MODEL, MEMKV_TN), lambda j: (0, j)),
        ],
        out_specs=pl.BlockSpec((MEM_LEN, MEMKV_TN), lambda j: (0, j)),
        out_shape=jax.ShapeDtypeStruct((MEM_LEN, 2 * MEM_W), F32),
        compiler_params=pltpu.CompilerParams(
            dimension_semantics=("arbitrary",), vmem_limit_bytes=VMEM_LIMIT),
        name="memkv",
    )(mem2d, g, w)


OUT_TN = 256


def _rope(x, cos, sin_signed):
    return x * cos + pltpu.roll(x, HEAD_DIM // 2, axis=1) * sin_signed


IN_TN = 256
SWA_POINTS = 3 * SWA_KV_HEADS
MEM_POINTS = 2 * MEM_HEADS
MIXER_WEIGHTS = ([1.0, 3.0, 1.0] * SWA_KV_HEADS + [1.0] * MEM_HEADS + [0.5] * MEM_HEADS
                 + [1.0] + [1.5] * ML_HEADS + [1.0] * ML_HEADS + [0.5] * ML_HEADS)


def _spread(tasks, weights):
    total = sum(weights)
    bounds = [0]
    acc = 0.0
    for w in weights:
        acc += w
        bounds.append(round(len(tasks) * acc / total))
    return [tasks[a:b] for a, b in zip(bounds[:-1], bounds[1:])]


def _zip_then_rest(a, b):
    k = min(len(a), len(b))
    return [t for pair in zip(a[:k], b[:k]) for t in pair] + a[k:] + b[k:]


def _prompt_kernel(*refs):
    pair = pl.program_id(0)
    n_pairs = pl.num_programs(0) - 1

    carried = refs[_PROMPT_CARRIED]

    @pl.when(pair == 0)
    def _():
        for ref in carried:
            ref[...] = jnp.zeros_like(ref)
        _prompt_step(0, 0, True, False, False, *refs)
        _prompt_step(1, 1, True, True, False, *refs)

    @pl.when((pair > 0) & (pair < n_pairs))
    def _():
        _prompt_step(2 * pair, 0, True, True, True, *refs)
        _prompt_step(2 * pair + 1, 1, True, True, True, *refs)

    @pl.when(pair == n_pairs)
    def _():
        _prompt_step(2 * pair, 0, False, True, True, *refs)
        _prompt_step(2 * pair + 1, 1, False, False, True, *refs)


_PROMPT_CARRIED = slice(17, 22)


def _prompt_step(s, half, do_in, do_mix, do_out,
                 sinks_ref, xin2_ref, xres2_ref, gn_ref, w_ref, b_ref, wg_ref, bg_ref,
                 rope_c_ref, rope_i_ref, mk_ref, mv_ref, wout_ref, gfin_ref,
                 out2_ref, ko_ref, vo_ref, ct_ref, n_ref, m_ref,
                 kprev_ref, vprev_ref, u_ref, z_ref, zg_ref, zprev_ref, zgprev_ref,
                 y_ref, yprev_ref):
    rows = slice(half * CHUNK, (half + 1) * CHUNK)
    xin_ref = xin2_ref.at[rows]
    xres_ref = xres2_ref.at[rows]
    out_ref = out2_ref.at[rows]
    n_in = D_MAIN // IN_TN
    n_out = D_MODEL // OUT_TN

    def in_norm():
        xf = xin_ref[...]
        ms = jnp.mean(xf * xf, axis=-1, keepdims=True)
        u_ref[...] = ((xf * lax.rsqrt(ms + EPS)) * gn_ref[...]).astype(BF16)

    def in_tile(t):
        if t == n_in:
            zg_ref[...] = _dot(u_ref[...], wg_ref[...]) + bg_ref[...]
        else:
            cols = slice(t * IN_TN, (t + 1) * IN_TN)
            z_ref[:, cols] = _dot(u_ref[...], w_ref[:, cols]) + b_ref[:, cols]

    def in_rotate(lo, hi):
        zprev_ref[:, lo:hi] = z_ref[:, lo:hi]

    def in_rotate_rest():
        in_rotate(O_MQ, O_CQ)
        zgprev_ref[...] = zg_ref[...]

    def out_tile(t):
        cols = slice(t * OUT_TN, (t + 1) * OUT_TN)
        out_ref[:, cols] = _dot(yprev_ref[...], wout_ref[:, cols]) + xres_ref[:, cols]

    def out_finish():
        acc = out_ref[...]
        ms = jnp.mean(acc * acc, axis=-1, keepdims=True)
        out_ref[...] = (acc * lax.rsqrt(ms + EPS)) * gfin_ref[...]

    tile_of = lambda col: col // IN_TN
    in_order = (list(range(0, tile_of(O_MQ))) + list(range(tile_of(O_CQ), n_in))
                + list(range(tile_of(O_MQ), tile_of(O_CQ))) + [n_in])
    in_tasks = []
    for t in in_order:
        in_tasks.append((functools.partial(in_tile, t), 0))
        if t == tile_of(O_MQ) - 1:
            in_tasks.append((functools.partial(in_rotate, 0, O_MQ), SWA_POINTS))
        if t == n_in - 1:
            in_tasks.append((functools.partial(in_rotate, O_CQ, D_MAIN), SWA_POINTS + MEM_POINTS))
    out_tasks = [(functools.partial(out_tile, t), 0) for t in range(n_out)] + [(out_finish, 0)]

    def mix(tasks):
        groups = iter(_spread(tasks, MIXER_WEIGHTS))
        points_done = [0]

        def between():
            for emit, first_point in next(groups):
                assert points_done[0] >= first_point
                emit()
            points_done[0] += 1

        _prompt_mixers(s - 1, sinks_ref, zprev_ref, zgprev_ref, rope_c_ref, rope_i_ref, mk_ref, mv_ref,
                       y_ref, ko_ref, vo_ref, ct_ref, n_ref, m_ref, kprev_ref, vprev_ref, between)
        assert next(groups, None) is None

    def run(tasks):
        for emit, _ in tasks:
            emit()

    if do_in:
        in_norm()
    if do_in and do_out:
        tasks = out_tasks[:4] + _zip_then_rest(in_tasks, out_tasks[4:])
    else:
        tasks = in_tasks if do_in else out_tasks
    if do_mix:
        mix(tasks)
    else:
        run(tasks)
    if do_in:
        in_rotate_rest()
    if do_mix:
        yprev_ref[...] = y_ref[...]


def _prompt_mixers(c, sinks_ref, z_ref, zg_ref, rope_c_ref, rope_i_ref, mk_ref, mv_ref,
                   y_ref, ko_ref, vo_ref, ct_ref, n_ref, m_ref, kprev_ref, vprev_ref, between):
    L = CHUNK
    scale = HEAD_DIM ** -0.5

    cc = rope_c_ref[0, pl.ds(c, 1), :]
    sc = rope_c_ref[1, pl.ds(c, 1), :]
    cos = cc * rope_i_ref[0] - sc * rope_i_ref[1]
    sin = sc * rope_i_ref[2] + cc * rope_i_ref[3]

    ri = lax.broadcasted_iota(jnp.int32, (GQA_GROUP * L, 2 * L), 0) & (L - 1)
    cj = lax.broadcasted_iota(jnp.int32, (GQA_GROUP * L, 2 * L), 1)
    j_low = jnp.where(c > 0, 0, L)
    band = (cj > ri) & (cj <= ri + L) & (cj >= j_low)
    for kv in range(SWA_KV_HEADS):
        between()
        k_new = _rope(z_ref[:, O_SK + kv * HEAD_DIM:O_SK + (kv + 1) * HEAD_DIM], cos, sin)
        v_new = z_ref[:, O_SV + kv * HEAD_DIM:O_SV + (kv + 1) * HEAD_DIM]
        ko_ref[:, kv * HEAD_DIM:(kv + 1) * HEAD_DIM] = k_new
        vo_ref[:, kv * HEAD_DIM:(kv + 1) * HEAD_DIM] = v_new
        k_new_b = k_new.astype(BF16)
        v_new_b = v_new.astype(BF16)
        kcat = jnp.concatenate([kprev_ref[:, kv * HEAD_DIM:(kv + 1) * HEAD_DIM], k_new_b], axis=0)
        vcat = jnp.concatenate([vprev_ref[:, kv * HEAD_DIM:(kv + 1) * HEAD_DIM], v_new_b], axis=0)
        kprev_ref[:, kv * HEAD_DIM:(kv + 1) * HEAD_DIM] = k_new_b
        vprev_ref[:, kv * HEAD_DIM:(kv + 1) * HEAD_DIM] = v_new_b
        qs = []
        sks = []
        for g in range(GQA_GROUP):
            h = kv * GQA_GROUP + g
            qs.append(_rope(z_ref[:, O_SQ + h * HEAD_DIM:O_SQ + (h + 1) * HEAD_DIM], cos, sin).astype(BF16))
            sks.append(jnp.full((L, 1), sinks_ref[h], F32))
        q_st = jnp.concatenate(qs, axis=0)
        sk = jnp.concatenate(sks, axis=0)
        s = _dot_nt(q_st, kcat)
        between()
        s = jnp.where(band, s * scale, -jnp.inf)
        m = jnp.maximum(jnp.max(s, axis=-1, keepdims=True), sk)
        p = jnp.exp(s - m)
        l = jnp.sum(p, axis=-1, keepdims=True) + jnp.exp(sk - m)
        o = _dot(p.astype(BF16), vcat)
        between()
        o = o * (1.0 / l)
        for g in range(GQA_GROUP):
            h = kv * GQA_GROUP + g
            gate = _silu(z_ref[:, O_SG + h * HEAD_DIM:O_SG + (h + 1) * HEAD_DIM])
            y_ref[:, h * HEAD_DIM:(h + 1) * HEAD_DIM] = (o[g * L:(g + 1) * L] * gate).astype(BF16)

    mem_s, mem_l, mem_o = [], [], []
    for h in range(MEM_HEADS):
        q = z_ref[:, O_CQ + h * HEAD_DIM:O_CQ + (h + 1) * HEAD_DIM].astype(BF16)
        mem_s.append(_dot_nt(q, mk_ref[:, h * HEAD_DIM:(h + 1) * HEAD_DIM]))
        between()
    for h in range(MEM_HEADS):
        s = mem_s[h] * scale
        m = jnp.max(s, axis=-1, keepdims=True)
        p = jnp.exp(s - m)
        mem_l.append(jnp.sum(p, axis=-1, keepdims=True))
        mem_o.append(_dot(p.astype(BF16), mv_ref[:, h * HEAD_DIM:(h + 1) * HEAD_DIM]))
        between()
    for h in range(MEM_HEADS):
        o = mem_o[h] * (1.0 / mem_l[h])
        gate = _silu(z_ref[:, O_CG + h * HEAD_DIM:O_CG + (h + 1) * HEAD_DIM])
        col = SWA_W + ML_W + h * HEAD_DIM
        y_ref[:, col:col + HEAD_DIM] = (o * gate).astype(BF16)

    lane = lax.broadcasted_iota(jnp.int32, (1, GATE_LANES), 1)
    zg = zg_ref[...]
    gates = jnp.where(lane < LANE_F, zg, _log_sigmoid(zg))
    gates = jnp.where(lane < 2 * ML_HEADS, gates, 0.0)
    rr = lax.broadcasted_iota(jnp.int32, (L, L), 0)
    cc = lax.broadcasted_iota(jnp.int32, (L, L), 1)
    causal = rr >= cc
    tri = jnp.where(causal, 1.0, 0.0).astype(BF16)
    csum = _exact_sel_dot(tri, gates)
    gates_t = gates.T
    csum_t = csum.T
    half_lo = lane < ML_DQK
    between()
    heads = []
    for h in range(ML_HEADS):
        pair, half = divmod(h, 2)
        hd = dict(pair=pair, half=half)
        hmask = half_lo if half == 0 else jnp.logical_not(half_lo)
        q_pair = z_ref[:, O_MQ + pair * 128:O_MQ + (pair + 1) * 128]
        hd["k_pair"] = z_ref[:, O_MK + pair * 128:O_MK + (pair + 1) * 128] * (ML_DQK ** -0.5)
        hd["v"] = z_ref[:, O_MV + h * ML_DV:O_MV + (h + 1) * ML_DV].astype(BF16)
        hd["q"] = jnp.where(hmask, q_pair, 0.0)
        q_hb = hd["q"].astype(BF16)
        hd["bt_c"] = csum[:, LANE_F + h:LANE_F + h + 1]
        hd["it_c"] = gates[:, LANE_I + h:LANE_I + h + 1]
        hd["m_prev"] = m_ref[:, h:h + 1]
        ct_pair = ct_ref[pair * 128:(pair + 1) * 128, :]
        hd["qk"] = _dot_nt(q_hb, hd["k_pair"].astype(BF16))
        hd["state_read"] = _dot(q_hb, ct_pair.astype(BF16))
        heads.append(hd)
        between()
    for h, hd in enumerate(heads):
        bt_r = csum_t[LANE_F + h:LANE_F + h + 1, :]
        it_r = gates_t[LANE_I + h:LANE_I + h + 1, :]
        dlog = jnp.where(causal, hd["bt_c"] - bt_r + it_r, -jnp.inf)
        inter = hd["bt_c"] + hd["m_prev"]
        hd["m_t"] = jnp.maximum(inter, jnp.max(dlog, axis=-1, keepdims=True))
        hd["w_state"] = jnp.exp(inter - hd["m_t"])
        a = jnp.exp(dlog - hd["m_t"]) * hd["qk"]
        hd["a_sum"] = jnp.sum(a, axis=-1, keepdims=True)
        hd["intra"] = _dot(a.astype(BF16), hd["v"])
        between()
    for h, hd in enumerate(heads):
        num = hd["intra"] + hd["w_state"] * hd["state_read"]
        n_pair = n_ref[:, hd["pair"] * 128:(hd["pair"] + 1) * 128]
        den = hd["a_sum"] + hd["w_state"] * jnp.sum(hd["q"] * n_pair, axis=-1, keepdims=True)
        denom = jnp.maximum(jnp.abs(den), jnp.exp(-hd["m_t"]))
        hid = num * (1.0 / denom)
        o_gate = _sigmoid(z_ref[:, O_MO + h * ML_DV:O_MO + (h + 1) * ML_DV])
        gate = _silu(z_ref[:, O_MG + h * ML_DV:O_MG + (h + 1) * ML_DV])
        col = SWA_W + h * ML_DV
        y_ref[:, col:col + ML_DV] = ((hid * o_gate) * gate).astype(BF16)

        hd["m_new"] = hd["m_t"][L - 1:L, :]
        bt_last = hd["bt_c"][L - 1:L, :]
        w_s = jnp.exp(bt_last - hd["bt_c"] + hd["it_c"] - hd["m_new"])
        hd["decay"] = jnp.exp(bt_last + hd["m_prev"] - hd["m_new"])
        kw = hd["k_pair"] * w_s
        hd["ksum"] = jnp.sum(kw, axis=0, keepdims=True)
        hd["upd"] = _dot(kw.T.astype(BF16), hd["v"])
        between()
    for h, hd in enumerate(heads):
        half, decay = hd["half"], hd["decay"]
        rows = slice(h * ML_DQK, (h + 1) * ML_DQK)
        ct_ref[rows, :] = decay * ct_ref[rows, :] + hd["upd"][half * ML_DQK:(half + 1) * ML_DQK, :]
        n_ref[:, rows] = decay * n_ref[:, rows] + hd["ksum"][:, half * ML_DQK:(half + 1) * ML_DQK]
        m_ref[:, h:h + 1] = hd["m_new"]


def _prompt(sinks, x2d, g_norm, w, b, wg, bg, rope_c, rope_i, mk, mv, w_out, g_final):
    n = x2d.shape[0]
    nc = n // CHUNK
    const = lambda c, s: (0, 0)
    resident = pl.Buffered(1)
    assert nc % 2 == 0
    n_pairs = nc // 2
    x_in = lambda p, s: (jnp.minimum(p, n_pairs - 1), 0)
    x_res = lambda p, s: (jnp.clip(p - 1, 0, n_pairs - 1), 0)
    return pl.pallas_call(
        _prompt_kernel,
        grid_spec=pltpu.PrefetchScalarGridSpec(
            num_scalar_prefetch=1,
            grid=(n_pairs + 1,),
            in_specs=[
                pl.BlockSpec((2 * CHUNK, D_MODEL), x_in),
                pl.BlockSpec((2 * CHUNK, D_MODEL), x_res),
                pl.BlockSpec((1, D_MODEL), const, pipeline_mode=resident),
                pl.BlockSpec((D_MODEL, D_MAIN), const, pipeline_mode=resident),
                pl.BlockSpec((1, D_MAIN), const, pipeline_mode=resident),
                pl.BlockSpec((D_MODEL, GATE_LANES), const, pipeline_mode=resident),
                pl.BlockSpec((1, GATE_LANES), const, pipeline_mode=resident),
                pl.BlockSpec((2, nc, HEAD_DIM), lambda c, s: (0, 0, 0), pipeline_mode=resident),
                pl.BlockSpec((4, CHUNK, HEAD_DIM), lambda c, s: (0, 0, 0), pipeline_mode=resident),
                pl.BlockSpec((MEM_LEN, MEM_W), const, pipeline_mode=resident),
                pl.BlockSpec((MEM_LEN, MEM_W), const, pipeline_mode=resident),
                pl.BlockSpec((D_MIX, D_MODEL), const, pipeline_mode=resident),
                pl.BlockSpec((1, D_MODEL), const, pipeline_mode=resident),
            ],
            out_specs=[
                pl.BlockSpec((2 * CHUNK, D_MODEL), x_res),
                pl.BlockSpec((CHUNK, SWA_KV_W), const),
                pl.BlockSpec((CHUNK, SWA_KV_W), const),
                pl.BlockSpec((ML_QK_W, ML_DV), const),
                pl.BlockSpec((1, ML_QK_W), const),
                pl.BlockSpec((1, GATE_LANES), const),
            ],
            scratch_shapes=[
                pltpu.VMEM((CHUNK, SWA_KV_W), BF16),
                pltpu.VMEM((CHUNK, SWA_KV_W), BF16),
                pltpu.VMEM((CHUNK, D_MODEL), BF16),
                pltpu.VMEM((CHUNK, D_MAIN), F32),
                pltpu.VMEM((CHUNK, GATE_LANES), F32),
                pltpu.VMEM((CHUNK, D_MAIN), F32),
                pltpu.VMEM((CHUNK, GATE_LANES), F32),
                pltpu.VMEM((CHUNK, D_MIX), BF16),
                pltpu.VMEM((CHUNK, D_MIX), BF16),
            ],
        ),
        out_shape=[
            jax.ShapeDtypeStruct((n, D_MODEL), F32),
            jax.ShapeDtypeStruct((CHUNK, SWA_KV_W), F32),
            jax.ShapeDtypeStruct((CHUNK, SWA_KV_W), F32),
            jax.ShapeDtypeStruct((ML_QK_W, ML_DV), F32),
            jax.ShapeDtypeStruct((1, ML_QK_W), F32),
            jax.ShapeDtypeStruct((1, GATE_LANES), F32),
        ],
        compiler_params=pltpu.CompilerParams(
            dimension_semantics=("arbitrary",), vmem_limit_bytes=VMEM_LIMIT),
        name="prompt",
    )(sinks, x2d, x2d, g_norm, w, b, wg, bg, rope_c, rope_i, mk, mv, w_out, g_final)


OUT_TM = 512


def _outproj_kernel(y_ref, x_ref, w_ref, g_ref, o_ref):
    acc = _dot(y_ref[...], w_ref[...]) + x_ref[...]
    ms = jnp.mean(acc * acc, axis=-1, keepdims=True)
    o_ref[...] = (acc * lax.rsqrt(ms + EPS)) * g_ref[...]


def _outproj(y, x2d, w, g):
    n = x2d.shape[0]
    return pl.pallas_call(
        _outproj_kernel,
        grid=(n // OUT_TM,),
        in_specs=[
            pl.BlockSpec((OUT_TM, D_MIX), lambda i: (i, 0)),
            pl.BlockSpec((OUT_TM, D_MODEL), lambda i: (i, 0)),
            pl.BlockSpec((D_MIX, D_MODEL), lambda i: (0, 0)),
            pl.BlockSpec((1, D_MODEL), lambda i: (0, 0)),
        ],
        out_specs=pl.BlockSpec((OUT_TM, D_MODEL), lambda i: (i, 0)),
        out_shape=jax.ShapeDtypeStruct((n, D_MODEL), F32),
        compiler_params=pltpu.CompilerParams(
            dimension_semantics=("arbitrary",), vmem_limit_bytes=VMEM_LIMIT),
        name="outproj",
    )(y, x2d, w, g)


SB = 16
SR = SB * DEC_SEQ
SEQ_UNROLL = 4
SWA_UNROLL = 8


def _mix_sample_kernel(sinks_ref, z_ref, zg_ref, cos_ref, sin_ref, ck_ref, cv_ref,
                       c_ref, nst_ref, nrep_ref, mrep_ref,
                       y_ref, ko_ref, vo_ref, co_ref, no_ref, mo_ref,
                       q_s, k_s, o_s):
    T = DEC_SEQ
    scale = HEAD_DIM ** -0.5
    cos = cos_ref[...]
    sin = sin_ref[...]

    for h in range(SWA_HEADS):
        q_s[:, h * HEAD_DIM:(h + 1) * HEAD_DIM] = _rope(
            z_ref[:, O_SQ + h * HEAD_DIM:O_SQ + (h + 1) * HEAD_DIM], cos, sin)
    for kv in range(SWA_KV_HEADS):
        k_s[:, kv * HEAD_DIM:(kv + 1) * HEAD_DIM] = _rope(
            z_ref[:, O_SK + kv * HEAD_DIM:O_SK + (kv + 1) * HEAD_DIM], cos, sin)

    KP = 2 * WINDOW
    rt = lax.broadcasted_iota(jnp.int32, (GQA_GROUP * T, KP), 0) & (T - 1)
    cj = lax.broadcasted_iota(jnp.int32, (GQA_GROUP * T, KP), 1)
    mask = (cj > rt) & (cj <= rt + WINDOW)
    kpad = jnp.zeros((KP - WINDOW - T, HEAD_DIM), F32)
    NKV = SWA_KV_HEADS
    SEQ_ROWS = WINDOW * NKV

    keep = SEQ_ROWS - T * NKV
    sink_cols = [jnp.concatenate([jnp.full((T, 1), sinks_ref[kv * GQA_GROUP + g], F32)
                                  for g in range(GQA_GROUP)], axis=0) for kv in range(NKV)]

    def per_group(i, carry):
        chains = [(i * SWA_UNROLL + j, kv) for j in range(SWA_UNROLL) for kv in range(NKV)]
        scores, values = [], []
        for b, kv in chains:
            rows = pl.ds(pl.multiple_of(b * T, T), T)
            base = pl.multiple_of(b * SEQ_ROWS, SEQ_ROWS)
            cols = slice(kv * HEAD_DIM, (kv + 1) * HEAD_DIM)
            k_new = k_s[rows, cols]
            v_new = z_ref[rows, O_SV + kv * HEAD_DIM:O_SV + (kv + 1) * HEAD_DIM]
            if kv == 0:
                ko_ref[pl.ds(base, keep), :] = ck_ref[pl.ds(base + T * NKV, keep), :]
                vo_ref[pl.ds(base, keep), :] = cv_ref[pl.ds(base + T * NKV, keep), :]
            ko_ref[pl.ds(base + keep + kv, T, stride=NKV), :] = k_new
            vo_ref[pl.ds(base + keep + kv, T, stride=NKV), :] = v_new
            kc = ck_ref[pl.ds(base + kv, WINDOW, stride=NKV), :]
            vc = cv_ref[pl.ds(base + kv, WINDOW, stride=NKV), :]
            k_all = jnp.concatenate([kc, k_new, kpad], axis=0).astype(BF16)
            values.append(jnp.concatenate([vc, v_new, kpad], axis=0).astype(BF16))
            q_st = jnp.concatenate([q_s[rows, (kv * GQA_GROUP + g) * HEAD_DIM:(kv * GQA_GROUP + g + 1) * HEAD_DIM]
                                    for g in range(GQA_GROUP)], axis=0).astype(BF16)
            scores.append(_dot_nt(q_st, k_all))
        probs, norms = [], []
        for (b, kv), s in zip(chains, scores):
            s = jnp.where(mask, s * scale, -jnp.inf)
            m = jnp.maximum(jnp.max(s, axis=-1, keepdims=True), sink_cols[kv])
            p = jnp.exp(s - m)
            norms.append(jnp.sum(p, axis=-1, keepdims=True) + jnp.exp(sink_cols[kv] - m))
            probs.append(p.astype(BF16))
        outs = [_dot(p, v) for p, v in zip(probs, values)]
        for (b, kv), o, l in zip(chains, outs, norms):
            rows = pl.ds(pl.multiple_of(b * T, T), T)
            o = o * (1.0 / l)
            for g in range(GQA_GROUP):
                h = kv * GQA_GROUP + g
                o_s[rows, h * HEAD_DIM:(h + 1) * HEAD_DIM] = o[g * T:(g + 1) * T]
        return carry

    lax.fori_loop(0, SB // SWA_UNROLL, per_group, 0)

    for h in range(SWA_HEADS):
        cols = slice(h * HEAD_DIM, (h + 1) * HEAD_DIM)
        gate = _silu(z_ref[:, O_SG + h * HEAD_DIM:O_SG + (h + 1) * HEAD_DIM])
        y_ref[:, cols] = (o_s[:, cols] * gate).astype(BF16)
    y_ref[:, SWA_W + ML_W:] = jnp.zeros((SR, MEM_W), BF16)

    R = SR
    lane = lax.broadcasted_iota(jnp.int32, (1, GATE_LANES), 1)
    zg = zg_ref[...]
    gates = jnp.where(lane < LANE_F, zg, _log_sigmoid(zg))
    gates = jnp.where(lane < 2 * ML_HEADS, gates, 0.0)
    rr = lax.broadcasted_iota(jnp.int32, (R, R), 0)
    cc = lax.broadcasted_iota(jnp.int32, (R, R), 1)
    same_seq = (rr >> 3) == (cc >> 3)
    causal = same_seq & (rr >= cc)
    tri = jnp.where(causal, 1.0, 0.0).astype(BF16)
    csum = _exact_sel_dot(tri, gates)
    gates_t = gates.T
    csum_t = csum.T
    half_lo = lane < ML_DQK
    seq_of_col = lax.broadcasted_iota(jnp.int32, (SB, 1, R), 2) >> 3
    seq_id = lax.broadcasted_iota(jnp.int32, (SB, 1, R), 0)
    own_cols = seq_of_col == seq_id
    own_blk = ((lax.broadcasted_iota(jnp.int32, (R, SB * 128), 0) >> 3)
               == (lax.broadcasted_iota(jnp.int32, (R, SB * 128), 1) >> 7))
    k_t = [(z_ref[:, O_MK + p * 128:O_MK + (p + 1) * 128] * (ML_DQK ** -0.5)).T for p in range(2)]
    mrep = mrep_ref[...]
    tok3 = lax.broadcasted_iota(jnp.int32, (SB, T, 1), 1)
    mo_ref[...] = jnp.zeros_like(mo_ref)

    def per_seq_value(col):
        return jnp.max(col.reshape(SB, T, 1), axis=1, keepdims=True)

    def last_of_seq(col):
        c3 = jnp.where(tok3 == T - 1, col.reshape(SB, T, 1), -jnp.inf)
        return jnp.broadcast_to(jnp.max(c3, axis=1, keepdims=True), (SB, T, 1)).reshape(R, 1)

    for h in range(ML_HEADS):
        pair, half = divmod(h, 2)
        hmask = half_lo if half == 0 else jnp.logical_not(half_lo)
        q_pair = z_ref[:, O_MQ + pair * 128:O_MQ + (pair + 1) * 128]
        k_pair = z_ref[:, O_MK + pair * 128:O_MK + (pair + 1) * 128] * (ML_DQK ** -0.5)
        v_f = z_ref[:, O_MV + h * ML_DV:O_MV + (h + 1) * ML_DV]
        v_h = v_f.astype(BF16)
        q_h = jnp.where(hmask, q_pair, 0.0)
        q_hb = q_h.astype(BF16)
        k_pb = k_pair.astype(BF16)

        bt_c = csum[:, LANE_F + h:LANE_F + h + 1]
        it_c = gates[:, LANE_I + h:LANE_I + h + 1]
        bt_r = csum_t[LANE_F + h:LANE_F + h + 1, :]
        it_r = gates_t[LANE_I + h:LANE_I + h + 1, :]
        m_prev = mrep[:, h:h + 1]

        dlog = jnp.where(causal, bt_c - bt_r + it_r, -jnp.inf)
        inter = bt_c + m_prev
        m_t = jnp.maximum(inter, jnp.max(dlog, axis=-1, keepdims=True))
        w_intra = jnp.exp(dlog - m_t)
        w_state = jnp.exp(inter - m_t)
        a = w_intra * _dot_nt(q_hb, k_pb)

        ct_st = c_ref[:, 2 * pair:2 * pair + 2].reshape(SB * 128, ML_DV).astype(BF16)
        q_blk = jnp.where(own_blk, jnp.tile(q_h, (1, SB)), 0.0).astype(BF16)
        num_state = _dot(q_blk, ct_st)

        num = _dot(a.astype(BF16), v_h) + w_state * num_state
        n_pair = nrep_ref[:, pair * 128:(pair + 1) * 128]
        den = jnp.sum(a, axis=-1, keepdims=True) + w_state * jnp.sum(q_h * n_pair, axis=-1, keepdims=True)
        denom = jnp.maximum(jnp.abs(den), jnp.exp(-m_t))
        hid = num * (1.0 / denom)
        o_gate = _sigmoid(z_ref[:, O_MO + h * ML_DV:O_MO + (h + 1) * ML_DV])
        gate = _silu(z_ref[:, O_MG + h * ML_DV:O_MG + (h + 1) * ML_DV])
        col = SWA_W + h * ML_DV
        y_ref[:, col:col + ML_DV] = ((hid * o_gate) * gate).astype(BF16)

        m_new = last_of_seq(m_t)
        bt_last = last_of_seq(bt_c)
        w_s = jnp.exp(bt_last - bt_c + it_c - m_new)
        decay = jnp.exp(bt_last + m_prev - m_new)
        decay_seq = per_seq_value(decay)

        kt_h = k_t[pair][half * ML_DQK:(half + 1) * ML_DQK, :]
        lhs = jnp.where(own_cols, kt_h[None, :, :], 0.0).reshape(SB * ML_DQK, R).astype(BF16)
        upd = _dot(lhs, (v_f * w_s).astype(BF16)).reshape(SB, ML_DQK, ML_DV)
        co_ref[:, h] = decay_seq * c_ref[:, h] + upd

        kw = (k_pair * w_s).reshape(SB, T, 128)
        ksum = jnp.sum(kw, axis=1)
        n_old = nst_ref[:, h * ML_DQK:(h + 1) * ML_DQK]
        dec2 = decay_seq.reshape(SB, 1)
        no_ref[:, h * ML_DQK:(h + 1) * ML_DQK] = dec2 * n_old + ksum[:, half * ML_DQK:(half + 1) * ML_DQK]
        mo_ref[:, h:h + 1] = per_seq_value(m_new).reshape(SB, 1)


def _mix_sample(sinks, z, zg, cos, sin, ck, cv, cst, nst, nrep, mrep):
    nb = cst.shape[0]
    steps = nb // SB
    const = lambda i, s: (0, 0)
    cache_rows = SB * WINDOW * SWA_KV_HEADS
    return pl.pallas_call(
        _mix_sample_kernel,
        grid_spec=pltpu.PrefetchScalarGridSpec(
            num_scalar_prefetch=1,
            grid=(steps,),
            in_specs=[
                pl.BlockSpec((SR, D_MAIN), lambda i, s: (i, 0)),
                pl.BlockSpec((SR, GATE_LANES), lambda i, s: (i, 0)),
                pl.BlockSpec((SR, HEAD_DIM), const),
                pl.BlockSpec((SR, HEAD_DIM), const),
                pl.BlockSpec((cache_rows, HEAD_DIM), lambda i, s: (i, 0)),
                pl.BlockSpec((cache_rows, HEAD_DIM), lambda i, s: (i, 0)),
                pl.BlockSpec((SB, ML_HEADS, ML_DQK, ML_DV), lambda i, s: (i, 0, 0, 0)),
                pl.BlockSpec((SB, ML_QK_W), lambda i, s: (i, 0)),
                pl.BlockSpec((SR, ML_QK_W), lambda i, s: (i, 0)),
                pl.BlockSpec((SR, GATE_LANES), lambda i, s: (i, 0)),
            ],
            out_specs=[
                pl.BlockSpec((SR, D_MIX), lambda i, s: (i, 0)),
                pl.BlockSpec((cache_rows, HEAD_DIM), lambda i, s: (i, 0)),
                pl.BlockSpec((cache_rows, HEAD_DIM), lambda i, s: (i, 0)),
                pl.BlockSpec((SB, ML_HEADS, ML_DQK, ML_DV), lambda i, s: (i, 0, 0, 0)),
                pl.BlockSpec((SB, ML_QK_W), lambda i, s: (i, 0)),
                pl.BlockSpec((SB, GATE_LANES), lambda i, s: (i, 0)),
            ],
            scratch_shapes=[
                pltpu.VMEM((SR, SWA_W), F32),
                pltpu.VMEM((SR, SWA_KV_W), F32),
                pltpu.VMEM((SR, SWA_W), F32),
            ],
        ),
        out_shape=[
            jax.ShapeDtypeStruct((nb * DEC_SEQ, D_MIX), BF16),
            jax.ShapeDtypeStruct(ck.shape, F32),
            jax.ShapeDtypeStruct(cv.shape, F32),
            jax.ShapeDtypeStruct((nb, ML_HEADS, ML_DQK, ML_DV), F32),
            jax.ShapeDtypeStruct((nb, ML_QK_W), F32),
            jax.ShapeDtypeStruct((nb, GATE_LANES), F32),
        ],
        compiler_params=pltpu.CompilerParams(
            dimension_semantics=("arbitrary",), vmem_limit_bytes=VMEM_LIMIT),
        name="mix_sample",
    )(sinks, z, zg, cos, sin, ck, cv, cst, nst, nrep, mrep)


MB = 8
MR = MB * DEC_SEQ


MEM_RING = 3


def _mem_sample_kernel(cq_ref, cg_ref, mk_hbm, mv_hbm, y_in_ref, y_ref, o_s, kbuf, vbuf, sems):
    del y_in_ref
    step = pl.program_id(0)
    n_steps = pl.num_programs(0)
    blk = MB * MEM_LEN * MEM_HEADS

    def copies(block, slot):
        rows = pl.ds(pl.multiple_of(block * blk, blk), blk)
        return (pltpu.make_async_copy(mk_hbm.at[rows], kbuf.at[slot], sems.at[0, slot]),
                pltpu.make_async_copy(mv_hbm.at[rows], vbuf.at[slot], sems.at[1, slot]))

    @pl.when(step == 0)
    def _():
        for b in range(MEM_RING):
            for cp in copies(b, b):
                cp.start()

    slot = lax.rem(step, MEM_RING)
    for cp in copies(step, slot):
        cp.wait()
    mk_ref = kbuf.at[slot]
    mv_ref = vbuf.at[slot]
    T = DEC_SEQ
    scale = HEAD_DIM ** -0.5
    zpad = jnp.zeros((T, HEAD_DIM), F32)

    seq_rows = MEM_LEN * MEM_HEADS

    def per_group(i, carry):
        chains = [(i * SEQ_UNROLL + j, h) for j in range(SEQ_UNROLL) for h in range(MEM_HEADS)]
        scores = []
        for b, h in chains:
            rows = pl.ds(pl.multiple_of(b * T, T), T)
            base = pl.multiple_of(b * seq_rows, seq_rows)
            kb = mk_ref[pl.ds(base + h, MEM_LEN, stride=MEM_HEADS), :].astype(BF16)
            q = jnp.concatenate([cq_ref[rows, h * HEAD_DIM:(h + 1) * HEAD_DIM], zpad], axis=0).astype(BF16)
            scores.append(_dot_nt(q, kb))
        probs, norms = [], []
        for s in scores:
            s = s * scale
            p = jnp.exp(s - jnp.max(s, axis=-1, keepdims=True))
            norms.append(jnp.sum(p, axis=-1, keepdims=True))
            probs.append(p.astype(BF16))
        outs = []
        for (b, h), p in zip(chains, probs):
            base = pl.multiple_of(b * seq_rows, seq_rows)
            vb = mv_ref[pl.ds(base + h, MEM_LEN, stride=MEM_HEADS), :].astype(BF16)
            outs.append(_dot(p, vb))
        for (b, h), o, l in zip(chains, outs, norms):
            rows = pl.ds(pl.multiple_of(b * T, T), T)
            o_s[rows, h * HEAD_DIM:(h + 1) * HEAD_DIM] = (o * (1.0 / l))[0:T]
        return carry

    lax.fori_loop(0, MB // SEQ_UNROLL, per_group, 0)

    @pl.when(step + MEM_RING < n_steps)
    def _():
        for cp in copies(step + MEM_RING, slot):
            cp.start()

    y_ref[...] = (o_s[...] * _silu(cg_ref[...])).astype(BF16)


def _mem_sample(z, mk, mv, y):
    nb = z.shape[0] // DEC_SEQ
    cache_rows = MB * MEM_LEN * MEM_HEADS
    return pl.pallas_call(
        _mem_sample_kernel,
        grid=(nb // MB,),
        in_specs=[
            pl.BlockSpec((MR, MEM_W), lambda i: (i, O_CQ // MEM_W)),
            pl.BlockSpec((MR, MEM_W), lambda i: (i, O_CG // MEM_W)),
            pl.BlockSpec(memory_space=pl.ANY),
            pl.BlockSpec(memory_space=pl.ANY),
            pl.BlockSpec(memory_space=pl.ANY),
        ],
        out_specs=pl.BlockSpec((MR, MEM_W), lambda i: (i, (SWA_W + ML_W) // MEM_W)),
        out_shape=jax.ShapeDtypeStruct(y.shape, y.dtype),
        input_output_aliases={4: 0},
        scratch_shapes=[pltpu.VMEM((MR, MEM_W), F32),
                        pltpu.VMEM((MEM_RING, cache_rows, HEAD_DIM), F32),
                        pltpu.VMEM((MEM_RING, cache_rows, HEAD_DIM), F32),
                        pltpu.SemaphoreType.DMA((2, MEM_RING))],
        compiler_params=pltpu.CompilerParams(
            dimension_semantics=("arbitrary",), vmem_limit_bytes=VMEM_LIMIT),
        name="mem_sample",
    )(z, z, mk, mv, y)


def _rope_cos_sin(pos):
    half = HEAD_DIM // 2
    inv = jnp.power(ROPE_THETA, -(jnp.arange(half, dtype=F32) * 2.0 / HEAD_DIM))
    ang = pos.astype(F32)[:, None] * inv[None, :]
    return jnp.cos(ang), jnp.sin(ang)


def _rope_tables(pos):
    cos, sin = _rope_cos_sin(pos)
    return jnp.concatenate([cos, cos], axis=-1), jnp.concatenate([-sin, sin], axis=-1)


def _rope_split_tables(n_chunks):
    ca, sa = _rope_cos_sin(jnp.arange(n_chunks, dtype=jnp.int32) * CHUNK)
    cb, sb = _rope_cos_sin(jnp.arange(CHUNK, dtype=jnp.int32))
    dup = lambda t: jnp.concatenate([t, t], axis=-1)
    sgn = lambda t: jnp.concatenate([-t, t], axis=-1)
    return jnp.stack([dup(ca), dup(sa)]), jnp.stack([dup(cb), dup(sb), sgn(cb), sgn(sb)])


def _relayout_in_proj(w_in, b_in):
    w_t = jnp.swapaxes(w_in, 0, 1)
    w_main = _wprep(w_t)
    w_gate = _wgate(w_t)
    b_main = jnp.concatenate([b_in[_R_SQ:_R_MI], b_in[_R_MO:_R_END]])[None, :]
    b_gate = jnp.pad(b_in[_R_MI:_R_MO], (0, GATE_LANES - 2 * ML_HEADS))[None, :]
    return w_main, b_main, w_gate, b_gate


def _layer(xp, xs, mem, ck, cv, c_st, n_st, m_st, cmk, cmv,
           g_norm, w_in, b_in, sinks, g_mem, w_mem_kv, w_out, g_final):
    bp, sp, _ = xp.shape
    bs, ts, _ = xs.shape
    xp2 = xp.reshape(bp * sp, D_MODEL)
    xs2 = xs.reshape(bs * ts, D_MODEL)
    w_main, b_main, w_gate, b_gate = _relayout_in_proj(w_in, b_in)
    g_norm2 = g_norm[None, :]
    w_out_b = w_out.astype(BF16)
    g_final2 = g_final[None, :]
    sinks_flat = sinks.reshape(SWA_HEADS)

    memkv = _memkv(mem.reshape(MEM_LEN, D_MODEL), g_mem[None, :], w_mem_kv.astype(BF16))
    mem_k = memkv[:, :MEM_W]
    mem_v = memkv[:, MEM_W:]
    rope_c, rope_i = _rope_split_tables(sp // CHUNK)
    out_p, pk, pv, ct, n_p, m_p = _prompt(sinks_flat, xp2, g_norm2, w_main, b_main, w_gate, b_gate,
                                           rope_c, rope_i, mem_k.astype(BF16), mem_v.astype(BF16),
                                           w_out_b, g_final2)

    zs, zgs = _proj(xs2, g_norm2, w_main, b_main, w_gate, b_gate)
    cos_s, sin_s = _rope_tables(PAST_LEN + jnp.arange(ts, dtype=jnp.int32))
    cos_s = jnp.tile(cos_s, (SB, 1))
    sin_s = jnp.tile(sin_s, (SB, 1))
    nrep = jnp.repeat(n_st.reshape(bs, ML_QK_W), ts, axis=0)
    m_pad = jnp.pad(m_st, ((0, 0), (0, GATE_LANES - ML_HEADS)))
    mrep = jnp.repeat(m_pad, ts, axis=0)
    ya, sk_o, sv_o, ct_o, n_o, m_o = _mix_sample(
        sinks_flat, zs, zgs, cos_s, sin_s,
        ck.reshape(bs * WINDOW * SWA_KV_HEADS, HEAD_DIM), cv.reshape(bs * WINDOW * SWA_KV_HEADS, HEAD_DIM),
        jnp.swapaxes(c_st, -1, -2), n_st.reshape(bs, ML_QK_W), nrep, mrep)
    ys = _mem_sample(zs, cmk.reshape(bs * MEM_LEN * MEM_HEADS, HEAD_DIM),
                     cmv.reshape(bs * MEM_LEN * MEM_HEADS, HEAD_DIM), ya)
    out_s = _outproj(ys, xs2, w_out_b, g_final2)
    c_o = jnp.swapaxes(ct_o, -1, -2)

    p_state = (
        pk.reshape(bp, WINDOW, SWA_KV_HEADS, HEAD_DIM),
        pv.reshape(bp, WINDOW, SWA_KV_HEADS, HEAD_DIM),
        ct.reshape(ML_HEADS, ML_DQK, ML_DV).transpose(0, 2, 1)[None],
        n_p.reshape(bp, ML_HEADS, ML_DQK),
        m_p[:, :ML_HEADS],
        mem_k.reshape(bp, MEM_LEN, MEM_HEADS, HEAD_DIM),
        mem_v.reshape(bp, MEM_LEN, MEM_HEADS, HEAD_DIM),
    )
    s_state = (
        sk_o.reshape(bs, WINDOW, SWA_KV_HEADS, HEAD_DIM),
        sv_o.reshape(bs, WINDOW, SWA_KV_HEADS, HEAD_DIM),
        c_o,
        n_o.reshape(bs, ML_HEADS, ML_DQK),
        m_o[:, :ML_HEADS],
    )
    return out_p.reshape(bp, sp, D_MODEL), out_s.reshape(bs, ts, D_MODEL), p_state, s_state


def kernel(x_prompt, x_sample, mem_prompt, cache_swa_k, cache_swa_v, state_mlstm_C, state_mlstm_n,
           state_mlstm_m, cache_mem_k, cache_mem_v, g_norm, w_in, b_in, swa_sinks, g_mem, w_mem_kv,
           w_out, g_final):
    depth = g_norm.shape[0]
    assert depth == 1 and x_prompt.shape[0] == 1
    y_p, y_s, p_state, s_state = _layer(
        x_prompt, x_sample, mem_prompt, cache_swa_k[0], cache_swa_v[0], state_mlstm_C[0],
        state_mlstm_n[0], state_mlstm_m[0], cache_mem_k[0], cache_mem_v[0],
        g_norm[0], w_in[0], b_in[0], swa_sinks[0], g_mem[0], w_mem_kv[0], w_out[0], g_final)
    return (y_p, y_s) + tuple(s[None] for s in p_state) + tuple(s[None] for s in s_state)
```
